```python
import jax
import jax.numpy as jnp
from jax import lax
import numpy as np

D_MODEL = 1024
BATCH = 16
SEQ = 2048
DEPTH = 4

CTX_LEN = 256
GRID_W = 64
EPS = 1e-6

GLA_HEADS = 4
GLA_DK = 48
GLA_DV = 96
GLA_QK = GLA_HEADS * GLA_DK
GLA_WIDTH = GLA_HEADS * GLA_DV
GLA_LOWRANK = 16
GLA_TAU = 16.0
GLA_CHUNK = 64
ROPE_BASE = 10000.0

NA_HEADS = 6
NA_DH = 64
NA_WIDTH = NA_HEADS * NA_DH
NA_KH = 8
NA_KW = 16

CONV_WIDTH = 256
CONV_K = 31

MIX_WIDTH = GLA_WIDTH + NA_WIDTH + CONV_WIDTH
IN_SIZES = (GLA_QK, GLA_QK, GLA_WIDTH, GLA_WIDTH, 2 * GLA_LOWRANK, NA_WIDTH, NA_WIDTH, NA_WIDTH, 2 * CONV_WIDTH)
IN_COLS = sum(IN_SIZES)

N_EXPERTS = 16
N_GROUPS = 4
EXPERTS_PER_GROUP = N_EXPERTS // N_GROUPS
TOP_K = 2
D_FF_EXPERT = 1024
MOE_BLOCK = 256

kernel_name = 'hybrid_gla_natten_conformer_groupmoe_dit'


def rmsnorm(x, g):
    xf = x.astype(jnp.float32)
    y = xf * lax.rsqrt(jnp.mean(xf * xf, axis=-1, keepdims=True) + EPS)
    return (y * g.astype(jnp.float32)).astype(x.dtype)


def layernorm(x, g, b):
    xf = x.astype(jnp.float32)
    xc = xf - jnp.mean(xf, axis=-1, keepdims=True)
    y = xc * lax.rsqrt(jnp.mean(xc * xc, axis=-1, keepdims=True) + EPS)
    return (y * g.astype(jnp.float32) + b.astype(jnp.float32)).astype(x.dtype)


def to_heads(a, n_heads):
    bb, n, _ = a.shape
    return a.reshape(bb, n, n_heads, -1).transpose(0, 2, 1, 3)


def from_heads(a):
    bb, h, n, d = a.shape
    return a.transpose(0, 2, 1, 3).reshape(bb, n, h * d)


def axial_rope(n, dim):
    t = jnp.arange(n)
    row = (t // GRID_W).astype(jnp.float32)
    col = (t % GRID_W).astype(jnp.float32)
    half = dim // 2
    inv = ROPE_BASE ** (-jnp.arange(0, half, 2, dtype=jnp.float32) / half)
    ang = jnp.concatenate([row[:, None] * inv, col[:, None] * inv], axis=-1)
    return jnp.cos(ang), jnp.sin(ang)


def apply_rope(x, cos, sin):
    cos = cos.astype(x.dtype)
    sin = sin.astype(x.dtype)
    x1, x2 = x[..., 0::2], x[..., 1::2]
    return jnp.stack([x1 * cos - x2 * sin, x1 * sin + x2 * cos], axis=-1).reshape(x.shape)


def gla_scan(q, k, v, log_g, s0):
    bb, h, n, dk = q.shape
    dv = v.shape[-1]
    nc = n // GLA_CHUNK
    order_mask = jnp.tril(jnp.ones((GLA_CHUNK, GLA_CHUNK), dtype=bool))[:, :, None]

    def chunks(a):
        return jnp.moveaxis(a.reshape(bb, h, nc, GLA_CHUNK, a.shape[-1]), 2, 0)

    def step(state, inp):
        qc, kc, vc, gc = [t.astype(jnp.float32) for t in inp]
        cum = jnp.cumsum(gc, axis=2)
        o_inter = jnp.einsum('bhtd,bhde->bhte', qc * jnp.exp(cum), state)
        decay = jnp.exp(jnp.where(order_mask, cum[:, :, :, None, :] - cum[:, :, None, :, :], -jnp.inf))
        att = jnp.einsum('bhtd,bhsd,bhtsd->bhts', qc, kc, decay)
        o = o_inter + jnp.einsum('bhts,bhse->bhte', att, vc)
        end = cum[:, :, -1:, :]
        new_state = (jnp.exp(end[:, :, 0, :])[..., None] * state
                     + jnp.einsum('bhsd,bhse->bhde', kc * jnp.exp(end - cum), vc))
        return new_state, o

    s_fin, o = lax.scan(step, s0, (chunks(q), chunks(k), chunks(v), chunks(log_g)))
    return jnp.moveaxis(o, 0, 2).reshape(bb, h, n, dv).astype(v.dtype), s_fin


def gla_inputs(q, k, v, a, wa_f, ba_f, wa_b, ba_b):
    q = to_heads(q, GLA_HEADS) * (GLA_DK ** -0.5)
    k = to_heads(k, GLA_HEADS)
    v = to_heads(v, GLA_HEADS)
    a_f, a_b = jnp.split(a, 2, axis=-1)
    g_f = to_heads(jax.nn.log_sigmoid((a_f @ wa_f + ba_f).astype(jnp.float32)) / GLA_TAU, GLA_HEADS)
    g_b = to_heads(jax.nn.log_sigmoid((a_b @ wa_b + ba_b).astype(jnp.float32)) / GLA_TAU, GLA_HEADS)
    return q, k, v, g_f, g_b


def gla_bidir(q, k, v, g_f, g_b, s_f, s_b):
    o_f, s_f_new = gla_scan(q, k, v, g_f, s_f)
    flip = lambda t: jnp.flip(t, axis=2)
    o_b, s_b_new = gla_scan(flip(q), flip(k), flip(v), flip(g_b), s_b)
    return o_f + flip(o_b), s_f_new, s_b_new


def gla_output(o, gate, g_norm):
    of = o.astype(jnp.float32)
    of = of * lax.rsqrt(jnp.mean(of * of, axis=-1, keepdims=True) + EPS)
    of = from_heads(of) * g_norm.astype(jnp.float32)
    return of.astype(gate.dtype) * jax.nn.silu(gate)


def na_heads(q, k, v):
    return to_heads(q, NA_HEADS) * (NA_DH ** -0.5), to_heads(k, NA_HEADS), to_heads(v, NA_HEADS)


def na_latent(q, k, v, kc, vc, rpb):
    bb, h, n, dh = q.shape
    rows = n // GRID_W
    kh = min(NA_KH, rows)
    kw = NA_KW
    cq = np.arange(GRID_W)
    c0 = np.clip(cq - kw // 2, 0, GRID_W - kw)
    colidx = c0[:, None] + np.arange(kw)[None, :]
    dcol = colidx - cq[:, None] + (NA_KW - 1)
    kg = k.reshape(bb, h, rows, GRID_W, dh)
    vg = v.reshape(bb, h, rows, GRID_W, dh)
    q_rows = jnp.moveaxis(q.reshape(bb, h, rows, GRID_W, dh), 2, 0)

    def row_block(args):
        qr, r = args
        r0 = jnp.clip(r - kh // 2, 0, rows - kh)
        kb = lax.dynamic_slice_in_dim(kg, r0, kh, axis=2)[:, :, :, colidx]
        vb = lax.dynamic_slice_in_dim(vg, r0, kh, axis=2)[:, :, :, colidx]
        drow = r0 + jnp.arange(kh) - r + (NA_KH - 1)
        bias = jnp.transpose(rpb[:, drow][:, :, dcol], (0, 2, 1, 3))
        s_loc = jnp.einsum('bhqd,bhiqjd->bhqij', qr, kb).astype(jnp.float32) + bias.astype(jnp.float32)
        s_ctx = jnp.einsum('bhqd,bhld->bhql', qr, kc).astype(jnp.float32)
        s = jnp.concatenate([s_loc.reshape(bb, h, GRID_W, kh * kw), s_ctx], axis=-1)
        p = jax.nn.softmax(s, axis=-1).astype(v.dtype)
        p_loc = p[..., :kh * kw].reshape(bb, h, GRID_W, kh, kw)
        return (jnp.einsum('bhqij,bhiqjd->bhqd', p_loc, vb)
                + jnp.einsum('bhql,bhld->bhqd', p[..., kh * kw:], vc))

    o = lax.map(row_block, (q_rows, jnp.arange(rows)))
    return jnp.moveaxis(o, 0, 2).reshape(bb, h, n, dh)


def na_context(qc, kc, vc):
    s = jnp.einsum('bhqd,bhkd->bhqk', qc, kc).astype(jnp.float32)
    p = jax.nn.softmax(s, axis=-1).astype(vc.dtype)
    return jnp.einsum('bhqk,bhkd->bhqd', p, vc)


def conv_module(u, w_dw, b_dw, ln_g, ln_b):
    a, gate = jnp.split(u, 2, axis=-1)
    hcv = a * jax.nn.sigmoid(gate)
    hcv = lax.conv_general_dilated(
        hcv, w_dw[:, None, :].astype(hcv.dtype), window_strides=(1,),
        padding=((CONV_K // 2, CONV_K // 2),), dimension_numbers=('NWC', 'WIO', 'NWC'),
        feature_group_count=CONV_WIDTH) + b_dw
    return jax.nn.silu(layernorm(hcv, ln_g, ln_b))


def moe(h, router_w, router_bias, wg, wu, wd):
    t, d = h.shape
    aff = jax.nn.sigmoid((h @ router_w).astype(jnp.float32))
    sel = aff + router_bias.astype(jnp.float32)
    gscore = lax.top_k(sel.reshape(t, N_GROUPS, EXPERTS_PER_GROUP), TOP_K)[0].sum(-1)
    gbest = jnp.argmax(gscore, axis=-1)
    in_group = (jnp.arange(N_EXPERTS) // EXPERTS_PER_GROUP)[None, :] == gbest[:, None]
    _, eidx = lax.top_k(jnp.where(in_group, sel, -jnp.inf), TOP_K)
    w = jnp.take_along_axis(aff, eidx, axis=-1)
    w = w / jnp.sum(w, axis=-1, keepdims=True)
    flat_e = eidx.reshape(-1)
    flat_tok = jnp.repeat(jnp.arange(t, dtype=jnp.int32), TOP_K)
    flat_w = w.reshape(-1)
    n_slots = flat_e.shape[0]
    order = jnp.argsort(flat_e)
    e_s, tok_s, w_s = flat_e[order], flat_tok[order], flat_w[order]
    counts = jnp.bincount(flat_e, length=N_EXPERTS)
    padded = (counts + MOE_BLOCK - 1) // MOE_BLOCK * MOE_BLOCK
    pad_end = jnp.cumsum(padded)
    pad_start = pad_end - padded
    start = jnp.cumsum(counts) - counts
    dest = pad_start[e_s] + jnp.arange(n_slots) - start[e_s]
    n_blocks = -(-n_slots // MOE_BLOCK) + N_EXPERTS
    p_rows = n_blocks * MOE_BLOCK
    row_tok = jnp.full((p_rows,), t, jnp.int32).at[dest].set(tok_s)
    row_w = jnp.zeros((p_rows,), h.dtype).at[dest].set(w_s.astype(h.dtype))
    blk_e = jnp.minimum(jnp.sum(pad_end[None, :] <= (jnp.arange(n_blocks) * MOE_BLOCK)[:, None], axis=-1),
                        N_EXPERTS - 1)
    h_pad = jnp.concatenate([h, jnp.zeros((1, d), h.dtype)], axis=0)
    xb = h_pad[row_tok].reshape(n_blocks, MOE_BLOCK, d)

    def expert_block(args):
        xe, e = args
        return (jax.nn.silu(xe @ wg[e]) * (xe @ wu[e])) @ wd[e]

    yb = lax.map(expert_block, (xb, blk_e)).reshape(p_rows, d)
    out = jnp.zeros((t + 1, d), h.dtype).at[row_tok].add(yb * row_w[:, None])
    return out[:t]


def setup_inputs(seed: int = 0) -> dict:
    key = jax.random.key(seed)
    ks = iter(jax.random.split(key, 40))
    nrm = lambda shape, s: jax.random.normal(next(ks), shape, jnp.float32) * s
    L, D = DEPTH, D_MODEL
    return {
        'x': nrm((BATCH, SEQ, D), 1.0),
        'c': nrm((BATCH, D), 1.0),
        'ctx': nrm((BATCH, CTX_LEN, D), 1.0),
        'c_ctx': nrm((D,), 1.0),
        'w_mod': nrm((L, D, 6 * D), 0.5 * D ** -0.5),
        'b_mod': nrm((L, 6 * D), 0.01),
        'g_mix': 1.0 + nrm((L, D), 0.01),
        'g_ffn': 1.0 + nrm((L, D), 0.01),
        'w_in': nrm((L, D, IN_COLS), D ** -0.5),
        'gla_wa_f': nrm((L, GLA_LOWRANK, GLA_QK), GLA_LOWRANK ** -0.5),
        'gla_ba_f': nrm((L, GLA_QK), 0.01),
        'gla_wa_b': nrm((L, GLA_LOWRANK, GLA_QK), GLA_LOWRANK ** -0.5),
        'gla_ba_b': nrm((L, GLA_QK), 0.01),
        'gla_g_norm': 1.0 + nrm((L, GLA_WIDTH), 0.01),
        'na_rpb': nrm((L, NA_HEADS, 2 * NA_KH - 1, 2 * NA_KW - 1), 0.1),
        'conv_w': nrm((L, CONV_K, CONV_WIDTH), CONV_K ** -0.5),
        'conv_b': nrm((L, CONV_WIDTH), 0.01),
        'conv_ln_g': 1.0 + nrm((L, CONV_WIDTH), 0.01),
        'conv_ln_b': nrm((L, CONV_WIDTH), 0.01),
        'w_out': nrm((L, MIX_WIDTH, D), MIX_WIDTH ** -0.5),
        'router_w': nrm((D, N_EXPERTS), D ** -0.5),
        'router_bias': nrm((N_EXPERTS,), 0.01),
        'w_gate': nrm((L, N_EXPERTS, D, D_FF_EXPERT), D ** -0.5),
        'w_up': nrm((L, N_EXPERTS, D, D_FF_EXPERT), D ** -0.5),
        'w_down': nrm((L, N_EXPERTS, D_FF_EXPERT, D), D_FF_EXPERT ** -0.5),
        'g_final': 1.0 + nrm((D,), 0.01),
    }


def reference(x, c, ctx, c_ctx, w_mod, b_mod, g_mix, g_ffn, w_in, gla_wa_f, gla_ba_f, gla_wa_b, gla_ba_b,
              gla_g_norm, na_rpb, conv_w, conv_b, conv_ln_g, conv_ln_b, w_out, router_w, router_bias,
              w_gate, w_up, w_down, g_final):
    b, n, d = x.shape
    split_at = [int(s) for s in np.cumsum(IN_SIZES)[:-1]]
    cos, sin = axial_rope(n, GLA_DK)
    s_lat = jax.nn.silu(c)
    s_ctx = jax.nn.silu(c_ctx)[None]
    xc = ctx
    for l in range(DEPTH):
        last = l == DEPTH - 1
        mx = jnp.split((s_lat @ w_mod[l] + b_mod[l])[:, None, :], 6, axis=-1)
        mc = jnp.split((s_ctx @ w_mod[l] + b_mod[l])[:, None, :], 6, axis=-1)
        hx = rmsnorm(x, g_mix[l]) * (1 + mx[1]) + mx[0]
        hc = rmsnorm(xc, g_mix[l]) * (1 + mc[1]) + mc[0]
        px = jnp.split(hx @ w_in[l], split_at, axis=-1)
        pc = jnp.split(hc @ w_in[l], split_at, axis=-1)
        gw = (gla_wa_f[l], gla_ba_f[l], gla_wa_b[l], gla_ba_b[l])
        q_c, k_c, v_c, gf_c, gb_c = gla_inputs(pc[0], pc[1], pc[2], pc[4], *gw)
        q_x, k_x, v_x, gf_x, gb_x = gla_inputs(px[0], px[1], px[2], px[4], *gw)
        q_x, k_x = apply_rope(q_x, cos, sin), apply_rope(k_x, cos, sin)
        zero = jnp.zeros((b, GLA_HEADS, GLA_DK, GLA_DV), jnp.float32)
        o_gc, st_f, st_b = gla_bidir(q_c, k_c, v_c, gf_c, gb_c, zero, zero)
        o_gx, _, _ = gla_bidir(q_x, k_x, v_x, gf_x, gb_x, st_f, st_b)
        nq_x, nk_x, nv_x = na_heads(px[5], px[6], px[7])
        nq_c, nk_c, nv_c = na_heads(pc[5], pc[6], pc[7])
        o_nx = na_latent(nq_x, nk_x, nv_x, nk_c, nv_c, na_rpb[l])
        o_cx = conv_module(px[8], conv_w[l], conv_b[l], conv_ln_g[l], conv_ln_b[l])
        y_x = jnp.concatenate([gla_output(o_gx, px[3], gla_g_norm[l]), from_heads(o_nx), o_cx], axis=-1) @ w_out[l]
        x = x + mx[2] * y_x
        h2x = rmsnorm(x, g_ffn[l]) * (1 + mx[4]) + mx[3]
        if last:
            x = x + mx[5] * moe(h2x.reshape(b * n, d), router_w, router_bias,
                                w_gate[l], w_up[l], w_down[l]).reshape(b, n, d)
        else:
            o_nc = na_context(nq_c, nk_c, nv_c)
            o_cc = conv_module(pc[8], conv_w[l], conv_b[l], conv_ln_g[l], conv_ln_b[l])
            y_c = jnp.concatenate([gla_output(o_gc, pc[3], gla_g_norm[l]), from_heads(o_nc), o_cc], axis=-1) @ w_out[l]
            xc = xc + mc[2] * y_c
            h2c = rmsnorm(xc, g_ffn[l]) * (1 + mc[4]) + mc[3]
            lc = xc.shape[1]
            y = moe(jnp.concatenate([h2x.reshape(b * n, d), h2c.reshape(b * lc, d)], axis=0),
                    router_w, router_bias, w_gate[l], w_up[l], w_down[l])
            x = x + mx[5] * y[:b * n].reshape(b, n, d)
            xc = xc + mc[5] * y[b * n:].reshape(b, lc, d)
    return rmsnorm(x, g_final)
```

```python
import functools

import numpy as np
import jax
import jax.numpy as jnp
from jax import lax
from jax.experimental import pallas as pl
from jax.experimental.pallas import tpu as pltpu

GRID_W = 64
EPS = 1e-6
GLA_HEADS, GLA_DK, GLA_DV = 4, 48, 96
GLA_QK = GLA_HEADS * GLA_DK
GLA_WIDTH = GLA_HEADS * GLA_DV
GLA_LOWRANK = 16
GLA_TAU = 16.0
ROPE_BASE = 10000.0
NA_HEADS, NA_DH = 6, 64
NA_WIDTH = NA_HEADS * NA_DH
NA_KH, NA_KW = 8, 16
CONV_WIDTH, CONV_K = 256, 31
N_EXPERTS, N_GROUPS, EXPERTS_PER_GROUP = 16, 4, 4
PAIRS = ((0, 1), (0, 2), (0, 3), (1, 2), (1, 3), (2, 3))
N_CLASSES = N_GROUPS * len(PAIRS)

LANE = 128
QK_PAD = 256
C_Q, C_QS, C_K, C_KS = 0, 256, 512, 768
C_V, C_GATE, C_A = 1024, 1408, 1792
C_NQ, C_NK, C_NV, C_CONV = 1920, 2304, 2688, 3072
IN_COLS_PAD = 3584

TOK_TILE = 512
GLA_CHUNK = 64
GLA_SUB = 16
MOE_BLK = 256
ROUTE_TILE = 512
NEG_BIG = -1e30
VMEM_LIMIT = 48 * 1024 * 1024

F32 = jnp.float32
BF16 = jnp.bfloat16


def _cparams(sem):
    return pltpu.CompilerParams(dimension_semantics=sem, vmem_limit_bytes=VMEM_LIMIT)


def _dot(a, b):
    return jnp.dot(a, b, preferred_element_type=F32)


def _dot_nt(a, b):
    return lax.dot_general(a, b, (((1,), (1,)), ((), ())), preferred_element_type=F32)


def _split(a):
    hi = a.astype(BF16)
    lo = (a - hi.astype(F32)).astype(BF16)
    return hi, lo


def _sigmoid(x):
    return 1.0 / (1.0 + jnp.exp(-x))


def _mod_kernel(c_ref, w_ref, b_ref, o_ref):
    cv = c_ref[...]
    s = cv * _sigmoid(cv)
    s_hi, s_lo = _split(s)
    w_hi, w_lo = _split(w_ref[...])
    o_ref[...] = _dot(s_hi, w_hi) + _dot(s_lo, w_hi) + _dot(s_hi, w_lo) + b_ref[...]


def _modulation(c_pad, w_mod, b_mod):
    depth, d, six_d = w_mod.shape
    rows = c_pad.shape[0]
    nt = 1536
    return pl.pallas_call(
        _mod_kernel,
        grid=(depth, six_d // nt),
        in_specs=[
            pl.BlockSpec((rows, d), lambda l, j: (0, 0)),
            pl.BlockSpec((None, d, nt), lambda l, j: (l, 0, j)),
            pl.BlockSpec((None, 1, nt), lambda l, j: (l, 0, j)),
        ],
        out_specs=pl.BlockSpec((None, rows, nt), lambda l, j: (l, 0, j)),
        out_shape=jax.ShapeDtypeStruct((depth, rows, six_d), F32),
        compiler_params=_cparams(("arbitrary", "arbitrary")),
        name="modulation",
    )(c_pad, w_mod, b_mod.reshape(depth, 1, six_d))


def _inproj_kernel(*refs, has_res):
    if has_res:
        x_ref, y_ref, g5_ref = refs[:3]
        refs = refs[3:]
    else:
        x_ref = refs[0]
        refs = refs[1:]
    (sh_ref, sc_ref, gmix_ref, cos_ref, sin_ref, w_ref, wa_ref, ba_ref) = refs[:8]
    outs = refs[8:]
    if has_res:
        xnew_ref = outs[0]
        outs = outs[1:]
    (q_ref, k_ref, v_ref, sg_ref, gf_ref, gb_ref, nq_ref, nk_ref, nv_ref, hcv_ref) = outs

    x = x_ref[...]
    if has_res:
        x = x + g5_ref[...] * y_ref[...].astype(F32)
        xnew_ref[...] = x
    ms = jnp.mean(x * x, axis=-1, keepdims=True)
    h = x * lax.rsqrt(ms + EPS) * gmix_ref[...]
    h = h * (1.0 + sc_ref[...]) + sh_ref[...]
    hb = h.astype(BF16)

    def proj(c0, n):
        return _dot(hb, w_ref[:, c0:c0 + n])

    cos = cos_ref[...]
    sin = sin_ref[...]
    q = (proj(C_Q, QK_PAD) * cos + proj(C_QS, QK_PAD) * sin) * (GLA_DK ** -0.5)
    q_ref[...] = q.astype(BF16)
    k = proj(C_K, QK_PAD) * cos + proj(C_KS, QK_PAD) * sin
    k_ref[...] = k.astype(BF16)
    v_ref[...] = proj(C_V, GLA_WIDTH).astype(BF16)
    gate = proj(C_GATE, GLA_WIDTH)
    sg_ref[...] = (gate * _sigmoid(gate)).astype(BF16)

    a_hi, a_lo = _split(proj(C_A, LANE))
    wa_hi, wa_lo = _split(wa_ref[...])
    z = _dot(a_hi, wa_hi) + _dot(a_lo, wa_hi) + _dot(a_hi, wa_lo) + ba_ref[...]
    logsig = jnp.minimum(z, 0.0) - jnp.log(1.0 + jnp.exp(-jnp.abs(z)))
    g = logsig * (1.0 / GLA_TAU)
    gf_ref[...] = g[:, :QK_PAD]
    gb_ref[...] = g[:, QK_PAD:]

    nq_ref[...] = (proj(C_NQ, NA_WIDTH) * (NA_DH ** -0.5)).astype(BF16)
    nk_ref[...] = proj(C_NK, NA_WIDTH).astype(BF16)
    nv_ref[...] = proj(C_NV, NA_WIDTH).astype(BF16)

    u = proj(C_CONV, 2 * CONV_WIDTH)
    hcv_ref[...] = u[:, :CONV_WIDTH] * _sigmoid(u[:, CONV_WIDTH:])


def _inproj(x_all, y, mod5, layer, gmix, cos_t, sin_t, w_aug, wa_aug, ba_aug, *, n_lat_tiles, tiles_per_seq,
            n_batch, res_layer):
    t_all, d = x_all.shape
    nt = t_all // TOK_TILE
    has_res = y is not None

    def bidx(i):
        return jnp.where(i < n_lat_tiles, i // tiles_per_seq, n_batch)

    def ridx(i):
        return jnp.where(i < n_lat_tiles, i % tiles_per_seq, tiles_per_seq)

    def modspec(l, j):
        return pl.BlockSpec((None, None, None, 1, d), lambda i: (l, bidx(i), j, 0, 0))

    tok = lambda w: pl.BlockSpec((TOK_TILE, w), lambda i: (i, 0))
    full = lambda a: pl.BlockSpec(a.shape, lambda i: (0,) * a.ndim)

    in_specs = [tok(d)]
    args = [x_all]
    if has_res:
        in_specs += [tok(d), modspec(res_layer, 5)]
        args += [y, mod5]
    in_specs += [modspec(layer, 0), modspec(layer, 1), full(gmix),
                 pl.BlockSpec((TOK_TILE, QK_PAD), lambda i: (ridx(i), 0)),
                 pl.BlockSpec((TOK_TILE, QK_PAD), lambda i: (ridx(i), 0)),
                 full(w_aug), full(wa_aug), full(ba_aug)]
    args += [mod5, mod5, gmix, cos_t, sin_t, w_aug, wa_aug, ba_aug]

    out_widths = [(QK_PAD, BF16), (QK_PAD, BF16), (GLA_WIDTH, BF16), (GLA_WIDTH, BF16), (QK_PAD, F32), (QK_PAD, F32),
                  (NA_WIDTH, BF16), (NA_WIDTH, BF16), (NA_WIDTH, BF16), (CONV_WIDTH, F32)]
    if has_res:
        out_widths = [(d, F32)] + out_widths
    out_specs = [tok(w) for w, _ in out_widths]
    out_shape = [jax.ShapeDtypeStruct((t_all, w), dt) for w, dt in out_widths]
    res = pl.pallas_call(
        functools.partial(_inproj_kernel, has_res=has_res),
        grid=(nt,),
        in_specs=in_specs,
        out_specs=out_specs,
        out_shape=out_shape,
        compiler_params=_cparams(("parallel",)),
        name="inproj",
    )(*args)
    if has_res:
        return res[0], res[1:]
    return x_all, res


def _gla_masks():
    c, sub = GLA_CHUNK, GLA_SUB
    lane_qk = lax.broadcasted_iota(jnp.int32, (1, QK_PAD), 1)
    head_qk = ((lane_qk >= GLA_DK).astype(jnp.int32) + (lane_qk >= 2 * GLA_DK).astype(jnp.int32)
               + (lane_qk >= 3 * GLA_DK).astype(jnp.int32) + 4 * (lane_qk >= 4 * GLA_DK).astype(jnp.int32))
    row_h = jnp.right_shift(lax.broadcasted_iota(jnp.int32, (c, 1), 0), GLA_SUB.bit_length() - 1)
    hm = (row_h == head_qk).astype(F32)
    row_v = lax.broadcasted_iota(jnp.int32, (GLA_WIDTH, 1), 0)
    head_v = ((row_v >= GLA_DV).astype(jnp.int32) + (row_v >= 2 * GLA_DV).astype(jnp.int32)
              + (row_v >= 3 * GLA_DV).astype(jnp.int32))
    bd = (head_v == head_qk).astype(F32)
    lane_v = lax.broadcasted_iota(jnp.int32, (1, GLA_WIDTH), 1)
    vm = [((lane_v >= h * GLA_DV) & (lane_v < (h + 1) * GLA_DV)).astype(F32) for h in range(GLA_HEADS)]
    return hm, bd, vm


def _gla_chunk(q, k, v, g, s_t, fwd, hm, bd, vm):
    c, sub = GLA_CHUNK, GLA_SUB
    nsub = c // sub
    ri = lax.broadcasted_iota(jnp.int32, (c, c), 0)
    ci = lax.broadcasted_iota(jnp.int32, (c, c), 1)
    tri = ((ci <= ri) if fwd else (ci >= ri)).astype(BF16)
    g_hi, g_lo = _split(g)
    cum = _dot(tri, g_hi) + _dot(tri, g_lo)
    tot = cum[c - 1:c] if fwd else cum[0:1]

    qe = (q * jnp.exp(cum)).astype(BF16)
    o_inter = _dot_nt(qe, s_t.astype(BF16))
    k_end = (k * jnp.exp(tot - cum)).astype(BF16)
    kv_t = lax.dot_general(v, k_end, (((0,), (0,)), ((), ())), preferred_element_type=F32)
    s_new = s_t * jnp.exp(tot) + bd * kv_t

    key_row = lax.broadcasted_iota(jnp.int32, (c, 1), 0)
    att_row = jnp.bitwise_and(lax.broadcasted_iota(jnp.int32, (c, c), 0), sub - 1)
    att_col = lax.broadcasted_iota(jnp.int32, (c, c), 1)
    outs = []
    for i in range(nsub):
        lo, hi = i * sub, (i + 1) * sub
        if fwd:
            ref = cum[lo - 1:lo] if i > 0 else jnp.zeros((1, QK_PAD), F32)
            key_ok = key_row < hi
            causal = att_col <= att_row + lo
        else:
            ref = cum[hi:hi + 1] if i < nsub - 1 else jnp.zeros((1, QK_PAD), F32)
            key_ok = key_row >= lo
            causal = att_col >= att_row + lo
        qi = q[lo:hi] * jnp.exp(cum[lo:hi] - ref)
        qs = (jnp.concatenate([qi] * GLA_HEADS, axis=0) * hm).astype(BF16)
        ki = (k * jnp.exp(jnp.where(key_ok, ref - cum, NEG_BIG))).astype(BF16)
        att = _dot_nt(qs, ki)
        att = jnp.where(causal, att, 0.0).astype(BF16)
        r = _dot(att, v)
        oi = r[0:sub] * vm[0]
        for h in range(1, GLA_HEADS):
            oi = oi + r[h * sub:(h + 1) * sub] * vm[h]
        outs.append(oi)
    o = o_inter + jnp.concatenate(outs, axis=0)
    return o, s_new


def _gla_kernel(q_ref, k_ref, v_ref, gf_ref, gb_ref, sf0_ref, sb0_ref, o_ref, sf_ref, sb_ref):
    n = q_ref.shape[0]
    nc = n // GLA_CHUNK
    hm, bd, vm = _gla_masks()
    o_ref[...] = jnp.zeros_like(o_ref)
    sf_ref[...] = sf0_ref[...]
    sb_ref[...] = sb0_ref[...]

    def body(j, carry):
        for fwd in (True, False):
            cidx = j if fwd else nc - 1 - j
            rows = pl.ds(pl.multiple_of(cidx * GLA_CHUNK, GLA_CHUNK), GLA_CHUNK)
            s_ref = sf_ref if fwd else sb_ref
            g_ref = gf_ref if fwd else gb_ref
            o, s_new = _gla_chunk(q_ref[rows, :].astype(F32), k_ref[rows, :].astype(F32), v_ref[rows, :],
                                  g_ref[rows, :], s_ref[...], fwd, hm, bd, vm)
            s_ref[...] = s_new
            o_ref[rows, :] += o
        return carry

    lax.fori_loop(0, nc, body, 0)


def _gla(q, k, v, gf, gb, sf0, sb0, o_prev, *, seq, blk0, n_batch):
    t_all = q.shape[0]
    tokw = lambda w: pl.BlockSpec((seq, w), lambda b: (blk0 + b, 0))
    st = pl.BlockSpec((None, GLA_WIDTH, QK_PAD), lambda b: (b, 0, 0))
    in_specs = [tokw(QK_PAD), tokw(QK_PAD), tokw(GLA_WIDTH), tokw(QK_PAD), tokw(QK_PAD), st, st]
    args = [q, k, v, gf, gb, sf0, sb0]
    aliases = {}
    if o_prev is not None:
        in_specs.append(pl.BlockSpec(memory_space=pl.ANY))
        args.append(o_prev)
        aliases = {7: 0}

    def kern(*refs):
        if o_prev is not None:
            refs = refs[:7] + refs[8:]
        _gla_kernel(*refs)

    st_shape = jax.ShapeDtypeStruct((n_batch, GLA_WIDTH, QK_PAD), F32)
    return pl.pallas_call(
        kern,
        grid=(n_batch,),
        in_specs=in_specs,
        out_specs=[tokw(GLA_WIDTH), st, st],
        out_shape=[jax.ShapeDtypeStruct((t_all, GLA_WIDTH), F32), st_shape, st_shape],
        input_output_aliases=aliases,
        compiler_params=_cparams(("parallel",)),
        name="gla",
    )(*args)


def _na_kernel(q_ref, k_ref, v_ref, kc_ref, vc_ref, bias_ref, o_ref):
    n = q_ref.shape[0]
    rows = n // GRID_W
    nkeys = NA_KH * GRID_W
    lane = lax.broadcasted_iota(jnp.int32, (1, LANE), 1)
    first = lane < NA_DH
    kc = kc_ref[...]
    vc = vc_ref[...]

    def body(r, carry):
        r0 = jnp.clip(r - NA_KH // 2, 0, rows - NA_KH)
        var = r - r0
        qrows = pl.ds(pl.multiple_of(r * GRID_W, GRID_W), GRID_W)
        krows = pl.ds(pl.multiple_of(r0 * GRID_W, GRID_W), nkeys)
        qr = q_ref[qrows, :]
        kb = k_ref[krows, :]
        vb = v_ref[krows, :]
        res = []
        for h in range(2):
            sel = first if h == 0 else jnp.logical_not(first)
            qh = jnp.where(sel, qr, jnp.zeros_like(qr))
            s_loc = _dot_nt(qh, kb) + bias_ref[h, var]
            s_ctx = _dot_nt(qh, kc)
            m = jnp.maximum(jnp.max(s_loc, axis=-1, keepdims=True), jnp.max(s_ctx, axis=-1, keepdims=True))
            p_loc = jnp.exp(s_loc - m)
            p_ctx = jnp.exp(s_ctx - m)
            l = jnp.sum(p_loc, axis=-1, keepdims=True) + jnp.sum(p_ctx, axis=-1, keepdims=True)
            o = _dot(p_loc.astype(BF16), vb) + _dot(p_ctx.astype(BF16), vc)
            res.append(o / l)
        o_ref[qrows, :] = jnp.where(first, res[0], res[1]).astype(o_ref.dtype)
        return carry

    lax.fori_loop(0, rows, body, 0)


def _na_latent(nq, nk, nv, bias_tab, *, seq, ctx_len, n_batch, ctx_blk0):
    t_all = nq.shape[0]
    npair = NA_HEADS // 2
    lat = pl.BlockSpec((seq, LANE), lambda b, p: (b, p))
    ctx = pl.BlockSpec((ctx_len, LANE), lambda b, p: (ctx_blk0 + b, p))
    bias = pl.BlockSpec((2,) + bias_tab.shape[1:], lambda b, p: (p, 0, 0, 0))
    return pl.pallas_call(
        _na_kernel,
        grid=(n_batch, npair),
        in_specs=[lat, lat, lat, ctx, ctx, bias],
        out_specs=lat,
        out_shape=jax.ShapeDtypeStruct((t_all, NA_WIDTH), BF16),
        compiler_params=_cparams(("parallel", "arbitrary")),
        name="na_latent",
    )(nq, nk, nv, nk, nv, bias_tab)


def _na_ctx_kernel(q_ref, k_ref, v_ref, o_in_ref, o_ref):
    del o_in_ref
    lane = lax.broadcasted_iota(jnp.int32, (1, LANE), 1)
    first = lane < NA_DH
    q = q_ref[...]
    k = k_ref[...]
    v = v_ref[...]
    res = []
    for h in range(2):
        sel = first if h == 0 else jnp.logical_not(first)
        qh = jnp.where(sel, q, jnp.zeros_like(q))
        s = _dot_nt(qh, k)
        m = jnp.max(s, axis=-1, keepdims=True)
        p = jnp.exp(s - m)
        l = jnp.sum(p, axis=-1, keepdims=True)
        res.append(_dot(p.astype(BF16), v) / l)
    o_ref[...] = jnp.where(first, res[0], res[1]).astype(o_ref.dtype)


def _na_context(nq, nk, nv, o_prev, *, ctx_len, n_batch, ctx_blk0):
    npair = NA_HEADS // 2
    ctx = pl.BlockSpec((ctx_len, LANE), lambda b, p: (ctx_blk0 + b, p))
    return pl.pallas_call(
        _na_ctx_kernel,
        grid=(n_batch, npair),
        in_specs=[ctx, ctx, ctx, pl.BlockSpec(memory_space=pl.ANY)],
        out_specs=ctx,
        out_shape=jax.ShapeDtypeStruct(o_prev.shape, o_prev.dtype),
        input_output_aliases={3: 0},
        compiler_params=_cparams(("parallel", "arbitrary")),
        name="na_context",
    )(nq, nk, nv, o_prev)


CONV_HALO = 16
CONV_ROWS = 128


def _conv_kernel(*refs, has_prev):
    if has_prev:
        h_ref, w_ref, b_ref, lg_ref, lb_ref, _, o_ref, pad_ref = refs
    else:
        h_ref, w_ref, b_ref, lg_ref, lb_ref, o_ref, pad_ref = refs
    n = h_ref.shape[0]
    zeros = jnp.zeros((CONV_HALO, CONV_WIDTH), F32)
    pad_ref[0:CONV_HALO, :] = zeros
    pad_ref[CONV_HALO + n:CONV_HALO + n + CONV_HALO, :] = zeros
    pad_ref[CONV_HALO:CONV_HALO + n, :] = h_ref[...]
    w = w_ref[...]
    off = CONV_HALO - CONV_K // 2
    for cidx in range(n // CONV_ROWS):
        base = cidx * CONV_ROWS
        acc = jnp.zeros((CONV_ROWS, CONV_WIDTH), F32) + b_ref[...]
        for j in range(CONV_K):
            acc = acc + pad_ref[base + off + j:base + off + j + CONV_ROWS, :] * w[j:j + 1, :]
        mu = jnp.mean(acc, axis=-1, keepdims=True)
        xc = acc - mu
        var = jnp.mean(xc * xc, axis=-1, keepdims=True)
        y = xc * lax.rsqrt(var + EPS) * lg_ref[...] + lb_ref[...]
        o_ref[base:base + CONV_ROWS, :] = (y * _sigmoid(y)).astype(o_ref.dtype)


def _conv(hcv, w, b, lg, lb, o_prev, *, seq, blk0, n_batch):
    t_all = hcv.shape[0]
    tok = pl.BlockSpec((seq, CONV_WIDTH), lambda i: (blk0 + i, 0))
    full = lambda a: pl.BlockSpec(a.shape, lambda i: (0,) * a.ndim)
    in_specs = [tok, full(w), full(b), full(lg), full(lb)]
    args = [hcv, w, b, lg, lb]
    aliases = {}
    if o_prev is not None:
        in_specs.append(pl.BlockSpec(memory_space=pl.ANY))
        args.append(o_prev)
        aliases = {5: 0}
    return pl.pallas_call(
        functools.partial(_conv_kernel, has_prev=o_prev is not None),
        grid=(n_batch,),
        in_specs=in_specs,
        out_specs=tok,
        out_shape=jax.ShapeDtypeStruct((t_all, CONV_WIDTH), BF16),
        scratch_shapes=[pltpu.VMEM((seq + 2 * CONV_HALO, CONV_WIDTH), F32)],
        input_output_aliases=aliases,
        compiler_params=_cparams(("parallel",)),
        name="conv",
    )(*args)


def _outproj_kernel(og_ref, sg_ref, on_ref, oc_ref, x_ref, g2_ref, sh_ref, sc_ref, gn_ref, e_ref,
                    wo_ref, gffn_ref, rwh_ref, rwl_ref, xmid_ref, h2_ref, lg_ref):
    of = og_ref[...]
    sq_hi, sq_lo = _split(of * of)
    e = e_ref[...]
    ms = (_dot(sq_hi, e) + _dot(sq_lo, e)) * (1.0 / GLA_DV)
    og = of * lax.rsqrt(ms + EPS) * gn_ref[...] * sg_ref[...].astype(F32)
    y = _dot(og.astype(BF16), wo_ref[0:GLA_WIDTH, :])
    y = y + _dot(on_ref[...], wo_ref[GLA_WIDTH:GLA_WIDTH + NA_WIDTH, :])
    y = y + _dot(oc_ref[...], wo_ref[GLA_WIDTH + NA_WIDTH:, :])
    x = x_ref[...] + g2_ref[...] * y
    xmid_ref[...] = x
    ms2 = jnp.mean(x * x, axis=-1, keepdims=True)
    h2 = x * lax.rsqrt(ms2 + EPS) * gffn_ref[...]
    h2 = h2 * (1.0 + sc_ref[...]) + sh_ref[...]
    h_hi, h_lo = _split(h2)
    h2_ref[...] = h_hi
    rwh = rwh_ref[...]
    lg_ref[...] = _dot(h_hi, rwh) + _dot(h_lo, rwh) + _dot(h_hi, rwl_ref[...])


def _outproj(o_g, sgate, o_n, o_c, x_all, mod5, layer, gnorm, e_mat, w_out, gffn, rw_hi, rw_lo, *, n_tiles,
             n_lat_tiles, tiles_per_seq, n_batch):
    d = x_all.shape[1]
    t_out = n_tiles * TOK_TILE

    def bidx(i):
        return jnp.where(i < n_lat_tiles, i // tiles_per_seq, n_batch)

    def modspec(j):
        return pl.BlockSpec((None, None, None, 1, d), lambda i: (layer, bidx(i), j, 0, 0))

    tok = lambda w: pl.BlockSpec((TOK_TILE, w), lambda i: (i, 0))
    full = lambda a: pl.BlockSpec(a.shape, lambda i: (0,) * a.ndim)
    return pl.pallas_call(
        _outproj_kernel,
        grid=(n_tiles,),
        in_specs=[tok(GLA_WIDTH), tok(GLA_WIDTH), tok(NA_WIDTH), tok(CONV_WIDTH), tok(d),
                  modspec(2), modspec(3), modspec(4), full(gnorm), full(e_mat), full(w_out), full(gffn),
                  full(rw_hi), full(rw_lo)],
        out_specs=[tok(d), tok(d), tok(LANE)],
        out_shape=[jax.ShapeDtypeStruct((t_out, d), F32), jax.ShapeDtypeStruct((t_out, d), BF16),
                   jax.ShapeDtypeStruct((t_out, LANE), F32)],
        compiler_params=_cparams(("parallel",)),
        name="outproj",
    )(o_g, sgate, o_n, o_c, x_all, mod5, mod5, mod5, gnorm, e_mat, w_out, gffn, rw_hi, rw_lo)


def _route_kernel(lg_ref, bias_ref, meta_ref, cnt_ref, carry_ref):
    tile = lg_ref.shape[0]

    @pl.when(pl.program_id(0) == 0)
    def _():
        carry_ref[...] = jnp.zeros_like(carry_ref)

    lt = lg_ref[...].T
    aff = _sigmoid(lt[0:N_EXPERTS])
    sel = aff + bias_ref[...]
    s = [sel[e:e + 1] for e in range(N_EXPERTS)]
    a = [aff[e:e + 1] for e in range(N_EXPERTS)]

    def top2sum(v):
        best = v[0] + v[1]
        for i, j in PAIRS[1:]:
            best = jnp.maximum(best, v[i] + v[j])
        return best

    gs = [top2sum(s[4 * g:4 * g + 4]) for g in range(N_GROUPS)]
    gbest = jnp.zeros_like(gs[0], dtype=jnp.int32)
    gmax = gs[0]
    for g in range(1, N_GROUPS):
        upd = gs[g] > gmax
        gbest = jnp.where(upd, g, gbest)
        gmax = jnp.where(upd, gs[g], gmax)

    def pick(vals, j):
        out = vals[j]
        for g in range(1, N_GROUPS):
            out = jnp.where(gbest == g, vals[4 * g + j], out)
        return out

    sv = [pick(s, j) for j in range(EXPERTS_PER_GROUP)]
    av = [pick(a, j) for j in range(EXPERTS_PER_GROUP)]
    i1 = jnp.zeros_like(gbest)
    m1 = sv[0]
    for j in range(1, EXPERTS_PER_GROUP):
        upd = sv[j] > m1
        i1 = jnp.where(upd, j, i1)
        m1 = jnp.where(upd, sv[j], m1)
    i2 = jnp.full_like(gbest, -1)
    m2 = jnp.zeros_like(m1)
    for j in range(EXPERTS_PER_GROUP):
        upd = (i1 != j) & ((sv[j] > m2) | (i2 < 0))
        i2 = jnp.where(upd, j, i2)
        m2 = jnp.where(upd, sv[j], m2)
    ia = jnp.minimum(i1, i2)
    ib = jnp.maximum(i1, i2)
    pair = jnp.where(ia == 0, ib - 1, jnp.where(ia == 1, ib + 1, 5))
    cls = gbest * len(PAIRS) + pair

    def take(vals, idx):
        out = vals[0]
        for j in range(1, EXPERTS_PER_GROUP):
            out = jnp.where(idx == j, vals[j], out)
        return out

    w1 = take(av, i1)
    w2 = take(av, i2)
    tot = w1 + w2
    wa = jnp.where(i1 < i2, w1, w2) / tot
    wb = jnp.where(i1 < i2, w2, w1) / tot

    crow = lax.broadcasted_iota(jnp.int32, (32, tile), 0)
    oh = (crow == cls).astype(F32)
    us = lax.broadcasted_iota(jnp.int32, (tile, tile), 0)
    ut = lax.broadcasted_iota(jnp.int32, (tile, tile), 1)
    upper = (us < ut).astype(BF16)
    prefix = _dot(oh.astype(BF16), upper)
    carry = carry_ref[...]
    rank = jnp.sum(oh * (prefix + carry), axis=0, keepdims=True)
    carry_new = carry + jnp.sum(oh, axis=1, keepdims=True)
    carry_ref[...] = carry_new
    cnt_ref[...] = jnp.broadcast_to(carry_new, cnt_ref.shape)

    meta_ref[...] = jnp.zeros_like(meta_ref)
    meta_ref[0:1, :] = cls.astype(F32)
    meta_ref[1:2, :] = rank
    meta_ref[2:3, :] = wa
    meta_ref[3:4, :] = wb


def _route(logits, bias_col):
    t = logits.shape[0]
    nt = t // ROUTE_TILE
    return pl.pallas_call(
        _route_kernel,
        grid=(nt,),
        in_specs=[pl.BlockSpec((ROUTE_TILE, LANE), lambda i: (i, 0)),
                  pl.BlockSpec(bias_col.shape, lambda i: (0, 0))],
        out_specs=[pl.BlockSpec((8, ROUTE_TILE), lambda i: (0, i)),
                   pl.BlockSpec((32, LANE), lambda i: (0, 0))],
        out_shape=[jax.ShapeDtypeStruct((8, t), F32), jax.ShapeDtypeStruct((32, LANE), F32)],
        scratch_shapes=[pltpu.VMEM((32, 1), F32)],
        compiler_params=_cparams(("arbitrary",)),
        name="route",
    )(logits, bias_col)


FF_TILE = 512


def _expert_kernel(ea_ref, eb_ref, valid_ref, xs_ref, ws_ref, wga_ref, wua_ref, wda_ref, wgb_ref, wub_ref,
                   wdb_ref, y_ref):
    del ea_ref, eb_ref
    j = pl.program_id(0)

    @pl.when(valid_ref[j] == 0)
    def _():
        y_ref[...] = jnp.zeros_like(y_ref)

    @pl.when(valid_ref[j] != 0)
    def _():
        x = xs_ref[...]
        ws = ws_ref[...]
        ff = wga_ref.shape[1]

        def ffn(wg_ref, wu_ref, wd_ref):
            acc = None
            for f0 in range(0, ff, FF_TILE):
                hg = _dot(x, wg_ref[:, f0:f0 + FF_TILE])
                hu = _dot(x, wu_ref[:, f0:f0 + FF_TILE])
                hh = (hg * _sigmoid(hg) * hu).astype(BF16)
                part = _dot(hh, wd_ref[f0:f0 + FF_TILE, :])
                acc = part if acc is None else acc + part
            return acc

        ya = ffn(wga_ref, wua_ref, wda_ref)
        yb = ffn(wgb_ref, wub_ref, wdb_ref)
        y_ref[...] = (ya * ws[:, 0:1] + yb * ws[:, 1:2]).astype(y_ref.dtype)


def _experts(xs, ws, ea, eb, valid, wg, wu, wd):
    p_rows, d = xs.shape
    nb = p_rows // MOE_BLK
    ff = wg.shape[2]
    wspec_in = lambda which: pl.BlockSpec((None, d, ff), lambda j, ea, eb, v: ((ea, eb)[which][j], 0, 0))
    wspec_out = lambda which: pl.BlockSpec((None, ff, d), lambda j, ea, eb, v: ((ea, eb)[which][j], 0, 0))
    grid_spec = pltpu.PrefetchScalarGridSpec(
        num_scalar_prefetch=3,
        grid=(nb,),
        in_specs=[pl.BlockSpec((MOE_BLK, d), lambda j, ea, eb, v: (j, 0)),
                  pl.BlockSpec((MOE_BLK, 2), lambda j, ea, eb, v: (j, 0)),
                  wspec_in(0), wspec_in(0), wspec_out(0), wspec_in(1), wspec_in(1), wspec_out(1)],
        out_specs=pl.BlockSpec((MOE_BLK, d), lambda j, ea, eb, v: (j, 0)),
    )
    return pl.pallas_call(
        _expert_kernel,
        grid_spec=grid_spec,
        out_shape=jax.ShapeDtypeStruct((p_rows, d), BF16),
        compiler_params=_cparams(("arbitrary",)),
        name="experts",
    )(ea, eb, valid, xs, ws, wg, wu, wd, wg, wu, wd)


def _moe(h2, logits, bias_col, wg, wu, wd):
    t, d = h2.shape
    meta, cnt = _route(logits, bias_col)
    cls = meta[0].astype(jnp.int32)
    rank = meta[1].astype(jnp.int32)
    counts = cnt[:N_CLASSES, 0].astype(jnp.int32)
    padded = (counts + MOE_BLK - 1) // MOE_BLK * MOE_BLK
    pad_end = jnp.cumsum(padded)
    pad_start = pad_end - padded
    dest = pad_start[cls] + rank
    nb = t // MOE_BLK + N_CLASSES
    p_rows = nb * MOE_BLK
    blk_start = jnp.arange(nb, dtype=jnp.int32) * MOE_BLK
    valid = (blk_start < pad_end[-1]).astype(jnp.int32)
    blk_cls = jnp.sum((pad_end[None, :] <= blk_start[:, None]).astype(jnp.int32), axis=-1)
    last_cls = jnp.sum((pad_end <= pad_end[-1] - 1).astype(jnp.int32))
    blk_cls = jnp.where(valid > 0, blk_cls, last_cls)
    blk_cls = jnp.minimum(blk_cls, N_CLASSES - 1)
    pair_a = jnp.array([p[0] for p in PAIRS], jnp.int32)
    pair_b = jnp.array([p[1] for p in PAIRS], jnp.int32)
    grp = blk_cls // len(PAIRS)
    ea = grp * EXPERTS_PER_GROUP + pair_a[blk_cls % len(PAIRS)]
    eb = grp * EXPERTS_PER_GROUP + pair_b[blk_cls % len(PAIRS)]

    row_tok = jnp.full((p_rows,), t, jnp.int32).at[dest].set(jnp.arange(t, dtype=jnp.int32))
    xs = jnp.concatenate([h2, jnp.zeros((1, d), h2.dtype)], axis=0)[row_tok]
    ws = jnp.zeros((p_rows, 2), F32).at[dest].set(jnp.stack([meta[2], meta[3]], axis=1))
    ys = _experts(xs, ws, ea, eb, valid, wg, wu, wd)
    return ys[dest]


def _final_kernel(x_ref, y_ref, g5_ref, gf_ref, o_ref):
    x = x_ref[...] + g5_ref[...] * y_ref[...].astype(F32)
    ms = jnp.mean(x * x, axis=-1, keepdims=True)
    o_ref[...] = x * lax.rsqrt(ms + EPS) * gf_ref[...]


def _final(x_mid, y, mod5, layer, g_final, *, tiles_per_seq):
    t, d = x_mid.shape
    tok = pl.BlockSpec((TOK_TILE, d), lambda i: (i, 0))
    return pl.pallas_call(
        _final_kernel,
        grid=(t // TOK_TILE,),
        in_specs=[tok, tok,
                  pl.BlockSpec((None, None, None, 1, d), lambda i: (layer, i // tiles_per_seq, 5, 0, 0)),
                  pl.BlockSpec(g_final.shape, lambda i: (0, 0))],
        out_specs=tok,
        out_shape=jax.ShapeDtypeStruct((t, d), F32),
        compiler_params=_cparams(("parallel",)),
        name="final_norm",
    )(x_mid, y, mod5, g_final)


def _rope_tables(seq):
    t = jnp.arange(seq)
    row = (t // GRID_W).astype(F32)
    col = (t % GRID_W).astype(F32)
    half = GLA_DK // 2
    inv = ROPE_BASE ** (-jnp.arange(0, half, 2, dtype=F32) / half)
    ang = jnp.concatenate([row[:, None] * inv, col[:, None] * inv], axis=-1)
    cos = jnp.repeat(jnp.cos(ang), 2, axis=-1)
    sin = jnp.repeat(jnp.sin(ang), 2, axis=-1)
    cos = jnp.tile(cos, (1, GLA_HEADS))
    sin = jnp.tile(sin, (1, GLA_HEADS))
    padw = QK_PAD - GLA_QK
    cos = jnp.pad(cos, ((0, 0), (0, padw)), constant_values=1.0)
    sin = jnp.pad(sin, ((0, 0), (0, padw)))
    cos = jnp.concatenate([cos, jnp.ones((TOK_TILE, QK_PAD), F32)], axis=0)
    sin = jnp.concatenate([sin, jnp.zeros((TOK_TILE, QK_PAD), F32)], axis=0)
    return cos, sin


def _pair_swap(w):
    w2 = w.reshape(w.shape[0], -1, 2)
    return jnp.stack([-w2[..., 1], w2[..., 0]], axis=-1).reshape(w.shape)


def _in_weights(w_in_l):
    d = w_in_l.shape[0]
    offs = np.cumsum([0, GLA_QK, GLA_QK, GLA_WIDTH, GLA_WIDTH, 2 * GLA_LOWRANK, NA_WIDTH, NA_WIDTH, NA_WIDTH,
                      2 * CONV_WIDTH])
    seg = [w_in_l[:, offs[i]:offs[i + 1]] for i in range(9)]
    padc = lambda w, n: jnp.pad(w, ((0, 0), (0, n - w.shape[1])))
    cols = [padc(seg[0], QK_PAD), padc(_pair_swap(seg[0]), QK_PAD), padc(seg[1], QK_PAD),
            padc(_pair_swap(seg[1]), QK_PAD), seg[2], seg[3], padc(seg[4], LANE), seg[5], seg[6], seg[7], seg[8]]
    w = jnp.concatenate(cols, axis=1).astype(BF16)
    assert w.shape == (d, IN_COLS_PAD)
    return w


def _gate_weights(wa_f, ba_f, wa_b, ba_b):
    wa = jnp.zeros((LANE, 2 * QK_PAD), F32)
    wa = wa.at[0:GLA_LOWRANK, 0:GLA_QK].set(wa_f)
    wa = wa.at[GLA_LOWRANK:2 * GLA_LOWRANK, QK_PAD:QK_PAD + GLA_QK].set(wa_b)
    ba = jnp.zeros((1, 2 * QK_PAD), F32)
    ba = ba.at[0, 0:GLA_QK].set(ba_f)
    ba = ba.at[0, QK_PAD:QK_PAD + GLA_QK].set(ba_b)
    return wa, ba


def _na_bias_tables(rpb):
    cq = np.arange(GRID_W)
    c0 = np.clip(cq - NA_KW // 2, 0, GRID_W - NA_KW)
    kc = np.arange(GRID_W)
    valid = (kc[None, :] >= c0[:, None]) & (kc[None, :] < c0[:, None] + NA_KW)
    dcol = np.clip(kc[None, :] - cq[:, None] + (NA_KW - 1), 0, 2 * NA_KW - 2)
    var = np.arange(NA_KH)
    drow = np.arange(NA_KH)[None, :] - var[:, None] + (NA_KH - 1)
    tab = rpb[:, drow][:, :, :, dcol]
    tab = jnp.where(valid[None, None, None], tab, NEG_BIG)
    tab = jnp.transpose(tab, (0, 1, 3, 2, 4))
    return tab.reshape(rpb.shape[0], NA_KH, GRID_W, NA_KH * GRID_W).astype(F32)


def _head_mean_matrix():
    h = np.arange(GLA_WIDTH) // GLA_DV
    return jnp.asarray((h[:, None] == h[None, :]).astype(np.float32), dtype=BF16)


def kernel(x, c, ctx, c_ctx, w_mod, b_mod, g_mix, g_ffn, w_in, gla_wa_f, gla_ba_f, gla_wa_b, gla_ba_b, gla_g_norm,
           na_rpb, conv_w, conv_b, conv_ln_g, conv_ln_b, w_out, router_w, router_bias, w_gate, w_up, w_down,
           g_final):
    bsz, seq, d = x.shape
    ctx_len = ctx.shape[1]
    depth = w_mod.shape[0]
    t_lat, t_ctx = bsz * seq, bsz * ctx_len
    assert seq % TOK_TILE == 0 and t_ctx % TOK_TILE == 0 and seq % ctx_len == 0
    assert seq // GRID_W >= NA_KH and ctx_len % GLA_CHUNK == 0
    tiles_per_seq = seq // TOK_TILE
    n_lat_tiles = t_lat // TOK_TILE
    n_all_tiles = (t_lat + t_ctx) // TOK_TILE
    ctx_blk0 = t_lat // ctx_len

    mod_rows = -(-(bsz + 1) // 8) * 8
    c_pad = jnp.zeros((mod_rows, d), F32).at[:bsz].set(c).at[bsz].set(c_ctx)
    mod = _modulation(c_pad, w_mod, b_mod)
    mod5 = mod.reshape(depth, mod_rows, 6, 1, d)

    cos_t, sin_t = _rope_tables(seq)
    e_mat = _head_mean_matrix()
    rw = jnp.pad(router_w, ((0, 0), (0, LANE - N_EXPERTS)))
    rw_hi = rw.astype(BF16)
    rw_lo = (rw - rw_hi.astype(F32)).astype(BF16)
    bias_col = router_bias.reshape(N_EXPERTS, 1).astype(F32)
    zero_state = jnp.zeros((bsz, GLA_WIDTH, QK_PAD), F32)

    x_all = jnp.concatenate([x.reshape(t_lat, d), ctx.reshape(t_ctx, d)], axis=0)
    y_moe = None
    for l in range(depth):
        last = l == depth - 1
        w_aug = _in_weights(w_in[l])
        wa_aug, ba_aug = _gate_weights(gla_wa_f[l], gla_ba_f[l], gla_wa_b[l], gla_ba_b[l])
        x_all, (q, k, v, sgate, gf, gb, nq, nk, nv, hcv) = _inproj(
            x_all, y_moe, mod5, l, g_mix[l].reshape(1, d), cos_t, sin_t, w_aug, wa_aug, ba_aug,
            n_lat_tiles=n_lat_tiles, tiles_per_seq=tiles_per_seq, n_batch=bsz, res_layer=l - 1)

        o_g, st_f, st_b = _gla(q, k, v, gf, gb, zero_state, zero_state, None, seq=ctx_len, blk0=ctx_blk0,
                               n_batch=bsz)
        o_g, _, _ = _gla(q, k, v, gf, gb, st_f, st_b, o_g, seq=seq, blk0=0, n_batch=bsz)

        bias_tab = _na_bias_tables(na_rpb[l])
        o_n = _na_latent(nq, nk, nv, bias_tab, seq=seq, ctx_len=ctx_len, n_batch=bsz, ctx_blk0=ctx_blk0)
        cw, cb = conv_w[l], conv_b[l].reshape(1, -1)
        clg, clb = conv_ln_g[l].reshape(1, -1), conv_ln_b[l].reshape(1, -1)
        o_c = _conv(hcv, cw, cb, clg, clb, None, seq=seq, blk0=0, n_batch=bsz)
        if not last:
            o_n = _na_context(nq, nk, nv, o_n, ctx_len=ctx_len, n_batch=bsz, ctx_blk0=ctx_blk0)
            o_c = _conv(hcv, cw, cb, clg, clb, o_c, seq=ctx_len, blk0=ctx_blk0, n_batch=bsz)

        n_tiles = n_lat_tiles if last else n_all_tiles
        x_mid, h2, logits = _outproj(
            o_g, sgate, o_n, o_c, x_all, mod5, l, gla_g_norm[l].reshape(1, -1), e_mat, w_out[l].astype(BF16),
            g_ffn[l].reshape(1, d), rw_hi, rw_lo, n_tiles=n_tiles, n_lat_tiles=n_lat_tiles,
            tiles_per_seq=tiles_per_seq, n_batch=bsz)
        y_moe = _moe(h2, logits, bias_col, w_gate[l].astype(BF16), w_up[l].astype(BF16), w_down[l].astype(BF16))
        x_all = x_mid

    out = _final(x_all, y_moe, mod5, depth - 1, g_final.reshape(1, d), tiles_per_seq=tiles_per_seq)
    return out.reshape(bsz, seq, d)
```

```python
import functools

import numpy as np
import jax
import jax.numpy as jnp
from jax import lax
from jax.experimental import pallas as pl
from jax.experimental.pallas import tpu as pltpu

GRID_W = 64
EPS = 1e-6
GLA_HEADS, GLA_DK, GLA_DV = 4, 48, 96
GLA_QK = GLA_HEADS * GLA_DK
GLA_WIDTH = GLA_HEADS * GLA_DV
GLA_LOWRANK = 16
GLA_TAU = 16.0
ROPE_BASE = 10000.0
NA_HEADS, NA_DH = 6, 64
NA_WIDTH = NA_HEADS * NA_DH
NA_KH, NA_KW = 8, 16
CONV_WIDTH, CONV_K = 256, 31
N_EXPERTS, N_GROUPS, EXPERTS_PER_GROUP = 16, 4, 4
PAIRS = ((0, 1), (0, 2), (0, 3), (1, 2), (1, 3), (2, 3))
N_CLASSES = N_GROUPS * len(PAIRS)

LANE = 128
QK_PAD = 256
C_Q, C_QS, C_K, C_KS = 0, 256, 512, 768
C_V, C_GATE, C_A = 1024, 1408, 1792
C_NQ, C_NK, C_NV, C_CONV = 1920, 2304, 2688, 3072
IN_COLS_PAD = 3584

TOK_TILE = 512
GLA_CHUNK = 64
GLA_SUB = 16
MOE_BLK = 256
ROUTE_TILE = 512
NA_UNROLL = 4
NEG_BIG = -1e30
VMEM_LIMIT = 48 * 1024 * 1024

F32 = jnp.float32
BF16 = jnp.bfloat16


def _cparams(sem):
    return pltpu.CompilerParams(dimension_semantics=sem, vmem_limit_bytes=VMEM_LIMIT)


def _dot(a, b):
    return jnp.dot(a, b, preferred_element_type=F32)


def _dot_nt(a, b):
    return lax.dot_general(a, b, (((1,), (1,)), ((), ())), preferred_element_type=F32)


def _split(a):
    hi = a.astype(BF16)
    lo = (a - hi.astype(F32)).astype(BF16)
    return hi, lo


def _sigmoid(x):
    return 1.0 / (1.0 + jnp.exp(-x))


def _mod_kernel(c_ref, w_ref, b_ref, o_ref):
    cv = c_ref[...]
    s = cv * _sigmoid(cv)
    s_hi, s_lo = _split(s)
    w_hi, w_lo = _split(w_ref[...])
    o_ref[...] = _dot(s_hi, w_hi) + _dot(s_lo, w_hi) + _dot(s_hi, w_lo) + b_ref[...]


def _modulation(c_pad, w_mod, b_mod):
    depth, d, six_d = w_mod.shape
    rows = c_pad.shape[0]
    nt = 1536
    return pl.pallas_call(
        _mod_kernel,
        grid=(depth, six_d // nt),
        in_specs=[
            pl.BlockSpec((rows, d), lambda l, j: (0, 0)),
            pl.BlockSpec((None, d, nt), lambda l, j: (l, 0, j)),
            pl.BlockSpec((None, 1, nt), lambda l, j: (l, 0, j)),
        ],
        out_specs=pl.BlockSpec((None, rows, nt), lambda l, j: (l, 0, j)),
        out_shape=jax.ShapeDtypeStruct((depth, rows, six_d), F32),
        compiler_params=_cparams(("arbitrary", "arbitrary")),
        name="modulation",
    )(c_pad, w_mod, b_mod.reshape(depth, 1, six_d))


def _inproj_kernel(*refs, has_res):
    if has_res:
        x_ref, y_ref, g5_ref = refs[:3]
        refs = refs[3:]
    else:
        x_ref = refs[0]
        refs = refs[1:]
    (sh_ref, sc_ref, gmix_ref, cos_ref, sin_ref, w_ref, wa_ref, ba_ref) = refs[:8]
    outs = refs[8:]
    if has_res:
        xnew_ref = outs[0]
        outs = outs[1:]
    (q_ref, k_ref, v_ref, sg_ref, gf_ref, gb_ref, nq_ref, nk_ref, nv_ref, hcv_ref) = outs

    x = x_ref[...]
    if has_res:
        x = x + g5_ref[...] * y_ref[...].astype(F32)
        xnew_ref[...] = x
    ms = jnp.mean(x * x, axis=-1, keepdims=True)
    h = x * lax.rsqrt(ms + EPS) * gmix_ref[...]
    h = h * (1.0 + sc_ref[...]) + sh_ref[...]
    hb = h.astype(BF16)

    def proj(c0, n):
        return _dot(hb, w_ref[:, c0:c0 + n])

    cos = cos_ref[...]
    sin = sin_ref[...]
    q = (proj(C_Q, QK_PAD) * cos + proj(C_QS, QK_PAD) * sin) * (GLA_DK ** -0.5)
    q_ref[...] = q.astype(BF16)
    k = proj(C_K, QK_PAD) * cos + proj(C_KS, QK_PAD) * sin
    k_ref[...] = k.astype(BF16)
    v_ref[...] = proj(C_V, GLA_WIDTH).astype(BF16)
    gate = proj(C_GATE, GLA_WIDTH)
    sg_ref[...] = (gate * _sigmoid(gate)).astype(BF16)

    a_hi, a_lo = _split(proj(C_A, LANE))
    wa_hi, wa_lo = _split(wa_ref[...])
    z = _dot(a_hi, wa_hi) + _dot(a_lo, wa_hi) + _dot(a_hi, wa_lo) + ba_ref[...]
    logsig = jnp.minimum(z, 0.0) - jnp.log(1.0 + jnp.exp(-jnp.abs(z)))
    g = logsig * (1.0 / GLA_TAU)
    gf_ref[...] = g[:, :QK_PAD]
    gb_ref[...] = g[:, QK_PAD:]

    nq_ref[...] = (proj(C_NQ, NA_WIDTH) * (NA_DH ** -0.5)).astype(BF16)
    nk_ref[...] = proj(C_NK, NA_WIDTH).astype(BF16)
    nv_ref[...] = proj(C_NV, NA_WIDTH).astype(BF16)

    u = proj(C_CONV, 2 * CONV_WIDTH)
    hcv_ref[...] = u[:, :CONV_WIDTH] * _sigmoid(u[:, CONV_WIDTH:])


def _inproj(x_all, y, mod5, layer, gmix, cos_t, sin_t, w_aug, wa_aug, ba_aug, *, n_lat_tiles, tiles_per_seq,
            n_batch, res_layer):
    t_all, d = x_all.shape
    nt = t_all // TOK_TILE
    has_res = y is not None

    def bidx(i):
        return jnp.where(i < n_lat_tiles, i // tiles_per_seq, n_batch)

    def ridx(i):
        return jnp.where(i < n_lat_tiles, i % tiles_per_seq, tiles_per_seq)

    def modspec(l, j):
        return pl.BlockSpec((None, None, None, 1, d), lambda i: (l, bidx(i), j, 0, 0))

    tok = lambda w: pl.BlockSpec((TOK_TILE, w), lambda i: (i, 0))
    full = lambda a: pl.BlockSpec(a.shape, lambda i: (0,) * a.ndim)

    in_specs = [tok(d)]
    args = [x_all]
    if has_res:
        in_specs += [tok(d), modspec(res_layer, 5)]
        args += [y, mod5]
    in_specs += [modspec(layer, 0), modspec(layer, 1), full(gmix),
                 pl.BlockSpec((TOK_TILE, QK_PAD), lambda i: (ridx(i), 0)),
                 pl.BlockSpec((TOK_TILE, QK_PAD), lambda i: (ridx(i), 0)),
                 full(w_aug), full(wa_aug), full(ba_aug)]
    args += [mod5, mod5, gmix, cos_t, sin_t, w_aug, wa_aug, ba_aug]

    out_widths = [(QK_PAD, BF16), (QK_PAD, BF16), (GLA_WIDTH, BF16), (GLA_WIDTH, BF16), (QK_PAD, F32), (QK_PAD, F32),
                  (NA_WIDTH, BF16), (NA_WIDTH, BF16), (NA_WIDTH, BF16), (CONV_WIDTH, F32)]
    if has_res:
        out_widths = [(d, F32)] + out_widths
    out_specs = [tok(w) for w, _ in out_widths]
    out_shape = [jax.ShapeDtypeStruct((t_all, w), dt) for w, dt in out_widths]
    res = pl.pallas_call(
        functools.partial(_inproj_kernel, has_res=has_res),
        grid=(nt,),
        in_specs=in_specs,
        out_specs=out_specs,
        out_shape=out_shape,
        compiler_params=_cparams(("parallel",)),
        name="inproj",
    )(*args)
    if has_res:
        return res[0], res[1:]
    return x_all, res


def _gla_masks():
    c, sub = GLA_CHUNK, GLA_SUB
    lane_qk = lax.broadcasted_iota(jnp.int32, (1, QK_PAD), 1)
    head_qk = ((lane_qk >= GLA_DK).astype(jnp.int32) + (lane_qk >= 2 * GLA_DK).astype(jnp.int32)
               + (lane_qk >= 3 * GLA_DK).astype(jnp.int32) + 4 * (lane_qk >= 4 * GLA_DK).astype(jnp.int32))
    row_h = jnp.right_shift(lax.broadcasted_iota(jnp.int32, (c, 1), 0), GLA_SUB.bit_length() - 1)
    hm = (row_h == head_qk).astype(F32)
    row_v = lax.broadcasted_iota(jnp.int32, (GLA_WIDTH, 1), 0)
    head_v = ((row_v >= GLA_DV).astype(jnp.int32) + (row_v >= 2 * GLA_DV).astype(jnp.int32)
              + (row_v >= 3 * GLA_DV).astype(jnp.int32))
    bd = (head_v == head_qk).astype(F32)
    lane_v = lax.broadcasted_iota(jnp.int32, (1, GLA_WIDTH), 1)
    vm = [((lane_v >= h * GLA_DV) & (lane_v < (h + 1) * GLA_DV)).astype(F32) for h in range(GLA_HEADS)]
    return hm, bd, vm


def _gla_pair(q, k, v, g, s_t, hm, bd, vm):
    c, sub = GLA_CHUNK, GLA_SUB
    nsub = c // sub
    dirs = (True, False)
    ri = lax.broadcasted_iota(jnp.int32, (c, c), 0)
    ci = lax.broadcasted_iota(jnp.int32, (c, c), 1)
    key_row = lax.broadcasted_iota(jnp.int32, (c, 1), 0)
    att_row = jnp.bitwise_and(ri, sub - 1)

    cums = []
    for d, fwd in enumerate(dirs):
        tri = ((ci <= ri) if fwd else (ci >= ri)).astype(BF16)
        g_hi, g_lo = _split(g[d])
        cums.append(_dot(tri, g_hi) + _dot(tri, g_lo))

    o_inter, s_new, atts = [], [], []
    for d, fwd in enumerate(dirs):
        cum = cums[d]
        tot = cum[c - 1:c] if fwd else cum[0:1]
        qe = (q[d] * jnp.exp(cum)).astype(BF16)
        o_inter.append(_dot_nt(qe, s_t[d].astype(BF16)))
        k_end = (k[d] * jnp.exp(tot - cum)).astype(BF16)
        kv_t = lax.dot_general(v[d], k_end, (((0,), (0,)), ((), ())), preferred_element_type=F32)
        s_new.append(s_t[d] * jnp.exp(tot) + bd * kv_t)
        att_d = []
        for i in range(nsub):
            lo, hi = i * sub, (i + 1) * sub
            if fwd:
                ref = cum[lo - 1:lo] if i > 0 else jnp.zeros((1, QK_PAD), F32)
                key_ok = key_row < hi
                causal = ci <= att_row + lo
            else:
                ref = cum[hi:hi + 1] if i < nsub - 1 else jnp.zeros((1, QK_PAD), F32)
                key_ok = key_row >= lo
                causal = ci >= att_row + lo
            qi = q[d][lo:hi] * jnp.exp(cum[lo:hi] - ref)
            qs = (jnp.concatenate([qi] * GLA_HEADS, axis=0) * hm).astype(BF16)
            ki = (k[d] * jnp.exp(jnp.where(key_ok, ref - cum, NEG_BIG))).astype(BF16)
            att = _dot_nt(qs, ki)
            att_d.append(jnp.where(causal, att, 0.0).astype(BF16))
        atts.append(jnp.concatenate(att_d, axis=0))

    outs = []
    for d in range(2):
        r = _dot(atts[d], v[d])
        blocks = []
        for i in range(nsub):
            base = i * c
            oi = r[base:base + sub] * vm[0]
            for h in range(1, GLA_HEADS):
                oi = oi + r[base + h * sub:base + (h + 1) * sub] * vm[h]
            blocks.append(oi)
        outs.append(o_inter[d] + jnp.concatenate(blocks, axis=0))
    return outs, s_new


def _gla_kernel(q_ref, k_ref, v_ref, gf_ref, gb_ref, sf0_ref, sb0_ref, of_ref, ob_ref, sf_ref, sb_ref):
    n = q_ref.shape[0]
    nc = n // GLA_CHUNK
    hm, bd, vm = _gla_masks()
    sf_ref[...] = sf0_ref[...]
    sb_ref[...] = sb0_ref[...]

    def body(j, carry):
        rows = [pl.ds(pl.multiple_of(cidx * GLA_CHUNK, GLA_CHUNK), GLA_CHUNK) for cidx in (j, nc - 1 - j)]
        q = [q_ref[r, :].astype(F32) for r in rows]
        k = [k_ref[r, :].astype(F32) for r in rows]
        v = [v_ref[r, :] for r in rows]
        g = [gf_ref[rows[0], :], gb_ref[rows[1], :]]
        outs, s_new = _gla_pair(q, k, v, g, [sf_ref[...], sb_ref[...]], hm, bd, vm)
        sf_ref[...] = s_new[0]
        sb_ref[...] = s_new[1]
        of_ref[rows[0], :] = outs[0]
        ob_ref[rows[1], :] = outs[1]
        return carry

    lax.fori_loop(0, nc, body, 0)


def _gla(q, k, v, gf, gb, sf0, sb0, o_prev, *, seq, blk0, n_batch):
    t_all = q.shape[0]
    tokw = lambda w: pl.BlockSpec((seq, w), lambda b: (blk0 + b, 0))
    st = pl.BlockSpec((None, GLA_WIDTH, QK_PAD), lambda b: (b, 0, 0))
    in_specs = [tokw(QK_PAD), tokw(QK_PAD), tokw(GLA_WIDTH), tokw(QK_PAD), tokw(QK_PAD), st, st]
    args = [q, k, v, gf, gb, sf0, sb0]
    aliases = {}
    n_in = len(args)
    if o_prev is not None:
        in_specs += [pl.BlockSpec(memory_space=pl.ANY)] * 2
        args += list(o_prev)
        aliases = {n_in: 0, n_in + 1: 1}

    def kern(*refs):
        _gla_kernel(*refs[:n_in], *refs[len(args):])

    st_shape = jax.ShapeDtypeStruct((n_batch, GLA_WIDTH, QK_PAD), F32)
    o_shape = jax.ShapeDtypeStruct((t_all, GLA_WIDTH), F32)
    o_f, o_b, s_f, s_b = pl.pallas_call(
        kern,
        grid=(n_batch,),
        in_specs=in_specs,
        out_specs=[tokw(GLA_WIDTH), tokw(GLA_WIDTH), st, st],
        out_shape=[o_shape, o_shape, st_shape, st_shape],
        input_output_aliases=aliases,
        compiler_params=_cparams(("parallel",)),
        name="gla",
    )(*args)
    return (o_f, o_b), s_f, s_b


def _na_kernel(q_ref, k_ref, v_ref, kc_ref, vc_ref, bias_ref, o_ref):
    n = q_ref.shape[0]
    rows = n // GRID_W
    nkeys = NA_KH * GRID_W
    lane = lax.broadcasted_iota(jnp.int32, (1, LANE), 1)
    first = lane < NA_DH
    kc = kc_ref[...]
    vc = vc_ref[...]

    def body(jb, carry):
        items = []
        for j in range(NA_UNROLL):
            r = jb * NA_UNROLL + j
            r0 = jnp.clip(r - NA_KH // 2, 0, rows - NA_KH)
            var = r - r0
            qrows = pl.ds(pl.multiple_of(r * GRID_W, GRID_W), GRID_W)
            krows = pl.ds(pl.multiple_of(r0 * GRID_W, GRID_W), nkeys)
            qr = q_ref[qrows, :]
            kb = k_ref[krows, :]
            for h in range(2):
                sel = first if h == 0 else jnp.logical_not(first)
                qh = jnp.where(sel, qr, jnp.zeros_like(qr))
                items.append((qrows, krows, h, var, _dot_nt(qh, kb), _dot_nt(qh, kc)))
        probs = []
        for qrows, krows, h, var, s_loc, s_ctx in items:
            s_loc = s_loc + bias_ref[h, var]
            m = jnp.maximum(jnp.max(s_loc, axis=-1, keepdims=True), jnp.max(s_ctx, axis=-1, keepdims=True))
            p_loc = jnp.exp(s_loc - m)
            p_ctx = jnp.exp(s_ctx - m)
            l = jnp.sum(p_loc, axis=-1, keepdims=True) + jnp.sum(p_ctx, axis=-1, keepdims=True)
            probs.append((p_loc.astype(BF16), p_ctx.astype(BF16), l))
        res = []
        for (qrows, krows, h, var, _, _), (p_loc, p_ctx, l) in zip(items, probs):
            o = _dot(p_loc, v_ref[krows, :]) + _dot(p_ctx, vc)
            res.append(o / l)
        for j in range(NA_UNROLL):
            o_ref[items[2 * j][0], :] = jnp.where(first, res[2 * j], res[2 * j + 1]).astype(o_ref.dtype)
        return carry

    lax.fori_loop(0, rows // NA_UNROLL, body, 0)


def _na_latent(nq, nk, nv, bias_tab, *, seq, ctx_len, n_batch, ctx_blk0):
    t_all = nq.shape[0]
    npair = NA_HEADS // 2
    lat = pl.BlockSpec((seq, LANE), lambda b, p: (b, p))
    ctx = pl.BlockSpec((ctx_len, LANE), lambda b, p: (ctx_blk0 + b, p))
    bias = pl.BlockSpec((2,) + bias_tab.shape[1:], lambda b, p: (p, 0, 0, 0))
    return pl.pallas_call(
        _na_kernel,
        grid=(n_batch, npair),
        in_specs=[lat, lat, lat, ctx, ctx, bias],
        out_specs=lat,
        out_shape=jax.ShapeDtypeStruct((t_all, NA_WIDTH), BF16),
        compiler_params=_cparams(("parallel", "arbitrary")),
        name="na_latent",
    )(nq, nk, nv, nk, nv, bias_tab)


def _na_ctx_kernel(q_ref, k_ref, v_ref, o_in_ref, o_ref):
    del o_in_ref
    lane = lax.broadcasted_iota(jnp.int32, (1, LANE), 1)
    first = lane < NA_DH
    q = q_ref[...]
    k = k_ref[...]
    v = v_ref[...]
    res = []
    for h in range(2):
        sel = first if h == 0 else jnp.logical_not(first)
        qh = jnp.where(sel, q, jnp.zeros_like(q))
        s = _dot_nt(qh, k)
        m = jnp.max(s, axis=-1, keepdims=True)
        p = jnp.exp(s - m)
        l = jnp.sum(p, axis=-1, keepdims=True)
        res.append(_dot(p.astype(BF16), v) / l)
    o_ref[...] = jnp.where(first, res[0], res[1]).astype(o_ref.dtype)


def _na_context(nq, nk, nv, o_prev, *, ctx_len, n_batch, ctx_blk0):
    npair = NA_HEADS // 2
    ctx = pl.BlockSpec((ctx_len, LANE), lambda b, p: (ctx_blk0 + b, p))
    return pl.pallas_call(
        _na_ctx_kernel,
        grid=(n_batch, npair),
        in_specs=[ctx, ctx, ctx, pl.BlockSpec(memory_space=pl.ANY)],
        out_specs=ctx,
        out_shape=jax.ShapeDtypeStruct(o_prev.shape, o_prev.dtype),
        input_output_aliases={3: 0},
        compiler_params=_cparams(("parallel", "arbitrary")),
        name="na_context",
    )(nq, nk, nv, o_prev)


CONV_HALO = 16
CONV_ROWS = 128


def _conv_kernel(*refs, has_prev):
    if has_prev:
        h_ref, w_ref, b_ref, lg_ref, lb_ref, _, o_ref, pad_ref = refs
    else:
        h_ref, w_ref, b_ref, lg_ref, lb_ref, o_ref, pad_ref = refs
    n = h_ref.shape[0]
    zeros = jnp.zeros((CONV_HALO, CONV_WIDTH), F32)
    pad_ref[0:CONV_HALO, :] = zeros
    pad_ref[CONV_HALO + n:CONV_HALO + n + CONV_HALO, :] = zeros
    pad_ref[CONV_HALO:CONV_HALO + n, :] = h_ref[...]
    w = w_ref[...]
    off = CONV_HALO - CONV_K // 2
    for cidx in range(n // CONV_ROWS):
        base = cidx * CONV_ROWS
        acc = jnp.zeros((CONV_ROWS, CONV_WIDTH), F32) + b_ref[...]
        for j in range(CONV_K):
            acc = acc + pad_ref[base + off + j:base + off + j + CONV_ROWS, :] * w[j:j + 1, :]
        mu = jnp.mean(acc, axis=-1, keepdims=True)
        xc = acc - mu
        var = jnp.mean(xc * xc, axis=-1, keepdims=True)
        y = xc * lax.rsqrt(var + EPS) * lg_ref[...] + lb_ref[...]
        o_ref[base:base + CONV_ROWS, :] = (y * _sigmoid(y)).astype(o_ref.dtype)


def _conv(hcv, w, b, lg, lb, o_prev, *, seq, blk0, n_batch):
    t_all = hcv.shape[0]
    tok = pl.BlockSpec((seq, CONV_WIDTH), lambda i: (blk0 + i, 0))
    full = lambda a: pl.BlockSpec(a.shape, lambda i: (0,) * a.ndim)
    in_specs = [tok, full(w), full(b), full(lg), full(lb)]
    args = [hcv, w, b, lg, lb]
    aliases = {}
    if o_prev is not None:
        in_specs.append(pl.BlockSpec(memory_space=pl.ANY))
        args.append(o_prev)
        aliases = {5: 0}
    return pl.pallas_call(
        functools.partial(_conv_kernel, has_prev=o_prev is not None),
        grid=(n_batch,),
        in_specs=in_specs,
        out_specs=tok,
        out_shape=jax.ShapeDtypeStruct((t_all, CONV_WIDTH), BF16),
        scratch_shapes=[pltpu.VMEM((seq + 2 * CONV_HALO, CONV_WIDTH), F32)],
        input_output_aliases=aliases,
        compiler_params=_cparams(("parallel",)),
        name="conv",
    )(*args)


def _outproj_kernel(ogf_ref, ogb_ref, sg_ref, on_ref, oc_ref, x_ref, g2_ref, sh_ref, sc_ref, gn_ref, e_ref,
                    wo_ref, gffn_ref, rwh_ref, rwl_ref, xmid_ref, h2_ref, lg_ref):
    of = ogf_ref[...] + ogb_ref[...]
    sq_hi, sq_lo = _split(of * of)
    e = e_ref[...]
    ms = (_dot(sq_hi, e) + _dot(sq_lo, e)) * (1.0 / GLA_DV)
    og = of * lax.rsqrt(ms + EPS) * gn_ref[...] * sg_ref[...].astype(F32)
    y = _dot(og.astype(BF16), wo_ref[0:GLA_WIDTH, :])
    y = y + _dot(on_ref[...], wo_ref[GLA_WIDTH:GLA_WIDTH + NA_WIDTH, :])
    y = y + _dot(oc_ref[...], wo_ref[GLA_WIDTH + NA_WIDTH:, :])
    x = x_ref[...] + g2_ref[...] * y
    xmid_ref[...] = x
    ms2 = jnp.mean(x * x, axis=-1, keepdims=True)
    h2 = x * lax.rsqrt(ms2 + EPS) * gffn_ref[...]
    h2 = h2 * (1.0 + sc_ref[...]) + sh_ref[...]
    h_hi, h_lo = _split(h2)
    h2_ref[...] = h_hi
    rwh = rwh_ref[...]
    lg_ref[...] = _dot(h_hi, rwh) + _dot(h_lo, rwh) + _dot(h_hi, rwl_ref[...])


def _outproj(o_gf, o_gb, sgate, o_n, o_c, x_all, mod5, layer, gnorm, e_mat, w_out, gffn, rw_hi, rw_lo, *, n_tiles,
             n_lat_tiles, tiles_per_seq, n_batch):
    d = x_all.shape[1]
    t_out = n_tiles * TOK_TILE

    def bidx(i):
        return jnp.where(i < n_lat_tiles, i // tiles_per_seq, n_batch)

    def modspec(j):
        return pl.BlockSpec((None, None, None, 1, d), lambda i: (layer, bidx(i), j, 0, 0))

    tok = lambda w: pl.BlockSpec((TOK_TILE, w), lambda i: (i, 0))
    full = lambda a: pl.BlockSpec(a.shape, lambda i: (0,) * a.ndim)
    return pl.pallas_call(
        _outproj_kernel,
        grid=(n_tiles,),
        in_specs=[tok(GLA_WIDTH), tok(GLA_WIDTH), tok(GLA_WIDTH), tok(NA_WIDTH), tok(CONV_WIDTH), tok(d),
                  modspec(2), modspec(3), modspec(4), full(gnorm), full(e_mat), full(w_out), full(gffn),
                  full(rw_hi), full(rw_lo)],
        out_specs=[tok(d), tok(d), tok(LANE)],
        out_shape=[jax.ShapeDtypeStruct((t_out, d), F32), jax.ShapeDtypeStruct((t_out, d), BF16),
                   jax.ShapeDtypeStruct((t_out, LANE), F32)],
        compiler_params=_cparams(("parallel",)),
        name="outproj",
    )(o_gf, o_gb, sgate, o_n, o_c, x_all, mod5, mod5, mod5, gnorm, e_mat, w_out, gffn, rw_hi, rw_lo)


def _route_kernel(lg_ref, bias_ref, meta_ref, cnt_ref, carry_ref):
    tile = lg_ref.shape[0]

    @pl.when(pl.program_id(0) == 0)
    def _():
        carry_ref[...] = jnp.zeros_like(carry_ref)

    lt = lg_ref[...].T
    aff = _sigmoid(lt[0:N_EXPERTS])
    sel = aff + bias_ref[...]
    s = [sel[e:e + 1] for e in range(N_EXPERTS)]
    a = [aff[e:e + 1] for e in range(N_EXPERTS)]

    def top2sum(v):
        best = v[0] + v[1]
        for i, j in PAIRS[1:]:
            best = jnp.maximum(best, v[i] + v[j])
        return best

    gs = [top2sum(s[4 * g:4 * g + 4]) for g in range(N_GROUPS)]
    gbest = jnp.zeros_like(gs[0], dtype=jnp.int32)
    gmax = gs[0]
    for g in range(1, N_GROUPS):
        upd = gs[g] > gmax
        gbest = jnp.where(upd, g, gbest)
        gmax = jnp.where(upd, gs[g], gmax)

    def pick(vals, j):
        out = vals[j]
        for g in range(1, N_GROUPS):
            out = jnp.where(gbest == g, vals[4 * g + j], out)
        return out

    sv = [pick(s, j) for j in range(EXPERTS_PER_GROUP)]
    av = [pick(a, j) for j in range(EXPERTS_PER_GROUP)]
    i1 = jnp.zeros_like(gbest)
    m1 = sv[0]
    for j in range(1, EXPERTS_PER_GROUP):
        upd = sv[j] > m1
        i1 = jnp.where(upd, j, i1)
        m1 = jnp.where(upd, sv[j], m1)
    i2 = jnp.full_like(gbest, -1)
    m2 = jnp.zeros_like(m1)
    for j in range(EXPERTS_PER_GROUP):
        upd = (i1 != j) & ((sv[j] > m2) | (i2 < 0))
        i2 = jnp.where(upd, j, i2)
        m2 = jnp.where(upd, sv[j], m2)
    ia = jnp.minimum(i1, i2)
    ib = jnp.maximum(i1, i2)
    pair = jnp.where(ia == 0, ib - 1, jnp.where(ia == 1, ib + 1, 5))
    cls = gbest * len(PAIRS) + pair

    def take(vals, idx):
        out = vals[0]
        for j in range(1, EXPERTS_PER_GROUP):
            out = jnp.where(idx == j, vals[j], out)
        return out

    w1 = take(av, i1)
    w2 = take(av, i2)
    tot = w1 + w2
    wa = jnp.where(i1 < i2, w1, w2) / tot
    wb = jnp.where(i1 < i2, w2, w1) / tot

    crow = lax.broadcasted_iota(jnp.int32, (32, tile), 0)
    oh = (crow == cls).astype(F32)
    us = lax.broadcasted_iota(jnp.int32, (tile, tile), 0)
    ut = lax.broadcasted_iota(jnp.int32, (tile, tile), 1)
    upper = (us < ut).astype(BF16)
    prefix = _dot(oh.astype(BF16), upper)
    carry = carry_ref[...]
    rank = jnp.sum(oh * (prefix + carry), axis=0, keepdims=True)
    carry_new = carry + jnp.sum(oh, axis=1, keepdims=True)
    carry_ref[...] = carry_new
    cnt_ref[...] = jnp.broadcast_to(carry_new, cnt_ref.shape)

    meta_ref[...] = jnp.zeros_like(meta_ref)
    meta_ref[0:1, :] = cls.astype(F32)
    meta_ref[1:2, :] = rank
    meta_ref[2:3, :] = wa
    meta_ref[3:4, :] = wb


def _route(logits, bias_col):
    t = logits.shape[0]
    nt = t // ROUTE_TILE
    return pl.pallas_call(
        _route_kernel,
        grid=(nt,),
        in_specs=[pl.BlockSpec((ROUTE_TILE, LANE), lambda i: (i, 0)),
                  pl.BlockSpec(bias_col.shape, lambda i: (0, 0))],
        out_specs=[pl.BlockSpec((8, ROUTE_TILE), lambda i: (0, i)),
                   pl.BlockSpec((32, LANE), lambda i: (0, 0))],
        out_shape=[jax.ShapeDtypeStruct((8, t), F32), jax.ShapeDtypeStruct((32, LANE), F32)],
        scratch_shapes=[pltpu.VMEM((32, 1), F32)],
        compiler_params=_cparams(("arbitrary",)),
        name="route",
    )(logits, bias_col)


FF_TILE = 512


def _expert_kernel(ea_ref, eb_ref, valid_ref, xs_ref, ws_ref, wga_ref, wua_ref, wda_ref, wgb_ref, wub_ref,
                   wdb_ref, y_ref):
    del ea_ref, eb_ref
    j = pl.program_id(0)

    @pl.when(valid_ref[j] == 0)
    def _():
        y_ref[...] = jnp.zeros_like(y_ref)

    @pl.when(valid_ref[j] != 0)
    def _():
        x = xs_ref[...]
        ws = ws_ref[...]
        ff = wga_ref.shape[1]

        def ffn(wg_ref, wu_ref, wd_ref):
            acc = None
            for f0 in range(0, ff, FF_TILE):
                hg = _dot(x, wg_ref[:, f0:f0 + FF_TILE])
                hu = _dot(x, wu_ref[:, f0:f0 + FF_TILE])
                hh = (hg * _sigmoid(hg) * hu).astype(BF16)
                part = _dot(hh, wd_ref[f0:f0 + FF_TILE, :])
                acc = part if acc is None else acc + part
            return acc

        ya = ffn(wga_ref, wua_ref, wda_ref)
        yb = ffn(wgb_ref, wub_ref, wdb_ref)
        y_ref[...] = (ya * ws[:, 0:1] + yb * ws[:, 1:2]).astype(y_ref.dtype)


def _experts(xs, ws, ea, eb, valid, wg, wu, wd):
    p_rows, d = xs.shape
    nb = p_rows // MOE_BLK
    ff = wg.shape[2]
    wspec_in = lambda which: pl.BlockSpec((None, d, ff), lambda j, ea, eb, v: ((ea, eb)[which][j], 0, 0))
    wspec_out = lambda which: pl.BlockSpec((None, ff, d), lambda j, ea, eb, v: ((ea, eb)[which][j], 0, 0))
    grid_spec = pltpu.PrefetchScalarGridSpec(
        num_scalar_prefetch=3,
        grid=(nb,),
        in_specs=[pl.BlockSpec((MOE_BLK, d), lambda j, ea, eb, v: (j, 0)),
                  pl.BlockSpec((MOE_BLK, 2), lambda j, ea, eb, v: (j, 0)),
                  wspec_in(0), wspec_in(0), wspec_out(0), wspec_in(1), wspec_in(1), wspec_out(1)],
        out_specs=pl.BlockSpec((MOE_BLK, d), lambda j, ea, eb, v: (j, 0)),
    )
    return pl.pallas_call(
        _expert_kernel,
        grid_spec=grid_spec,
        out_shape=jax.ShapeDtypeStruct((p_rows, d), BF16),
        compiler_params=_cparams(("arbitrary",)),
        name="experts",
    )(ea, eb, valid, xs, ws, wg, wu, wd, wg, wu, wd)


def _moe(h2, logits, bias_col, wg, wu, wd):
    t, d = h2.shape
    meta, cnt = _route(logits, bias_col)
    cls = meta[0].astype(jnp.int32)
    rank = meta[1].astype(jnp.int32)
    counts = cnt[:N_CLASSES, 0].astype(jnp.int32)
    padded = (counts + MOE_BLK - 1) // MOE_BLK * MOE_BLK
    pad_end = jnp.cumsum(padded)
    pad_start = pad_end - padded
    dest = pad_start[cls] + rank
    nb = t // MOE_BLK + N_CLASSES
    p_rows = nb * MOE_BLK
    blk_start = jnp.arange(nb, dtype=jnp.int32) * MOE_BLK
    valid = (blk_start < pad_end[-1]).astype(jnp.int32)
    blk_cls = jnp.sum((pad_end[None, :] <= blk_start[:, None]).astype(jnp.int32), axis=-1)
    last_cls = jnp.sum((pad_end <= pad_end[-1] - 1).astype(jnp.int32))
    blk_cls = jnp.where(valid > 0, blk_cls, last_cls)
    blk_cls = jnp.minimum(blk_cls, N_CLASSES - 1)
    pair_a = jnp.array([p[0] for p in PAIRS], jnp.int32)
    pair_b = jnp.array([p[1] for p in PAIRS], jnp.int32)
    grp = blk_cls // len(PAIRS)
    ea = grp * EXPERTS_PER_GROUP + pair_a[blk_cls % len(PAIRS)]
    eb = grp * EXPERTS_PER_GROUP + pair_b[blk_cls % len(PAIRS)]

    row_tok = jnp.full((p_rows,), t, jnp.int32).at[dest].set(jnp.arange(t, dtype=jnp.int32))
    xs = jnp.concatenate([h2, jnp.zeros((1, d), h2.dtype)], axis=0)[row_tok]
    ws = jnp.zeros((p_rows, 2), F32).at[dest].set(jnp.stack([meta[2], meta[3]], axis=1))
    ys = _experts(xs, ws, ea, eb, valid, wg, wu, wd)
    return ys[dest]


def _final_kernel(x_ref, y_ref, g5_ref, gf_ref, o_ref):
    x = x_ref[...] + g5_ref[...] * y_ref[...].astype(F32)
    ms = jnp.mean(x * x, axis=-1, keepdims=True)
    o_ref[...] = x * lax.rsqrt(ms + EPS) * gf_ref[...]


def _final(x_mid, y, mod5, layer, g_final, *, tiles_per_seq):
    t, d = x_mid.shape
    tok = pl.BlockSpec((TOK_TILE, d), lambda i: (i, 0))
    return pl.pallas_call(
        _final_kernel,
        grid=(t // TOK_TILE,),
        in_specs=[tok, tok,
                  pl.BlockSpec((None, None, None, 1, d), lambda i: (layer, i // tiles_per_seq, 5, 0, 0)),
                  pl.BlockSpec(g_final.shape, lambda i: (0, 0))],
        out_specs=tok,
        out_shape=jax.ShapeDtypeStruct((t, d), F32),
        compiler_params=_cparams(("parallel",)),
        name="final_norm",
    )(x_mid, y, mod5, g_final)


def _rope_tables(seq):
    t = jnp.arange(seq)
    row = (t // GRID_W).astype(F32)
    col = (t % GRID_W).astype(F32)
    half = GLA_DK // 2
    inv = ROPE_BASE ** (-jnp.arange(0, half, 2, dtype=F32) / half)
    ang = jnp.concatenate([row[:, None] * inv, col[:, None] * inv], axis=-1)
    cos = jnp.repeat(jnp.cos(ang), 2, axis=-1)
    sin = jnp.repeat(jnp.sin(ang), 2, axis=-1)
    cos = jnp.tile(cos, (1, GLA_HEADS))
    sin = jnp.tile(sin, (1, GLA_HEADS))
    padw = QK_PAD - GLA_QK
    cos = jnp.pad(cos, ((0, 0), (0, padw)), constant_values=1.0)
    sin = jnp.pad(sin, ((0, 0), (0, padw)))
    cos = jnp.concatenate([cos, jnp.ones((TOK_TILE, QK_PAD), F32)], axis=0)
    sin = jnp.concatenate([sin, jnp.zeros((TOK_TILE, QK_PAD), F32)], axis=0)
    return cos, sin


def _pair_swap(w):
    w2 = w.reshape(w.shape[0], -1, 2)
    return jnp.stack([-w2[..., 1], w2[..., 0]], axis=-1).reshape(w.shape)


def _in_weights(w_in_l):
    d = w_in_l.shape[0]
    offs = np.cumsum([0, GLA_QK, GLA_QK, GLA_WIDTH, GLA_WIDTH, 2 * GLA_LOWRANK, NA_WIDTH, NA_WIDTH, NA_WIDTH,
                      2 * CONV_WIDTH])
    seg = [w_in_l[:, offs[i]:offs[i + 1]] for i in range(9)]
    padc = lambda w, n: jnp.pad(w, ((0, 0), (0, n - w.shape[1])))
    cols = [padc(seg[0], QK_PAD), padc(_pair_swap(seg[0]), QK_PAD), padc(seg[1], QK_PAD),
            padc(_pair_swap(seg[1]), QK_PAD), seg[2], seg[3], padc(seg[4], LANE), seg[5], seg[6], seg[7], seg[8]]
    w = jnp.concatenate(cols, axis=1).astype(BF16)
    assert w.shape == (d, IN_COLS_PAD)
    return w


def _gate_weights(wa_f, ba_f, wa_b, ba_b):
    wa = jnp.zeros((LANE, 2 * QK_PAD), F32)
    wa = wa.at[0:GLA_LOWRANK, 0:GLA_QK].set(wa_f)
    wa = wa.at[GLA_LOWRANK:2 * GLA_LOWRANK, QK_PAD:QK_PAD + GLA_QK].set(wa_b)
    ba = jnp.zeros((1, 2 * QK_PAD), F32)
    ba = ba.at[0, 0:GLA_QK].set(ba_f)
    ba = ba.at[0, QK_PAD:QK_PAD + GLA_QK].set(ba_b)
    return wa, ba


def _na_bias_tables(rpb):
    cq = np.arange(GRID_W)
    c0 = np.clip(cq - NA_KW // 2, 0, GRID_W - NA_KW)
    kc = np.arange(GRID_W)
    valid = (kc[None, :] >= c0[:, None]) & (kc[None, :] < c0[:, None] + NA_KW)
    dcol = np.clip(kc[None, :] - cq[:, None] + (NA_KW - 1), 0, 2 * NA_KW - 2)
    var = np.arange(NA_KH)
    drow = np.arange(NA_KH)[None, :] - var[:, None] + (NA_KH - 1)
    tab = rpb[:, drow][:, :, :, dcol]
    tab = jnp.where(valid[None, None, None], tab, NEG_BIG)
    tab = jnp.transpose(tab, (0, 1, 3, 2, 4))
    return tab.reshape(rpb.shape[0], NA_KH, GRID_W, NA_KH * GRID_W).astype(F32)


def _head_mean_matrix():
    h = np.arange(GLA_WIDTH) // GLA_DV
    return jnp.asarray((h[:, None] == h[None, :]).astype(np.float32), dtype=BF16)


def kernel(x, c, ctx, c_ctx, w_mod, b_mod, g_mix, g_ffn, w_in, gla_wa_f, gla_ba_f, gla_wa_b, gla_ba_b, gla_g_norm,
           na_rpb, conv_w, conv_b, conv_ln_g, conv_ln_b, w_out, router_w, router_bias, w_gate, w_up, w_down,
           g_final):
    bsz, seq, d = x.shape
    ctx_len = ctx.shape[1]
    depth = w_mod.shape[0]
    t_lat, t_ctx = bsz * seq, bsz * ctx_len
    assert seq % TOK_TILE == 0 and t_ctx % TOK_TILE == 0 and seq % ctx_len == 0
    assert seq // GRID_W >= NA_KH and (seq // GRID_W) % NA_UNROLL == 0 and ctx_len % GLA_CHUNK == 0
    tiles_per_seq = seq // TOK_TILE
    n_lat_tiles = t_lat // TOK_TILE
    n_all_tiles = (t_lat + t_ctx) // TOK_TILE
    ctx_blk0 = t_lat // ctx_len

    mod_rows = -(-(bsz + 1) // 8) * 8
    c_pad = jnp.zeros((mod_rows, d), F32).at[:bsz].set(c).at[bsz].set(c_ctx)
    mod = _modulation(c_pad, w_mod, b_mod)
    mod5 = mod.reshape(depth, mod_rows, 6, 1, d)

    cos_t, sin_t = _rope_tables(seq)
    e_mat = _head_mean_matrix()
    rw = jnp.pad(router_w, ((0, 0), (0, LANE - N_EXPERTS)))
    rw_hi = rw.astype(BF16)
    rw_lo = (rw - rw_hi.astype(F32)).astype(BF16)
    bias_col = router_bias.reshape(N_EXPERTS, 1).astype(F32)
    zero_state = jnp.zeros((bsz, GLA_WIDTH, QK_PAD), F32)

    x_all = jnp.concatenate([x.reshape(t_lat, d), ctx.reshape(t_ctx, d)], axis=0)
    y_moe = None
    for l in range(depth):
        last = l == depth - 1
        w_aug = _in_weights(w_in[l])
        wa_aug, ba_aug = _gate_weights(gla_wa_f[l], gla_ba_f[l], gla_wa_b[l], gla_ba_b[l])
        x_all, (q, k, v, sgate, gf, gb, nq, nk, nv, hcv) = _inproj(
            x_all, y_moe, mod5, l, g_mix[l].reshape(1, d), cos_t, sin_t, w_aug, wa_aug, ba_aug,
            n_lat_tiles=n_lat_tiles, tiles_per_seq=tiles_per_seq, n_batch=bsz, res_layer=l - 1)

        o_g, st_f, st_b = _gla(q, k, v, gf, gb, zero_state, zero_state, None, seq=ctx_len, blk0=ctx_blk0,
                               n_batch=bsz)
        o_g, _, _ = _gla(q, k, v, gf, gb, st_f, st_b, o_g, seq=seq, blk0=0, n_batch=bsz)

        bias_tab = _na_bias_tables(na_rpb[l])
        o_n = _na_latent(nq, nk, nv, bias_tab, seq=seq, ctx_len=ctx_len, n_batch=bsz, ctx_blk0=ctx_blk0)
        cw, cb = conv_w[l], conv_b[l].reshape(1, -1)
        clg, clb = conv_ln_g[l].reshape(1, -1), conv_ln_b[l].reshape(1, -1)
        o_c = _conv(hcv, cw, cb, clg, clb, None, seq=seq, blk0=0, n_batch=bsz)
        if not last:
            o_n = _na_context(nq, nk, nv, o_n, ctx_len=ctx_len, n_batch=bsz, ctx_blk0=ctx_blk0)
            o_c = _conv(hcv, cw, cb, clg, clb, o_c, seq=ctx_len, blk0=ctx_blk0, n_batch=bsz)

        n_tiles = n_lat_tiles if last else n_all_tiles
        x_mid, h2, logits = _outproj(
            o_g[0], o_g[1], sgate, o_n, o_c, x_all, mod5, l, gla_g_norm[l].reshape(1, -1), e_mat, w_out[l].astype(BF16),
            g_ffn[l].reshape(1, d), rw_hi, rw_lo, n_tiles=n_tiles, n_lat_tiles=n_lat_tiles,
            tiles_per_seq=tiles_per_seq, n_batch=bsz)
        y_moe = _moe(h2, logits, bias_col, w_gate[l].astype(BF16), w_up[l].astype(BF16), w_down[l].astype(BF16))
        x_all = x_mid

    out = _final(x_all, y_moe, mod5, depth - 1, g_final.reshape(1, d), tiles_per_seq=tiles_per_seq)
    return out.reshape(bsz, seq, d)
```

```python
import functools

import numpy as np
import jax
import jax.numpy as jnp
from jax import lax
from jax.experimental import pallas as pl
from jax.experimental.pallas import tpu as pltpu
from jax.experimental.pallas import tpu_sc as plsc

GRID_W = 64
EPS = 1e-6
GLA_HEADS, GLA_DK, GLA_DV = 4, 48, 96
GLA_QK = GLA_HEADS * GLA_DK
GLA_WIDTH = GLA_HEADS * GLA_DV
GLA_LOWRANK = 16
GLA_TAU = 16.0
ROPE_BASE = 10000.0
NA_HEADS, NA_DH = 6, 64
NA_WIDTH = NA_HEADS * NA_DH
NA_KH, NA_KW = 8, 16
CONV_WIDTH, CONV_K = 256, 31
N_EXPERTS, N_GROUPS, EXPERTS_PER_GROUP = 16, 4, 4
PAIRS = ((0, 1), (0, 2), (0, 3), (1, 2), (1, 3), (2, 3))
N_CLASSES = N_GROUPS * len(PAIRS)

LANE = 128
QK_PAD = 256
C_Q, C_QS, C_K, C_KS = 0, 256, 512, 768
C_V, C_GATE, C_A = 1024, 1408, 1792
C_NQ, C_NK, C_NV, C_CONV = 1920, 2304, 2688, 3072
IN_COLS_PAD = 3584

TOK_TILE = 512
GLA_CHUNK = 64
GLA_SUB = 16
MOE_BLK = 256
ROUTE_TILE = 512
NA_UNROLL = 4
NEG_BIG = -1e30
VMEM_LIMIT = 48 * 1024 * 1024

F32 = jnp.float32
BF16 = jnp.bfloat16


def _cparams(sem):
    return pltpu.CompilerParams(dimension_semantics=sem, vmem_limit_bytes=VMEM_LIMIT)


def _dot(a, b):
    return jnp.dot(a, b, preferred_element_type=F32)


def _dot_nt(a, b):
    return lax.dot_general(a, b, (((1,), (1,)), ((), ())), preferred_element_type=F32)


def _split(a):
    hi = a.astype(BF16)
    lo = (a - hi.astype(F32)).astype(BF16)
    return hi, lo


def _sigmoid(x):
    return 1.0 / (1.0 + jnp.exp(-x))


SUB_X, SUB_Y = 5, 4
SC_WINDOW = 256


def _pack_bf16_pairs(a, b):
    ua = lax.bitcast_convert_type(a.astype(BF16).astype(F32), jnp.uint32)
    ub = lax.bitcast_convert_type(b.astype(BF16).astype(F32), jnp.uint32)
    return jnp.bitwise_or(jnp.right_shift(ua, jnp.uint32(16)), ub)


def _unpack_bf16_pairs(w):
    lo = lax.bitcast_convert_type(jnp.left_shift(w, jnp.uint32(16)), F32)
    hi = lax.bitcast_convert_type(jnp.bitwise_and(w, jnp.uint32(0xFFFF0000)), F32)
    return lo, hi


def _store_subrows(ref, val, j0):
    r = val.shape[0]
    for j in range(val.shape[1] // LANE):
        ref[:, j0 + j, :, :] = val[:, j * LANE:(j + 1) * LANE].reshape(r // 8, 8, LANE)


def _load_subrows(ref, j0, n):
    r = ref.shape[0] * 8
    return jnp.concatenate([ref[:, j0 + j, :, :].reshape(r, LANE) for j in range(n)], axis=1)


def _pack_rows(x):
    half = x.shape[1] // 2
    return _pack_bf16_pairs(x[:, :half], x[:, half:])


def _unpack_rows(w):
    lo, hi = _unpack_bf16_pairs(w)
    return jnp.concatenate([lo, hi], axis=1)


def _sc_mesh():
    return plsc.VectorSubcoreMesh(core_axis_name="c", subcore_axis_name="s")


def _sc_scatter(src, idx, n_out):
    n, w = src.shape
    idx2 = idx.reshape(1, n)

    @pl.kernel(out_type=jax.ShapeDtypeStruct((n_out, w), src.dtype), mesh=_sc_mesh(), scratch_types=[])
    def scatter_kernel(x_hbm, i_hbm, o_hbm):
        def body(x_vmem, i_vmem):
            pltpu.sync_copy(x_vmem, o_hbm.at[i_vmem.at[0]])

        pltpu.emit_pipeline(
            body,
            grid=(n // SC_WINDOW,),
            in_specs=[pl.BlockSpec((SC_WINDOW, w), index_map=lambda i: (i, 0)),
                      pl.BlockSpec((1, SC_WINDOW), index_map=lambda i: (0, i))],
            out_specs=[],
            core_axis_name=("c", "s"),
            dimension_semantics=(pltpu.PARALLEL,),
        )(x_hbm, i_hbm)

    return scatter_kernel(src, idx2)


def _sc_gather(src, idx):
    n = idx.shape[0]
    w = src.shape[1]
    idx2 = idx.reshape(1, n)

    @pl.kernel(out_type=jax.ShapeDtypeStruct((n, w), src.dtype), mesh=_sc_mesh())
    def gather_kernel(x_hbm, i_hbm, o_hbm):
        def body(i_vmem, o_vmem):
            pltpu.sync_copy(x_hbm.at[i_vmem.at[0]], o_vmem)

        pltpu.emit_pipeline(
            body,
            grid=(n // SC_WINDOW,),
            in_specs=[pl.BlockSpec((1, SC_WINDOW), index_map=lambda i: (0, i))],
            out_specs=[pl.BlockSpec((SC_WINDOW, w), index_map=lambda i: (i, 0))],
            core_axis_name=("c", "s"),
            dimension_semantics=(pltpu.PARALLEL,),
        )(i_hbm, o_hbm)

    return gather_kernel(src, idx2)


def _mod_kernel(c_ref, w_ref, b_ref, o_ref):
    cv = c_ref[...]
    s = cv * _sigmoid(cv)
    s_hi, s_lo = _split(s)
    w_hi, w_lo = _split(w_ref[...])
    o_ref[...] = _dot(s_hi, w_hi) + _dot(s_lo, w_hi) + _dot(s_hi, w_lo) + b_ref[...]


def _modulation(c_pad, w_mod, b_mod):
    depth, d, six_d = w_mod.shape
    rows = c_pad.shape[0]
    nt = 1536
    return pl.pallas_call(
        _mod_kernel,
        grid=(depth, six_d // nt),
        in_specs=[
            pl.BlockSpec((rows, d), lambda l, j: (0, 0)),
            pl.BlockSpec((None, d, nt), lambda l, j: (l, 0, j)),
            pl.BlockSpec((None, 1, nt), lambda l, j: (l, 0, j)),
        ],
        out_specs=pl.BlockSpec((None, rows, nt), lambda l, j: (l, 0, j)),
        out_shape=jax.ShapeDtypeStruct((depth, rows, six_d), F32),
        compiler_params=_cparams(("arbitrary", "arbitrary")),
        name="modulation",
    )(c_pad, w_mod, b_mod.reshape(depth, 1, six_d))


def _inproj_kernel(*refs, has_res):
    if has_res:
        x_ref, y_ref, g5_ref = refs[:3]
        refs = refs[3:]
    else:
        x_ref = refs[0]
        refs = refs[1:]
    (sh_ref, sc_ref, gmix_ref, cos_ref, sin_ref, w_ref, wa_ref, ba_ref) = refs[:8]
    outs = refs[8:]
    if has_res:
        xnew_ref = outs[0]
        outs = outs[1:]
    (q_ref, k_ref, v_ref, sg_ref, gf_ref, gb_ref, nq_ref, nk_ref, nv_ref, hcv_ref) = outs

    x = x_ref[...]
    if has_res:
        x = x + g5_ref[...] * _unpack_rows(_load_subrows(y_ref, 0, SUB_Y))
        xnew_ref[...] = x
    ms = jnp.mean(x * x, axis=-1, keepdims=True)
    h = x * lax.rsqrt(ms + EPS) * gmix_ref[...]
    h = h * (1.0 + sc_ref[...]) + sh_ref[...]
    hb = h.astype(BF16)

    def proj(c0, n):
        return _dot(hb, w_ref[:, c0:c0 + n])

    cos = cos_ref[...]
    sin = sin_ref[...]
    q = (proj(C_Q, QK_PAD) * cos + proj(C_QS, QK_PAD) * sin) * (GLA_DK ** -0.5)
    q_ref[...] = q.astype(BF16)
    k = proj(C_K, QK_PAD) * cos + proj(C_KS, QK_PAD) * sin
    k_ref[...] = k.astype(BF16)
    v_ref[...] = proj(C_V, GLA_WIDTH).astype(BF16)
    gate = proj(C_GATE, GLA_WIDTH)
    sg_ref[...] = (gate * _sigmoid(gate)).astype(BF16)

    a_hi, a_lo = _split(proj(C_A, LANE))
    wa_hi, wa_lo = _split(wa_ref[...])
    z = _dot(a_hi, wa_hi) + _dot(a_lo, wa_hi) + _dot(a_hi, wa_lo) + ba_ref[...]
    logsig = jnp.minimum(z, 0.0) - jnp.log(1.0 + jnp.exp(-jnp.abs(z)))
    g = logsig * (1.0 / GLA_TAU)
    gf_ref[...] = g[:, :QK_PAD]
    gb_ref[...] = g[:, QK_PAD:]

    nq_ref[...] = (proj(C_NQ, NA_WIDTH) * (NA_DH ** -0.5)).astype(BF16)
    nk_ref[...] = proj(C_NK, NA_WIDTH).astype(BF16)
    nv_ref[...] = proj(C_NV, NA_WIDTH).astype(BF16)

    u = proj(C_CONV, 2 * CONV_WIDTH)
    hcv_ref[...] = u[:, :CONV_WIDTH] * _sigmoid(u[:, CONV_WIDTH:])


def _inproj(x_all, y, mod5, layer, gmix, cos_t, sin_t, w_aug, wa_aug, ba_aug, *, n_lat_tiles, tiles_per_seq,
            n_batch, res_layer):
    t_all, d = x_all.shape
    nt = t_all // TOK_TILE
    has_res = y is not None

    def bidx(i):
        return jnp.where(i < n_lat_tiles, i // tiles_per_seq, n_batch)

    def ridx(i):
        return jnp.where(i < n_lat_tiles, i % tiles_per_seq, tiles_per_seq)

    def modspec(l, j):
        return pl.BlockSpec((None, None, None, 1, d), lambda i: (l, bidx(i), j, 0, 0))

    tok = lambda w: pl.BlockSpec((TOK_TILE, w), lambda i: (i, 0))
    full = lambda a: pl.BlockSpec(a.shape, lambda i: (0,) * a.ndim)

    in_specs = [tok(d)]
    args = [x_all]
    if has_res:
        in_specs += [pl.BlockSpec((TOK_TILE // 8, SUB_Y, 8, LANE), lambda i: (i, 0, 0, 0)), modspec(res_layer, 5)]
        args += [y, mod5]
    in_specs += [modspec(layer, 0), modspec(layer, 1), full(gmix),
                 pl.BlockSpec((TOK_TILE, QK_PAD), lambda i: (ridx(i), 0)),
                 pl.BlockSpec((TOK_TILE, QK_PAD), lambda i: (ridx(i), 0)),
                 full(w_aug), full(wa_aug), full(ba_aug)]
    args += [mod5, mod5, gmix, cos_t, sin_t, w_aug, wa_aug, ba_aug]

    out_widths = [(QK_PAD, BF16), (QK_PAD, BF16), (GLA_WIDTH, BF16), (GLA_WIDTH, BF16), (QK_PAD, F32), (QK_PAD, F32),
                  (NA_WIDTH, BF16), (NA_WIDTH, BF16), (NA_WIDTH, BF16), (CONV_WIDTH, F32)]
    if has_res:
        out_widths = [(d, F32)] + out_widths
    out_specs = [tok(w) for w, _ in out_widths]
    out_shape = [jax.ShapeDtypeStruct((t_all, w), dt) for w, dt in out_widths]
    res = pl.pallas_call(
        functools.partial(_inproj_kernel, has_res=has_res),
        grid=(nt,),
        in_specs=in_specs,
        out_specs=out_specs,
        out_shape=out_shape,
        compiler_params=_cparams(("parallel",)),
        name="inproj",
    )(*args)
    if has_res:
        return res[0], res[1:]
    return x_all, res


def _gla_masks():
    c, sub = GLA_CHUNK, GLA_SUB
    lane_qk = lax.broadcasted_iota(jnp.int32, (1, QK_PAD), 1)
    head_qk = ((lane_qk >= GLA_DK).astype(jnp.int32) + (lane_qk >= 2 * GLA_DK).astype(jnp.int32)
               + (lane_qk >= 3 * GLA_DK).astype(jnp.int32) + 4 * (lane_qk >= 4 * GLA_DK).astype(jnp.int32))
    row_h = jnp.right_shift(lax.broadcasted_iota(jnp.int32, (c, 1), 0), GLA_SUB.bit_length() - 1)
    hm = (row_h == head_qk).astype(F32)
    row_v = lax.broadcasted_iota(jnp.int32, (GLA_WIDTH, 1), 0)
    head_v = ((row_v >= GLA_DV).astype(jnp.int32) + (row_v >= 2 * GLA_DV).astype(jnp.int32)
              + (row_v >= 3 * GLA_DV).astype(jnp.int32))
    bd = (head_v == head_qk).astype(F32)
    lane_v = lax.broadcasted_iota(jnp.int32, (1, GLA_WIDTH), 1)
    vm = [((lane_v >= h * GLA_DV) & (lane_v < (h + 1) * GLA_DV)).astype(F32) for h in range(GLA_HEADS)]
    return hm, bd, vm


def _gla_pair(q, k, v, g, s_t, hm, bd, vm):
    c, sub = GLA_CHUNK, GLA_SUB
    nsub = c // sub
    dirs = (True, False)
    ri = lax.broadcasted_iota(jnp.int32, (c, c), 0)
    ci = lax.broadcasted_iota(jnp.int32, (c, c), 1)
    key_row = lax.broadcasted_iota(jnp.int32, (c, 1), 0)
    att_row = jnp.bitwise_and(ri, sub - 1)

    cums = []
    for d, fwd in enumerate(dirs):
        tri = ((ci <= ri) if fwd else (ci >= ri)).astype(BF16)
        g_hi, g_lo = _split(g[d])
        cums.append(_dot(tri, g_hi) + _dot(tri, g_lo))

    o_inter, s_new, atts = [], [], []
    for d, fwd in enumerate(dirs):
        cum = cums[d]
        tot = cum[c - 1:c] if fwd else cum[0:1]
        qe = (q[d] * jnp.exp(cum)).astype(BF16)
        o_inter.append(_dot_nt(qe, s_t[d].astype(BF16)))
        k_end = (k[d] * jnp.exp(tot - cum)).astype(BF16)
        kv_t = lax.dot_general(v[d], k_end, (((0,), (0,)), ((), ())), preferred_element_type=F32)
        s_new.append(s_t[d] * jnp.exp(tot) + bd * kv_t)
        att_d = []
        for i in range(nsub):
            lo, hi = i * sub, (i + 1) * sub
            if fwd:
                ref = cum[lo - 1:lo] if i > 0 else jnp.zeros((1, QK_PAD), F32)
                key_ok = key_row < hi
                causal = ci <= att_row + lo
            else:
                ref = cum[hi:hi + 1] if i < nsub - 1 else jnp.zeros((1, QK_PAD), F32)
                key_ok = key_row >= lo
                causal = ci >= att_row + lo
            qi = q[d][lo:hi] * jnp.exp(cum[lo:hi] - ref)
            qs = (jnp.concatenate([qi] * GLA_HEADS, axis=0) * hm).astype(BF16)
            ki = (k[d] * jnp.exp(jnp.where(key_ok, ref - cum, NEG_BIG))).astype(BF16)
            att = _dot_nt(qs, ki)
            att_d.append(jnp.where(causal, att, 0.0).astype(BF16))
        atts.append(jnp.concatenate(att_d, axis=0))

    outs = []
    for d in range(2):
        r = _dot(atts[d], v[d])
        blocks = []
        for i in range(nsub):
            base = i * c
            oi = r[base:base + sub] * vm[0]
            for h in range(1, GLA_HEADS):
                oi = oi + r[base + h * sub:base + (h + 1) * sub] * vm[h]
            blocks.append(oi)
        outs.append(o_inter[d] + jnp.concatenate(blocks, axis=0))
    return outs, s_new


def _gla_kernel(q_ref, k_ref, v_ref, gf_ref, gb_ref, sf0_ref, sb0_ref, of_ref, ob_ref, sf_ref, sb_ref):
    n = q_ref.shape[0]
    nc = n // GLA_CHUNK
    hm, bd, vm = _gla_masks()
    sf_ref[...] = sf0_ref[...]
    sb_ref[...] = sb0_ref[...]

    def body(j, carry):
        rows = [pl.ds(pl.multiple_of(cidx * GLA_CHUNK, GLA_CHUNK), GLA_CHUNK) for cidx in (j, nc - 1 - j)]
        q = [q_ref[r, :].astype(F32) for r in rows]
        k = [k_ref[r, :].astype(F32) for r in rows]
        v = [v_ref[r, :] for r in rows]
        g = [gf_ref[rows[0], :], gb_ref[rows[1], :]]
        outs, s_new = _gla_pair(q, k, v, g, [sf_ref[...], sb_ref[...]], hm, bd, vm)
        sf_ref[...] = s_new[0]
        sb_ref[...] = s_new[1]
        of_ref[rows[0], :] = outs[0]
        ob_ref[rows[1], :] = outs[1]
        return carry

    lax.fori_loop(0, nc, body, 0)


def _gla(q, k, v, gf, gb, sf0, sb0, o_prev, *, seq, blk0, n_batch):
    t_all = q.shape[0]
    tokw = lambda w: pl.BlockSpec((seq, w), lambda b: (blk0 + b, 0))
    st = pl.BlockSpec((None, GLA_WIDTH, QK_PAD), lambda b: (b, 0, 0))
    in_specs = [tokw(QK_PAD), tokw(QK_PAD), tokw(GLA_WIDTH), tokw(QK_PAD), tokw(QK_PAD), st, st]
    args = [q, k, v, gf, gb, sf0, sb0]
    aliases = {}
    n_in = len(args)
    if o_prev is not None:
        in_specs += [pl.BlockSpec(memory_space=pl.ANY)] * 2
        args += list(o_prev)
        aliases = {n_in: 0, n_in + 1: 1}

    def kern(*refs):
        _gla_kernel(*refs[:n_in], *refs[len(args):])

    st_shape = jax.ShapeDtypeStruct((n_batch, GLA_WIDTH, QK_PAD), F32)
    o_shape = jax.ShapeDtypeStruct((t_all, GLA_WIDTH), F32)
    o_f, o_b, s_f, s_b = pl.pallas_call(
        kern,
        grid=(n_batch,),
        in_specs=in_specs,
        out_specs=[tokw(GLA_WIDTH), tokw(GLA_WIDTH), st, st],
        out_shape=[o_shape, o_shape, st_shape, st_shape],
        input_output_aliases=aliases,
        compiler_params=_cparams(("parallel",)),
        name="gla",
    )(*args)
    return (o_f, o_b), s_f, s_b


def _na_kernel(q_ref, k_ref, v_ref, kc_ref, vc_ref, bias_ref, o_ref):
    n = q_ref.shape[0]
    rows = n // GRID_W
    nkeys = NA_KH * GRID_W
    lane = lax.broadcasted_iota(jnp.int32, (1, LANE), 1)
    first = lane < NA_DH
    kc = kc_ref[...]
    vc = vc_ref[...]

    def body(jb, carry):
        items = []
        for j in range(NA_UNROLL):
            r = jb * NA_UNROLL + j
            r0 = jnp.clip(r - NA_KH // 2, 0, rows - NA_KH)
            var = r - r0
            qrows = pl.ds(pl.multiple_of(r * GRID_W, GRID_W), GRID_W)
            krows = pl.ds(pl.multiple_of(r0 * GRID_W, GRID_W), nkeys)
            qr = q_ref[qrows, :]
            kb = k_ref[krows, :]
            for h in range(2):
                sel = first if h == 0 else jnp.logical_not(first)
                qh = jnp.where(sel, qr, jnp.zeros_like(qr))
                items.append((qrows, krows, h, var, _dot_nt(qh, kb), _dot_nt(qh, kc)))
        probs = []
        for qrows, krows, h, var, s_loc, s_ctx in items:
            s_loc = s_loc + bias_ref[h, var]
            m = jnp.maximum(jnp.max(s_loc, axis=-1, keepdims=True), jnp.max(s_ctx, axis=-1, keepdims=True))
            p_loc = jnp.exp(s_loc - m)
            p_ctx = jnp.exp(s_ctx - m)
            l = jnp.sum(p_loc, axis=-1, keepdims=True) + jnp.sum(p_ctx, axis=-1, keepdims=True)
            probs.append((p_loc.astype(BF16), p_ctx.astype(BF16), l))
        res = []
        for (qrows, krows, h, var, _, _), (p_loc, p_ctx, l) in zip(items, probs):
            o = _dot(p_loc, v_ref[krows, :]) + _dot(p_ctx, vc)
            res.append(o / l)
        for j in range(NA_UNROLL):
            o_ref[items[2 * j][0], :] = jnp.where(first, res[2 * j], res[2 * j + 1]).astype(o_ref.dtype)
        return carry

    lax.fori_loop(0, rows // NA_UNROLL, body, 0)


def _na_latent(nq, nk, nv, bias_tab, *, seq, ctx_len, n_batch, ctx_blk0):
    t_all = nq.shape[0]
    npair = NA_HEADS // 2
    lat = pl.BlockSpec((seq, LANE), lambda b, p: (b, p))
    ctx = pl.BlockSpec((ctx_len, LANE), lambda b, p: (ctx_blk0 + b, p))
    bias = pl.BlockSpec((2,) + bias_tab.shape[1:], lambda b, p: (p, 0, 0, 0))
    return pl.pallas_call(
        _na_kernel,
        grid=(n_batch, npair),
        in_specs=[lat, lat, lat, ctx, ctx, bias],
        out_specs=lat,
        out_shape=jax.ShapeDtypeStruct((t_all, NA_WIDTH), BF16),
        compiler_params=_cparams(("parallel", "arbitrary")),
        name="na_latent",
    )(nq, nk, nv, nk, nv, bias_tab)


def _na_ctx_kernel(q_ref, k_ref, v_ref, o_in_ref, o_ref):
    del o_in_ref
    lane = lax.broadcasted_iota(jnp.int32, (1, LANE), 1)
    first = lane < NA_DH
    q = q_ref[...]
    k = k_ref[...]
    v = v_ref[...]
    res = []
    for h in range(2):
        sel = first if h == 0 else jnp.logical_not(first)
        qh = jnp.where(sel, q, jnp.zeros_like(q))
        s = _dot_nt(qh, k)
        m = jnp.max(s, axis=-1, keepdims=True)
        p = jnp.exp(s - m)
        l = jnp.sum(p, axis=-1, keepdims=True)
        res.append(_dot(p.astype(BF16), v) / l)
    o_ref[...] = jnp.where(first, res[0], res[1]).astype(o_ref.dtype)


def _na_context(nq, nk, nv, o_prev, *, ctx_len, n_batch, ctx_blk0):
    npair = NA_HEADS // 2
    ctx = pl.BlockSpec((ctx_len, LANE), lambda b, p: (ctx_blk0 + b, p))
    return pl.pallas_call(
        _na_ctx_kernel,
        grid=(n_batch, npair),
        in_specs=[ctx, ctx, ctx, pl.BlockSpec(memory_space=pl.ANY)],
        out_specs=ctx,
        out_shape=jax.ShapeDtypeStruct(o_prev.shape, o_prev.dtype),
        input_output_aliases={3: 0},
        compiler_params=_cparams(("parallel", "arbitrary")),
        name="na_context",
    )(nq, nk, nv, o_prev)


CONV_HALO = 16
CONV_ROWS = 128


def _conv_kernel(*refs, has_prev):
    if has_prev:
        h_ref, w_ref, b_ref, lg_ref, lb_ref, _, o_ref, pad_ref = refs
    else:
        h_ref, w_ref, b_ref, lg_ref, lb_ref, o_ref, pad_ref = refs
    n = h_ref.shape[0]
    zeros = jnp.zeros((CONV_HALO, CONV_WIDTH), F32)
    pad_ref[0:CONV_HALO, :] = zeros
    pad_ref[CONV_HALO + n:CONV_HALO + n + CONV_HALO, :] = zeros
    pad_ref[CONV_HALO:CONV_HALO + n, :] = h_ref[...]
    w = w_ref[...]
    off = CONV_HALO - CONV_K // 2
    for cidx in range(n // CONV_ROWS):
        base = cidx * CONV_ROWS
        acc = jnp.zeros((CONV_ROWS, CONV_WIDTH), F32) + b_ref[...]
        for j in range(CONV_K):
            acc = acc + pad_ref[base + off + j:base + off + j + CONV_ROWS, :] * w[j:j + 1, :]
        mu = jnp.mean(acc, axis=-1, keepdims=True)
        xc = acc - mu
        var = jnp.mean(xc * xc, axis=-1, keepdims=True)
        y = xc * lax.rsqrt(var + EPS) * lg_ref[...] + lb_ref[...]
        o_ref[base:base + CONV_ROWS, :] = (y * _sigmoid(y)).astype(o_ref.dtype)


def _conv(hcv, w, b, lg, lb, o_prev, *, seq, blk0, n_batch):
    t_all = hcv.shape[0]
    tok = pl.BlockSpec((seq, CONV_WIDTH), lambda i: (blk0 + i, 0))
    full = lambda a: pl.BlockSpec(a.shape, lambda i: (0,) * a.ndim)
    in_specs = [tok, full(w), full(b), full(lg), full(lb)]
    args = [hcv, w, b, lg, lb]
    aliases = {}
    if o_prev is not None:
        in_specs.append(pl.BlockSpec(memory_space=pl.ANY))
        args.append(o_prev)
        aliases = {5: 0}
    return pl.pallas_call(
        functools.partial(_conv_kernel, has_prev=o_prev is not None),
        grid=(n_batch,),
        in_specs=in_specs,
        out_specs=tok,
        out_shape=jax.ShapeDtypeStruct((t_all, CONV_WIDTH), BF16),
        scratch_shapes=[pltpu.VMEM((seq + 2 * CONV_HALO, CONV_WIDTH), F32)],
        input_output_aliases=aliases,
        compiler_params=_cparams(("parallel",)),
        name="conv",
    )(*args)


def _outproj_kernel(ogf_ref, ogb_ref, sg_ref, on_ref, oc_ref, x_ref, g2_ref, sh_ref, sc_ref, gn_ref, e_ref,
                    wo_ref, gffn_ref, rwh_ref, rwl_ref, xmid_ref, h2_ref, lg_ref):
    of = ogf_ref[...] + ogb_ref[...]
    sq_hi, sq_lo = _split(of * of)
    e = e_ref[...]
    ms = (_dot(sq_hi, e) + _dot(sq_lo, e)) * (1.0 / GLA_DV)
    og = of * lax.rsqrt(ms + EPS) * gn_ref[...] * sg_ref[...].astype(F32)
    y = _dot(og.astype(BF16), wo_ref[0:GLA_WIDTH, :])
    y = y + _dot(on_ref[...], wo_ref[GLA_WIDTH:GLA_WIDTH + NA_WIDTH, :])
    y = y + _dot(oc_ref[...], wo_ref[GLA_WIDTH + NA_WIDTH:, :])
    x = x_ref[...] + g2_ref[...] * y
    xmid_ref[...] = x
    ms2 = jnp.mean(x * x, axis=-1, keepdims=True)
    h2 = x * lax.rsqrt(ms2 + EPS) * gffn_ref[...]
    h2 = h2 * (1.0 + sc_ref[...]) + sh_ref[...]
    h_hi, h_lo = _split(h2)
    _store_subrows(h2_ref, _pack_rows(h2), 0)
    rwh = rwh_ref[...]
    lg_ref[...] = _dot(h_hi, rwh) + _dot(h_lo, rwh) + _dot(h_hi, rwl_ref[...])


def _outproj(o_gf, o_gb, sgate, o_n, o_c, x_all, mod5, layer, gnorm, e_mat, w_out, gffn, rw_hi, rw_lo, *, n_tiles,
             n_lat_tiles, tiles_per_seq, n_batch):
    d = x_all.shape[1]
    t_out = n_tiles * TOK_TILE

    def bidx(i):
        return jnp.where(i < n_lat_tiles, i // tiles_per_seq, n_batch)

    def modspec(j):
        return pl.BlockSpec((None, None, None, 1, d), lambda i: (layer, bidx(i), j, 0, 0))

    tok = lambda w: pl.BlockSpec((TOK_TILE, w), lambda i: (i, 0))
    full = lambda a: pl.BlockSpec(a.shape, lambda i: (0,) * a.ndim)
    return pl.pallas_call(
        _outproj_kernel,
        grid=(n_tiles,),
        in_specs=[tok(GLA_WIDTH), tok(GLA_WIDTH), tok(GLA_WIDTH), tok(NA_WIDTH), tok(CONV_WIDTH), tok(d),
                  modspec(2), modspec(3), modspec(4), full(gnorm), full(e_mat), full(w_out), full(gffn),
                  full(rw_hi), full(rw_lo)],
        out_specs=[tok(d), pl.BlockSpec((TOK_TILE // 8, SUB_Y, 8, LANE), lambda i: (i, 0, 0, 0)), tok(LANE)],
        out_shape=[jax.ShapeDtypeStruct((t_out, d), F32),
                   jax.ShapeDtypeStruct((t_out // 8, SUB_X, 8, LANE), jnp.uint32),
                   jax.ShapeDtypeStruct((t_out, LANE), F32)],
        compiler_params=_cparams(("parallel",)),
        name="outproj",
    )(o_gf, o_gb, sgate, o_n, o_c, x_all, mod5, mod5, mod5, gnorm, e_mat, w_out, gffn, rw_hi, rw_lo)


def _route_kernel(lg_ref, bias_ref, h2_in_ref, meta_ref, cnt_ref, h2w_ref, carry_ref):
    del h2_in_ref
    tile = lg_ref.shape[0]

    @pl.when(pl.program_id(0) == 0)
    def _():
        carry_ref[...] = jnp.zeros_like(carry_ref)

    lt = lg_ref[...].T
    aff = _sigmoid(lt[0:N_EXPERTS])
    sel = aff + bias_ref[...]
    s = [sel[e:e + 1] for e in range(N_EXPERTS)]
    a = [aff[e:e + 1] for e in range(N_EXPERTS)]

    def top2sum(v):
        best = v[0] + v[1]
        for i, j in PAIRS[1:]:
            best = jnp.maximum(best, v[i] + v[j])
        return best

    gs = [top2sum(s[4 * g:4 * g + 4]) for g in range(N_GROUPS)]
    gbest = jnp.zeros_like(gs[0], dtype=jnp.int32)
    gmax = gs[0]
    for g in range(1, N_GROUPS):
        upd = gs[g] > gmax
        gbest = jnp.where(upd, g, gbest)
        gmax = jnp.where(upd, gs[g], gmax)

    def pick(vals, j):
        out = vals[j]
        for g in range(1, N_GROUPS):
            out = jnp.where(gbest == g, vals[4 * g + j], out)
        return out

    sv = [pick(s, j) for j in range(EXPERTS_PER_GROUP)]
    av = [pick(a, j) for j in range(EXPERTS_PER_GROUP)]
    i1 = jnp.zeros_like(gbest)
    m1 = sv[0]
    for j in range(1, EXPERTS_PER_GROUP):
        upd = sv[j] > m1
        i1 = jnp.where(upd, j, i1)
        m1 = jnp.where(upd, sv[j], m1)
    i2 = jnp.full_like(gbest, -1)
    m2 = jnp.zeros_like(m1)
    for j in range(EXPERTS_PER_GROUP):
        upd = (i1 != j) & ((sv[j] > m2) | (i2 < 0))
        i2 = jnp.where(upd, j, i2)
        m2 = jnp.where(upd, sv[j], m2)
    ia = jnp.minimum(i1, i2)
    ib = jnp.maximum(i1, i2)
    pair = jnp.where(ia == 0, ib - 1, jnp.where(ia == 1, ib + 1, 5))
    cls = gbest * len(PAIRS) + pair

    def take(vals, idx):
        out = vals[0]
        for j in range(1, EXPERTS_PER_GROUP):
            out = jnp.where(idx == j, vals[j], out)
        return out

    w1 = take(av, i1)
    w2 = take(av, i2)
    tot = w1 + w2
    wa = jnp.where(i1 < i2, w1, w2) / tot
    wb = jnp.where(i1 < i2, w2, w1) / tot

    crow = lax.broadcasted_iota(jnp.int32, (32, tile), 0)
    oh = (crow == cls).astype(F32)
    us = lax.broadcasted_iota(jnp.int32, (tile, tile), 0)
    ut = lax.broadcasted_iota(jnp.int32, (tile, tile), 1)
    upper = (us < ut).astype(BF16)
    prefix = _dot(oh.astype(BF16), upper)
    carry = carry_ref[...]
    rank = jnp.sum(oh * (prefix + carry), axis=0, keepdims=True)
    carry_new = carry + jnp.sum(oh, axis=1, keepdims=True)
    carry_ref[...] = carry_new
    cnt_ref[...] = jnp.broadcast_to(carry_new, cnt_ref.shape)

    meta_ref[...] = jnp.zeros_like(meta_ref)
    meta_ref[0:1, :] = cls.astype(F32)
    meta_ref[1:2, :] = rank

    wrow = lax.broadcasted_iota(jnp.int32, (LANE, tile), 0)
    wmat = jnp.where(wrow == 0, wa, jnp.where(wrow == 1, wb, 0.0))
    wtok = lax.bitcast_convert_type(wmat.T, jnp.uint32)
    h2w_ref[:, 0, :, :] = wtok.reshape(tile // 8, 8, LANE)


def _route(logits, bias_col, h2_sub):
    t = logits.shape[0]
    nt = t // ROUTE_TILE
    meta, cnt, h2_sub = pl.pallas_call(
        _route_kernel,
        grid=(nt,),
        in_specs=[pl.BlockSpec((ROUTE_TILE, LANE), lambda i: (i, 0)),
                  pl.BlockSpec(bias_col.shape, lambda i: (0, 0)),
                  pl.BlockSpec(memory_space=pl.ANY)],
        out_specs=[pl.BlockSpec((8, ROUTE_TILE), lambda i: (0, i)),
                   pl.BlockSpec((32, LANE), lambda i: (0, 0)),
                   pl.BlockSpec((ROUTE_TILE // 8, 1, 8, LANE), lambda i: (i, SUB_Y, 0, 0))],
        out_shape=[jax.ShapeDtypeStruct((8, t), F32), jax.ShapeDtypeStruct((32, LANE), F32),
                   jax.ShapeDtypeStruct(h2_sub.shape, h2_sub.dtype)],
        scratch_shapes=[pltpu.VMEM((32, 1), F32)],
        input_output_aliases={2: 2},
        compiler_params=_cparams(("arbitrary",)),
        name="route",
    )(logits, bias_col, h2_sub)
    return meta, cnt, h2_sub


FF_TILE = 512


def _expert_kernel(ea_ref, eb_ref, nvalid_ref, xs_ref, wga_ref, wua_ref, wda_ref, wgb_ref, wub_ref, wdb_ref, y_ref):
    del ea_ref, eb_ref
    j = pl.program_id(0)
    nvalid = nvalid_ref[j]

    @pl.when(nvalid == 0)
    def _():
        y_ref[...] = jnp.zeros_like(y_ref)

    @pl.when(nvalid != 0)
    def _():
        rows = y_ref.shape[0] * 8
        live = lax.broadcasted_iota(jnp.int32, (rows, 1), 0) < nvalid
        x = jnp.where(live, _unpack_rows(_load_subrows(xs_ref, 0, SUB_Y)), 0.0).astype(BF16)
        ws = jnp.where(live, lax.bitcast_convert_type(xs_ref[:, SUB_Y, :, :].reshape(rows, LANE), F32), 0.0)
        ff = wga_ref.shape[1]

        def ffn(wg_ref, wu_ref, wd_ref):
            acc = None
            for f0 in range(0, ff, FF_TILE):
                hg = _dot(x, wg_ref[:, f0:f0 + FF_TILE])
                hu = _dot(x, wu_ref[:, f0:f0 + FF_TILE])
                hh = (hg * _sigmoid(hg) * hu).astype(BF16)
                part = _dot(hh, wd_ref[f0:f0 + FF_TILE, :])
                acc = part if acc is None else acc + part
            return acc

        ya = ffn(wga_ref, wua_ref, wda_ref)
        yb = ffn(wgb_ref, wub_ref, wdb_ref)
        _store_subrows(y_ref, _pack_rows(ya * ws[:, 0:1] + yb * ws[:, 1:2]), 0)


def _experts(xs_sub, ea, eb, nvalid, wg, wu, wd):
    nb = xs_sub.shape[0] * 8 // MOE_BLK
    d, ff = wg.shape[1], wg.shape[2]
    wspec_in = lambda which: pl.BlockSpec((None, d, ff), lambda j, ea, eb, v: ((ea, eb)[which][j], 0, 0))
    wspec_out = lambda which: pl.BlockSpec((None, ff, d), lambda j, ea, eb, v: ((ea, eb)[which][j], 0, 0))
    grid_spec = pltpu.PrefetchScalarGridSpec(
        num_scalar_prefetch=3,
        grid=(nb,),
        in_specs=[pl.BlockSpec((MOE_BLK // 8, SUB_X, 8, LANE), lambda j, ea, eb, v: (j, 0, 0, 0)),
                  wspec_in(0), wspec_in(0), wspec_out(0), wspec_in(1), wspec_in(1), wspec_out(1)],
        out_specs=pl.BlockSpec((MOE_BLK // 8, SUB_Y, 8, LANE), lambda j, ea, eb, v: (j, 0, 0, 0)),
    )
    return pl.pallas_call(
        _expert_kernel,
        grid_spec=grid_spec,
        out_shape=jax.ShapeDtypeStruct((nb * MOE_BLK // 8, SUB_Y, 8, LANE), jnp.uint32),
        compiler_params=_cparams(("arbitrary",)),
        name="experts",
    )(ea, eb, nvalid, xs_sub, wg, wu, wd, wg, wu, wd)


def _subrow_index(dest, nsub):
    t = dest.shape[0]
    d3 = dest.reshape(t // 8, 1, 8)
    j = jnp.arange(nsub, dtype=jnp.int32).reshape(1, nsub, 1)
    return ((d3 // 8) * (nsub * 8) + j * 8 + d3 % 8).reshape(t * nsub)


def _moe(h2_sub, logits, bias_col, wg, wu, wd):
    t = logits.shape[0]
    meta, cnt, h2_sub = _route(logits, bias_col, h2_sub)
    cls = meta[0].astype(jnp.int32)
    rank = meta[1].astype(jnp.int32)
    counts = cnt[:N_CLASSES, 0].astype(jnp.int32)
    padded = (counts + MOE_BLK - 1) // MOE_BLK * MOE_BLK
    pad_end = jnp.cumsum(padded)
    pad_start = pad_end - padded
    dest = pad_start[cls] + rank
    nb = t // MOE_BLK + N_CLASSES
    p_rows = nb * MOE_BLK
    blk_start = jnp.arange(nb, dtype=jnp.int32) * MOE_BLK
    valid = blk_start < pad_end[-1]
    blk_cls = jnp.sum((pad_end[None, :] <= blk_start[:, None]).astype(jnp.int32), axis=-1)
    last_cls = jnp.sum((pad_end <= pad_end[-1] - 1).astype(jnp.int32))
    blk_cls = jnp.minimum(jnp.where(valid, blk_cls, last_cls), N_CLASSES - 1)
    nvalid = jnp.where(valid, jnp.clip(pad_start[blk_cls] + counts[blk_cls] - blk_start, 0, MOE_BLK), 0)
    pair_a = jnp.array([p[0] for p in PAIRS], jnp.int32)
    pair_b = jnp.array([p[1] for p in PAIRS], jnp.int32)
    grp = blk_cls // len(PAIRS)
    ea = grp * EXPERTS_PER_GROUP + pair_a[blk_cls % len(PAIRS)]
    eb = grp * EXPERTS_PER_GROUP + pair_b[blk_cls % len(PAIRS)]

    xs = _sc_scatter(h2_sub.reshape(t * SUB_X, LANE), _subrow_index(dest, SUB_X), p_rows * SUB_X)
    ys = _experts(xs.reshape(p_rows // 8, SUB_X, 8, LANE), ea, eb, nvalid.astype(jnp.int32), wg, wu, wd)
    y = _sc_gather(ys.reshape(p_rows * SUB_Y, LANE), _subrow_index(dest, SUB_Y))
    return y.reshape(t // 8, SUB_Y, 8, LANE)


def _final_kernel(x_ref, y_ref, g5_ref, gf_ref, o_ref):
    x = x_ref[...] + g5_ref[...] * _unpack_rows(_load_subrows(y_ref, 0, SUB_Y))
    ms = jnp.mean(x * x, axis=-1, keepdims=True)
    o_ref[...] = x * lax.rsqrt(ms + EPS) * gf_ref[...]


def _final(x_mid, y, mod5, layer, g_final, *, tiles_per_seq):
    t, d = x_mid.shape
    tok = pl.BlockSpec((TOK_TILE, d), lambda i: (i, 0))
    return pl.pallas_call(
        _final_kernel,
        grid=(t // TOK_TILE,),
        in_specs=[tok, pl.BlockSpec((TOK_TILE // 8, SUB_Y, 8, LANE), lambda i: (i, 0, 0, 0)),
                  pl.BlockSpec((None, None, None, 1, d), lambda i: (layer, i // tiles_per_seq, 5, 0, 0)),
                  pl.BlockSpec(g_final.shape, lambda i: (0, 0))],
        out_specs=tok,
        out_shape=jax.ShapeDtypeStruct((t, d), F32),
        compiler_params=_cparams(("parallel",)),
        name="final_norm",
    )(x_mid, y, mod5, g_final)


def _rope_tables(seq):
    t = jnp.arange(seq)
    row = (t // GRID_W).astype(F32)
    col = (t % GRID_W).astype(F32)
    half = GLA_DK // 2
    inv = ROPE_BASE ** (-jnp.arange(0, half, 2, dtype=F32) / half)
    ang = jnp.concatenate([row[:, None] * inv, col[:, None] * inv], axis=-1)
    cos = jnp.repeat(jnp.cos(ang), 2, axis=-1)
    sin = jnp.repeat(jnp.sin(ang), 2, axis=-1)
    cos = jnp.tile(cos, (1, GLA_HEADS))
    sin = jnp.tile(sin, (1, GLA_HEADS))
    padw = QK_PAD - GLA_QK
    cos = jnp.pad(cos, ((0, 0), (0, padw)), constant_values=1.0)
    sin = jnp.pad(sin, ((0, 0), (0, padw)))
    cos = jnp.concatenate([cos, jnp.ones((TOK_TILE, QK_PAD), F32)], axis=0)
    sin = jnp.concatenate([sin, jnp.zeros((TOK_TILE, QK_PAD), F32)], axis=0)
    return cos, sin


def _pair_swap(w):
    w2 = w.reshape(w.shape[0], -1, 2)
    return jnp.stack([-w2[..., 1], w2[..., 0]], axis=-1).reshape(w.shape)


def _in_weights(w_in_l):
    d = w_in_l.shape[0]
    offs = np.cumsum([0, GLA_QK, GLA_QK, GLA_WIDTH, GLA_WIDTH, 2 * GLA_LOWRANK, NA_WIDTH, NA_WIDTH, NA_WIDTH,
                      2 * CONV_WIDTH])
    seg = [w_in_l[:, offs[i]:offs[i + 1]] for i in range(9)]
    padc = lambda w, n: jnp.pad(w, ((0, 0), (0, n - w.shape[1])))
    cols = [padc(seg[0], QK_PAD), padc(_pair_swap(seg[0]), QK_PAD), padc(seg[1], QK_PAD),
            padc(_pair_swap(seg[1]), QK_PAD), seg[2], seg[3], padc(seg[4], LANE), seg[5], seg[6], seg[7], seg[8]]
    w = jnp.concatenate(cols, axis=1).astype(BF16)
    assert w.shape == (d, IN_COLS_PAD)
    return w


def _gate_weights(wa_f, ba_f, wa_b, ba_b):
    wa = jnp.zeros((LANE, 2 * QK_PAD), F32)
    wa = wa.at[0:GLA_LOWRANK, 0:GLA_QK].set(wa_f)
    wa = wa.at[GLA_LOWRANK:2 * GLA_LOWRANK, QK_PAD:QK_PAD + GLA_QK].set(wa_b)
    ba = jnp.zeros((1, 2 * QK_PAD), F32)
    ba = ba.at[0, 0:GLA_QK].set(ba_f)
    ba = ba.at[0, QK_PAD:QK_PAD + GLA_QK].set(ba_b)
    return wa, ba


def _na_bias_tables(rpb):
    cq = np.arange(GRID_W)
    c0 = np.clip(cq - NA_KW // 2, 0, GRID_W - NA_KW)
    kc = np.arange(GRID_W)
    valid = (kc[None, :] >= c0[:, None]) & (kc[None, :] < c0[:, None] + NA_KW)
    dcol = np.clip(kc[None, :] - cq[:, None] + (NA_KW - 1), 0, 2 * NA_KW - 2)
    var = np.arange(NA_KH)
    drow = np.arange(NA_KH)[None, :] - var[:, None] + (NA_KH - 1)
    tab = rpb[:, drow][:, :, :, dcol]
    tab = jnp.where(valid[None, None, None], tab, NEG_BIG)
    tab = jnp.transpose(tab, (0, 1, 3, 2, 4))
    return tab.reshape(rpb.shape[0], NA_KH, GRID_W, NA_KH * GRID_W).astype(F32)


def _head_mean_matrix():
    h = np.arange(GLA_WIDTH) // GLA_DV
    return jnp.asarray((h[:, None] == h[None, :]).astype(np.float32), dtype=BF16)


def kernel(x, c, ctx, c_ctx, w_mod, b_mod, g_mix, g_ffn, w_in, gla_wa_f, gla_ba_f, gla_wa_b, gla_ba_b, gla_g_norm,
           na_rpb, conv_w, conv_b, conv_ln_g, conv_ln_b, w_out, router_w, router_bias, w_gate, w_up, w_down,
           g_final):
    bsz, seq, d = x.shape
    ctx_len = ctx.shape[1]
    depth = w_mod.shape[0]
    t_lat, t_ctx = bsz * seq, bsz * ctx_len
    assert seq % TOK_TILE == 0 and t_ctx % TOK_TILE == 0 and seq % ctx_len == 0
    assert d == 2 * SUB_Y * LANE
    assert seq // GRID_W >= NA_KH and (seq // GRID_W) % NA_UNROLL == 0 and ctx_len % GLA_CHUNK == 0
    tiles_per_seq = seq // TOK_TILE
    n_lat_tiles = t_lat // TOK_TILE
    n_all_tiles = (t_lat + t_ctx) // TOK_TILE
    ctx_blk0 = t_lat // ctx_len

    mod_rows = -(-(bsz + 1) // 8) * 8
    c_pad = jnp.zeros((mod_rows, d), F32).at[:bsz].set(c).at[bsz].set(c_ctx)
    mod = _modulation(c_pad, w_mod, b_mod)
    mod5 = mod.reshape(depth, mod_rows, 6, 1, d)

    cos_t, sin_t = _rope_tables(seq)
    e_mat = _head_mean_matrix()
    rw = jnp.pad(router_w, ((0, 0), (0, LANE - N_EXPERTS)))
    rw_hi = rw.astype(BF16)
    rw_lo = (rw - rw_hi.astype(F32)).astype(BF16)
    bias_col = router_bias.reshape(N_EXPERTS, 1).astype(F32)
    zero_state = jnp.zeros((bsz, GLA_WIDTH, QK_PAD), F32)

    x_all = jnp.concatenate([x.reshape(t_lat, d), ctx.reshape(t_ctx, d)], axis=0)
    y_moe = None
    for l in range(depth):
        last = l == depth - 1
        w_aug = _in_weights(w_in[l])
        wa_aug, ba_aug = _gate_weights(gla_wa_f[l], gla_ba_f[l], gla_wa_b[l], gla_ba_b[l])
        x_all, (q, k, v, sgate, gf, gb, nq, nk, nv, hcv) = _inproj(
            x_all, y_moe, mod5, l, g_mix[l].reshape(1, d), cos_t, sin_t, w_aug, wa_aug, ba_aug,
            n_lat_tiles=n_lat_tiles, tiles_per_seq=tiles_per_seq, n_batch=bsz, res_layer=l - 1)

        o_g, st_f, st_b = _gla(q, k, v, gf, gb, zero_state, zero_state, None, seq=ctx_len, blk0=ctx_blk0,
                               n_batch=bsz)
        o_g, _, _ = _gla(q, k, v, gf, gb, st_f, st_b, o_g, seq=seq, blk0=0, n_batch=bsz)

        bias_tab = _na_bias_tables(na_rpb[l])
        o_n = _na_latent(nq, nk, nv, bias_tab, seq=seq, ctx_len=ctx_len, n_batch=bsz, ctx_blk0=ctx_blk0)
        cw, cb = conv_w[l], conv_b[l].reshape(1, -1)
        clg, clb = conv_ln_g[l].reshape(1, -1), conv_ln_b[l].reshape(1, -1)
        o_c = _conv(hcv, cw, cb, clg, clb, None, seq=seq, blk0=0, n_batch=bsz)
        if not last:
            o_n = _na_context(nq, nk, nv, o_n, ctx_len=ctx_len, n_batch=bsz, ctx_blk0=ctx_blk0)
            o_c = _conv(hcv, cw, cb, clg, clb, o_c, seq=ctx_len, blk0=ctx_blk0, n_batch=bsz)

        n_tiles = n_lat_tiles if last else n_all_tiles
        x_mid, h2, logits = _outproj(
            o_g[0], o_g[1], sgate, o_n, o_c, x_all, mod5, l, gla_g_norm[l].reshape(1, -1), e_mat, w_out[l].astype(BF16),
            g_ffn[l].reshape(1, d), rw_hi, rw_lo, n_tiles=n_tiles, n_lat_tiles=n_lat_tiles,
            tiles_per_seq=tiles_per_seq, n_batch=bsz)
        y_moe = _moe(h2, logits, bias_col, w_gate[l].astype(BF16), w_up[l].astype(BF16), w_down[l].astype(BF16))
        x_all = x_mid

    out = _final(x_all, y_moe, mod5, depth - 1, g_final.reshape(1, d), tiles_per_seq=tiles_per_seq)
    return out.reshape(bsz, seq, d)
```

```python
import functools

import numpy as np
import jax
import jax.numpy as jnp
from jax import lax
from jax.experimental import pallas as pl
from jax.experimental.pallas import tpu as pltpu
from jax.experimental.pallas import tpu_sc as plsc

GRID_W = 64
EPS = 1e-6
GLA_HEADS, GLA_DK, GLA_DV = 4, 48, 96
GLA_QK = GLA_HEADS * GLA_DK
GLA_WIDTH = GLA_HEADS * GLA_DV
GLA_LOWRANK = 16
GLA_TAU = 16.0
ROPE_BASE = 10000.0
NA_HEADS, NA_DH = 6, 64
NA_WIDTH = NA_HEADS * NA_DH
NA_KH, NA_KW = 8, 16
CONV_WIDTH, CONV_K = 256, 31
N_EXPERTS, N_GROUPS, EXPERTS_PER_GROUP = 16, 4, 4
PAIRS = ((0, 1), (0, 2), (0, 3), (1, 2), (1, 3), (2, 3))
N_CLASSES = N_GROUPS * len(PAIRS)

LANE = 128
SUBLANES = 8
QK_PAD = 256
C_Q, C_QS, C_K, C_KS = 0, 256, 512, 768
C_V, C_GATE, C_A = 1024, 1408, 1792
C_NQ, C_NK, C_NV, C_CONV = 1920, 2304, 2688, 3072
IN_COLS_PAD = 3584

TOK_TILE = 512
ROW_PARTS = 2
GLA_CHUNK = 64
GLA_SUB = 16
MOE_BLK = 256
ROUTE_TILE = 512
NA_UNROLL = 4
NEG_BIG = -1e30
VMEM_LIMIT = 48 * 1024 * 1024

F32 = jnp.float32
BF16 = jnp.bfloat16


def _cparams(sem):
    return pltpu.CompilerParams(dimension_semantics=sem, vmem_limit_bytes=VMEM_LIMIT)


def _dot(a, b):
    return jnp.dot(a, b, preferred_element_type=F32)


def _dot_nt(a, b):
    return lax.dot_general(a, b, (((1,), (1,)), ((), ())), preferred_element_type=F32)


def _split(a):
    hi = a.astype(BF16)
    lo = (a - hi.astype(F32)).astype(BF16)
    return hi, lo


def _sigmoid(x):
    return 1.0 / (1.0 + jnp.exp(-x))


SUB_X, SUB_Y = 5, 4
SC_WINDOW = 256


def _pack_bf16_pairs(a, b):
    ua = lax.bitcast_convert_type(a.astype(BF16).astype(F32), jnp.uint32)
    ub = lax.bitcast_convert_type(b.astype(BF16).astype(F32), jnp.uint32)
    return jnp.bitwise_or(jnp.right_shift(ua, jnp.uint32(16)), ub)


def _unpack_bf16_pairs(w):
    lo = lax.bitcast_convert_type(jnp.left_shift(w, jnp.uint32(16)), F32)
    hi = lax.bitcast_convert_type(jnp.bitwise_and(w, jnp.uint32(0xFFFF0000)), F32)
    return lo, hi


def _store_subrows(ref, val, j0, row0=0):
    r = val.shape[0]
    for j in range(val.shape[1] // LANE):
        ref[row0 // 8:(row0 + r) // 8, j0 + j, :, :] = val[:, j * LANE:(j + 1) * LANE].reshape(r // 8, 8, LANE)


def _load_subrows(ref, j0, n, row0=0, rows=None):
    r = ref.shape[0] * 8 if rows is None else rows
    return jnp.concatenate([ref[row0 // 8:(row0 + r) // 8, j0 + j, :, :].reshape(r, LANE) for j in range(n)],
                           axis=1)


def _pack_rows(x):
    half = x.shape[1] // 2
    return _pack_bf16_pairs(x[:, :half], x[:, half:])


def _unpack_rows(w):
    lo, hi = _unpack_bf16_pairs(w)
    return jnp.concatenate([lo, hi], axis=1)


def _sc_mesh():
    return plsc.VectorSubcoreMesh(core_axis_name="c", subcore_axis_name="s")


def _sc_scatter(src, idx, n_out):
    n, w = src.shape
    idx2 = idx.reshape(1, n)

    @pl.kernel(out_type=jax.ShapeDtypeStruct((n_out, w), src.dtype), mesh=_sc_mesh(), scratch_types=[])
    def scatter_kernel(x_hbm, i_hbm, o_hbm):
        def body(x_vmem, i_vmem):
            pltpu.sync_copy(x_vmem, o_hbm.at[i_vmem.at[0]])

        pltpu.emit_pipeline(
            body,
            grid=(n // SC_WINDOW,),
            in_specs=[pl.BlockSpec((SC_WINDOW, w), index_map=lambda i: (i, 0)),
                      pl.BlockSpec((1, SC_WINDOW), index_map=lambda i: (0, i))],
            out_specs=[],
            core_axis_name=("c", "s"),
            dimension_semantics=(pltpu.PARALLEL,),
        )(x_hbm, i_hbm)

    return scatter_kernel(src, idx2)


def _sc_gather(src, idx):
    n = idx.shape[0]
    w = src.shape[1]
    idx2 = idx.reshape(1, n)

    @pl.kernel(out_type=jax.ShapeDtypeStruct((n, w), src.dtype), mesh=_sc_mesh())
    def gather_kernel(x_hbm, i_hbm, o_hbm):
        def body(i_vmem, o_vmem):
            pltpu.sync_copy(x_hbm.at[i_vmem.at[0]], o_vmem)

        pltpu.emit_pipeline(
            body,
            grid=(n // SC_WINDOW,),
            in_specs=[pl.BlockSpec((1, SC_WINDOW), index_map=lambda i: (0, i))],
            out_specs=[pl.BlockSpec((SC_WINDOW, w), index_map=lambda i: (i, 0))],
            core_axis_name=("c", "s"),
            dimension_semantics=(pltpu.PARALLEL,),
        )(i_hbm, o_hbm)

    return gather_kernel(src, idx2)


def _mod_kernel(c_ref, w_ref, b_ref, o_ref):
    cv = c_ref[...]
    s = cv * _sigmoid(cv)
    s_hi, s_lo = _split(s)
    w_hi, w_lo = _split(w_ref[...])
    o_ref[...] = _dot(s_hi, w_hi) + _dot(s_lo, w_hi) + _dot(s_hi, w_lo) + b_ref[...]


def _modulation(c_pad, w_mod, b_mod):
    depth, d, six_d = w_mod.shape
    rows = c_pad.shape[0]
    nt = 1536
    return pl.pallas_call(
        _mod_kernel,
        grid=(depth, six_d // nt),
        in_specs=[
            pl.BlockSpec((rows, d), lambda l, j: (0, 0)),
            pl.BlockSpec((None, d, nt), lambda l, j: (l, 0, j)),
            pl.BlockSpec((None, 1, nt), lambda l, j: (l, 0, j)),
        ],
        out_specs=pl.BlockSpec((None, rows, nt), lambda l, j: (l, 0, j)),
        out_shape=jax.ShapeDtypeStruct((depth, rows, six_d), F32),
        compiler_params=_cparams(("arbitrary", "arbitrary")),
        name="modulation",
    )(c_pad, w_mod, b_mod.reshape(depth, 1, six_d))


def _inproj_kernel(*refs, has_res):
    if has_res:
        x_ref, y_ref, g5_ref = refs[:3]
        refs = refs[3:]
    else:
        x_ref = refs[0]
        refs = refs[1:]
    (sh_ref, sc_ref, gmix_ref, cos_ref, sin_ref, w_ref, wa_ref, ba_ref) = refs[:8]
    outs = refs[8:]
    if has_res:
        xnew_ref = outs[0]
        outs = outs[1:]
    (q_ref, k_ref, v_ref, sg_ref, gf_ref, gb_ref, nq_ref, nk_ref, nv_ref, hcv_ref) = outs

    tile = x_ref.shape[0]
    part = tile // ROW_PARTS
    wa_hi, wa_lo = _split(wa_ref[...])
    hbs = []
    for p in range(ROW_PARTS):
        rows = pl.ds(p * part, part)
        x = x_ref[rows, :]
        if has_res:
            x = x + g5_ref[...] * _unpack_rows(_load_subrows(y_ref, 0, SUB_Y, row0=p * part, rows=part))
            xnew_ref[rows, :] = x
        ms = jnp.mean(x * x, axis=-1, keepdims=True)
        h = x * lax.rsqrt(ms + EPS) * gmix_ref[...]
        h = h * (1.0 + sc_ref[...]) + sh_ref[...]
        hbs.append(h.astype(BF16))

    for p, hb in enumerate(hbs):
        rows = pl.ds(p * part, part)

        def proj(c0, n, hb=hb):
            return _dot(hb, w_ref[:, c0:c0 + n])

        cos = cos_ref[rows, :]
        sin = sin_ref[rows, :]
        a_hi, a_lo = _split(proj(C_A, LANE))
        q = (proj(C_Q, QK_PAD) * cos + proj(C_QS, QK_PAD) * sin) * (GLA_DK ** -0.5)
        q_ref[rows, :] = q.astype(BF16)
        k = proj(C_K, QK_PAD) * cos + proj(C_KS, QK_PAD) * sin
        k_ref[rows, :] = k.astype(BF16)

        z = _dot(a_hi, wa_hi) + _dot(a_lo, wa_hi) + _dot(a_hi, wa_lo) + ba_ref[...]
        logsig = jnp.minimum(z, 0.0) - jnp.log(1.0 + jnp.exp(-jnp.abs(z)))
        g = logsig * (1.0 / GLA_TAU)
        gf_ref[rows, :] = g[:, :QK_PAD]
        gb_ref[rows, :] = g[:, QK_PAD:]

        v_ref[rows, :] = proj(C_V, GLA_WIDTH).astype(BF16)
        gate = proj(C_GATE, GLA_WIDTH)
        sg_ref[rows, :] = (gate * _sigmoid(gate)).astype(BF16)
        nq_ref[rows, :] = (proj(C_NQ, NA_WIDTH) * (NA_DH ** -0.5)).astype(BF16)
        nk_ref[rows, :] = proj(C_NK, NA_WIDTH).astype(BF16)
        nv_ref[rows, :] = proj(C_NV, NA_WIDTH).astype(BF16)
        u = proj(C_CONV, 2 * CONV_WIDTH)
        hcv_ref[rows, :] = u[:, :CONV_WIDTH] * _sigmoid(u[:, CONV_WIDTH:])


def _inproj(x_all, y, mod5, layer, gmix, cos_t, sin_t, w_aug, wa_aug, ba_aug, *, n_lat_tiles, tiles_per_seq,
            n_batch, res_layer):
    t_all, d = x_all.shape
    nt = t_all // TOK_TILE
    has_res = y is not None

    def bidx(i):
        return jnp.where(i < n_lat_tiles, i // tiles_per_seq, n_batch)

    def ridx(i):
        return jnp.where(i < n_lat_tiles, i % tiles_per_seq, tiles_per_seq)

    def modspec(l, j):
        return pl.BlockSpec((None, None, None, 1, d), lambda i: (l, bidx(i), j, 0, 0))

    tok = lambda w: pl.BlockSpec((TOK_TILE, w), lambda i: (i, 0))
    full = lambda a: pl.BlockSpec(a.shape, lambda i: (0,) * a.ndim)

    in_specs = [tok(d)]
    args = [x_all]
    if has_res:
        in_specs += [pl.BlockSpec((TOK_TILE // 8, SUB_Y, 8, LANE), lambda i: (i, 0, 0, 0)), modspec(res_layer, 5)]
        args += [y, mod5]
    in_specs += [modspec(layer, 0), modspec(layer, 1), full(gmix),
                 pl.BlockSpec((TOK_TILE, QK_PAD), lambda i: (ridx(i), 0)),
                 pl.BlockSpec((TOK_TILE, QK_PAD), lambda i: (ridx(i), 0)),
                 full(w_aug), full(wa_aug), full(ba_aug)]
    args += [mod5, mod5, gmix, cos_t, sin_t, w_aug, wa_aug, ba_aug]

    out_widths = [(QK_PAD, BF16), (QK_PAD, BF16), (GLA_WIDTH, BF16), (GLA_WIDTH, BF16), (QK_PAD, F32), (QK_PAD, F32),
                  (NA_WIDTH, BF16), (NA_WIDTH, BF16), (NA_WIDTH, BF16), (CONV_WIDTH, F32)]
    if has_res:
        out_widths = [(d, F32)] + out_widths
    out_specs = [tok(w) for w, _ in out_widths]
    out_shape = [jax.ShapeDtypeStruct((t_all, w), dt) for w, dt in out_widths]
    res = pl.pallas_call(
        functools.partial(_inproj_kernel, has_res=has_res),
        grid=(nt,),
        in_specs=in_specs,
        out_specs=out_specs,
        out_shape=out_shape,
        compiler_params=_cparams(("parallel",)),
        name="inproj",
    )(*args)
    if has_res:
        return res[0], res[1:]
    return x_all, res


def _gla_masks():
    c, sub = GLA_CHUNK, GLA_SUB
    lane_qk = lax.broadcasted_iota(jnp.int32, (1, QK_PAD), 1)
    head_qk = ((lane_qk >= GLA_DK).astype(jnp.int32) + (lane_qk >= 2 * GLA_DK).astype(jnp.int32)
               + (lane_qk >= 3 * GLA_DK).astype(jnp.int32) + 4 * (lane_qk >= 4 * GLA_DK).astype(jnp.int32))
    row_h = jnp.right_shift(lax.broadcasted_iota(jnp.int32, (c, 1), 0), GLA_SUB.bit_length() - 1)
    hm = (row_h == head_qk).astype(F32)
    row_v = lax.broadcasted_iota(jnp.int32, (GLA_WIDTH, 1), 0)
    head_v = ((row_v >= GLA_DV).astype(jnp.int32) + (row_v >= 2 * GLA_DV).astype(jnp.int32)
              + (row_v >= 3 * GLA_DV).astype(jnp.int32))
    bd = (head_v == head_qk).astype(F32)
    lane_v = lax.broadcasted_iota(jnp.int32, (1, GLA_WIDTH), 1)
    vm = [((lane_v >= h * GLA_DV) & (lane_v < (h + 1) * GLA_DV)).astype(F32) for h in range(GLA_HEADS)]
    return hm, bd, vm


def _gla_pair(q, k, v, g, s_t, hm, bd, vm):
    c, sub = GLA_CHUNK, GLA_SUB
    nsub = c // sub
    dirs = (True, False)
    ri = lax.broadcasted_iota(jnp.int32, (c, c), 0)
    ci = lax.broadcasted_iota(jnp.int32, (c, c), 1)
    key_row = lax.broadcasted_iota(jnp.int32, (c, 1), 0)
    att_row = jnp.bitwise_and(ri, sub - 1)

    cums = []
    for d, fwd in enumerate(dirs):
        tri = ((ci <= ri) if fwd else (ci >= ri)).astype(BF16)
        g_hi, g_lo = _split(g[d])
        cums.append(_dot(tri, g_hi) + _dot(tri, g_lo))

    o_inter, s_new, atts = [], [], []
    for d, fwd in enumerate(dirs):
        cum = cums[d]
        tot = cum[c - 1:c] if fwd else cum[0:1]
        qe = (q[d] * jnp.exp(cum)).astype(BF16)
        o_inter.append(_dot_nt(qe, s_t[d].astype(BF16)))
        k_end = (k[d] * jnp.exp(tot - cum)).astype(BF16)
        kv_t = lax.dot_general(v[d], k_end, (((0,), (0,)), ((), ())), preferred_element_type=F32)
        s_new.append(s_t[d] * jnp.exp(tot) + bd * kv_t)
        att_d = []
        for i in range(nsub):
            lo, hi = i * sub, (i + 1) * sub
            if fwd:
                ref = cum[lo - 1:lo] if i > 0 else jnp.zeros((1, QK_PAD), F32)
                key_ok = key_row < hi
                causal = ci <= att_row + lo
            else:
                ref = cum[hi:hi + 1] if i < nsub - 1 else jnp.zeros((1, QK_PAD), F32)
                key_ok = key_row >= lo
                causal = ci >= att_row + lo
            qi = q[d][lo:hi] * jnp.exp(cum[lo:hi] - ref)
            qs = (jnp.concatenate([qi] * GLA_HEADS, axis=0) * hm).astype(BF16)
            ki = (k[d] * jnp.exp(jnp.where(key_ok, ref - cum, NEG_BIG))).astype(BF16)
            att = _dot_nt(qs, ki)
            att_d.append(jnp.where(causal, att, 0.0).astype(BF16))
        atts.append(jnp.concatenate(att_d, axis=0))

    outs = []
    for d in range(2):
        r = _dot(atts[d], v[d])
        blocks = []
        for i in range(nsub):
            base = i * c
            oi = r[base:base + sub] * vm[0]
            for h in range(1, GLA_HEADS):
                oi = oi + r[base + h * sub:base + (h + 1) * sub] * vm[h]
            blocks.append(oi)
        outs.append(o_inter[d] + jnp.concatenate(blocks, axis=0))
    return outs, s_new


def _gla_kernel(q_ref, k_ref, v_ref, gf_ref, gb_ref, sf0_ref, sb0_ref, of_ref, ob_ref, sf_ref, sb_ref):
    n = q_ref.shape[0]
    nc = n // GLA_CHUNK
    hm, bd, vm = _gla_masks()
    sf_ref[...] = sf0_ref[...]
    sb_ref[...] = sb0_ref[...]

    def body(j, carry):
        rows = [pl.ds(pl.multiple_of(cidx * GLA_CHUNK, GLA_CHUNK), GLA_CHUNK) for cidx in (j, nc - 1 - j)]
        q = [q_ref[r, :].astype(F32) for r in rows]
        k = [k_ref[r, :].astype(F32) for r in rows]
        v = [v_ref[r, :] for r in rows]
        g = [gf_ref[rows[0], :], gb_ref[rows[1], :]]
        outs, s_new = _gla_pair(q, k, v, g, [sf_ref[...], sb_ref[...]], hm, bd, vm)
        sf_ref[...] = s_new[0]
        sb_ref[...] = s_new[1]
        of_ref[rows[0], :] = outs[0]
        ob_ref[rows[1], :] = outs[1]
        return carry

    lax.fori_loop(0, nc, body, 0)


def _gla(q, k, v, gf, gb, sf0, sb0, o_prev, *, seq, blk0, n_batch):
    t_all = q.shape[0]
    tokw = lambda w: pl.BlockSpec((seq, w), lambda b: (blk0 + b, 0))
    st = pl.BlockSpec((None, GLA_WIDTH, QK_PAD), lambda b: (b, 0, 0))
    in_specs = [tokw(QK_PAD), tokw(QK_PAD), tokw(GLA_WIDTH), tokw(QK_PAD), tokw(QK_PAD), st, st]
    args = [q, k, v, gf, gb, sf0, sb0]
    aliases = {}
    n_in = len(args)
    if o_prev is not None:
        in_specs += [pl.BlockSpec(memory_space=pl.ANY)] * 2
        args += list(o_prev)
        aliases = {n_in: 0, n_in + 1: 1}

    def kern(*refs):
        _gla_kernel(*refs[:n_in], *refs[len(args):])

    st_shape = jax.ShapeDtypeStruct((n_batch, GLA_WIDTH, QK_PAD), F32)
    o_shape = jax.ShapeDtypeStruct((t_all, GLA_WIDTH), F32)
    o_f, o_b, s_f, s_b = pl.pallas_call(
        kern,
        grid=(n_batch,),
        in_specs=in_specs,
        out_specs=[tokw(GLA_WIDTH), tokw(GLA_WIDTH), st, st],
        out_shape=[o_shape, o_shape, st_shape, st_shape],
        input_output_aliases=aliases,
        compiler_params=_cparams(("parallel",)),
        name="gla",
    )(*args)
    return (o_f, o_b), s_f, s_b


def _na_kernel(q_ref, k_ref, v_ref, kc_ref, vc_ref, bias_ref, o_ref):
    n = q_ref.shape[0]
    rows = n // GRID_W
    nkeys = NA_KH * GRID_W
    lane = lax.broadcasted_iota(jnp.int32, (1, LANE), 1)
    first = lane < NA_DH
    kc = kc_ref[...]
    vc = vc_ref[...]

    def body(jb, carry):
        items = []
        for j in range(NA_UNROLL):
            r = jb * NA_UNROLL + j
            r0 = jnp.clip(r - NA_KH // 2, 0, rows - NA_KH)
            var = r - r0
            qrows = pl.ds(pl.multiple_of(r * GRID_W, GRID_W), GRID_W)
            krows = pl.ds(pl.multiple_of(r0 * GRID_W, GRID_W), nkeys)
            qr = q_ref[qrows, :]
            kb = k_ref[krows, :]
            q2 = jnp.concatenate([jnp.where(first, qr, jnp.zeros_like(qr)),
                                  jnp.where(first, jnp.zeros_like(qr), qr)], axis=0)
            items.append((qrows, krows, var, _dot_nt(q2, kb), _dot_nt(q2, kc)))
        probs = []
        for qrows, krows, var, s_loc, s_ctx in items:
            s_loc = s_loc + jnp.concatenate([bias_ref[0, var], bias_ref[1, var]], axis=0)
            m = jnp.maximum(jnp.max(s_loc, axis=-1, keepdims=True), jnp.max(s_ctx, axis=-1, keepdims=True))
            p_loc = jnp.exp(s_loc - m)
            p_ctx = jnp.exp(s_ctx - m)
            l = jnp.sum(p_loc, axis=-1, keepdims=True) + jnp.sum(p_ctx, axis=-1, keepdims=True)
            probs.append((p_loc.astype(BF16), p_ctx.astype(BF16), l))
        for (qrows, krows, var, _, _), (p_loc, p_ctx, l) in zip(items, probs):
            o = (_dot(p_loc, v_ref[krows, :]) + _dot(p_ctx, vc)) / l
            o_ref[qrows, :] = jnp.where(first, o[:GRID_W], o[GRID_W:]).astype(o_ref.dtype)
        return carry

    lax.fori_loop(0, rows // NA_UNROLL, body, 0)


def _na_latent(nq, nk, nv, bias_tab, *, seq, ctx_len, n_batch, ctx_blk0):
    t_all = nq.shape[0]
    npair = NA_HEADS // 2
    lat = pl.BlockSpec((seq, LANE), lambda b, p: (b, p))
    ctx = pl.BlockSpec((ctx_len, LANE), lambda b, p: (ctx_blk0 + b, p))
    bias = pl.BlockSpec((2,) + bias_tab.shape[1:], lambda b, p: (p, 0, 0, 0))
    return pl.pallas_call(
        _na_kernel,
        grid=(n_batch, npair),
        in_specs=[lat, lat, lat, ctx, ctx, bias],
        out_specs=lat,
        out_shape=jax.ShapeDtypeStruct((t_all, NA_WIDTH), BF16),
        compiler_params=_cparams(("parallel", "arbitrary")),
        name="na_latent",
    )(nq, nk, nv, nk, nv, bias_tab)


def _na_ctx_kernel(q_ref, k_ref, v_ref, o_in_ref, o_ref):
    del o_in_ref
    lane = lax.broadcasted_iota(jnp.int32, (1, LANE), 1)
    first = lane < NA_DH
    q = q_ref[...]
    k = k_ref[...]
    v = v_ref[...]
    res = []
    for h in range(2):
        sel = first if h == 0 else jnp.logical_not(first)
        qh = jnp.where(sel, q, jnp.zeros_like(q))
        s = _dot_nt(qh, k)
        m = jnp.max(s, axis=-1, keepdims=True)
        p = jnp.exp(s - m)
        l = jnp.sum(p, axis=-1, keepdims=True)
        res.append(_dot(p.astype(BF16), v) / l)
    o_ref[...] = jnp.where(first, res[0], res[1]).astype(o_ref.dtype)


def _na_context(nq, nk, nv, o_prev, *, ctx_len, n_batch, ctx_blk0):
    npair = NA_HEADS // 2
    ctx = pl.BlockSpec((ctx_len, LANE), lambda b, p: (ctx_blk0 + b, p))
    return pl.pallas_call(
        _na_ctx_kernel,
        grid=(n_batch, npair),
        in_specs=[ctx, ctx, ctx, pl.BlockSpec(memory_space=pl.ANY)],
        out_specs=ctx,
        out_shape=jax.ShapeDtypeStruct(o_prev.shape, o_prev.dtype),
        input_output_aliases={3: 0},
        compiler_params=_cparams(("parallel", "arbitrary")),
        name="na_context",
    )(nq, nk, nv, o_prev)


CONV_HALO = 16
CONV_ROWS = 128


def _conv_kernel(*refs, has_prev):
    if has_prev:
        h_ref, w_ref, b_ref, lg_ref, lb_ref, _, o_ref, pad_ref, sh_ref = refs
    else:
        h_ref, w_ref, b_ref, lg_ref, lb_ref, o_ref, pad_ref, sh_ref = refs
    n = h_ref.shape[0]
    zeros = jnp.zeros((CONV_HALO, CONV_WIDTH), F32)
    pad_ref[0:CONV_HALO, :] = zeros
    pad_ref[CONV_HALO + n:CONV_HALO + n + CONV_HALO, :] = zeros
    pad_ref[CONV_HALO:CONV_HALO + n, :] = h_ref[...]
    span = n + 2 * CONV_HALO - SUBLANES
    for s in range(SUBLANES):
        sh_ref[s, 0:span, :] = pad_ref[s:s + span, :]
    w = w_ref[...]
    off = CONV_HALO - CONV_K // 2

    def chunk(cidx, carry):
        base = pl.multiple_of(cidx * CONV_ROWS, CONV_ROWS)
        acc = jnp.zeros((CONV_ROWS, CONV_WIDTH), F32) + b_ref[...]
        for j in range(CONV_K):
            s = (off + j) % SUBLANES
            acc = acc + sh_ref[s, pl.ds(base + (off + j - s), CONV_ROWS), :] * w[j:j + 1, :]
        mu = jnp.mean(acc, axis=-1, keepdims=True)
        xc = acc - mu
        var = jnp.mean(xc * xc, axis=-1, keepdims=True)
        y = xc * lax.rsqrt(var + EPS) * lg_ref[...] + lb_ref[...]
        o_ref[pl.ds(base, CONV_ROWS), :] = (y * _sigmoid(y)).astype(o_ref.dtype)
        return carry

    lax.fori_loop(0, n // CONV_ROWS, chunk, 0)


def _conv(hcv, w, b, lg, lb, o_prev, *, seq, blk0, n_batch):
    t_all = hcv.shape[0]
    tok = pl.BlockSpec((seq, CONV_WIDTH), lambda i: (blk0 + i, 0))
    full = lambda a: pl.BlockSpec(a.shape, lambda i: (0,) * a.ndim)
    in_specs = [tok, full(w), full(b), full(lg), full(lb)]
    args = [hcv, w, b, lg, lb]
    aliases = {}
    if o_prev is not None:
        in_specs.append(pl.BlockSpec(memory_space=pl.ANY))
        args.append(o_prev)
        aliases = {5: 0}
    return pl.pallas_call(
        functools.partial(_conv_kernel, has_prev=o_prev is not None),
        grid=(n_batch,),
        in_specs=in_specs,
        out_specs=tok,
        out_shape=jax.ShapeDtypeStruct((t_all, CONV_WIDTH), BF16),
        scratch_shapes=[pltpu.VMEM((seq + 2 * CONV_HALO, CONV_WIDTH), F32),
                        pltpu.VMEM((SUBLANES, seq + 2 * CONV_HALO, CONV_WIDTH), F32)],
        input_output_aliases=aliases,
        compiler_params=_cparams(("parallel",)),
        name="conv",
    )(*args)


def _outproj_kernel(ogf_ref, ogb_ref, sg_ref, on_ref, oc_ref, x_ref, g2_ref, sh_ref, sc_ref, gn_ref, e_ref,
                    wo_ref, gffn_ref, rwh_ref, rwl_ref, xmid_ref, h2_ref, lg_ref):
    tile = x_ref.shape[0]
    parts = [pl.ds(p * (tile // ROW_PARTS), tile // ROW_PARTS) for p in range(ROW_PARTS)]
    e = e_ref[...]
    rwh = rwh_ref[...]

    ofs, mss = [], []
    for rows in parts:
        of = ogf_ref[rows, :] + ogb_ref[rows, :]
        sq_hi, sq_lo = _split(of * of)
        ofs.append(of)
        mss.append((_dot(sq_hi, e) + _dot(sq_lo, e)) * (1.0 / GLA_DV))
    ys = []
    for rows, of, ms in zip(parts, ofs, mss):
        og = of * lax.rsqrt(ms + EPS) * gn_ref[...] * sg_ref[rows, :].astype(F32)
        y = _dot(og.astype(BF16), wo_ref[0:GLA_WIDTH, :])
        y = y + _dot(on_ref[rows, :], wo_ref[GLA_WIDTH:GLA_WIDTH + NA_WIDTH, :])
        ys.append(y + _dot(oc_ref[rows, :], wo_ref[GLA_WIDTH + NA_WIDTH:, :]))
    for p, (rows, y) in enumerate(zip(parts, ys)):
        x = x_ref[rows, :] + g2_ref[...] * y
        xmid_ref[rows, :] = x
        ms2 = jnp.mean(x * x, axis=-1, keepdims=True)
        h2 = x * lax.rsqrt(ms2 + EPS) * gffn_ref[...]
        h2 = h2 * (1.0 + sc_ref[...]) + sh_ref[...]
        h_hi, h_lo = _split(h2)
        _store_subrows(h2_ref, _pack_rows(h2), 0, row0=p * (tile // ROW_PARTS))
        lg_ref[rows, :] = _dot(h_hi, rwh) + _dot(h_lo, rwh) + _dot(h_hi, rwl_ref[...])


def _outproj(o_gf, o_gb, sgate, o_n, o_c, x_all, mod5, layer, gnorm, e_mat, w_out, gffn, rw_hi, rw_lo, *, n_tiles,
             n_lat_tiles, tiles_per_seq, n_batch):
    d = x_all.shape[1]
    t_out = n_tiles * TOK_TILE

    def bidx(i):
        return jnp.where(i < n_lat_tiles, i // tiles_per_seq, n_batch)

    def modspec(j):
        return pl.BlockSpec((None, None, None, 1, d), lambda i: (layer, bidx(i), j, 0, 0))

    tok = lambda w: pl.BlockSpec((TOK_TILE, w), lambda i: (i, 0))
    full = lambda a: pl.BlockSpec(a.shape, lambda i: (0,) * a.ndim)
    return pl.pallas_call(
        _outproj_kernel,
        grid=(n_tiles,),
        in_specs=[tok(GLA_WIDTH), tok(GLA_WIDTH), tok(GLA_WIDTH), tok(NA_WIDTH), tok(CONV_WIDTH), tok(d),
                  modspec(2), modspec(3), modspec(4), full(gnorm), full(e_mat), full(w_out), full(gffn),
                  full(rw_hi), full(rw_lo)],
        out_specs=[tok(d), pl.BlockSpec((TOK_TILE // 8, SUB_Y, 8, LANE), lambda i: (i, 0, 0, 0)), tok(LANE)],
        out_shape=[jax.ShapeDtypeStruct((t_out, d), F32),
                   jax.ShapeDtypeStruct((t_out // 8, SUB_X, 8, LANE), jnp.uint32),
                   jax.ShapeDtypeStruct((t_out, LANE), F32)],
        compiler_params=_cparams(("parallel",)),
        name="outproj",
    )(o_gf, o_gb, sgate, o_n, o_c, x_all, mod5, mod5, mod5, gnorm, e_mat, w_out, gffn, rw_hi, rw_lo)


def _route_kernel(lg_ref, bias_ref, h2_in_ref, meta_ref, cnt_ref, h2w_ref, carry_ref):
    del h2_in_ref
    tile = lg_ref.shape[0]

    @pl.when(pl.program_id(0) == 0)
    def _():
        carry_ref[...] = jnp.zeros_like(carry_ref)

    lt = lg_ref[...].T
    aff = _sigmoid(lt[0:N_EXPERTS])
    sel = aff + bias_ref[...]
    s = [sel[e:e + 1] for e in range(N_EXPERTS)]
    a = [aff[e:e + 1] for e in range(N_EXPERTS)]

    def top2sum(v):
        best = v[0] + v[1]
        for i, j in PAIRS[1:]:
            best = jnp.maximum(best, v[i] + v[j])
        return best

    gs = [top2sum(s[4 * g:4 * g + 4]) for g in range(N_GROUPS)]
    gbest = jnp.zeros_like(gs[0], dtype=jnp.int32)
    gmax = gs[0]
    for g in range(1, N_GROUPS):
        upd = gs[g] > gmax
        gbest = jnp.where(upd, g, gbest)
        gmax = jnp.where(upd, gs[g], gmax)

    def pick(vals, j):
        out = vals[j]
        for g in range(1, N_GROUPS):
            out = jnp.where(gbest == g, vals[4 * g + j], out)
        return out

    sv = [pick(s, j) for j in range(EXPERTS_PER_GROUP)]
    av = [pick(a, j) for j in range(EXPERTS_PER_GROUP)]
    i1 = jnp.zeros_like(gbest)
    m1 = sv[0]
    for j in range(1, EXPERTS_PER_GROUP):
        upd = sv[j] > m1
        i1 = jnp.where(upd, j, i1)
        m1 = jnp.where(upd, sv[j], m1)
    i2 = jnp.full_like(gbest, -1)
    m2 = jnp.zeros_like(m1)
    for j in range(EXPERTS_PER_GROUP):
        upd = (i1 != j) & ((sv[j] > m2) | (i2 < 0))
        i2 = jnp.where(upd, j, i2)
        m2 = jnp.where(upd, sv[j], m2)
    ia = jnp.minimum(i1, i2)
    ib = jnp.maximum(i1, i2)
    pair = jnp.where(ia == 0, ib - 1, jnp.where(ia == 1, ib + 1, 5))
    cls = gbest * len(PAIRS) + pair

    def take(vals, idx):
        out = vals[0]
        for j in range(1, EXPERTS_PER_GROUP):
            out = jnp.where(idx == j, vals[j], out)
        return out

    w1 = take(av, i1)
    w2 = take(av, i2)
    tot = w1 + w2
    wa = jnp.where(i1 < i2, w1, w2) / tot
    wb = jnp.where(i1 < i2, w2, w1) / tot

    crow = lax.broadcasted_iota(jnp.int32, (32, tile), 0)
    oh = (crow == cls).astype(F32)
    us = lax.broadcasted_iota(jnp.int32, (tile, tile), 0)
    ut = lax.broadcasted_iota(jnp.int32, (tile, tile), 1)
    upper = (us < ut).astype(BF16)
    prefix = _dot(oh.astype(BF16), upper)
    carry = carry_ref[...]
    rank = jnp.sum(oh * (prefix + carry), axis=0, keepdims=True)
    carry_new = carry + jnp.sum(oh, axis=1, keepdims=True)
    carry_ref[...] = carry_new
    cnt_ref[...] = jnp.broadcast_to(carry_new, cnt_ref.shape)

    meta_ref[...] = jnp.zeros_like(meta_ref)
    meta_ref[0:1, :] = cls.astype(F32)
    meta_ref[1:2, :] = rank

    wrow = lax.broadcasted_iota(jnp.int32, (LANE, tile), 0)
    wmat = jnp.where(wrow == 0, wa, jnp.where(wrow == 1, wb, 0.0))
    wtok = lax.bitcast_convert_type(wmat.T, jnp.uint32)
    h2w_ref[:, 0, :, :] = wtok.reshape(tile // 8, 8, LANE)


def _route(logits, bias_col, h2_sub):
    t = logits.shape[0]
    nt = t // ROUTE_TILE
    meta, cnt, h2_sub = pl.pallas_call(
        _route_kernel,
        grid=(nt,),
        in_specs=[pl.BlockSpec((ROUTE_TILE, LANE), lambda i: (i, 0)),
                  pl.BlockSpec(bias_col.shape, lambda i: (0, 0)),
                  pl.BlockSpec(memory_space=pl.ANY)],
        out_specs=[pl.BlockSpec((8, ROUTE_TILE), lambda i: (0, i)),
                   pl.BlockSpec((32, LANE), lambda i: (0, 0)),
                   pl.BlockSpec((ROUTE_TILE // 8, 1, 8, LANE), lambda i: (i, SUB_Y, 0, 0))],
        out_shape=[jax.ShapeDtypeStruct((8, t), F32), jax.ShapeDtypeStruct((32, LANE), F32),
                   jax.ShapeDtypeStruct(h2_sub.shape, h2_sub.dtype)],
        scratch_shapes=[pltpu.VMEM((32, 1), F32)],
        input_output_aliases={2: 2},
        compiler_params=_cparams(("arbitrary",)),
        name="route",
    )(logits, bias_col, h2_sub)
    return meta, cnt, h2_sub


FF_TILE = 512


def _expert_kernel(ea_ref, eb_ref, nvalid_ref, xs_ref, wga_ref, wua_ref, wda_ref, wgb_ref, wub_ref, wdb_ref, y_ref):
    del ea_ref, eb_ref
    j = pl.program_id(0)
    nvalid = nvalid_ref[j]

    @pl.when(nvalid == 0)
    def _():
        y_ref[...] = jnp.zeros_like(y_ref)

    @pl.when(nvalid != 0)
    def _():
        rows = y_ref.shape[0] * 8
        live = lax.broadcasted_iota(jnp.int32, (rows, 1), 0) < nvalid
        x = jnp.where(live, _unpack_rows(_load_subrows(xs_ref, 0, SUB_Y)), 0.0).astype(BF16)
        ws = jnp.where(live, lax.bitcast_convert_type(xs_ref[:, SUB_Y, :, :].reshape(rows, LANE), F32), 0.0)
        ff = wga_ref.shape[1]

        items = [(w, f0) for w in ((wga_ref, wua_ref, wda_ref), (wgb_ref, wub_ref, wdb_ref))
                 for f0 in range(0, ff, FF_TILE)]

        def up(item):
            (wg_ref, wu_ref, _), f0 = item
            return _dot(x, wg_ref[:, f0:f0 + FF_TILE]), _dot(x, wu_ref[:, f0:f0 + FF_TILE])

        ups = [up(items[0]), up(items[1])]
        parts = []
        for c, ((_, _, wd_ref), f0) in enumerate(items):
            hg, hu = ups[c]
            hh = (hg * _sigmoid(hg) * hu).astype(BF16)
            if c + 2 < len(items):
                ups.append(up(items[c + 2]))
            parts.append(_dot(hh, wd_ref[f0:f0 + FF_TILE, :]))
        per = len(items) // 2
        ya = functools.reduce(lambda a, b: a + b, parts[:per])
        yb = functools.reduce(lambda a, b: a + b, parts[per:])
        _store_subrows(y_ref, _pack_rows(ya * ws[:, 0:1] + yb * ws[:, 1:2]), 0)


def _experts(xs_sub, ea, eb, nvalid, wg, wu, wd):
    nb = xs_sub.shape[0] * 8 // MOE_BLK
    d, ff = wg.shape[1], wg.shape[2]
    wspec_in = lambda which: pl.BlockSpec((None, d, ff), lambda j, ea, eb, v: ((ea, eb)[which][j], 0, 0))
    wspec_out = lambda which: pl.BlockSpec((None, ff, d), lambda j, ea, eb, v: ((ea, eb)[which][j], 0, 0))
    grid_spec = pltpu.PrefetchScalarGridSpec(
        num_scalar_prefetch=3,
        grid=(nb,),
        in_specs=[pl.BlockSpec((MOE_BLK // 8, SUB_X, 8, LANE), lambda j, ea, eb, v: (j, 0, 0, 0)),
                  wspec_in(0), wspec_in(0), wspec_out(0), wspec_in(1), wspec_in(1), wspec_out(1)],
        out_specs=pl.BlockSpec((MOE_BLK // 8, SUB_Y, 8, LANE), lambda j, ea, eb, v: (j, 0, 0, 0)),
    )
    return pl.pallas_call(
        _expert_kernel,
        grid_spec=grid_spec,
        out_shape=jax.ShapeDtypeStruct((nb * MOE_BLK // 8, SUB_Y, 8, LANE), jnp.uint32),
        compiler_params=_cparams(("arbitrary",)),
        name="experts",
    )(ea, eb, nvalid, xs_sub, wg, wu, wd, wg, wu, wd)


def _subrow_index(dest, nsub):
    t = dest.shape[0]
    d3 = dest.reshape(t // 8, 1, 8)
    j = jnp.arange(nsub, dtype=jnp.int32).reshape(1, nsub, 1)
    return ((d3 // 8) * (nsub * 8) + j * 8 + d3 % 8).reshape(t * nsub)


def _moe(h2_sub, logits, bias_col, wg, wu, wd):
    t = logits.shape[0]
    meta, cnt, h2_sub = _route(logits, bias_col, h2_sub)
    cls = meta[0].astype(jnp.int32)
    rank = meta[1].astype(jnp.int32)
    counts = cnt[:N_CLASSES, 0].astype(jnp.int32)
    padded = (counts + MOE_BLK - 1) // MOE_BLK * MOE_BLK
    pad_end = jnp.cumsum(padded)
    pad_start = pad_end - padded
    class_ids = jnp.arange(N_CLASSES, dtype=jnp.int32)
    dest = rank + jnp.sum(jnp.where(cls[:, None] == class_ids[None, :], pad_start[None, :], 0), axis=1)
    nb = t // MOE_BLK + N_CLASSES
    p_rows = nb * MOE_BLK
    blk_start = jnp.arange(nb, dtype=jnp.int32) * MOE_BLK
    valid = blk_start < pad_end[-1]
    blk_cls = jnp.sum((pad_end[None, :] <= blk_start[:, None]).astype(jnp.int32), axis=-1)
    last_cls = jnp.sum((pad_end <= pad_end[-1] - 1).astype(jnp.int32))
    blk_cls = jnp.minimum(jnp.where(valid, blk_cls, last_cls), N_CLASSES - 1)
    nvalid = jnp.where(valid, jnp.clip(pad_start[blk_cls] + counts[blk_cls] - blk_start, 0, MOE_BLK), 0)
    pair_a = jnp.array([p[0] for p in PAIRS], jnp.int32)
    pair_b = jnp.array([p[1] for p in PAIRS], jnp.int32)
    grp = blk_cls // len(PAIRS)
    ea = grp * EXPERTS_PER_GROUP + pair_a[blk_cls % len(PAIRS)]
    eb = grp * EXPERTS_PER_GROUP + pair_b[blk_cls % len(PAIRS)]

    xs = _sc_scatter(h2_sub.reshape(t * SUB_X, LANE), _subrow_index(dest, SUB_X), p_rows * SUB_X)
    ys = _experts(xs.reshape(p_rows // 8, SUB_X, 8, LANE), ea, eb, nvalid.astype(jnp.int32), wg, wu, wd)
    y = _sc_gather(ys.reshape(p_rows * SUB_Y, LANE), _subrow_index(dest, SUB_Y))
    return y.reshape(t // 8, SUB_Y, 8, LANE)


def _final_kernel(x_ref, y_ref, g5_ref, gf_ref, o_ref):
    x = x_ref[...] + g5_ref[...] * _unpack_rows(_load_subrows(y_ref, 0, SUB_Y))
    ms = jnp.mean(x * x, axis=-1, keepdims=True)
    o_ref[...] = x * lax.rsqrt(ms + EPS) * gf_ref[...]


def _final(x_mid, y, mod5, layer, g_final, *, tiles_per_seq):
    t, d = x_mid.shape
    tok = pl.BlockSpec((TOK_TILE, d), lambda i: (i, 0))
    return pl.pallas_call(
        _final_kernel,
        grid=(t // TOK_TILE,),
        in_specs=[tok, pl.BlockSpec((TOK_TILE // 8, SUB_Y, 8, LANE), lambda i: (i, 0, 0, 0)),
                  pl.BlockSpec((None, None, None, 1, d), lambda i: (layer, i // tiles_per_seq, 5, 0, 0)),
                  pl.BlockSpec(g_final.shape, lambda i: (0, 0))],
        out_specs=tok,
        out_shape=jax.ShapeDtypeStruct((t, d), F32),
        compiler_params=_cparams(("parallel",)),
        name="final_norm",
    )(x_mid, y, mod5, g_final)


def _rope_tables(seq):
    t = jnp.arange(seq)
    row = (t // GRID_W).astype(F32)
    col = (t % GRID_W).astype(F32)
    half = GLA_DK // 2
    inv = ROPE_BASE ** (-jnp.arange(0, half, 2, dtype=F32) / half)
    ang = jnp.concatenate([row[:, None] * inv, col[:, None] * inv], axis=-1)
    cos = jnp.repeat(jnp.cos(ang), 2, axis=-1)
    sin = jnp.repeat(jnp.sin(ang), 2, axis=-1)
    cos = jnp.tile(cos, (1, GLA_HEADS))
    sin = jnp.tile(sin, (1, GLA_HEADS))
    padw = QK_PAD - GLA_QK
    cos = jnp.pad(cos, ((0, 0), (0, padw)), constant_values=1.0)
    sin = jnp.pad(sin, ((0, 0), (0, padw)))
    cos = jnp.concatenate([cos, jnp.ones((TOK_TILE, QK_PAD), F32)], axis=0)
    sin = jnp.concatenate([sin, jnp.zeros((TOK_TILE, QK_PAD), F32)], axis=0)
    return cos, sin


def _pair_swap(w):
    w2 = w.reshape(w.shape[0], -1, 2)
    return jnp.stack([-w2[..., 1], w2[..., 0]], axis=-1).reshape(w.shape)


def _in_weights(w_in_l):
    d = w_in_l.shape[0]
    offs = np.cumsum([0, GLA_QK, GLA_QK, GLA_WIDTH, GLA_WIDTH, 2 * GLA_LOWRANK, NA_WIDTH, NA_WIDTH, NA_WIDTH,
                      2 * CONV_WIDTH])
    seg = [w_in_l[:, offs[i]:offs[i + 1]] for i in range(9)]
    padc = lambda w, n: jnp.pad(w, ((0, 0), (0, n - w.shape[1])))
    cols = [padc(seg[0], QK_PAD), padc(_pair_swap(seg[0]), QK_PAD), padc(seg[1], QK_PAD),
            padc(_pair_swap(seg[1]), QK_PAD), seg[2], seg[3], padc(seg[4], LANE), seg[5], seg[6], seg[7], seg[8]]
    w = jnp.concatenate(cols, axis=1).astype(BF16)
    assert w.shape == (d, IN_COLS_PAD)
    return w


def _gate_weights(wa_f, ba_f, wa_b, ba_b):
    wa = jnp.zeros((LANE, 2 * QK_PAD), F32)
    wa = wa.at[0:GLA_LOWRANK, 0:GLA_QK].set(wa_f)
    wa = wa.at[GLA_LOWRANK:2 * GLA_LOWRANK, QK_PAD:QK_PAD + GLA_QK].set(wa_b)
    ba = jnp.zeros((1, 2 * QK_PAD), F32)
    ba = ba.at[0, 0:GLA_QK].set(ba_f)
    ba = ba.at[0, QK_PAD:QK_PAD + GLA_QK].set(ba_b)
    return wa, ba


def _na_bias_tables(rpb):
    cq = np.arange(GRID_W)
    c0 = np.clip(cq - NA_KW // 2, 0, GRID_W - NA_KW)
    kc = np.arange(GRID_W)
    valid = (kc[None, :] >= c0[:, None]) & (kc[None, :] < c0[:, None] + NA_KW)
    dcol = np.clip(kc[None, :] - cq[:, None] + (NA_KW - 1), 0, 2 * NA_KW - 2)
    var = np.arange(NA_KH)
    drow = np.arange(NA_KH)[None, :] - var[:, None] + (NA_KH - 1)
    tab = rpb[:, drow][:, :, :, dcol]
    tab = jnp.where(valid[None, None, None], tab, NEG_BIG)
    tab = jnp.transpose(tab, (0, 1, 3, 2, 4))
    return tab.reshape(rpb.shape[0], NA_KH, GRID_W, NA_KH * GRID_W).astype(F32)


def _head_mean_matrix():
    h = np.arange(GLA_WIDTH) // GLA_DV
    return jnp.asarray((h[:, None] == h[None, :]).astype(np.float32), dtype=BF16)


def kernel(x, c, ctx, c_ctx, w_mod, b_mod, g_mix, g_ffn, w_in, gla_wa_f, gla_ba_f, gla_wa_b, gla_ba_b, gla_g_norm,
           na_rpb, conv_w, conv_b, conv_ln_g, conv_ln_b, w_out, router_w, router_bias, w_gate, w_up, w_down,
           g_final):
    bsz, seq, d = x.shape
    ctx_len = ctx.shape[1]
    depth = w_mod.shape[0]
    t_lat, t_ctx = bsz * seq, bsz * ctx_len
    assert seq % TOK_TILE == 0 and t_ctx % TOK_TILE == 0 and seq % ctx_len == 0
    assert d == 2 * SUB_Y * LANE
    assert seq // GRID_W >= NA_KH and (seq // GRID_W) % NA_UNROLL == 0 and ctx_len % GLA_CHUNK == 0
    tiles_per_seq = seq // TOK_TILE
    n_lat_tiles = t_lat // TOK_TILE
    n_all_tiles = (t_lat + t_ctx) // TOK_TILE
    ctx_blk0 = t_lat // ctx_len

    mod_rows = -(-(bsz + 1) // 8) * 8
    c_pad = jnp.zeros((mod_rows, d), F32).at[:bsz].set(c).at[bsz].set(c_ctx)
    mod = _modulation(c_pad, w_mod, b_mod)
    mod5 = mod.reshape(depth, mod_rows, 6, 1, d)

    cos_t, sin_t = _rope_tables(seq)
    e_mat = _head_mean_matrix()
    rw = jnp.pad(router_w, ((0, 0), (0, LANE - N_EXPERTS)))
    rw_hi = rw.astype(BF16)
    rw_lo = (rw - rw_hi.astype(F32)).astype(BF16)
    bias_col = router_bias.reshape(N_EXPERTS, 1).astype(F32)
    zero_state = jnp.zeros((bsz, GLA_WIDTH, QK_PAD), F32)

    x_all = jnp.concatenate([x.reshape(t_lat, d), ctx.reshape(t_ctx, d)], axis=0)
    y_moe = None
    for l in range(depth):
        last = l == depth - 1
        w_aug = _in_weights(w_in[l])
        wa_aug, ba_aug = _gate_weights(gla_wa_f[l], gla_ba_f[l], gla_wa_b[l], gla_ba_b[l])
        x_all, (q, k, v, sgate, gf, gb, nq, nk, nv, hcv) = _inproj(
            x_all, y_moe, mod5, l, g_mix[l].reshape(1, d), cos_t, sin_t, w_aug, wa_aug, ba_aug,
            n_lat_tiles=n_lat_tiles, tiles_per_seq=tiles_per_seq, n_batch=bsz, res_layer=l - 1)

        o_g, st_f, st_b = _gla(q, k, v, gf, gb, zero_state, zero_state, None, seq=ctx_len, blk0=ctx_blk0,
                               n_batch=bsz)
        o_g, _, _ = _gla(q, k, v, gf, gb, st_f, st_b, o_g, seq=seq, blk0=0, n_batch=bsz)

        bias_tab = _na_bias_tables(na_rpb[l])
        o_n = _na_latent(nq, nk, nv, bias_tab, seq=seq, ctx_len=ctx_len, n_batch=bsz, ctx_blk0=ctx_blk0)
        cw, cb = conv_w[l], conv_b[l].reshape(1, -1)
        clg, clb = conv_ln_g[l].reshape(1, -1), conv_ln_b[l].reshape(1, -1)
        o_c = _conv(hcv, cw, cb, clg, clb, None, seq=seq, blk0=0, n_batch=bsz)
        if not last:
            o_n = _na_context(nq, nk, nv, o_n, ctx_len=ctx_len, n_batch=bsz, ctx_blk0=ctx_blk0)
            o_c = _conv(hcv, cw, cb, clg, clb, o_c, seq=ctx_len, blk0=ctx_blk0, n_batch=bsz)

        n_tiles = n_lat_tiles if last else n_all_tiles
        x_mid, h2, logits = _outproj(
            o_g[0], o_g[1], sgate, o_n, o_c, x_all, mod5, l, gla_g_norm[l].reshape(1, -1), e_mat, w_out[l].astype(BF16),
            g_ffn[l].reshape(1, d), rw_hi, rw_lo, n_tiles=n_tiles, n_lat_tiles=n_lat_tiles,
            tiles_per_seq=tiles_per_seq, n_batch=bsz)
        y_moe = _moe(h2, logits, bias_col, w_gate[l].astype(BF16), w_up[l].astype(BF16), w_down[l].astype(BF16))
        x_all = x_mid

    out = _final(x_all, y_moe, mod5, depth - 1, g_final.reshape(1, d), tiles_per_seq=tiles_per_seq)
    return out.reshape(bsz, seq, d)
```

```python
import functools

import numpy as np
import jax
import jax.numpy as jnp
from jax import lax
from jax.experimental import pallas as pl
from jax.experimental.pallas import tpu as pltpu
from jax.experimental.pallas import tpu_sc as plsc

GRID_W = 64
EPS = 1e-6
GLA_HEADS, GLA_DK, GLA_DV = 4, 48, 96
GLA_QK = GLA_HEADS * GLA_DK
GLA_WIDTH = GLA_HEADS * GLA_DV
GLA_LOWRANK = 16
GLA_TAU = 16.0
ROPE_BASE = 10000.0
NA_HEADS, NA_DH = 6, 64
NA_WIDTH = NA_HEADS * NA_DH
NA_KH, NA_KW = 8, 16
CONV_WIDTH, CONV_K = 256, 31
N_EXPERTS, N_GROUPS, EXPERTS_PER_GROUP = 16, 4, 4
PAIRS = ((0, 1), (0, 2), (0, 3), (1, 2), (1, 3), (2, 3))
N_CLASSES = N_GROUPS * len(PAIRS)

LANE = 128
SUBLANES = 8
QK_PAD = 256
C_Q, C_K = 0, 256
C_V, C_GATE, C_A = 512, 896, 1280
C_NQ, C_NK, C_NV, C_CONV = 1408, 1792, 2176, 2560
IN_COLS_PAD = 3072

TOK_TILE = 512
ROW_PARTS = 2
GLA_CHUNK = 64
GLA_SUB = 16
GLA_STEPS = 2
MOE_BLK = 256
ROUTE_TILE = 512
NA_UNROLL = 4
NEG_BIG = -1e30
VMEM_LIMIT = 48 * 1024 * 1024

F32 = jnp.float32
BF16 = jnp.bfloat16


def _cparams(sem):
    return pltpu.CompilerParams(dimension_semantics=sem, vmem_limit_bytes=VMEM_LIMIT)


def _layer_spec(a, layer):
    return pl.BlockSpec((None,) + a.shape[1:], lambda *_: (layer,) + (0,) * (a.ndim - 1))


def _dot(a, b):
    return jnp.dot(a, b, preferred_element_type=F32)


def _dot_nt(a, b):
    return lax.dot_general(a, b, (((1,), (1,)), ((), ())), preferred_element_type=F32)


def _split(a):
    hi = a.astype(BF16)
    lo = (a - hi.astype(F32)).astype(BF16)
    return hi, lo


def _sigmoid(x):
    return 1.0 / (1.0 + jnp.exp(-x))


SUB_X, SUB_Y = 5, 4
SC_WINDOW = 256


def _pack_bf16_pairs(a, b):
    ua = lax.bitcast_convert_type(a.astype(BF16).astype(F32), jnp.uint32)
    ub = lax.bitcast_convert_type(b.astype(BF16).astype(F32), jnp.uint32)
    return jnp.bitwise_or(jnp.right_shift(ua, jnp.uint32(16)), ub)


def _unpack_bf16_pairs(w):
    lo = lax.bitcast_convert_type(jnp.left_shift(w, jnp.uint32(16)), F32)
    hi = lax.bitcast_convert_type(jnp.bitwise_and(w, jnp.uint32(0xFFFF0000)), F32)
    return lo, hi


def _store_subrows(ref, val, j0, row0=0):
    r = val.shape[0]
    for j in range(val.shape[1] // LANE):
        ref[row0 // 8:(row0 + r) // 8, j0 + j, :, :] = val[:, j * LANE:(j + 1) * LANE].reshape(r // 8, 8, LANE)


def _load_subrows(ref, j0, n, row0=0, rows=None):
    r = ref.shape[0] * 8 if rows is None else rows
    return jnp.concatenate([ref[row0 // 8:(row0 + r) // 8, j0 + j, :, :].reshape(r, LANE) for j in range(n)],
                           axis=1)


def _pack_rows(x):
    half = x.shape[1] // 2
    return _pack_bf16_pairs(x[:, :half], x[:, half:])


def _unpack_rows(w):
    lo, hi = _unpack_bf16_pairs(w)
    return jnp.concatenate([lo, hi], axis=1)


def _sc_mesh():
    return plsc.VectorSubcoreMesh(core_axis_name="c", subcore_axis_name="s")


def _sc_scatter(src, idx, n_out):
    n, w = src.shape
    idx2 = idx.reshape(1, n)

    @pl.kernel(out_type=jax.ShapeDtypeStruct((n_out, w), src.dtype), mesh=_sc_mesh(), scratch_types=[])
    def scatter_kernel(x_hbm, i_hbm, o_hbm):
        def body(x_vmem, i_vmem):
            pltpu.sync_copy(x_vmem, o_hbm.at[i_vmem.at[0]])

        pltpu.emit_pipeline(
            body,
            grid=(n // SC_WINDOW,),
            in_specs=[pl.BlockSpec((SC_WINDOW, w), index_map=lambda i: (i, 0)),
                      pl.BlockSpec((1, SC_WINDOW), index_map=lambda i: (0, i))],
            out_specs=[],
            core_axis_name=("c", "s"),
            dimension_semantics=(pltpu.PARALLEL,),
        )(x_hbm, i_hbm)

    return scatter_kernel(src, idx2)


def _sc_gather(src, idx):
    n = idx.shape[0]
    w = src.shape[1]
    idx2 = idx.reshape(1, n)

    @pl.kernel(out_type=jax.ShapeDtypeStruct((n, w), src.dtype), mesh=_sc_mesh())
    def gather_kernel(x_hbm, i_hbm, o_hbm):
        def body(i_vmem, o_vmem):
            pltpu.sync_copy(x_hbm.at[i_vmem.at[0]], o_vmem)

        pltpu.emit_pipeline(
            body,
            grid=(n // SC_WINDOW,),
            in_specs=[pl.BlockSpec((1, SC_WINDOW), index_map=lambda i: (0, i))],
            out_specs=[pl.BlockSpec((SC_WINDOW, w), index_map=lambda i: (i, 0))],
            core_axis_name=("c", "s"),
            dimension_semantics=(pltpu.PARALLEL,),
        )(i_hbm, o_hbm)

    return gather_kernel(src, idx2)


def _mod_kernel(c_ref, w_ref, b_ref, o_ref):
    cv = c_ref[...]
    s = cv * _sigmoid(cv)
    s_hi, s_lo = _split(s)
    w_hi, w_lo = _split(w_ref[...])
    o_ref[...] = _dot(s_hi, w_hi) + _dot(s_lo, w_hi) + _dot(s_hi, w_lo) + b_ref[...]


def _modulation(c_pad, w_mod, b_mod):
    depth, d, six_d = w_mod.shape
    rows = c_pad.shape[0]
    nt = 1536
    return pl.pallas_call(
        _mod_kernel,
        grid=(depth, six_d // nt),
        in_specs=[
            pl.BlockSpec((rows, d), lambda l, j: (0, 0)),
            pl.BlockSpec((None, d, nt), lambda l, j: (l, 0, j)),
            pl.BlockSpec((None, 1, nt), lambda l, j: (l, 0, j)),
        ],
        out_specs=pl.BlockSpec((None, rows, nt), lambda l, j: (l, 0, j)),
        out_shape=jax.ShapeDtypeStruct((depth, rows, six_d), F32),
        compiler_params=_cparams(("arbitrary", "arbitrary")),
        name="modulation",
    )(c_pad, w_mod, b_mod.reshape(depth, 1, six_d))


def _inproj_kernel(*refs, has_res):
    if has_res:
        x_ref, y_ref, g5_ref = refs[:3]
        refs = refs[3:]
    else:
        x_ref = refs[0]
        refs = refs[1:]
    (sh_ref, sc_ref, gmix_ref, cos_ref, sin_ref, w_ref, wa_ref, ba_ref) = refs[:8]
    outs = refs[8:]
    if has_res:
        xnew_ref = outs[0]
        outs = outs[1:]
    (q_ref, k_ref, v_ref, sg_ref, gf_ref, gb_ref, nq_ref, nk_ref, nv_ref, hcv_ref) = outs

    tile = x_ref.shape[0]
    part = tile // ROW_PARTS
    wa_hi, wa_lo = _split(wa_ref[...])
    even_lane = jnp.bitwise_and(lax.broadcasted_iota(jnp.int32, (1, LANE), 1), 1) == 0
    hbs = []
    for p in range(ROW_PARTS):
        rows = pl.ds(p * part, part)
        x = x_ref[rows, :]
        if has_res:
            x = x + g5_ref[...] * _unpack_rows(_load_subrows(y_ref, 0, SUB_Y, row0=p * part, rows=part))
            xnew_ref[rows, :] = x
        ms = jnp.mean(x * x, axis=-1, keepdims=True)
        h = x * lax.rsqrt(ms + EPS) * gmix_ref[...]
        h = h * (1.0 + sc_ref[...]) + sh_ref[...]
        hbs.append(h.astype(BF16))

    for p, hb in enumerate(hbs):
        rows = pl.ds(p * part, part)

        def proj(c0, n, hb=hb):
            return _dot(hb, w_ref[:, c0:c0 + n])

        cos = cos_ref[rows, :]
        sin = sin_ref[rows, :]

        def rope(t):
            halves = []
            for c0 in range(0, QK_PAD, LANE):
                th = t[:, c0:c0 + LANE]
                halves.append(jnp.where(even_lane, pltpu.roll(th, LANE - 1, 1), pltpu.roll(th, 1, 1)))
            return t * cos + jnp.concatenate(halves, axis=1) * sin

        a_hi, a_lo = _split(proj(C_A, LANE))
        q_ref[rows, :] = (rope(proj(C_Q, QK_PAD)) * (GLA_DK ** -0.5)).astype(BF16)
        k_ref[rows, :] = rope(proj(C_K, QK_PAD)).astype(BF16)

        z = _dot(a_hi, wa_hi) + _dot(a_lo, wa_hi) + _dot(a_hi, wa_lo) + ba_ref[...]
        logsig = jnp.minimum(z, 0.0) - jnp.log(1.0 + jnp.exp(-jnp.abs(z)))
        g = logsig * (1.0 / GLA_TAU)
        gf_ref[rows, :] = g[:, :QK_PAD]
        gb_ref[rows, :] = g[:, QK_PAD:]

        v_ref[rows, :] = proj(C_V, GLA_WIDTH).astype(BF16)
        gate = proj(C_GATE, GLA_WIDTH)
        sg_ref[rows, :] = (gate * _sigmoid(gate)).astype(BF16)
        nq_ref[rows, :] = (proj(C_NQ, NA_WIDTH) * (NA_DH ** -0.5)).astype(BF16)
        nk_ref[rows, :] = proj(C_NK, NA_WIDTH).astype(BF16)
        nv_ref[rows, :] = proj(C_NV, NA_WIDTH).astype(BF16)
        u = proj(C_CONV, 2 * CONV_WIDTH)
        hcv_ref[rows, :] = u[:, :CONV_WIDTH] * _sigmoid(u[:, CONV_WIDTH:])


def _inproj(x_all, y, mod5, layer, gmix, cos_t, sin_t, w_aug, wa_aug, ba_aug, *, n_lat_tiles, tiles_per_seq,
            n_batch, res_layer):
    t_all, d = x_all.shape
    nt = t_all // TOK_TILE
    has_res = y is not None

    def bidx(i):
        return jnp.where(i < n_lat_tiles, i // tiles_per_seq, n_batch)

    def ridx(i):
        return jnp.where(i < n_lat_tiles, i % tiles_per_seq, tiles_per_seq)

    def modspec(l, j):
        return pl.BlockSpec((None, None, None, 1, d), lambda i: (l, bidx(i), j, 0, 0))

    tok = lambda w: pl.BlockSpec((TOK_TILE, w), lambda i: (i, 0))
    full = lambda a: pl.BlockSpec(a.shape, lambda i: (0,) * a.ndim)

    in_specs = [tok(d)]
    args = [x_all]
    if has_res:
        in_specs += [pl.BlockSpec((TOK_TILE // 8, SUB_Y, 8, LANE), lambda i: (i, 0, 0, 0)), modspec(res_layer, 5)]
        args += [y, mod5]
    in_specs += [modspec(layer, 0), modspec(layer, 1), _layer_spec(gmix, layer),
                 pl.BlockSpec((TOK_TILE, QK_PAD), lambda i: (ridx(i), 0)),
                 pl.BlockSpec((TOK_TILE, QK_PAD), lambda i: (ridx(i), 0)),
                 _layer_spec(w_aug, layer), _layer_spec(wa_aug, layer), _layer_spec(ba_aug, layer)]
    args += [mod5, mod5, gmix, cos_t, sin_t, w_aug, wa_aug, ba_aug]

    out_widths = [(QK_PAD, BF16), (QK_PAD, BF16), (GLA_WIDTH, BF16), (GLA_WIDTH, BF16), (QK_PAD, F32), (QK_PAD, F32),
                  (NA_WIDTH, BF16), (NA_WIDTH, BF16), (NA_WIDTH, BF16), (CONV_WIDTH, F32)]
    if has_res:
        out_widths = [(d, F32)] + out_widths
    out_specs = [tok(w) for w, _ in out_widths]
    out_shape = [jax.ShapeDtypeStruct((t_all, w), dt) for w, dt in out_widths]
    res = pl.pallas_call(
        functools.partial(_inproj_kernel, has_res=has_res),
        grid=(nt,),
        in_specs=in_specs,
        out_specs=out_specs,
        out_shape=out_shape,
        compiler_params=_cparams(("parallel",)),
        name="inproj",
    )(*args)
    if has_res:
        return res[0], res[1:]
    return x_all, res


def _gla_masks():
    c, sub = GLA_CHUNK, GLA_SUB
    lane_qk = lax.broadcasted_iota(jnp.int32, (1, QK_PAD), 1)
    head_qk = ((lane_qk >= GLA_DK).astype(jnp.int32) + (lane_qk >= 2 * GLA_DK).astype(jnp.int32)
               + (lane_qk >= 3 * GLA_DK).astype(jnp.int32) + 4 * (lane_qk >= 4 * GLA_DK).astype(jnp.int32))
    row_h = jnp.right_shift(lax.broadcasted_iota(jnp.int32, (c, 1), 0), GLA_SUB.bit_length() - 1)
    hm = (row_h == head_qk).astype(F32)
    row_v = lax.broadcasted_iota(jnp.int32, (GLA_WIDTH, 1), 0)
    head_v = ((row_v >= GLA_DV).astype(jnp.int32) + (row_v >= 2 * GLA_DV).astype(jnp.int32)
              + (row_v >= 3 * GLA_DV).astype(jnp.int32))
    bd = (head_v == head_qk).astype(F32)
    lane_v = lax.broadcasted_iota(jnp.int32, (1, GLA_WIDTH), 1)
    vm = [((lane_v >= h * GLA_DV) & (lane_v < (h + 1) * GLA_DV)).astype(F32) for h in range(GLA_HEADS)]
    return hm, bd, vm


def _gla_steps(q, k, v, g, s_t, hm, bd, vm):
    c, sub = GLA_CHUNK, GLA_SUB
    nsub = c // sub
    dirs = (True, False)
    items = [(u, d) for u in range(GLA_STEPS) for d in range(2)]
    ri = lax.broadcasted_iota(jnp.int32, (c, c), 0)
    ci = lax.broadcasted_iota(jnp.int32, (c, c), 1)
    key_row = lax.broadcasted_iota(jnp.int32, (c, 1), 0)
    att_row = jnp.bitwise_and(ri, sub - 1)
    tri = [((ci <= ri) if fwd else (ci >= ri)).astype(BF16) for fwd in dirs]

    cums = {}
    for u, d in items:
        g_hi, g_lo = _split(g[u][d])
        cums[u, d] = _dot(tri[d], g_hi) + _dot(tri[d], g_lo)

    qe, kv, decay, atts = {}, {}, {}, {}
    for u, d in items:
        fwd = dirs[d]
        cum = cums[u, d]
        tot = cum[c - 1:c] if fwd else cum[0:1]
        qe[u, d] = (q[u][d] * jnp.exp(cum)).astype(BF16)
        k_end = (k[u][d] * jnp.exp(tot - cum)).astype(BF16)
        kv[u, d] = lax.dot_general(v[u][d], k_end, (((0,), (0,)), ((), ())), preferred_element_type=F32)
        decay[u, d] = jnp.exp(tot)
        att_d = []
        for i in range(nsub):
            lo, hi = i * sub, (i + 1) * sub
            if fwd:
                ref = cum[lo - 1:lo] if i > 0 else jnp.zeros((1, QK_PAD), F32)
                key_ok = key_row < hi
                causal = ci <= att_row + lo
            else:
                ref = cum[hi:hi + 1] if i < nsub - 1 else jnp.zeros((1, QK_PAD), F32)
                key_ok = key_row >= lo
                causal = ci >= att_row + lo
            qi = q[u][d][lo:hi] * jnp.exp(cum[lo:hi] - ref)
            qs = (jnp.concatenate([qi] * GLA_HEADS, axis=0) * hm).astype(BF16)
            ki = (k[u][d] * jnp.exp(jnp.where(key_ok, ref - cum, NEG_BIG))).astype(BF16)
            att = _dot_nt(qs, ki)
            att_d.append(jnp.where(causal, att, 0.0).astype(BF16))
        atts[u, d] = jnp.concatenate(att_d, axis=0)

    s = list(s_t)
    o_inter = {}
    for u, d in items:
        o_inter[u, d] = _dot_nt(qe[u, d], s[d].astype(BF16))
        s[d] = s[d] * decay[u, d] + bd * kv[u, d]

    outs = [[None, None] for _ in range(GLA_STEPS)]
    for u, d in items:
        r = _dot(atts[u, d], v[u][d])
        blocks = []
        for i in range(nsub):
            base = i * c
            oi = r[base:base + sub] * vm[0]
            for h in range(1, GLA_HEADS):
                oi = oi + r[base + h * sub:base + (h + 1) * sub] * vm[h]
            blocks.append(oi)
        outs[u][d] = o_inter[u, d] + jnp.concatenate(blocks, axis=0)
    return outs, s


def _gla_kernel(q_ref, k_ref, v_ref, gf_ref, gb_ref, sf0_ref, sb0_ref, of_ref, ob_ref, sf_ref, sb_ref):
    n = q_ref.shape[0]
    nc = n // GLA_CHUNK
    hm, bd, vm = _gla_masks()
    sf_ref[...] = sf0_ref[...]
    sb_ref[...] = sb0_ref[...]

    def body(j, carry):
        rows = [[pl.ds(pl.multiple_of(cidx * GLA_CHUNK, GLA_CHUNK), GLA_CHUNK)
                 for cidx in (j * GLA_STEPS + u, nc - 1 - (j * GLA_STEPS + u))] for u in range(GLA_STEPS)]
        q = [[q_ref[r, :].astype(F32) for r in ru] for ru in rows]
        k = [[k_ref[r, :].astype(F32) for r in ru] for ru in rows]
        v = [[v_ref[r, :] for r in ru] for ru in rows]
        g = [[gf_ref[ru[0], :], gb_ref[ru[1], :]] for ru in rows]
        outs, s_new = _gla_steps(q, k, v, g, [sf_ref[...], sb_ref[...]], hm, bd, vm)
        sf_ref[...] = s_new[0]
        sb_ref[...] = s_new[1]
        for u in range(GLA_STEPS):
            of_ref[rows[u][0], :] = outs[u][0]
            ob_ref[rows[u][1], :] = outs[u][1]
        return carry

    lax.fori_loop(0, nc // GLA_STEPS, body, 0)


def _gla(q, k, v, gf, gb, sf0, sb0, o_prev, *, seq, blk0, n_batch):
    t_all = q.shape[0]
    tokw = lambda w: pl.BlockSpec((seq, w), lambda b: (blk0 + b, 0))
    st = pl.BlockSpec((None, GLA_WIDTH, QK_PAD), lambda b: (b, 0, 0))
    in_specs = [tokw(QK_PAD), tokw(QK_PAD), tokw(GLA_WIDTH), tokw(QK_PAD), tokw(QK_PAD), st, st]
    args = [q, k, v, gf, gb, sf0, sb0]
    aliases = {}
    n_in = len(args)
    if o_prev is not None:
        in_specs += [pl.BlockSpec(memory_space=pl.ANY)] * 2
        args += list(o_prev)
        aliases = {n_in: 0, n_in + 1: 1}

    def kern(*refs):
        _gla_kernel(*refs[:n_in], *refs[len(args):])

    st_shape = jax.ShapeDtypeStruct((n_batch, GLA_WIDTH, QK_PAD), F32)
    o_shape = jax.ShapeDtypeStruct((t_all, GLA_WIDTH), F32)
    o_f, o_b, s_f, s_b = pl.pallas_call(
        kern,
        grid=(n_batch,),
        in_specs=in_specs,
        out_specs=[tokw(GLA_WIDTH), tokw(GLA_WIDTH), st, st],
        out_shape=[o_shape, o_shape, st_shape, st_shape],
        input_output_aliases=aliases,
        compiler_params=_cparams(("parallel",)),
        name="gla",
    )(*args)
    return (o_f, o_b), s_f, s_b


def _na_kernel(q_ref, k_ref, v_ref, kc_ref, vc_ref, bias_ref, o_ref):
    n = q_ref.shape[0]
    rows = n // GRID_W
    nkeys = NA_KH * GRID_W
    lane = lax.broadcasted_iota(jnp.int32, (1, LANE), 1)
    first = lane < NA_DH
    kc = kc_ref[...]
    vc = vc_ref[...]

    def body(jb, carry):
        items = []
        for j in range(NA_UNROLL):
            r = jb * NA_UNROLL + j
            r0 = jnp.clip(r - NA_KH // 2, 0, rows - NA_KH)
            var = r - r0
            qrows = pl.ds(pl.multiple_of(r * GRID_W, GRID_W), GRID_W)
            krows = pl.ds(pl.multiple_of(r0 * GRID_W, GRID_W), nkeys)
            qr = q_ref[qrows, :]
            kb = k_ref[krows, :]
            q2 = jnp.concatenate([jnp.where(first, qr, jnp.zeros_like(qr)),
                                  jnp.where(first, jnp.zeros_like(qr), qr)], axis=0)
            items.append((qrows, krows, var, _dot_nt(q2, kb), _dot_nt(q2, kc)))
        probs = []
        for qrows, krows, var, s_loc, s_ctx in items:
            s_loc = s_loc + jnp.concatenate([bias_ref[0, var], bias_ref[1, var]], axis=0)
            m = jnp.maximum(jnp.max(s_loc, axis=-1, keepdims=True), jnp.max(s_ctx, axis=-1, keepdims=True))
            p_loc = jnp.exp(s_loc - m)
            p_ctx = jnp.exp(s_ctx - m)
            l = jnp.sum(p_loc, axis=-1, keepdims=True) + jnp.sum(p_ctx, axis=-1, keepdims=True)
            probs.append((p_loc.astype(BF16), p_ctx.astype(BF16), l))
        for (qrows, krows, var, _, _), (p_loc, p_ctx, l) in zip(items, probs):
            o = (_dot(p_loc, v_ref[krows, :]) + _dot(p_ctx, vc)) / l
            o_ref[qrows, :] = jnp.where(first, o[:GRID_W], o[GRID_W:]).astype(o_ref.dtype)
        return carry

    lax.fori_loop(0, rows // NA_UNROLL, body, 0)


def _na_latent(nq, nk, nv, bias_tab, layer, *, seq, ctx_len, n_batch, ctx_blk0):
    t_all = nq.shape[0]
    npair = NA_HEADS // 2
    lat = pl.BlockSpec((seq, LANE), lambda b, p: (b, p))
    ctx = pl.BlockSpec((ctx_len, LANE), lambda b, p: (ctx_blk0 + b, p))
    bias = pl.BlockSpec((None, 2) + bias_tab.shape[2:], lambda b, p: (layer, p, 0, 0, 0))
    return pl.pallas_call(
        _na_kernel,
        grid=(n_batch, npair),
        in_specs=[lat, lat, lat, ctx, ctx, bias],
        out_specs=lat,
        out_shape=jax.ShapeDtypeStruct((t_all, NA_WIDTH), BF16),
        compiler_params=_cparams(("parallel", "arbitrary")),
        name="na_latent",
    )(nq, nk, nv, nk, nv, bias_tab)


def _na_ctx_kernel(q_ref, k_ref, v_ref, o_in_ref, o_ref):
    del o_in_ref
    lane = lax.broadcasted_iota(jnp.int32, (1, LANE), 1)
    first = lane < NA_DH
    q = q_ref[...]
    k = k_ref[...]
    v = v_ref[...]
    res = []
    for h in range(2):
        sel = first if h == 0 else jnp.logical_not(first)
        qh = jnp.where(sel, q, jnp.zeros_like(q))
        s = _dot_nt(qh, k)
        m = jnp.max(s, axis=-1, keepdims=True)
        p = jnp.exp(s - m)
        l = jnp.sum(p, axis=-1, keepdims=True)
        res.append(_dot(p.astype(BF16), v) / l)
    o_ref[...] = jnp.where(first, res[0], res[1]).astype(o_ref.dtype)


def _na_context(nq, nk, nv, o_prev, *, ctx_len, n_batch, ctx_blk0):
    npair = NA_HEADS // 2
    ctx = pl.BlockSpec((ctx_len, LANE), lambda b, p: (ctx_blk0 + b, p))
    return pl.pallas_call(
        _na_ctx_kernel,
        grid=(n_batch, npair),
        in_specs=[ctx, ctx, ctx, pl.BlockSpec(memory_space=pl.ANY)],
        out_specs=ctx,
        out_shape=jax.ShapeDtypeStruct(o_prev.shape, o_prev.dtype),
        input_output_aliases={3: 0},
        compiler_params=_cparams(("parallel", "arbitrary")),
        name="na_context",
    )(nq, nk, nv, o_prev)


CONV_HALO = 16
CONV_ROWS = 128


def _conv_kernel(*refs, has_prev):
    if has_prev:
        h_ref, w_ref, b_ref, lg_ref, lb_ref, _, o_ref, pad_ref, sh_ref = refs
    else:
        h_ref, w_ref, b_ref, lg_ref, lb_ref, o_ref, pad_ref, sh_ref = refs
    n = h_ref.shape[0]
    zeros = jnp.zeros((CONV_HALO, CONV_WIDTH), F32)
    pad_ref[0:CONV_HALO, :] = zeros
    pad_ref[CONV_HALO + n:CONV_HALO + n + CONV_HALO, :] = zeros
    pad_ref[CONV_HALO:CONV_HALO + n, :] = h_ref[...]
    span = n + 2 * CONV_HALO - SUBLANES
    for s in range(SUBLANES):
        sh_ref[s, 0:span, :] = pad_ref[s:s + span, :]
    w = w_ref[...]
    off = CONV_HALO - CONV_K // 2

    def chunk(cidx, carry):
        base = pl.multiple_of(cidx * CONV_ROWS, CONV_ROWS)
        acc = jnp.zeros((CONV_ROWS, CONV_WIDTH), F32) + b_ref[...]
        for j in range(CONV_K):
            s = (off + j) % SUBLANES
            acc = acc + sh_ref[s, pl.ds(base + (off + j - s), CONV_ROWS), :] * w[j:j + 1, :]
        mu = jnp.mean(acc, axis=-1, keepdims=True)
        xc = acc - mu
        var = jnp.mean(xc * xc, axis=-1, keepdims=True)
        y = xc * lax.rsqrt(var + EPS) * lg_ref[...] + lb_ref[...]
        o_ref[pl.ds(base, CONV_ROWS), :] = (y * _sigmoid(y)).astype(o_ref.dtype)
        return carry

    lax.fori_loop(0, n // CONV_ROWS, chunk, 0)


def _conv(hcv, w, b, lg, lb, layer, o_prev, *, seq, blk0, n_batch):
    t_all = hcv.shape[0]
    tok = pl.BlockSpec((seq, CONV_WIDTH), lambda i: (blk0 + i, 0))
    full = lambda a: pl.BlockSpec(a.shape, lambda i: (0,) * a.ndim)
    in_specs = [tok] + [_layer_spec(a, layer) for a in (w, b, lg, lb)]
    args = [hcv, w, b, lg, lb]
    aliases = {}
    if o_prev is not None:
        in_specs.append(pl.BlockSpec(memory_space=pl.ANY))
        args.append(o_prev)
        aliases = {5: 0}
    return pl.pallas_call(
        functools.partial(_conv_kernel, has_prev=o_prev is not None),
        grid=(n_batch,),
        in_specs=in_specs,
        out_specs=tok,
        out_shape=jax.ShapeDtypeStruct((t_all, CONV_WIDTH), BF16),
        scratch_shapes=[pltpu.VMEM((seq + 2 * CONV_HALO, CONV_WIDTH), F32),
                        pltpu.VMEM((SUBLANES, seq + 2 * CONV_HALO, CONV_WIDTH), F32)],
        input_output_aliases=aliases,
        compiler_params=_cparams(("parallel",)),
        name="conv",
    )(*args)


def _outproj_kernel(ogf_ref, ogb_ref, sg_ref, on_ref, oc_ref, x_ref, g2_ref, sh_ref, sc_ref, gn_ref, e_ref,
                    wo_ref, gffn_ref, rw_ref, xmid_ref, h2_ref, lg_ref):
    tile = x_ref.shape[0]
    parts = [pl.ds(p * (tile // ROW_PARTS), tile // ROW_PARTS) for p in range(ROW_PARTS)]
    e = e_ref[...]
    rw = rw_ref[...]

    ofs, mss = [], []
    for rows in parts:
        of = ogf_ref[rows, :] + ogb_ref[rows, :]
        sq_hi, sq_lo = _split(of * of)
        ofs.append(of)
        mss.append((_dot(sq_hi, e) + _dot(sq_lo, e)) * (1.0 / GLA_DV))
    ys = []
    for rows, of, ms in zip(parts, ofs, mss):
        og = of * lax.rsqrt(ms + EPS) * gn_ref[...] * sg_ref[rows, :].astype(F32)
        y = _dot(og.astype(BF16), wo_ref[0:GLA_WIDTH, :])
        y = y + _dot(on_ref[rows, :], wo_ref[GLA_WIDTH:GLA_WIDTH + NA_WIDTH, :])
        ys.append(y + _dot(oc_ref[rows, :], wo_ref[GLA_WIDTH + NA_WIDTH:, :]))
    for p, (rows, y) in enumerate(zip(parts, ys)):
        x = x_ref[rows, :] + g2_ref[...] * y
        xmid_ref[rows, :] = x
        ms2 = jnp.mean(x * x, axis=-1, keepdims=True)
        h2 = x * lax.rsqrt(ms2 + EPS) * gffn_ref[...]
        h2 = h2 * (1.0 + sc_ref[...]) + sh_ref[...]
        h_hi, h_lo = _split(h2)
        _store_subrows(h2_ref, _pack_rows(h2), 0, row0=p * (tile // ROW_PARTS))
        hw = _dot(h_hi, rw)
        lg_ref[rows, :] = hw[:, :LANE] + hw[:, LANE:] + _dot(h_lo, rw[:, :LANE])


def _outproj(o_gf, o_gb, sgate, o_n, o_c, x_all, mod5, layer, gnorm, e_mat, w_out, gffn, rw_split, *, n_tiles,
             n_lat_tiles, tiles_per_seq, n_batch):
    d = x_all.shape[1]
    t_out = n_tiles * TOK_TILE

    def bidx(i):
        return jnp.where(i < n_lat_tiles, i // tiles_per_seq, n_batch)

    def modspec(j):
        return pl.BlockSpec((None, None, None, 1, d), lambda i: (layer, bidx(i), j, 0, 0))

    tok = lambda w: pl.BlockSpec((TOK_TILE, w), lambda i: (i, 0))
    full = lambda a: pl.BlockSpec(a.shape, lambda i: (0,) * a.ndim)
    return pl.pallas_call(
        _outproj_kernel,
        grid=(n_tiles,),
        in_specs=[tok(GLA_WIDTH), tok(GLA_WIDTH), tok(GLA_WIDTH), tok(NA_WIDTH), tok(CONV_WIDTH), tok(d),
                  modspec(2), modspec(3), modspec(4), _layer_spec(gnorm, layer), full(e_mat),
                  _layer_spec(w_out, layer), _layer_spec(gffn, layer),
                  full(rw_split)],
        out_specs=[tok(d), pl.BlockSpec((TOK_TILE // 8, SUB_Y, 8, LANE), lambda i: (i, 0, 0, 0)), tok(LANE)],
        out_shape=[jax.ShapeDtypeStruct((t_out, d), F32),
                   jax.ShapeDtypeStruct((t_out // 8, SUB_X, 8, LANE), jnp.uint32),
                   jax.ShapeDtypeStruct((t_out, LANE), F32)],
        compiler_params=_cparams(("parallel",)),
        name="outproj",
    )(o_gf, o_gb, sgate, o_n, o_c, x_all, mod5, mod5, mod5, gnorm, e_mat, w_out, gffn, rw_split)


def _route_kernel(lg_ref, bias_ref, h2_in_ref, meta_ref, cnt_ref, h2w_ref, carry_ref):
    del h2_in_ref
    tile = lg_ref.shape[0]

    @pl.when(pl.program_id(0) == 0)
    def _():
        carry_ref[...] = jnp.zeros_like(carry_ref)

    lt = lg_ref[...].T
    aff = _sigmoid(lt[0:N_EXPERTS])
    sel = aff + bias_ref[...]
    s = [sel[e:e + 1] for e in range(N_EXPERTS)]
    a = [aff[e:e + 1] for e in range(N_EXPERTS)]

    def top2sum(v):
        best = v[0] + v[1]
        for i, j in PAIRS[1:]:
            best = jnp.maximum(best, v[i] + v[j])
        return best

    gs = [top2sum(s[4 * g:4 * g + 4]) for g in range(N_GROUPS)]
    gbest = jnp.zeros_like(gs[0], dtype=jnp.int32)
    gmax = gs[0]
    for g in range(1, N_GROUPS):
        upd = gs[g] > gmax
        gbest = jnp.where(upd, g, gbest)
        gmax = jnp.where(upd, gs[g], gmax)

    def pick(vals, j):
        out = vals[j]
        for g in range(1, N_GROUPS):
            out = jnp.where(gbest == g, vals[4 * g + j], out)
        return out

    sv = [pick(s, j) for j in range(EXPERTS_PER_GROUP)]
    av = [pick(a, j) for j in range(EXPERTS_PER_GROUP)]
    i1 = jnp.zeros_like(gbest)
    m1 = sv[0]
    for j in range(1, EXPERTS_PER_GROUP):
        upd = sv[j] > m1
        i1 = jnp.where(upd, j, i1)
        m1 = jnp.where(upd, sv[j], m1)
    i2 = jnp.full_like(gbest, -1)
    m2 = jnp.zeros_like(m1)
    for j in range(EXPERTS_PER_GROUP):
        upd = (i1 != j) & ((sv[j] > m2) | (i2 < 0))
        i2 = jnp.where(upd, j, i2)
        m2 = jnp.where(upd, sv[j], m2)
    ia = jnp.minimum(i1, i2)
    ib = jnp.maximum(i1, i2)
    pair = jnp.where(ia == 0, ib - 1, jnp.where(ia == 1, ib + 1, 5))
    cls = gbest * len(PAIRS) + pair

    def take(vals, idx):
        out = vals[0]
        for j in range(1, EXPERTS_PER_GROUP):
            out = jnp.where(idx == j, vals[j], out)
        return out

    w1 = take(av, i1)
    w2 = take(av, i2)
    tot = w1 + w2
    wa = jnp.where(i1 < i2, w1, w2) / tot
    wb = jnp.where(i1 < i2, w2, w1) / tot

    crow = lax.broadcasted_iota(jnp.int32, (32, tile), 0)
    oh = (crow == cls).astype(F32)
    us = lax.broadcasted_iota(jnp.int32, (tile, tile), 0)
    ut = lax.broadcasted_iota(jnp.int32, (tile, tile), 1)
    upper = (us < ut).astype(BF16)
    prefix = _dot(oh.astype(BF16), upper)
    carry = carry_ref[...]
    rank = jnp.sum(oh * (prefix + carry), axis=0, keepdims=True)
    carry_new = carry + jnp.sum(oh, axis=1, keepdims=True)
    carry_ref[...] = carry_new
    cnt_ref[...] = jnp.broadcast_to(carry_new, cnt_ref.shape)

    meta_ref[...] = jnp.zeros_like(meta_ref)
    meta_ref[0:1, :] = cls.astype(F32)
    meta_ref[1:2, :] = rank

    wrow = lax.broadcasted_iota(jnp.int32, (LANE, tile), 0)
    wmat = jnp.where(wrow == 0, wa, jnp.where(wrow == 1, wb, 0.0))
    wtok = lax.bitcast_convert_type(wmat.T, jnp.uint32)
    h2w_ref[:, 0, :, :] = wtok.reshape(tile // 8, 8, LANE)


def _route(logits, bias_col, h2_sub):
    t = logits.shape[0]
    nt = t // ROUTE_TILE
    meta, cnt, h2_sub = pl.pallas_call(
        _route_kernel,
        grid=(nt,),
        in_specs=[pl.BlockSpec((ROUTE_TILE, LANE), lambda i: (i, 0)),
                  pl.BlockSpec(bias_col.shape, lambda i: (0, 0)),
                  pl.BlockSpec(memory_space=pl.ANY)],
        out_specs=[pl.BlockSpec((8, ROUTE_TILE), lambda i: (0, i)),
                   pl.BlockSpec((32, LANE), lambda i: (0, 0)),
                   pl.BlockSpec((ROUTE_TILE // 8, 1, 8, LANE), lambda i: (i, SUB_Y, 0, 0))],
        out_shape=[jax.ShapeDtypeStruct((8, t), F32), jax.ShapeDtypeStruct((32, LANE), F32),
                   jax.ShapeDtypeStruct(h2_sub.shape, h2_sub.dtype)],
        scratch_shapes=[pltpu.VMEM((32, 1), F32)],
        input_output_aliases={2: 2},
        compiler_params=_cparams(("arbitrary",)),
        name="route",
    )(logits, bias_col, h2_sub)
    return meta, cnt, h2_sub


FF_TILE = 512


def _expert_kernel(ea_ref, eb_ref, nvalid_ref, xs_ref, wga_ref, wua_ref, wda_ref, wgb_ref, wub_ref, wdb_ref, y_ref):
    del ea_ref, eb_ref
    j = pl.program_id(0)
    nvalid = nvalid_ref[j]

    @pl.when(nvalid == 0)
    def _():
        y_ref[...] = jnp.zeros_like(y_ref)

    @pl.when(nvalid != 0)
    def _():
        rows = y_ref.shape[0] * 8
        live = lax.broadcasted_iota(jnp.int32, (rows, 1), 0) < nvalid
        x = jnp.where(live, _unpack_rows(_load_subrows(xs_ref, 0, SUB_Y)), 0.0).astype(BF16)
        ws = jnp.where(live, lax.bitcast_convert_type(xs_ref[:, SUB_Y, :, :].reshape(rows, LANE), F32), 0.0)
        ff = wga_ref.shape[1]

        items = [(w, f0) for w in ((wga_ref, wua_ref, wda_ref), (wgb_ref, wub_ref, wdb_ref))
                 for f0 in range(0, ff, FF_TILE)]

        def up(item):
            (wg_ref, wu_ref, _), f0 = item
            return _dot(x, wg_ref[:, f0:f0 + FF_TILE]), _dot(x, wu_ref[:, f0:f0 + FF_TILE])

        ups = [up(items[0]), up(items[1])]
        parts = []
        for c, ((_, _, wd_ref), f0) in enumerate(items):
            hg, hu = ups[c]
            hh = (hg * _sigmoid(hg) * hu).astype(BF16)
            if c + 2 < len(items):
                ups.append(up(items[c + 2]))
            parts.append(_dot(hh, wd_ref[f0:f0 + FF_TILE, :]))
        per = len(items) // 2
        ya = functools.reduce(lambda a, b: a + b, parts[:per])
        yb = functools.reduce(lambda a, b: a + b, parts[per:])
        _store_subrows(y_ref, _pack_rows(ya * ws[:, 0:1] + yb * ws[:, 1:2]), 0)


def _experts(xs_sub, ea, eb, nvalid, wg, wu, wd, layer):
    nb = xs_sub.shape[0] * 8 // MOE_BLK
    d, ff = wg.shape[2], wg.shape[3]
    wspec_in = lambda which: pl.BlockSpec((None, None, d, ff),
                                          lambda j, ea, eb, v: (layer, (ea, eb)[which][j], 0, 0))
    wspec_out = lambda which: pl.BlockSpec((None, None, ff, d),
                                           lambda j, ea, eb, v: (layer, (ea, eb)[which][j], 0, 0))
    grid_spec = pltpu.PrefetchScalarGridSpec(
        num_scalar_prefetch=3,
        grid=(nb,),
        in_specs=[pl.BlockSpec((MOE_BLK // 8, SUB_X, 8, LANE), lambda j, ea, eb, v: (j, 0, 0, 0)),
                  wspec_in(0), wspec_in(0), wspec_out(0), wspec_in(1), wspec_in(1), wspec_out(1)],
        out_specs=pl.BlockSpec((MOE_BLK // 8, SUB_Y, 8, LANE), lambda j, ea, eb, v: (j, 0, 0, 0)),
    )
    return pl.pallas_call(
        _expert_kernel,
        grid_spec=grid_spec,
        out_shape=jax.ShapeDtypeStruct((nb * MOE_BLK // 8, SUB_Y, 8, LANE), jnp.uint32),
        compiler_params=_cparams(("arbitrary",)),
        name="experts",
    )(ea, eb, nvalid, xs_sub, wg, wu, wd, wg, wu, wd)


def _subrow_index(dest, nsub):
    t = dest.shape[0]
    d3 = dest.reshape(t // 8, 1, 8)
    j = jnp.arange(nsub, dtype=jnp.int32).reshape(1, nsub, 1)
    return ((d3 // 8) * (nsub * 8) + j * 8 + d3 % 8).reshape(t * nsub)


def _moe(h2_sub, logits, bias_col, wg, wu, wd, layer):
    t = logits.shape[0]
    meta, cnt, h2_sub = _route(logits, bias_col, h2_sub)
    cls = meta[0].astype(jnp.int32)
    rank = meta[1].astype(jnp.int32)
    counts = cnt[:N_CLASSES, 0].astype(jnp.int32)
    padded = (counts + MOE_BLK - 1) // MOE_BLK * MOE_BLK
    pad_end = jnp.cumsum(padded)
    pad_start = pad_end - padded
    class_ids = jnp.arange(N_CLASSES, dtype=jnp.int32)
    dest = rank + jnp.sum(jnp.where(cls[:, None] == class_ids[None, :], pad_start[None, :], 0), axis=1)
    nb = t // MOE_BLK + N_CLASSES
    p_rows = nb * MOE_BLK
    blk_start = jnp.arange(nb, dtype=jnp.int32) * MOE_BLK
    valid = blk_start < pad_end[-1]
    blk_cls = jnp.sum((pad_end[None, :] <= blk_start[:, None]).astype(jnp.int32), axis=-1)
    last_cls = jnp.sum((pad_end <= pad_end[-1] - 1).astype(jnp.int32))
    blk_cls = jnp.minimum(jnp.where(valid, blk_cls, last_cls), N_CLASSES - 1)
    nvalid = jnp.where(valid, jnp.clip(pad_start[blk_cls] + counts[blk_cls] - blk_start, 0, MOE_BLK), 0)
    pair_a = jnp.array([p[0] for p in PAIRS], jnp.int32)
    pair_b = jnp.array([p[1] for p in PAIRS], jnp.int32)
    grp = blk_cls // len(PAIRS)
    ea = grp * EXPERTS_PER_GROUP + pair_a[blk_cls % len(PAIRS)]
    eb = grp * EXPERTS_PER_GROUP + pair_b[blk_cls % len(PAIRS)]

    xs = _sc_scatter(h2_sub.reshape(t * SUB_X, LANE), _subrow_index(dest, SUB_X), p_rows * SUB_X)
    ys = _experts(xs.reshape(p_rows // 8, SUB_X, 8, LANE), ea, eb, nvalid.astype(jnp.int32), wg, wu, wd, layer)
    y = _sc_gather(ys.reshape(p_rows * SUB_Y, LANE), _subrow_index(dest, SUB_Y))
    return y.reshape(t // 8, SUB_Y, 8, LANE)


def _final_kernel(x_ref, y_ref, g5_ref, gf_ref, o_ref):
    x = x_ref[...] + g5_ref[...] * _unpack_rows(_load_subrows(y_ref, 0, SUB_Y))
    ms = jnp.mean(x * x, axis=-1, keepdims=True)
    o_ref[...] = x * lax.rsqrt(ms + EPS) * gf_ref[...]


def _final(x_mid, y, mod5, layer, g_final, *, tiles_per_seq):
    t, d = x_mid.shape
    tok = pl.BlockSpec((TOK_TILE, d), lambda i: (i, 0))
    return pl.pallas_call(
        _final_kernel,
        grid=(t // TOK_TILE,),
        in_specs=[tok, pl.BlockSpec((TOK_TILE // 8, SUB_Y, 8, LANE), lambda i: (i, 0, 0, 0)),
                  pl.BlockSpec((None, None, None, 1, d), lambda i: (layer, i // tiles_per_seq, 5, 0, 0)),
                  pl.BlockSpec(g_final.shape, lambda i: (0, 0))],
        out_specs=tok,
        out_shape=jax.ShapeDtypeStruct((t, d), F32),
        compiler_params=_cparams(("parallel",)),
        name="final_norm",
    )(x_mid, y, mod5, g_final)


def _rope_tables(seq):
    t = jnp.arange(seq)
    row = (t // GRID_W).astype(F32)
    col = (t % GRID_W).astype(F32)
    half = GLA_DK // 2
    inv = ROPE_BASE ** (-jnp.arange(0, half, 2, dtype=F32) / half)
    ang = jnp.concatenate([row[:, None] * inv, col[:, None] * inv], axis=-1)
    cos = jnp.repeat(jnp.cos(ang), 2, axis=-1)
    sin = (jnp.sin(ang)[:, :, None] * jnp.array([-1.0, 1.0], F32)).reshape(seq, GLA_DK)
    cos = jnp.tile(cos, (1, GLA_HEADS))
    sin = jnp.tile(sin, (1, GLA_HEADS))
    padw = QK_PAD - GLA_QK
    cos = jnp.pad(cos, ((0, 0), (0, padw)), constant_values=1.0)
    sin = jnp.pad(sin, ((0, 0), (0, padw)))
    cos = jnp.concatenate([cos, jnp.ones((TOK_TILE, QK_PAD), F32)], axis=0)
    sin = jnp.concatenate([sin, jnp.zeros((TOK_TILE, QK_PAD), F32)], axis=0)
    return cos, sin


def _pad_last(w, n):
    return jnp.pad(w, [(0, 0)] * (w.ndim - 1) + [(0, n - w.shape[-1])])


def _in_weights(w_in):
    offs = np.cumsum([0, GLA_QK, GLA_QK, GLA_WIDTH, GLA_WIDTH, 2 * GLA_LOWRANK, NA_WIDTH, NA_WIDTH, NA_WIDTH,
                      2 * CONV_WIDTH])
    seg = [w_in[..., offs[i]:offs[i + 1]] for i in range(9)]
    cols = [_pad_last(seg[0], QK_PAD), _pad_last(seg[1], QK_PAD), seg[2], seg[3], _pad_last(seg[4], LANE), seg[5],
            seg[6], seg[7], seg[8]]
    w = jnp.concatenate(cols, axis=-1).astype(BF16)
    assert w.shape[-1] == IN_COLS_PAD
    return w


def _gate_weights(wa_f, ba_f, wa_b, ba_b):
    depth = wa_f.shape[0]
    zero = jnp.zeros((depth, GLA_LOWRANK, QK_PAD), F32)
    top = jnp.concatenate([_pad_last(wa_f, QK_PAD), zero], axis=-1)
    bot = jnp.concatenate([zero, _pad_last(wa_b, QK_PAD)], axis=-1)
    rest = jnp.zeros((depth, LANE - 2 * GLA_LOWRANK, 2 * QK_PAD), F32)
    wa = jnp.concatenate([top, bot, rest], axis=1)
    ba = jnp.concatenate([_pad_last(ba_f, QK_PAD), _pad_last(ba_b, QK_PAD)], axis=-1)[:, None, :]
    return wa, ba


def _na_bias_tables(rpb):
    cq = np.arange(GRID_W)
    c0 = np.clip(cq - NA_KW // 2, 0, GRID_W - NA_KW)
    kc = np.arange(GRID_W)
    valid = (kc[None, :] >= c0[:, None]) & (kc[None, :] < c0[:, None] + NA_KW)
    dcol = np.clip(kc[None, :] - cq[:, None] + (NA_KW - 1), 0, 2 * NA_KW - 2)
    band = jnp.where(valid, rpb[..., dcol], NEG_BIG)
    tab = jnp.stack([band[:, :, NA_KH - 1 - v:2 * NA_KH - 1 - v] for v in range(NA_KH)], axis=2)
    tab = jnp.transpose(tab, (0, 1, 2, 4, 3, 5))
    return tab.reshape(rpb.shape[:2] + (NA_KH, GRID_W, NA_KH * GRID_W)).astype(F32)


def _head_mean_matrix():
    h = np.arange(GLA_WIDTH) // GLA_DV
    return jnp.asarray((h[:, None] == h[None, :]).astype(np.float32), dtype=BF16)


def kernel(x, c, ctx, c_ctx, w_mod, b_mod, g_mix, g_ffn, w_in, gla_wa_f, gla_ba_f, gla_wa_b, gla_ba_b, gla_g_norm,
           na_rpb, conv_w, conv_b, conv_ln_g, conv_ln_b, w_out, router_w, router_bias, w_gate, w_up, w_down,
           g_final):
    bsz, seq, d = x.shape
    ctx_len = ctx.shape[1]
    depth = w_mod.shape[0]
    t_lat, t_ctx = bsz * seq, bsz * ctx_len
    assert seq % TOK_TILE == 0 and t_ctx % TOK_TILE == 0 and seq % ctx_len == 0
    assert d == 2 * SUB_Y * LANE
    assert seq // GRID_W >= NA_KH and (seq // GRID_W) % NA_UNROLL == 0
    assert ctx_len % (GLA_CHUNK * GLA_STEPS) == 0 and seq % (GLA_CHUNK * GLA_STEPS) == 0
    tiles_per_seq = seq // TOK_TILE
    n_lat_tiles = t_lat // TOK_TILE
    n_all_tiles = (t_lat + t_ctx) // TOK_TILE
    ctx_blk0 = t_lat // ctx_len

    mod_rows = -(-(bsz + 1) // 8) * 8
    c_pad = jnp.zeros((mod_rows, d), F32).at[:bsz].set(c).at[bsz].set(c_ctx)
    mod = _modulation(c_pad, w_mod, b_mod)
    mod5 = mod.reshape(depth, mod_rows, 6, 1, d)

    cos_t, sin_t = _rope_tables(seq)
    e_mat = _head_mean_matrix()
    rw = jnp.pad(router_w, ((0, 0), (0, LANE - N_EXPERTS)))
    rw_hi = rw.astype(BF16)
    rw_lo = (rw - rw_hi.astype(F32)).astype(BF16)
    rw_split = jnp.concatenate([rw_hi, rw_lo], axis=1)
    bias_col = router_bias.reshape(N_EXPERTS, 1).astype(F32)
    zero_state = jnp.zeros((bsz, GLA_WIDTH, QK_PAD), F32)

    w_aug = _in_weights(w_in)
    wa_aug, ba_aug = _gate_weights(gla_wa_f, gla_ba_f, gla_wa_b, gla_ba_b)
    bias_tab = _na_bias_tables(na_rpb)
    row = lambda a: a[:, None, :]
    gmix, gffn, gnorm = row(g_mix), row(g_ffn), row(gla_g_norm)
    cb, clg, clb = row(conv_b), row(conv_ln_g), row(conv_ln_b)
    w_out_b = w_out.astype(BF16)
    wg_b, wu_b, wd_b = w_gate.astype(BF16), w_up.astype(BF16), w_down.astype(BF16)

    x_all = jnp.concatenate([x.reshape(t_lat, d), ctx.reshape(t_ctx, d)], axis=0)
    y_moe = None
    for l in range(depth):
        last = l == depth - 1
        x_all, (q, k, v, sgate, gf, gb, nq, nk, nv, hcv) = _inproj(
            x_all, y_moe, mod5, l, gmix, cos_t, sin_t, w_aug, wa_aug, ba_aug,
            n_lat_tiles=n_lat_tiles, tiles_per_seq=tiles_per_seq, n_batch=bsz, res_layer=l - 1)

        o_g, st_f, st_b = _gla(q, k, v, gf, gb, zero_state, zero_state, None, seq=ctx_len, blk0=ctx_blk0,
                               n_batch=bsz)
        o_g, _, _ = _gla(q, k, v, gf, gb, st_f, st_b, o_g, seq=seq, blk0=0, n_batch=bsz)

        o_n = _na_latent(nq, nk, nv, bias_tab, l, seq=seq, ctx_len=ctx_len, n_batch=bsz, ctx_blk0=ctx_blk0)
        o_c = _conv(hcv, conv_w, cb, clg, clb, l, None, seq=seq, blk0=0, n_batch=bsz)
        if not last:
            o_n = _na_context(nq, nk, nv, o_n, ctx_len=ctx_len, n_batch=bsz, ctx_blk0=ctx_blk0)
            o_c = _conv(hcv, conv_w, cb, clg, clb, l, o_c, seq=ctx_len, blk0=ctx_blk0, n_batch=bsz)

        n_tiles = n_lat_tiles if last else n_all_tiles
        x_mid, h2, logits = _outproj(
            o_g[0], o_g[1], sgate, o_n, o_c, x_all, mod5, l, gnorm, e_mat, w_out_b, gffn, rw_split,
            n_tiles=n_tiles, n_lat_tiles=n_lat_tiles, tiles_per_seq=tiles_per_seq, n_batch=bsz)
        y_moe = _moe(h2, logits, bias_col, wg_b, wu_b, wd_b, l)
        x_all = x_mid

    out = _final(x_all, y_moe, mod5, depth - 1, g_final.reshape(1, d), tiles_per_seq=tiles_per_seq)
    return out.reshape(bsz, seq, d)
```

```python
import functools

import numpy as np
import jax
import jax.numpy as jnp
from jax import lax
from jax.experimental import pallas as pl
from jax.experimental.pallas import tpu as pltpu
from jax.experimental.pallas import tpu_sc as plsc

GRID_W = 64
EPS = 1e-6
GLA_HEADS, GLA_DK, GLA_DV = 4, 48, 96
GLA_QK = GLA_HEADS * GLA_DK
GLA_WIDTH = GLA_HEADS * GLA_DV
GLA_LOWRANK = 16
GLA_TAU = 16.0
ROPE_BASE = 10000.0
NA_HEADS, NA_DH = 6, 64
NA_WIDTH = NA_HEADS * NA_DH
NA_KH, NA_KW = 8, 16
CONV_WIDTH, CONV_K = 256, 31
N_EXPERTS, N_GROUPS, EXPERTS_PER_GROUP = 16, 4, 4
PAIRS = ((0, 1), (0, 2), (0, 3), (1, 2), (1, 3), (2, 3))
N_CLASSES = N_GROUPS * len(PAIRS)

LANE = 128
SUBLANES = 8
QK_PAD = 256
C_Q, C_K = 0, 256
C_A, C_V, C_GATE = 512, 640, 1024
C_NQ, C_NK, C_NV, C_CONV = 1408, 1792, 2176, 2560
IN_COLS_PAD = 3072
MXU_N = 256

TOK_TILE = 1024
ROW_PARTS = 4
GLA_CHUNK = 64
GLA_SUB = 16
GLA_STEPS = 4
MOE_BLK = 256
ROUTE_TILE = 512
NA_UNROLL = 8
NEG_BIG = -1e30
VMEM_LIMIT = 48 * 1024 * 1024

F32 = jnp.float32
BF16 = jnp.bfloat16


def _cparams(sem):
    return pltpu.CompilerParams(dimension_semantics=sem, vmem_limit_bytes=VMEM_LIMIT)


def _layer_spec(a, layer):
    return pl.BlockSpec((None,) + a.shape[1:], lambda *_: (layer,) + (0,) * (a.ndim - 1),
                        pipeline_mode=pl.Buffered(1))


def _dot(a, b):
    return jnp.dot(a, b, preferred_element_type=F32)


def _dot_nt(a, b):
    return lax.dot_general(a, b, (((1,), (1,)), ((), ())), preferred_element_type=F32)


def _split(a):
    hi = a.astype(BF16)
    lo = (a - hi.astype(F32)).astype(BF16)
    return hi, lo


def _sigmoid(x):
    return 1.0 / (1.0 + jnp.exp(-x))


SUB_X, SUB_Y = 5, 4
SC_WINDOW = 256


def _pack_bf16_pairs(a, b):
    ua = lax.bitcast_convert_type(a.astype(BF16).astype(F32), jnp.uint32)
    ub = lax.bitcast_convert_type(b.astype(BF16).astype(F32), jnp.uint32)
    return jnp.bitwise_or(jnp.right_shift(ua, jnp.uint32(16)), ub)


def _unpack_bf16_pairs(w):
    lo = lax.bitcast_convert_type(jnp.left_shift(w, jnp.uint32(16)), F32)
    hi = lax.bitcast_convert_type(jnp.bitwise_and(w, jnp.uint32(0xFFFF0000)), F32)
    return lo, hi


def _store_subrows(ref, val, j0, row0=0):
    r = val.shape[0]
    for j in range(val.shape[1] // LANE):
        ref[row0 // 8:(row0 + r) // 8, j0 + j, :, :] = val[:, j * LANE:(j + 1) * LANE].reshape(r // 8, 8, LANE)


def _load_subrows(ref, j0, n, row0=0, rows=None):
    r = ref.shape[0] * 8 if rows is None else rows
    return jnp.concatenate([ref[row0 // 8:(row0 + r) // 8, j0 + j, :, :].reshape(r, LANE) for j in range(n)],
                           axis=1)


def _pack_rows(x):
    half = x.shape[1] // 2
    return _pack_bf16_pairs(x[:, :half], x[:, half:])


def _unpack_rows(w):
    lo, hi = _unpack_bf16_pairs(w)
    return jnp.concatenate([lo, hi], axis=1)


def _sc_mesh():
    return plsc.VectorSubcoreMesh(core_axis_name="c", subcore_axis_name="s")


def _sc_scatter(src, idx, n_out):
    n, w = src.shape
    idx2 = idx.reshape(1, n)

    @pl.kernel(out_type=jax.ShapeDtypeStruct((n_out, w), src.dtype), mesh=_sc_mesh(), scratch_types=[])
    def scatter_kernel(x_hbm, i_hbm, o_hbm):
        def body(x_vmem, i_vmem):
            pltpu.sync_copy(x_vmem, o_hbm.at[i_vmem.at[0]])

        pltpu.emit_pipeline(
            body,
            grid=(n // SC_WINDOW,),
            in_specs=[pl.BlockSpec((SC_WINDOW, w), index_map=lambda i: (i, 0)),
                      pl.BlockSpec((1, SC_WINDOW), index_map=lambda i: (0, i))],
            out_specs=[],
            core_axis_name=("c", "s"),
            dimension_semantics=(pltpu.PARALLEL,),
        )(x_hbm, i_hbm)

    return scatter_kernel(src, idx2)


def _sc_gather(src, idx):
    n = idx.shape[0]
    w = src.shape[1]
    idx2 = idx.reshape(1, n)

    @pl.kernel(out_type=jax.ShapeDtypeStruct((n, w), src.dtype), mesh=_sc_mesh())
    def gather_kernel(x_hbm, i_hbm, o_hbm):
        def body(i_vmem, o_vmem):
            pltpu.sync_copy(x_hbm.at[i_vmem.at[0]], o_vmem)

        pltpu.emit_pipeline(
            body,
            grid=(n // SC_WINDOW,),
            in_specs=[pl.BlockSpec((1, SC_WINDOW), index_map=lambda i: (0, i))],
            out_specs=[pl.BlockSpec((SC_WINDOW, w), index_map=lambda i: (i, 0))],
            core_axis_name=("c", "s"),
            dimension_semantics=(pltpu.PARALLEL,),
        )(i_hbm, o_hbm)

    return gather_kernel(src, idx2)


def _mod_kernel(c_ref, w_ref, b_ref, o_ref):
    cv = c_ref[...]
    s = cv * _sigmoid(cv)
    s_hi, s_lo = _split(s)
    w_hi, w_lo = _split(w_ref[...])
    o_ref[...] = _dot(s_hi, w_hi) + _dot(s_lo, w_hi) + _dot(s_hi, w_lo) + b_ref[...]


def _modulation(c_pad, w_mod, b_mod):
    depth, d, six_d = w_mod.shape
    rows = c_pad.shape[0]
    nt = 1536
    return pl.pallas_call(
        _mod_kernel,
        grid=(depth, six_d // nt),
        in_specs=[
            pl.BlockSpec((rows, d), lambda l, j: (0, 0)),
            pl.BlockSpec((None, d, nt), lambda l, j: (l, 0, j)),
            pl.BlockSpec((None, 1, nt), lambda l, j: (l, 0, j)),
        ],
        out_specs=pl.BlockSpec((None, rows, nt), lambda l, j: (l, 0, j)),
        out_shape=jax.ShapeDtypeStruct((depth, rows, six_d), F32),
        compiler_params=_cparams(("arbitrary", "arbitrary")),
        name="modulation",
    )(c_pad, w_mod, b_mod.reshape(depth, 1, six_d))


def _inproj_kernel(*refs, has_res):
    if has_res:
        x_ref, y_ref, g5_ref = refs[:3]
        refs = refs[3:]
    else:
        x_ref = refs[0]
        refs = refs[1:]
    (sh_ref, sc_ref, gmix_ref, cos_ref, sin_ref, w_ref, wa_ref, ba_ref) = refs[:8]
    outs = refs[8:]
    if has_res:
        xnew_ref = outs[0]
        outs = outs[1:]
    (q_ref, k_ref, v_ref, sg_ref, gf_ref, gb_ref, nq_ref, nk_ref, nv_ref, hcv_ref) = outs

    tile = x_ref.shape[0]
    part = tile // ROW_PARTS
    wa_hi, wa_lo = _split(wa_ref[...])
    even_lane = jnp.bitwise_and(lax.broadcasted_iota(jnp.int32, (1, LANE), 1), 1) == 0
    hbs = []
    for p in range(ROW_PARTS):
        rows = pl.ds(p * part, part)
        x = x_ref[rows, :]
        if has_res:
            x = x + g5_ref[...] * _unpack_rows(_load_subrows(y_ref, 0, SUB_Y, row0=p * part, rows=part))
            xnew_ref[rows, :] = x
        ms = jnp.mean(x * x, axis=-1, keepdims=True)
        h = x * lax.rsqrt(ms + EPS) * gmix_ref[...]
        h = h * (1.0 + sc_ref[...]) + sh_ref[...]
        hbs.append(h.astype(BF16))

    for p, hb in enumerate(hbs):
        rows = pl.ds(p * part, part)

        def proj(c0, n, hb=hb):
            return _dot(hb, w_ref[:, c0:c0 + n])

        cos = cos_ref[rows, :]
        sin = sin_ref[rows, :]

        def rope(t):
            halves = []
            for c0 in range(0, QK_PAD, LANE):
                th = t[:, c0:c0 + LANE]
                halves.append(jnp.where(even_lane, pltpu.roll(th, LANE - 1, 1), pltpu.roll(th, 1, 1)))
            return t * cos + jnp.concatenate(halves, axis=1) * sin

        mid = (C_A + C_CONV) // 2
        assert (mid - C_A) % MXU_N == 0 and C_NQ < mid < C_NK
        d1 = proj(C_A, mid - C_A)
        a_hi, a_lo = _split(d1[:, :LANE])
        q_ref[rows, :] = (rope(proj(C_Q, QK_PAD)) * (GLA_DK ** -0.5)).astype(BF16)
        k_ref[rows, :] = rope(proj(C_K, QK_PAD)).astype(BF16)

        z = _dot(a_hi, wa_hi) + _dot(a_lo, wa_hi) + _dot(a_hi, wa_lo) + ba_ref[...]
        logsig = jnp.minimum(z, 0.0) - jnp.log(1.0 + jnp.exp(-jnp.abs(z)))
        g = logsig * (1.0 / GLA_TAU)
        gf_ref[rows, :] = g[:, :QK_PAD]
        gb_ref[rows, :] = g[:, QK_PAD:]

        v_ref[rows, :] = d1[:, C_V - C_A:C_GATE - C_A].astype(BF16)
        gate = d1[:, C_GATE - C_A:C_NQ - C_A]
        sg_ref[rows, :] = (gate * _sigmoid(gate)).astype(BF16)
        d2 = proj(mid, C_CONV - mid)
        nq = jnp.concatenate([d1[:, C_NQ - C_A:], d2[:, :C_NK - mid]], axis=1)
        nq_ref[rows, :] = (nq * (NA_DH ** -0.5)).astype(BF16)
        nk_ref[rows, :] = d2[:, C_NK - mid:C_NV - mid].astype(BF16)
        nv_ref[rows, :] = d2[:, C_NV - mid:].astype(BF16)
        u = proj(C_CONV, 2 * CONV_WIDTH)
        hcv_ref[rows, :] = u[:, :CONV_WIDTH] * _sigmoid(u[:, CONV_WIDTH:])


def _inproj(x_all, y, mod5, layer, gmix, cos_t, sin_t, w_aug, wa_aug, ba_aug, *, n_lat_tiles, tiles_per_seq,
            n_batch, res_layer):
    t_all, d = x_all.shape
    nt = t_all // TOK_TILE
    has_res = y is not None

    def bidx(i):
        return jnp.where(i < n_lat_tiles, i // tiles_per_seq, n_batch)

    def ridx(i):
        return jnp.where(i < n_lat_tiles, i % tiles_per_seq, tiles_per_seq)

    def modspec(l, j):
        return pl.BlockSpec((None, None, None, 1, d), lambda i: (l, bidx(i), j, 0, 0))

    tok = lambda w: pl.BlockSpec((TOK_TILE, w), lambda i: (i, 0))
    full = lambda a: pl.BlockSpec(a.shape, lambda i: (0,) * a.ndim)

    in_specs = [tok(d)]
    args = [x_all]
    if has_res:
        in_specs += [pl.BlockSpec((TOK_TILE // 8, SUB_Y, 8, LANE), lambda i: (i, 0, 0, 0)), modspec(res_layer, 5)]
        args += [y, mod5]
    in_specs += [modspec(layer, 0), modspec(layer, 1), _layer_spec(gmix, layer),
                 pl.BlockSpec((TOK_TILE, QK_PAD), lambda i: (ridx(i), 0)),
                 pl.BlockSpec((TOK_TILE, QK_PAD), lambda i: (ridx(i), 0)),
                 _layer_spec(w_aug, layer), _layer_spec(wa_aug, layer), _layer_spec(ba_aug, layer)]
    args += [mod5, mod5, gmix, cos_t, sin_t, w_aug, wa_aug, ba_aug]

    out_widths = [(QK_PAD, BF16), (QK_PAD, BF16), (GLA_WIDTH, BF16), (GLA_WIDTH, BF16), (QK_PAD, F32), (QK_PAD, F32),
                  (NA_WIDTH, BF16), (NA_WIDTH, BF16), (NA_WIDTH, BF16), (CONV_WIDTH, F32)]
    if has_res:
        out_widths = [(d, F32)] + out_widths
    out_specs = [tok(w) for w, _ in out_widths]
    out_shape = [jax.ShapeDtypeStruct((t_all, w), dt) for w, dt in out_widths]
    res = pl.pallas_call(
        functools.partial(_inproj_kernel, has_res=has_res),
        grid=(nt,),
        in_specs=in_specs,
        out_specs=out_specs,
        out_shape=out_shape,
        compiler_params=_cparams(("parallel",)),
        name="inproj",
    )(*args)
    if has_res:
        return res[0], res[1:]
    return x_all, res


def _gla_masks():
    c, sub = GLA_CHUNK, GLA_SUB
    lane_qk = lax.broadcasted_iota(jnp.int32, (1, QK_PAD), 1)
    head_qk = ((lane_qk >= GLA_DK).astype(jnp.int32) + (lane_qk >= 2 * GLA_DK).astype(jnp.int32)
               + (lane_qk >= 3 * GLA_DK).astype(jnp.int32) + 4 * (lane_qk >= 4 * GLA_DK).astype(jnp.int32))
    row_h = jnp.right_shift(lax.broadcasted_iota(jnp.int32, (c, 1), 0), GLA_SUB.bit_length() - 1)
    hm = (row_h == head_qk).astype(F32)
    row_v = lax.broadcasted_iota(jnp.int32, (GLA_WIDTH, 1), 0)
    head_v = ((row_v >= GLA_DV).astype(jnp.int32) + (row_v >= 2 * GLA_DV).astype(jnp.int32)
              + (row_v >= 3 * GLA_DV).astype(jnp.int32))
    bd = (head_v == head_qk).astype(F32)
    lane_v = lax.broadcasted_iota(jnp.int32, (1, GLA_WIDTH), 1)
    vm = [((lane_v >= h * GLA_DV) & (lane_v < (h + 1) * GLA_DV)).astype(F32) for h in range(GLA_HEADS)]
    return hm, bd, vm


def _gla_steps(q, k, v, g, s_t, hm, bd, vm):
    c, sub = GLA_CHUNK, GLA_SUB
    nsub = c // sub
    dirs = (True, False)
    items = [(u, d) for u in range(GLA_STEPS) for d in range(2)]
    ri = lax.broadcasted_iota(jnp.int32, (c, c), 0)
    ci = lax.broadcasted_iota(jnp.int32, (c, c), 1)
    key_row = lax.broadcasted_iota(jnp.int32, (c, 1), 0)
    att_row = jnp.bitwise_and(ri, sub - 1)
    tri = [((ci <= ri) if fwd else (ci >= ri)).astype(BF16) for fwd in dirs]

    cums = {}
    for u, d in items:
        g_hi, g_lo = _split(g[u][d])
        cums[u, d] = _dot(tri[d], g_hi) + _dot(tri[d], g_lo)

    qe, kv, decay, atts = {}, {}, {}, {}
    for u, d in items:
        fwd = dirs[d]
        cum = cums[u, d]
        tot = cum[c - 1:c] if fwd else cum[0:1]
        qe[u, d] = (q[u][d] * jnp.exp(cum)).astype(BF16)
        k_end = (k[u][d] * jnp.exp(tot - cum)).astype(BF16)
        kv[u, d] = lax.dot_general(v[u][d], k_end, (((0,), (0,)), ((), ())), preferred_element_type=F32)
        decay[u, d] = jnp.exp(tot)
        att_d = []
        for i in range(nsub):
            lo, hi = i * sub, (i + 1) * sub
            if fwd:
                ref = cum[lo - 1:lo] if i > 0 else jnp.zeros((1, QK_PAD), F32)
                key_ok = key_row < hi
                causal = ci <= att_row + lo
            else:
                ref = cum[hi:hi + 1] if i < nsub - 1 else jnp.zeros((1, QK_PAD), F32)
                key_ok = key_row >= lo
                causal = ci >= att_row + lo
            qi = q[u][d][lo:hi] * jnp.exp(cum[lo:hi] - ref)
            qs = (jnp.concatenate([qi] * GLA_HEADS, axis=0) * hm).astype(BF16)
            ki = (k[u][d] * jnp.exp(jnp.where(key_ok, ref - cum, NEG_BIG))).astype(BF16)
            att = _dot_nt(qs, ki)
            att_d.append(jnp.where(causal, att, 0.0).astype(BF16))
        atts[u, d] = jnp.concatenate(att_d, axis=0)

    s = list(s_t)
    o_inter = {}
    for u, d in items:
        o_inter[u, d] = _dot_nt(qe[u, d], s[d].astype(BF16))
        s[d] = s[d] * decay[u, d] + bd * kv[u, d]

    outs = [[None, None] for _ in range(GLA_STEPS)]
    for u, d in items:
        r = _dot(atts[u, d], v[u][d])
        blocks = []
        for i in range(nsub):
            base = i * c
            oi = r[base:base + sub] * vm[0]
            for h in range(1, GLA_HEADS):
                oi = oi + r[base + h * sub:base + (h + 1) * sub] * vm[h]
            blocks.append(oi)
        outs[u][d] = o_inter[u, d] + jnp.concatenate(blocks, axis=0)
    return outs, s


def _gla_kernel(q_ref, k_ref, v_ref, gf_ref, gb_ref, sf0_ref, sb0_ref, of_ref, ob_ref, sf_ref, sb_ref):
    n = q_ref.shape[0]
    nc = n // GLA_CHUNK
    hm, bd, vm = _gla_masks()
    sf_ref[...] = sf0_ref[...]
    sb_ref[...] = sb0_ref[...]

    def body(j, carry):
        rows = [[pl.ds(pl.multiple_of(cidx * GLA_CHUNK, GLA_CHUNK), GLA_CHUNK)
                 for cidx in (j * GLA_STEPS + u, nc - 1 - (j * GLA_STEPS + u))] for u in range(GLA_STEPS)]
        q = [[q_ref[r, :].astype(F32) for r in ru] for ru in rows]
        k = [[k_ref[r, :].astype(F32) for r in ru] for ru in rows]
        v = [[v_ref[r, :] for r in ru] for ru in rows]
        g = [[gf_ref[ru[0], :], gb_ref[ru[1], :]] for ru in rows]
        outs, s_new = _gla_steps(q, k, v, g, [sf_ref[...], sb_ref[...]], hm, bd, vm)
        sf_ref[...] = s_new[0]
        sb_ref[...] = s_new[1]
        for u in range(GLA_STEPS):
            of_ref[rows[u][0], :] = outs[u][0]
            ob_ref[rows[u][1], :] = outs[u][1]
        return carry

    lax.fori_loop(0, nc // GLA_STEPS, body, 0)


def _gla(q, k, v, gf, gb, sf0, sb0, o_prev, *, seq, blk0, n_batch):
    t_all = q.shape[0]
    tokw = lambda w: pl.BlockSpec((seq, w), lambda b: (blk0 + b, 0))
    st = pl.BlockSpec((None, GLA_WIDTH, QK_PAD), lambda b: (b, 0, 0))
    in_specs = [tokw(QK_PAD), tokw(QK_PAD), tokw(GLA_WIDTH), tokw(QK_PAD), tokw(QK_PAD), st, st]
    args = [q, k, v, gf, gb, sf0, sb0]
    aliases = {}
    n_in = len(args)
    if o_prev is not None:
        in_specs += [pl.BlockSpec(memory_space=pl.ANY)] * 2
        args += list(o_prev)
        aliases = {n_in: 0, n_in + 1: 1}

    def kern(*refs):
        _gla_kernel(*refs[:n_in], *refs[len(args):])

    st_shape = jax.ShapeDtypeStruct((n_batch, GLA_WIDTH, QK_PAD), F32)
    o_shape = jax.ShapeDtypeStruct((t_all, GLA_WIDTH), F32)
    o_f, o_b, s_f, s_b = pl.pallas_call(
        kern,
        grid=(n_batch,),
        in_specs=in_specs,
        out_specs=[tokw(GLA_WIDTH), tokw(GLA_WIDTH), st, st],
        out_shape=[o_shape, o_shape, st_shape, st_shape],
        input_output_aliases=aliases,
        compiler_params=_cparams(("parallel",)),
        name="gla",
    )(*args)
    return (o_f, o_b), s_f, s_b


def _na_kernel(q_ref, k_ref, v_ref, kc_ref, vc_ref, bias_ref, o_ref):
    n = q_ref.shape[0]
    rows = n // GRID_W
    nkeys = NA_KH * GRID_W
    lane = lax.broadcasted_iota(jnp.int32, (1, LANE), 1)
    first = lane < NA_DH
    kc = kc_ref[...]
    vc = vc_ref[...]

    def body(jb, carry):
        items = []
        for j in range(NA_UNROLL):
            r = jb * NA_UNROLL + j
            r0 = jnp.clip(r - NA_KH // 2, 0, rows - NA_KH)
            var = r - r0
            qrows = pl.ds(pl.multiple_of(r * GRID_W, GRID_W), GRID_W)
            krows = pl.ds(pl.multiple_of(r0 * GRID_W, GRID_W), nkeys)
            qr = q_ref[qrows, :]
            kb = k_ref[krows, :]
            q2 = jnp.concatenate([jnp.where(first, qr, jnp.zeros_like(qr)),
                                  jnp.where(first, jnp.zeros_like(qr), qr)], axis=0)
            items.append((qrows, krows, var, _dot_nt(q2, kb), _dot_nt(q2, kc)))
        probs = []
        for qrows, krows, var, s_loc, s_ctx in items:
            s_loc = s_loc + jnp.concatenate([bias_ref[0, var], bias_ref[1, var]], axis=0)
            m = jnp.maximum(jnp.max(s_loc, axis=-1, keepdims=True), jnp.max(s_ctx, axis=-1, keepdims=True))
            p_loc = jnp.exp(s_loc - m)
            p_ctx = jnp.exp(s_ctx - m)
            l = jnp.sum(p_loc, axis=-1, keepdims=True) + jnp.sum(p_ctx, axis=-1, keepdims=True)
            probs.append((p_loc.astype(BF16), p_ctx.astype(BF16), l))
        for (qrows, krows, var, _, _), (p_loc, p_ctx, l) in zip(items, probs):
            o = (_dot(p_loc, v_ref[krows, :]) + _dot(p_ctx, vc)) / l
            o_ref[qrows, :] = jnp.where(first, o[:GRID_W], o[GRID_W:]).astype(o_ref.dtype)
        return carry

    lax.fori_loop(0, rows // NA_UNROLL, body, 0)


def _na_latent(nq, nk, nv, bias_tab, layer, *, seq, ctx_len, n_batch, ctx_blk0):
    t_all = nq.shape[0]
    npair = NA_HEADS // 2
    lat = pl.BlockSpec((seq, LANE), lambda b, p: (b, p))
    ctx = pl.BlockSpec((ctx_len, LANE), lambda b, p: (ctx_blk0 + b, p))
    bias = pl.BlockSpec((None, 2) + bias_tab.shape[2:], lambda b, p: (layer, p, 0, 0, 0))
    return pl.pallas_call(
        _na_kernel,
        grid=(n_batch, npair),
        in_specs=[lat, lat, lat, ctx, ctx, bias],
        out_specs=lat,
        out_shape=jax.ShapeDtypeStruct((t_all, NA_WIDTH), BF16),
        compiler_params=_cparams(("parallel", "arbitrary")),
        name="na_latent",
    )(nq, nk, nv, nk, nv, bias_tab)


def _na_ctx_kernel(q_ref, k_ref, v_ref, o_in_ref, o_ref):
    del o_in_ref
    lane = lax.broadcasted_iota(jnp.int32, (1, LANE), 1)
    first = lane < NA_DH
    q = q_ref[...]
    k = k_ref[...]
    v = v_ref[...]
    res = []
    for h in range(2):
        sel = first if h == 0 else jnp.logical_not(first)
        qh = jnp.where(sel, q, jnp.zeros_like(q))
        s = _dot_nt(qh, k)
        m = jnp.max(s, axis=-1, keepdims=True)
        p = jnp.exp(s - m)
        l = jnp.sum(p, axis=-1, keepdims=True)
        res.append(_dot(p.astype(BF16), v) / l)
    o_ref[...] = jnp.where(first, res[0], res[1]).astype(o_ref.dtype)


def _na_context(nq, nk, nv, o_prev, *, ctx_len, n_batch, ctx_blk0):
    npair = NA_HEADS // 2
    ctx = pl.BlockSpec((ctx_len, LANE), lambda b, p: (ctx_blk0 + b, p))
    return pl.pallas_call(
        _na_ctx_kernel,
        grid=(n_batch, npair),
        in_specs=[ctx, ctx, ctx, pl.BlockSpec(memory_space=pl.ANY)],
        out_specs=ctx,
        out_shape=jax.ShapeDtypeStruct(o_prev.shape, o_prev.dtype),
        input_output_aliases={3: 0},
        compiler_params=_cparams(("parallel", "arbitrary")),
        name="na_context",
    )(nq, nk, nv, o_prev)


CONV_HALO = 16
CONV_ROWS = 128


def _conv_kernel(*refs, has_prev):
    if has_prev:
        h_ref, w_ref, b_ref, lg_ref, lb_ref, _, o_ref, pad_ref, sh_ref = refs
    else:
        h_ref, w_ref, b_ref, lg_ref, lb_ref, o_ref, pad_ref, sh_ref = refs
    n = h_ref.shape[0]
    zeros = jnp.zeros((CONV_HALO, CONV_WIDTH), F32)
    pad_ref[0:CONV_HALO, :] = zeros
    pad_ref[CONV_HALO + n:CONV_HALO + n + CONV_HALO, :] = zeros
    pad_ref[CONV_HALO:CONV_HALO + n, :] = h_ref[...]
    span = n + 2 * CONV_HALO - SUBLANES
    for s in range(SUBLANES):
        sh_ref[s, 0:span, :] = pad_ref[s:s + span, :]
    w = w_ref[...]
    off = CONV_HALO - CONV_K // 2

    def chunk(cidx, carry):
        base = pl.multiple_of(cidx * CONV_ROWS, CONV_ROWS)
        acc = jnp.zeros((CONV_ROWS, CONV_WIDTH), F32) + b_ref[...]
        for j in range(CONV_K):
            s = (off + j) % SUBLANES
            acc = acc + sh_ref[s, pl.ds(base + (off + j - s), CONV_ROWS), :] * w[j:j + 1, :]
        mu = jnp.mean(acc, axis=-1, keepdims=True)
        xc = acc - mu
        var = jnp.mean(xc * xc, axis=-1, keepdims=True)
        y = xc * lax.rsqrt(var + EPS) * lg_ref[...] + lb_ref[...]
        o_ref[pl.ds(base, CONV_ROWS), :] = (y * _sigmoid(y)).astype(o_ref.dtype)
        return carry

    lax.fori_loop(0, n // CONV_ROWS, chunk, 0)


def _conv(hcv, w, b, lg, lb, layer, o_prev, *, seq, blk0, n_batch):
    t_all = hcv.shape[0]
    tok = pl.BlockSpec((seq, CONV_WIDTH), lambda i: (blk0 + i, 0))
    full = lambda a: pl.BlockSpec(a.shape, lambda i: (0,) * a.ndim)
    in_specs = [tok] + [_layer_spec(a, layer) for a in (w, b, lg, lb)]
    args = [hcv, w, b, lg, lb]
    aliases = {}
    if o_prev is not None:
        in_specs.append(pl.BlockSpec(memory_space=pl.ANY))
        args.append(o_prev)
        aliases = {5: 0}
    return pl.pallas_call(
        functools.partial(_conv_kernel, has_prev=o_prev is not None),
        grid=(n_batch,),
        in_specs=in_specs,
        out_specs=tok,
        out_shape=jax.ShapeDtypeStruct((t_all, CONV_WIDTH), BF16),
        scratch_shapes=[pltpu.VMEM((seq + 2 * CONV_HALO, CONV_WIDTH), F32),
                        pltpu.VMEM((SUBLANES, seq + 2 * CONV_HALO, CONV_WIDTH), F32)],
        input_output_aliases=aliases,
        compiler_params=_cparams(("parallel",)),
        name="conv",
    )(*args)


def _outproj_kernel(ogf_ref, ogb_ref, sg_ref, on_ref, oc_ref, x_ref, g2_ref, sh_ref, sc_ref, gn_ref, e_ref,
                    wo_ref, gffn_ref, rw_ref, xmid_ref, h2_ref, lg_ref):
    tile = x_ref.shape[0]
    parts = [pl.ds(p * (tile // ROW_PARTS), tile // ROW_PARTS) for p in range(ROW_PARTS)]
    e = e_ref[...]
    rw = rw_ref[...]

    ofs, mss = [], []
    for rows in parts:
        of = ogf_ref[rows, :] + ogb_ref[rows, :]
        sq_hi, sq_lo = _split(of * of)
        ofs.append(of)
        mss.append(_dot(jnp.concatenate([sq_hi, sq_lo], axis=1), e) * (1.0 / GLA_DV))
    ys = []
    for rows, of, ms in zip(parts, ofs, mss):
        og = of * lax.rsqrt(ms + EPS) * gn_ref[...] * sg_ref[rows, :].astype(F32)
        mix = jnp.concatenate([og.astype(BF16), on_ref[rows, :], oc_ref[rows, :]], axis=1)
        ys.append(_dot(mix, wo_ref[...]))
    for p, (rows, y) in enumerate(zip(parts, ys)):
        x = x_ref[rows, :] + g2_ref[...] * y
        xmid_ref[rows, :] = x
        ms2 = jnp.mean(x * x, axis=-1, keepdims=True)
        h2 = x * lax.rsqrt(ms2 + EPS) * gffn_ref[...]
        h2 = h2 * (1.0 + sc_ref[...]) + sh_ref[...]
        h_hi, h_lo = _split(h2)
        _store_subrows(h2_ref, _pack_rows(h2), 0, row0=p * (tile // ROW_PARTS))
        hw = _dot(h_hi, rw)
        lg_ref[rows, :] = hw[:, :LANE] + hw[:, LANE:] + _dot(h_lo, rw[:, :LANE])


def _outproj(o_gf, o_gb, sgate, o_n, o_c, x_all, mod5, layer, gnorm, e_mat, w_out, gffn, rw_split, *, n_tiles,
             n_lat_tiles, tiles_per_seq, n_batch):
    d = x_all.shape[1]
    t_out = n_tiles * TOK_TILE

    def bidx(i):
        return jnp.where(i < n_lat_tiles, i // tiles_per_seq, n_batch)

    def modspec(j):
        return pl.BlockSpec((None, None, None, 1, d), lambda i: (layer, bidx(i), j, 0, 0))

    tok = lambda w: pl.BlockSpec((TOK_TILE, w), lambda i: (i, 0))
    full = lambda a: pl.BlockSpec(a.shape, lambda i: (0,) * a.ndim)
    return pl.pallas_call(
        _outproj_kernel,
        grid=(n_tiles,),
        in_specs=[tok(GLA_WIDTH), tok(GLA_WIDTH), tok(GLA_WIDTH), tok(NA_WIDTH), tok(CONV_WIDTH), tok(d),
                  modspec(2), modspec(3), modspec(4), _layer_spec(gnorm, layer), full(e_mat),
                  _layer_spec(w_out, layer), _layer_spec(gffn, layer),
                  full(rw_split)],
        out_specs=[tok(d), pl.BlockSpec((TOK_TILE // 8, SUB_Y, 8, LANE), lambda i: (i, 0, 0, 0)), tok(LANE)],
        out_shape=[jax.ShapeDtypeStruct((t_out, d), F32),
                   jax.ShapeDtypeStruct((t_out // 8, SUB_X, 8, LANE), jnp.uint32),
                   jax.ShapeDtypeStruct((t_out, LANE), F32)],
        compiler_params=_cparams(("parallel",)),
        name="outproj",
    )(o_gf, o_gb, sgate, o_n, o_c, x_all, mod5, mod5, mod5, gnorm, e_mat, w_out, gffn, rw_split)


def _route_kernel(lg_ref, bias_ref, h2_in_ref, meta_ref, cnt_ref, h2w_ref, carry_ref):
    del h2_in_ref
    tile = lg_ref.shape[0]

    @pl.when(pl.program_id(0) == 0)
    def _():
        carry_ref[...] = jnp.zeros_like(carry_ref)

    lt = lg_ref[...].T
    aff = _sigmoid(lt[0:N_EXPERTS])
    sel = aff + bias_ref[...]
    s = [sel[e:e + 1] for e in range(N_EXPERTS)]
    a = [aff[e:e + 1] for e in range(N_EXPERTS)]

    def top2sum(v):
        best = v[0] + v[1]
        for i, j in PAIRS[1:]:
            best = jnp.maximum(best, v[i] + v[j])
        return best

    gs = [top2sum(s[4 * g:4 * g + 4]) for g in range(N_GROUPS)]
    gbest = jnp.zeros_like(gs[0], dtype=jnp.int32)
    gmax = gs[0]
    for g in range(1, N_GROUPS):
        upd = gs[g] > gmax
        gbest = jnp.where(upd, g, gbest)
        gmax = jnp.where(upd, gs[g], gmax)

    def pick(vals, j):
        out = vals[j]
        for g in range(1, N_GROUPS):
            out = jnp.where(gbest == g, vals[4 * g + j], out)
        return out

    sv = [pick(s, j) for j in range(EXPERTS_PER_GROUP)]
    av = [pick(a, j) for j in range(EXPERTS_PER_GROUP)]
    i1 = jnp.zeros_like(gbest)
    m1 = sv[0]
    for j in range(1, EXPERTS_PER_GROUP):
        upd = sv[j] > m1
        i1 = jnp.where(upd, j, i1)
        m1 = jnp.where(upd, sv[j], m1)
    i2 = jnp.full_like(gbest, -1)
    m2 = jnp.zeros_like(m1)
    for j in range(EXPERTS_PER_GROUP):
        upd = (i1 != j) & ((sv[j] > m2) | (i2 < 0))
        i2 = jnp.where(upd, j, i2)
        m2 = jnp.where(upd, sv[j], m2)
    ia = jnp.minimum(i1, i2)
    ib = jnp.maximum(i1, i2)
    pair = jnp.where(ia == 0, ib - 1, jnp.where(ia == 1, ib + 1, 5))
    cls = gbest * len(PAIRS) + pair

    def take(vals, idx):
        out = vals[0]
        for j in range(1, EXPERTS_PER_GROUP):
            out = jnp.where(idx == j, vals[j], out)
        return out

    w1 = take(av, i1)
    w2 = take(av, i2)
    tot = w1 + w2
    wa = jnp.where(i1 < i2, w1, w2) / tot
    wb = jnp.where(i1 < i2, w2, w1) / tot

    crow = lax.broadcasted_iota(jnp.int32, (32, tile), 0)
    oh = (crow == cls).astype(F32)
    us = lax.broadcasted_iota(jnp.int32, (tile, tile), 0)
    ut = lax.broadcasted_iota(jnp.int32, (tile, tile), 1)
    upper = (us < ut).astype(BF16)
    prefix = _dot(oh.astype(BF16), upper)
    carry = carry_ref[...]
    rank = jnp.sum(oh * (prefix + carry), axis=0, keepdims=True)
    carry_new = carry + jnp.sum(oh, axis=1, keepdims=True)
    carry_ref[...] = carry_new
    cnt_ref[...] = jnp.broadcast_to(carry_new, cnt_ref.shape)

    meta_ref[...] = jnp.zeros_like(meta_ref)
    meta_ref[0:1, :] = cls.astype(F32)
    meta_ref[1:2, :] = rank

    wrow = lax.broadcasted_iota(jnp.int32, (LANE, tile), 0)
    wmat = jnp.where(wrow == 0, wa, jnp.where(wrow == 1, wb, 0.0))
    wtok = lax.bitcast_convert_type(wmat.T, jnp.uint32)
    h2w_ref[:, 0, :, :] = wtok.reshape(tile // 8, 8, LANE)


def _route(logits, bias_col, h2_sub):
    t = logits.shape[0]
    nt = t // ROUTE_TILE
    meta, cnt, h2_sub = pl.pallas_call(
        _route_kernel,
        grid=(nt,),
        in_specs=[pl.BlockSpec((ROUTE_TILE, LANE), lambda i: (i, 0)),
                  pl.BlockSpec(bias_col.shape, lambda i: (0, 0)),
                  pl.BlockSpec(memory_space=pl.ANY)],
        out_specs=[pl.BlockSpec((8, ROUTE_TILE), lambda i: (0, i)),
                   pl.BlockSpec((32, LANE), lambda i: (0, 0)),
                   pl.BlockSpec((ROUTE_TILE // 8, 1, 8, LANE), lambda i: (i, SUB_Y, 0, 0))],
        out_shape=[jax.ShapeDtypeStruct((8, t), F32), jax.ShapeDtypeStruct((32, LANE), F32),
                   jax.ShapeDtypeStruct(h2_sub.shape, h2_sub.dtype)],
        scratch_shapes=[pltpu.VMEM((32, 1), F32)],
        input_output_aliases={2: 2},
        compiler_params=_cparams(("arbitrary",)),
        name="route",
    )(logits, bias_col, h2_sub)
    return meta, cnt, h2_sub


FF_TILE = 512


def _expert_kernel(ea_ref, eb_ref, nvalid_ref, xs_ref, wga_ref, wua_ref, wda_ref, wgb_ref, wub_ref, wdb_ref, y_ref):
    del ea_ref, eb_ref
    j = pl.program_id(0)
    nvalid = nvalid_ref[j]

    @pl.when(nvalid == 0)
    def _():
        y_ref[...] = jnp.zeros_like(y_ref)

    @pl.when(nvalid != 0)
    def _():
        rows = y_ref.shape[0] * 8
        live = lax.broadcasted_iota(jnp.int32, (rows, 1), 0) < nvalid
        x = jnp.where(live, _unpack_rows(_load_subrows(xs_ref, 0, SUB_Y)), 0.0).astype(BF16)
        ws = jnp.where(live, lax.bitcast_convert_type(xs_ref[:, SUB_Y, :, :].reshape(rows, LANE), F32), 0.0)
        ff = wga_ref.shape[1]

        items = [(w, f0) for w in ((wga_ref, wua_ref, wda_ref), (wgb_ref, wub_ref, wdb_ref))
                 for f0 in range(0, ff, FF_TILE)]

        def up(item):
            (wg_ref, wu_ref, _), f0 = item
            return _dot(x, wg_ref[:, f0:f0 + FF_TILE]), _dot(x, wu_ref[:, f0:f0 + FF_TILE])

        ups = [up(items[0]), up(items[1])]
        parts = []
        for c, ((_, _, wd_ref), f0) in enumerate(items):
            hg, hu = ups[c]
            hh = (hg * _sigmoid(hg) * hu).astype(BF16)
            if c + 2 < len(items):
                ups.append(up(items[c + 2]))
            parts.append(_dot(hh, wd_ref[f0:f0 + FF_TILE, :]))
        per = len(items) // 2
        ya = functools.reduce(lambda a, b: a + b, parts[:per])
        yb = functools.reduce(lambda a, b: a + b, parts[per:])
        _store_subrows(y_ref, _pack_rows(ya * ws[:, 0:1] + yb * ws[:, 1:2]), 0)


def _experts(xs_sub, ea, eb, nvalid, wg, wu, wd, layer):
    nb = xs_sub.shape[0] * 8 // MOE_BLK
    d, ff = wg.shape[2], wg.shape[3]
    wspec_in = lambda which: pl.BlockSpec((None, None, d, ff),
                                          lambda j, ea, eb, v: (layer, (ea, eb)[which][j], 0, 0))
    wspec_out = lambda which: pl.BlockSpec((None, None, ff, d),
                                           lambda j, ea, eb, v: (layer, (ea, eb)[which][j], 0, 0))
    grid_spec = pltpu.PrefetchScalarGridSpec(
        num_scalar_prefetch=3,
        grid=(nb,),
        in_specs=[pl.BlockSpec((MOE_BLK // 8, SUB_X, 8, LANE), lambda j, ea, eb, v: (j, 0, 0, 0)),
                  wspec_in(0), wspec_in(0), wspec_out(0), wspec_in(1), wspec_in(1), wspec_out(1)],
        out_specs=pl.BlockSpec((MOE_BLK // 8, SUB_Y, 8, LANE), lambda j, ea, eb, v: (j, 0, 0, 0)),
    )
    return pl.pallas_call(
        _expert_kernel,
        grid_spec=grid_spec,
        out_shape=jax.ShapeDtypeStruct((nb * MOE_BLK // 8, SUB_Y, 8, LANE), jnp.uint32),
        compiler_params=_cparams(("arbitrary",)),
        name="experts",
    )(ea, eb, nvalid, xs_sub, wg, wu, wd, wg, wu, wd)


def _subrow_index(dest, nsub):
    t = dest.shape[0]
    d3 = dest.reshape(t // 8, 1, 8)
    j = jnp.arange(nsub, dtype=jnp.int32).reshape(1, nsub, 1)
    return ((d3 // 8) * (nsub * 8) + j * 8 + d3 % 8).reshape(t * nsub)


def _moe(h2_sub, logits, bias_col, wg, wu, wd, layer):
    t = logits.shape[0]
    meta, cnt, h2_sub = _route(logits, bias_col, h2_sub)
    cls = meta[0].astype(jnp.int32)
    rank = meta[1].astype(jnp.int32)
    counts = cnt[:N_CLASSES, 0].astype(jnp.int32)
    padded = (counts + MOE_BLK - 1) // MOE_BLK * MOE_BLK
    pad_end = jnp.cumsum(padded)
    pad_start = pad_end - padded
    class_ids = jnp.arange(N_CLASSES, dtype=jnp.int32)
    dest = rank + jnp.sum(jnp.where(cls[:, None] == class_ids[None, :], pad_start[None, :], 0), axis=1)
    nb = t // MOE_BLK + N_CLASSES
    p_rows = nb * MOE_BLK
    blk_start = jnp.arange(nb, dtype=jnp.int32) * MOE_BLK
    valid = blk_start < pad_end[-1]
    blk_cls = jnp.sum((pad_end[None, :] <= blk_start[:, None]).astype(jnp.int32), axis=-1)
    last_cls = jnp.sum((pad_end <= pad_end[-1] - 1).astype(jnp.int32))
    blk_cls = jnp.minimum(jnp.where(valid, blk_cls, last_cls), N_CLASSES - 1)
    nvalid = jnp.where(valid, jnp.clip(pad_start[blk_cls] + counts[blk_cls] - blk_start, 0, MOE_BLK), 0)
    pair_a = jnp.array([p[0] for p in PAIRS], jnp.int32)
    pair_b = jnp.array([p[1] for p in PAIRS], jnp.int32)
    grp = blk_cls // len(PAIRS)
    ea = grp * EXPERTS_PER_GROUP + pair_a[blk_cls % len(PAIRS)]
    eb = grp * EXPERTS_PER_GROUP + pair_b[blk_cls % len(PAIRS)]

    xs = _sc_scatter(h2_sub.reshape(t * SUB_X, LANE), _subrow_index(dest, SUB_X), p_rows * SUB_X)
    ys = _experts(xs.reshape(p_rows // 8, SUB_X, 8, LANE), ea, eb, nvalid.astype(jnp.int32), wg, wu, wd, layer)
    y = _sc_gather(ys.reshape(p_rows * SUB_Y, LANE), _subrow_index(dest, SUB_Y))
    return y.reshape(t // 8, SUB_Y, 8, LANE)


def _final_kernel(x_ref, y_ref, g5_ref, gf_ref, o_ref):
    x = x_ref[...] + g5_ref[...] * _unpack_rows(_load_subrows(y_ref, 0, SUB_Y))
    ms = jnp.mean(x * x, axis=-1, keepdims=True)
    o_ref[...] = x * lax.rsqrt(ms + EPS) * gf_ref[...]


def _final(x_mid, y, mod5, layer, g_final, *, tiles_per_seq):
    t, d = x_mid.shape
    tok = pl.BlockSpec((TOK_TILE, d), lambda i: (i, 0))
    return pl.pallas_call(
        _final_kernel,
        grid=(t // TOK_TILE,),
        in_specs=[tok, pl.BlockSpec((TOK_TILE // 8, SUB_Y, 8, LANE), lambda i: (i, 0, 0, 0)),
                  pl.BlockSpec((None, None, None, 1, d), lambda i: (layer, i // tiles_per_seq, 5, 0, 0)),
                  pl.BlockSpec(g_final.shape, lambda i: (0, 0))],
        out_specs=tok,
        out_shape=jax.ShapeDtypeStruct((t, d), F32),
        compiler_params=_cparams(("parallel",)),
        name="final_norm",
    )(x_mid, y, mod5, g_final)


def _rope_tables(seq):
    t = jnp.arange(seq)
    row = (t // GRID_W).astype(F32)
    col = (t % GRID_W).astype(F32)
    half = GLA_DK // 2
    inv = ROPE_BASE ** (-jnp.arange(0, half, 2, dtype=F32) / half)
    ang = jnp.concatenate([row[:, None] * inv, col[:, None] * inv], axis=-1)
    cos = jnp.repeat(jnp.cos(ang), 2, axis=-1)
    sin = (jnp.sin(ang)[:, :, None] * jnp.array([-1.0, 1.0], F32)).reshape(seq, GLA_DK)
    cos = jnp.tile(cos, (1, GLA_HEADS))
    sin = jnp.tile(sin, (1, GLA_HEADS))
    padw = QK_PAD - GLA_QK
    cos = jnp.pad(cos, ((0, 0), (0, padw)), constant_values=1.0)
    sin = jnp.pad(sin, ((0, 0), (0, padw)))
    cos = jnp.concatenate([cos, jnp.ones((TOK_TILE, QK_PAD), F32)], axis=0)
    sin = jnp.concatenate([sin, jnp.zeros((TOK_TILE, QK_PAD), F32)], axis=0)
    return cos, sin


def _pad_last(w, n):
    return jnp.pad(w, [(0, 0)] * (w.ndim - 1) + [(0, n - w.shape[-1])])


def _in_weights(w_in):
    offs = np.cumsum([0, GLA_QK, GLA_QK, GLA_WIDTH, GLA_WIDTH, 2 * GLA_LOWRANK, NA_WIDTH, NA_WIDTH, NA_WIDTH,
                      2 * CONV_WIDTH])
    seg = [w_in[..., offs[i]:offs[i + 1]] for i in range(9)]
    cols = [_pad_last(seg[0], QK_PAD), _pad_last(seg[1], QK_PAD), _pad_last(seg[4], LANE), seg[2], seg[3], seg[5],
            seg[6], seg[7], seg[8]]
    w = jnp.concatenate(cols, axis=-1).astype(BF16)
    assert w.shape[-1] == IN_COLS_PAD
    return w


def _gate_weights(wa_f, ba_f, wa_b, ba_b):
    depth = wa_f.shape[0]
    zero = jnp.zeros((depth, GLA_LOWRANK, QK_PAD), F32)
    top = jnp.concatenate([_pad_last(wa_f, QK_PAD), zero], axis=-1)
    bot = jnp.concatenate([zero, _pad_last(wa_b, QK_PAD)], axis=-1)
    rest = jnp.zeros((depth, LANE - 2 * GLA_LOWRANK, 2 * QK_PAD), F32)
    wa = jnp.concatenate([top, bot, rest], axis=1)
    ba = jnp.concatenate([_pad_last(ba_f, QK_PAD), _pad_last(ba_b, QK_PAD)], axis=-1)[:, None, :]
    return wa, ba


def _na_bias_tables(rpb):
    cq = np.arange(GRID_W)
    c0 = np.clip(cq - NA_KW // 2, 0, GRID_W - NA_KW)
    kc = np.arange(GRID_W)
    valid = (kc[None, :] >= c0[:, None]) & (kc[None, :] < c0[:, None] + NA_KW)
    dcol = np.clip(kc[None, :] - cq[:, None] + (NA_KW - 1), 0, 2 * NA_KW - 2)
    band = jnp.where(valid, rpb[..., dcol], NEG_BIG)
    tab = jnp.stack([band[:, :, NA_KH - 1 - v:2 * NA_KH - 1 - v] for v in range(NA_KH)], axis=2)
    tab = jnp.transpose(tab, (0, 1, 2, 4, 3, 5))
    return tab.reshape(rpb.shape[:2] + (NA_KH, GRID_W, NA_KH * GRID_W)).astype(F32)


def _head_mean_matrix():
    h = np.arange(GLA_WIDTH) // GLA_DV
    e = (h[:, None] == h[None, :]).astype(np.float32)
    return jnp.asarray(np.concatenate([e, e], axis=0), dtype=BF16)


def kernel(x, c, ctx, c_ctx, w_mod, b_mod, g_mix, g_ffn, w_in, gla_wa_f, gla_ba_f, gla_wa_b, gla_ba_b, gla_g_norm,
           na_rpb, conv_w, conv_b, conv_ln_g, conv_ln_b, w_out, router_w, router_bias, w_gate, w_up, w_down,
           g_final):
    bsz, seq, d = x.shape
    ctx_len = ctx.shape[1]
    depth = w_mod.shape[0]
    t_lat, t_ctx = bsz * seq, bsz * ctx_len
    assert seq % TOK_TILE == 0 and t_ctx % TOK_TILE == 0 and seq % ctx_len == 0
    assert d == 2 * SUB_Y * LANE
    assert seq // GRID_W >= NA_KH and (seq // GRID_W) % NA_UNROLL == 0
    assert ctx_len % (GLA_CHUNK * GLA_STEPS) == 0 and seq % (GLA_CHUNK * GLA_STEPS) == 0
    tiles_per_seq = seq // TOK_TILE
    n_lat_tiles = t_lat // TOK_TILE
    n_all_tiles = (t_lat + t_ctx) // TOK_TILE
    ctx_blk0 = t_lat // ctx_len

    mod_rows = -(-(bsz + 1) // 8) * 8
    c_pad = jnp.zeros((mod_rows, d), F32).at[:bsz].set(c).at[bsz].set(c_ctx)
    mod = _modulation(c_pad, w_mod, b_mod)
    mod5 = mod.reshape(depth, mod_rows, 6, 1, d)

    cos_t, sin_t = _rope_tables(seq)
    e_mat = _head_mean_matrix()
    rw = jnp.pad(router_w, ((0, 0), (0, LANE - N_EXPERTS)))
    rw_hi = rw.astype(BF16)
    rw_lo = (rw - rw_hi.astype(F32)).astype(BF16)
    rw_split = jnp.concatenate([rw_hi, rw_lo], axis=1)
    bias_col = router_bias.reshape(N_EXPERTS, 1).astype(F32)
    zero_state = jnp.zeros((bsz, GLA_WIDTH, QK_PAD), F32)

    w_aug = _in_weights(w_in)
    wa_aug, ba_aug = _gate_weights(gla_wa_f, gla_ba_f, gla_wa_b, gla_ba_b)
    bias_tab = _na_bias_tables(na_rpb)
    row = lambda a: a[:, None, :]
    gmix, gffn, gnorm = row(g_mix), row(g_ffn), row(gla_g_norm)
    cb, clg, clb = row(conv_b), row(conv_ln_g), row(conv_ln_b)
    w_out_b = w_out.astype(BF16)
    wg_b, wu_b, wd_b = w_gate.astype(BF16), w_up.astype(BF16), w_down.astype(BF16)

    x_all = jnp.concatenate([x.reshape(t_lat, d), ctx.reshape(t_ctx, d)], axis=0)
    y_moe = None
    for l in range(depth):
        last = l == depth - 1
        x_all, (q, k, v, sgate, gf, gb, nq, nk, nv, hcv) = _inproj(
            x_all, y_moe, mod5, l, gmix, cos_t, sin_t, w_aug, wa_aug, ba_aug,
            n_lat_tiles=n_lat_tiles, tiles_per_seq=tiles_per_seq, n_batch=bsz, res_layer=l - 1)

        o_g, st_f, st_b = _gla(q, k, v, gf, gb, zero_state, zero_state, None, seq=ctx_len, blk0=ctx_blk0,
                               n_batch=bsz)
        o_g, _, _ = _gla(q, k, v, gf, gb, st_f, st_b, o_g, seq=seq, blk0=0, n_batch=bsz)

        o_n = _na_latent(nq, nk, nv, bias_tab, l, seq=seq, ctx_len=ctx_len, n_batch=bsz, ctx_blk0=ctx_blk0)
        o_c = _conv(hcv, conv_w, cb, clg, clb, l, None, seq=seq, blk0=0, n_batch=bsz)
        if not last:
            o_n = _na_context(nq, nk, nv, o_n, ctx_len=ctx_len, n_batch=bsz, ctx_blk0=ctx_blk0)
            o_c = _conv(hcv, conv_w, cb, clg, clb, l, o_c, seq=ctx_len, blk0=ctx_blk0, n_batch=bsz)

        n_tiles = n_lat_tiles if last else n_all_tiles
        x_mid, h2, logits = _outproj(
            o_g[0], o_g[1], sgate, o_n, o_c, x_all, mod5, l, gnorm, e_mat, w_out_b, gffn, rw_split,
            n_tiles=n_tiles, n_lat_tiles=n_lat_tiles, tiles_per_seq=tiles_per_seq, n_batch=bsz)
        y_moe = _moe(h2, logits, bias_col, wg_b, wu_b, wd_b, l)
        x_all = x_mid

    out = _final(x_all, y_moe, mod5, depth - 1, g_final.reshape(1, d), tiles_per_seq=tiles_per_seq)
    return out.reshape(bsz, seq, d)
```

```python
import functools

import numpy as np
import jax
import jax.numpy as jnp
from jax import lax
from jax.experimental import pallas as pl
from jax.experimental.pallas import tpu as pltpu
from jax.experimental.pallas import tpu_sc as plsc

GRID_W = 64
EPS = 1e-6
GLA_HEADS, GLA_DK, GLA_DV = 4, 48, 96
GLA_QK = GLA_HEADS * GLA_DK
GLA_WIDTH = GLA_HEADS * GLA_DV
GLA_LOWRANK = 16
GLA_TAU = 16.0
ROPE_BASE = 10000.0
NA_HEADS, NA_DH = 6, 64
NA_WIDTH = NA_HEADS * NA_DH
NA_KH, NA_KW = 8, 16
CONV_WIDTH, CONV_K = 256, 31
N_EXPERTS, N_GROUPS, EXPERTS_PER_GROUP = 16, 4, 4
PAIRS = ((0, 1), (0, 2), (0, 3), (1, 2), (1, 3), (2, 3))
N_CLASSES = N_GROUPS * len(PAIRS)

LANE = 128
SUBLANES = 8
QK_PAD = 256
C_Q, C_K = 0, 256
C_A, C_V, C_GATE = 512, 640, 1024
C_NQ, C_NK, C_NV, C_CONV = 1408, 1792, 2176, 2560
IN_COLS_PAD = 3072
MXU_N = 256

TOK_TILE = 1024
ROW_PARTS = 4
GLA_CHUNK = 64
GLA_SUB = 16
GLA_STEPS = 4
MOE_BLK = 256
ROUTE_TILE = 512
NA_UNROLL = 8
NEG_BIG = -1e30
VMEM_LIMIT = 56 * 1024 * 1024

F32 = jnp.float32
BF16 = jnp.bfloat16


def _cparams(sem):
    return pltpu.CompilerParams(dimension_semantics=sem, vmem_limit_bytes=VMEM_LIMIT)


def _layer_spec(a, layer):
    return pl.BlockSpec((None,) + a.shape[1:], lambda *_: (layer,) + (0,) * (a.ndim - 1),
                        pipeline_mode=pl.Buffered(1))


def _dot(a, b):
    return jnp.dot(a, b, preferred_element_type=F32)


def _dot_nt(a, b):
    return lax.dot_general(a, b, (((1,), (1,)), ((), ())), preferred_element_type=F32)


def _split(a):
    hi = a.astype(BF16)
    lo = (a - hi.astype(F32)).astype(BF16)
    return hi, lo


def _sigmoid(x):
    return 1.0 / (1.0 + jnp.exp(-x))


SUB_X, SUB_Y = 5, 4
SC_WINDOW = 256


def _pack_bf16_pairs(a, b):
    ua = lax.bitcast_convert_type(a.astype(BF16).astype(F32), jnp.uint32)
    ub = lax.bitcast_convert_type(b.astype(BF16).astype(F32), jnp.uint32)
    return jnp.bitwise_or(jnp.right_shift(ua, jnp.uint32(16)), ub)


def _unpack_bf16_pairs(w):
    lo = lax.bitcast_convert_type(jnp.left_shift(w, jnp.uint32(16)), F32)
    hi = lax.bitcast_convert_type(jnp.bitwise_and(w, jnp.uint32(0xFFFF0000)), F32)
    return lo, hi


def _store_subrows(ref, val, j0, row0=0):
    r = val.shape[0]
    for j in range(val.shape[1] // LANE):
        ref[row0 // 8:(row0 + r) // 8, j0 + j, :, :] = val[:, j * LANE:(j + 1) * LANE].reshape(r // 8, 8, LANE)


def _load_subrows(ref, j0, n, row0=0, rows=None):
    r = ref.shape[0] * 8 if rows is None else rows
    return jnp.concatenate([ref[row0 // 8:(row0 + r) // 8, j0 + j, :, :].reshape(r, LANE) for j in range(n)],
                           axis=1)


def _pack_rows(x):
    half = x.shape[1] // 2
    return _pack_bf16_pairs(x[:, :half], x[:, half:])


def _unpack_rows(w):
    lo, hi = _unpack_bf16_pairs(w)
    return jnp.concatenate([lo, hi], axis=1)


def _sc_mesh():
    return plsc.VectorSubcoreMesh(core_axis_name="c", subcore_axis_name="s")


def _sc_scatter(src, idx, n_out):
    n, w = src.shape
    idx2 = idx.reshape(1, n)

    @pl.kernel(out_type=jax.ShapeDtypeStruct((n_out, w), src.dtype), mesh=_sc_mesh(), scratch_types=[])
    def scatter_kernel(x_hbm, i_hbm, o_hbm):
        def body(x_vmem, i_vmem):
            pltpu.sync_copy(x_vmem, o_hbm.at[i_vmem.at[0]])

        pltpu.emit_pipeline(
            body,
            grid=(n // SC_WINDOW,),
            in_specs=[pl.BlockSpec((SC_WINDOW, w), index_map=lambda i: (i, 0)),
                      pl.BlockSpec((1, SC_WINDOW), index_map=lambda i: (0, i))],
            out_specs=[],
            core_axis_name=("c", "s"),
            dimension_semantics=(pltpu.PARALLEL,),
        )(x_hbm, i_hbm)

    return scatter_kernel(src, idx2)


def _sc_gather(src, idx):
    n = idx.shape[0]
    w = src.shape[1]
    idx2 = idx.reshape(1, n)

    @pl.kernel(out_type=jax.ShapeDtypeStruct((n, w), src.dtype), mesh=_sc_mesh())
    def gather_kernel(x_hbm, i_hbm, o_hbm):
        def body(i_vmem, o_vmem):
            pltpu.sync_copy(x_hbm.at[i_vmem.at[0]], o_vmem)

        pltpu.emit_pipeline(
            body,
            grid=(n // SC_WINDOW,),
            in_specs=[pl.BlockSpec((1, SC_WINDOW), index_map=lambda i: (0, i))],
            out_specs=[pl.BlockSpec((SC_WINDOW, w), index_map=lambda i: (i, 0))],
            core_axis_name=("c", "s"),
            dimension_semantics=(pltpu.PARALLEL,),
        )(i_hbm, o_hbm)

    return gather_kernel(src, idx2)


def _mod_kernel(c_ref, w_ref, b_ref, o_ref):
    cv = c_ref[...]
    s = cv * _sigmoid(cv)
    s_hi, s_lo = _split(s)
    w_hi, w_lo = _split(w_ref[...])
    o_ref[...] = _dot(s_hi, w_hi) + _dot(s_lo, w_hi) + _dot(s_hi, w_lo) + b_ref[...]


def _modulation(c_pad, w_mod, b_mod):
    depth, d, six_d = w_mod.shape
    rows = c_pad.shape[0]
    nt = 1536
    return pl.pallas_call(
        _mod_kernel,
        grid=(depth, six_d // nt),
        in_specs=[
            pl.BlockSpec((rows, d), lambda l, j: (0, 0)),
            pl.BlockSpec((None, d, nt), lambda l, j: (l, 0, j)),
            pl.BlockSpec((None, 1, nt), lambda l, j: (l, 0, j)),
        ],
        out_specs=pl.BlockSpec((None, rows, nt), lambda l, j: (l, 0, j)),
        out_shape=jax.ShapeDtypeStruct((depth, rows, six_d), F32),
        compiler_params=_cparams(("arbitrary", "arbitrary")),
        name="modulation",
    )(c_pad, w_mod, b_mod.reshape(depth, 1, six_d))


def _inproj_kernel(*refs, has_res, n_lat_tiles):
    if has_res:
        x_ref, y_ref, g5_ref = refs[:3]
    else:
        x_ref, xctx_ref = refs[:2]
        is_lat = pl.program_id(0) < n_lat_tiles
    refs = refs[3 if has_res else 2:]
    (sh_ref, sc_ref, gmix_ref, cos_ref, sin_ref, w_ref, wa_ref, ba_ref) = refs[:8]
    (xnew_ref, q_ref, k_ref, v_ref, sg_ref, gf_ref, gb_ref, nq_ref, nk_ref, nv_ref, hcv_ref) = refs[8:]

    tile = x_ref.shape[0]
    part = tile // ROW_PARTS
    wa_hi, wa_lo = _split(wa_ref[...])
    even_lane = jnp.bitwise_and(lax.broadcasted_iota(jnp.int32, (1, LANE), 1), 1) == 0
    hbs = []
    for p in range(ROW_PARTS):
        rows = pl.ds(p * part, part)
        if has_res:
            x = x_ref[rows, :] + g5_ref[...] * _unpack_rows(_load_subrows(y_ref, 0, SUB_Y, row0=p * part, rows=part))
        else:
            x = jnp.where(is_lat, x_ref[rows, :], xctx_ref[rows, :])
        xnew_ref[rows, :] = x
        ms = jnp.mean(x * x, axis=-1, keepdims=True)
        h = x * lax.rsqrt(ms + EPS) * gmix_ref[...]
        h = h * (1.0 + sc_ref[...]) + sh_ref[...]
        hbs.append(h.astype(BF16))

    for p, hb in enumerate(hbs):
        rows = pl.ds(p * part, part)

        def proj(c0, n, hb=hb):
            return _dot(hb, w_ref[:, c0:c0 + n])

        cos = cos_ref[rows, :]
        sin = sin_ref[rows, :]

        def rope(t):
            halves = []
            for c0 in range(0, QK_PAD, LANE):
                th = t[:, c0:c0 + LANE]
                halves.append(jnp.where(even_lane, pltpu.roll(th, LANE - 1, 1), pltpu.roll(th, 1, 1)))
            return t * cos + jnp.concatenate(halves, axis=1) * sin

        mid = (C_A + C_CONV) // 2
        assert (mid - C_A) % MXU_N == 0 and C_NQ < mid < C_NK
        d1 = proj(C_A, mid - C_A)
        a_hi, a_lo = _split(d1[:, :LANE])
        q_ref[rows, :] = (rope(proj(C_Q, QK_PAD)) * (GLA_DK ** -0.5)).astype(BF16)
        k_ref[rows, :] = rope(proj(C_K, QK_PAD)).astype(BF16)

        z = _dot(a_hi, wa_hi) + _dot(a_lo, wa_hi) + _dot(a_hi, wa_lo) + ba_ref[...]
        logsig = jnp.minimum(z, 0.0) - jnp.log(1.0 + jnp.exp(-jnp.abs(z)))
        g = logsig * (1.0 / GLA_TAU)
        gf_ref[rows, :] = g[:, :QK_PAD]
        gb_ref[rows, :] = g[:, QK_PAD:]

        v_ref[rows, :] = d1[:, C_V - C_A:C_GATE - C_A].astype(BF16)
        gate = d1[:, C_GATE - C_A:C_NQ - C_A]
        sg_ref[rows, :] = (gate * _sigmoid(gate)).astype(BF16)
        d2 = proj(mid, C_CONV - mid)
        nq = jnp.concatenate([d1[:, C_NQ - C_A:], d2[:, :C_NK - mid]], axis=1)
        nq_ref[rows, :] = (nq * (NA_DH ** -0.5)).astype(BF16)
        nk_ref[rows, :] = d2[:, C_NK - mid:C_NV - mid].astype(BF16)
        nv_ref[rows, :] = d2[:, C_NV - mid:].astype(BF16)
        u = proj(C_CONV, 2 * CONV_WIDTH)
        hcv_ref[rows, :] = u[:, :CONV_WIDTH] * _sigmoid(u[:, CONV_WIDTH:])


def _inproj(x_all, y, mod5, layer, gmix, cos_t, sin_t, w_aug, wa_aug, ba_aug, *, n_lat_tiles, tiles_per_seq,
            n_batch, res_layer):
    has_res = y is not None
    d = x_all.shape[1] if has_res else x_all[0].shape[1]
    t_all = x_all.shape[0] if has_res else x_all[0].shape[0] + x_all[1].shape[0]
    nt = t_all // TOK_TILE

    def bidx(i):
        return jnp.where(i < n_lat_tiles, i // tiles_per_seq, n_batch)

    def ridx(i):
        return jnp.where(i < n_lat_tiles, i % tiles_per_seq, tiles_per_seq)

    def modspec(l, j):
        return pl.BlockSpec((None, None, None, 1, d), lambda i: (l, bidx(i), j, 0, 0))

    tok = lambda w: pl.BlockSpec((TOK_TILE, w), lambda i: (i, 0))
    full = lambda a: pl.BlockSpec(a.shape, lambda i: (0,) * a.ndim)

    if has_res:
        in_specs = [tok(d), pl.BlockSpec((TOK_TILE // 8, SUB_Y, 8, LANE), lambda i: (i, 0, 0, 0)),
                    modspec(res_layer, 5)]
        args = [x_all, y, mod5]
    else:
        in_specs = [pl.BlockSpec((TOK_TILE, d), lambda i: (jnp.minimum(i, n_lat_tiles - 1), 0)),
                    pl.BlockSpec((TOK_TILE, d), lambda i: (jnp.maximum(i - n_lat_tiles, 0), 0))]
        args = list(x_all)
    in_specs += [modspec(layer, 0), modspec(layer, 1), _layer_spec(gmix, layer),
                 pl.BlockSpec((TOK_TILE, QK_PAD), lambda i: (ridx(i), 0)),
                 pl.BlockSpec((TOK_TILE, QK_PAD), lambda i: (ridx(i), 0)),
                 _layer_spec(w_aug, layer), _layer_spec(wa_aug, layer), _layer_spec(ba_aug, layer)]
    args += [mod5, mod5, gmix, cos_t, sin_t, w_aug, wa_aug, ba_aug]

    out_widths = [(d, F32), (QK_PAD, BF16), (QK_PAD, BF16), (GLA_WIDTH, BF16), (GLA_WIDTH, BF16), (QK_PAD, F32),
                  (QK_PAD, F32), (NA_WIDTH, BF16), (NA_WIDTH, BF16), (NA_WIDTH, BF16), (CONV_WIDTH, F32)]
    out_specs = [tok(w) for w, _ in out_widths]
    out_shape = [jax.ShapeDtypeStruct((t_all, w), dt) for w, dt in out_widths]
    res = pl.pallas_call(
        functools.partial(_inproj_kernel, has_res=has_res, n_lat_tiles=n_lat_tiles),
        grid=(nt,),
        in_specs=in_specs,
        out_specs=out_specs,
        out_shape=out_shape,
        compiler_params=_cparams(("parallel",)),
        name="inproj",
    )(*args)
    return res[0], res[1:]


def _gla_masks():
    c, sub = GLA_CHUNK, GLA_SUB
    lane_qk = lax.broadcasted_iota(jnp.int32, (1, QK_PAD), 1)
    head_qk = ((lane_qk >= GLA_DK).astype(jnp.int32) + (lane_qk >= 2 * GLA_DK).astype(jnp.int32)
               + (lane_qk >= 3 * GLA_DK).astype(jnp.int32) + 4 * (lane_qk >= 4 * GLA_DK).astype(jnp.int32))
    row_h = jnp.right_shift(lax.broadcasted_iota(jnp.int32, (c, 1), 0), GLA_SUB.bit_length() - 1)
    hm = (row_h == head_qk).astype(F32)
    row_v = lax.broadcasted_iota(jnp.int32, (GLA_WIDTH, 1), 0)
    head_v = ((row_v >= GLA_DV).astype(jnp.int32) + (row_v >= 2 * GLA_DV).astype(jnp.int32)
              + (row_v >= 3 * GLA_DV).astype(jnp.int32))
    bd = (head_v == head_qk).astype(F32)
    lane_v = lax.broadcasted_iota(jnp.int32, (1, GLA_WIDTH), 1)
    vm = [((lane_v >= h * GLA_DV) & (lane_v < (h + 1) * GLA_DV)).astype(F32) for h in range(GLA_HEADS)]
    return hm, bd, vm


def _gla_steps(q, k, v, g, s_t, hm, bd, vm):
    c, sub = GLA_CHUNK, GLA_SUB
    nsub = c // sub
    dirs = (True, False)
    items = [(u, d) for u in range(GLA_STEPS) for d in range(2)]
    ri = lax.broadcasted_iota(jnp.int32, (c, c), 0)
    ci = lax.broadcasted_iota(jnp.int32, (c, c), 1)
    key_row = lax.broadcasted_iota(jnp.int32, (c, 1), 0)
    att_row = jnp.bitwise_and(ri, sub - 1)
    tri = [((ci <= ri) if fwd else (ci >= ri)).astype(BF16) for fwd in dirs]

    cums = {}
    for u, d in items:
        g_hi, g_lo = _split(g[u][d])
        cums[u, d] = _dot(tri[d], g_hi) + _dot(tri[d], g_lo)

    qe, kv, decay, atts = {}, {}, {}, {}
    for u, d in items:
        fwd = dirs[d]
        cum = cums[u, d]
        tot = cum[c - 1:c] if fwd else cum[0:1]
        qe[u, d] = (q[u][d] * jnp.exp(cum)).astype(BF16)
        k_end = (k[u][d] * jnp.exp(tot - cum)).astype(BF16)
        kv[u, d] = lax.dot_general(v[u][d], k_end, (((0,), (0,)), ((), ())), preferred_element_type=F32)
        decay[u, d] = jnp.exp(tot)
        att_d = []
        for i in range(nsub):
            lo, hi = i * sub, (i + 1) * sub
            if fwd:
                ref = cum[lo - 1:lo] if i > 0 else jnp.zeros((1, QK_PAD), F32)
                key_ok = key_row < hi
                causal = ci <= att_row + lo
            else:
                ref = cum[hi:hi + 1] if i < nsub - 1 else jnp.zeros((1, QK_PAD), F32)
                key_ok = key_row >= lo
                causal = ci >= att_row + lo
            qi = q[u][d][lo:hi] * jnp.exp(cum[lo:hi] - ref)
            qs = (jnp.concatenate([qi] * GLA_HEADS, axis=0) * hm).astype(BF16)
            ki = (k[u][d] * jnp.exp(jnp.where(key_ok, ref - cum, NEG_BIG))).astype(BF16)
            att = _dot_nt(qs, ki)
            att_d.append(jnp.where(causal, att, 0.0).astype(BF16))
        atts[u, d] = jnp.concatenate(att_d, axis=0)

    s = list(s_t)
    o_inter = {}
    for u, d in items:
        o_inter[u, d] = _dot_nt(qe[u, d], s[d].astype(BF16))
        s[d] = s[d] * decay[u, d] + bd * kv[u, d]

    outs = [[None, None] for _ in range(GLA_STEPS)]
    for u, d in items:
        r = _dot(atts[u, d], v[u][d])
        blocks = []
        for i in range(nsub):
            base = i * c
            oi = r[base:base + sub] * vm[0]
            for h in range(1, GLA_HEADS):
                oi = oi + r[base + h * sub:base + (h + 1) * sub] * vm[h]
            blocks.append(oi)
        outs[u][d] = o_inter[u, d] + jnp.concatenate(blocks, axis=0)
    return outs, s


def _gla_kernel(q_ref, k_ref, v_ref, gf_ref, gb_ref, sf0_ref, sb0_ref, of_ref, ob_ref, sf_ref, sb_ref):
    n = q_ref.shape[0]
    nc = n // GLA_CHUNK
    hm, bd, vm = _gla_masks()
    sf_ref[...] = sf0_ref[...]
    sb_ref[...] = sb0_ref[...]

    def body(j, carry):
        rows = [[pl.ds(pl.multiple_of(cidx * GLA_CHUNK, GLA_CHUNK), GLA_CHUNK)
                 for cidx in (j * GLA_STEPS + u, nc - 1 - (j * GLA_STEPS + u))] for u in range(GLA_STEPS)]
        q = [[q_ref[r, :].astype(F32) for r in ru] for ru in rows]
        k = [[k_ref[r, :].astype(F32) for r in ru] for ru in rows]
        v = [[v_ref[r, :] for r in ru] for ru in rows]
        g = [[gf_ref[ru[0], :], gb_ref[ru[1], :]] for ru in rows]
        outs, s_new = _gla_steps(q, k, v, g, [sf_ref[...], sb_ref[...]], hm, bd, vm)
        sf_ref[...] = s_new[0]
        sb_ref[...] = s_new[1]
        for u in range(GLA_STEPS):
            of_ref[rows[u][0], :] = outs[u][0]
            ob_ref[rows[u][1], :] = outs[u][1]
        return carry

    lax.fori_loop(0, nc // GLA_STEPS, body, 0)


def _gla(q, k, v, gf, gb, sf0, sb0, o_prev, *, seq, blk0, n_batch):
    t_all = q.shape[0]
    tokw = lambda w: pl.BlockSpec((seq, w), lambda b: (blk0 + b, 0))
    st = pl.BlockSpec((None, GLA_WIDTH, QK_PAD), lambda b: (b, 0, 0))
    in_specs = [tokw(QK_PAD), tokw(QK_PAD), tokw(GLA_WIDTH), tokw(QK_PAD), tokw(QK_PAD), st, st]
    args = [q, k, v, gf, gb, sf0, sb0]
    aliases = {}
    n_in = len(args)
    if o_prev is not None:
        in_specs += [pl.BlockSpec(memory_space=pl.ANY)] * 2
        args += list(o_prev)
        aliases = {n_in: 0, n_in + 1: 1}

    def kern(*refs):
        _gla_kernel(*refs[:n_in], *refs[len(args):])

    st_shape = jax.ShapeDtypeStruct((n_batch, GLA_WIDTH, QK_PAD), F32)
    o_shape = jax.ShapeDtypeStruct((t_all, GLA_WIDTH), F32)
    o_f, o_b, s_f, s_b = pl.pallas_call(
        kern,
        grid=(n_batch,),
        in_specs=in_specs,
        out_specs=[tokw(GLA_WIDTH), tokw(GLA_WIDTH), st, st],
        out_shape=[o_shape, o_shape, st_shape, st_shape],
        input_output_aliases=aliases,
        compiler_params=_cparams(("parallel",)),
        name="gla",
    )(*args)
    return (o_f, o_b), s_f, s_b


def _na_kernel(q_ref, k_ref, v_ref, kc_ref, vc_ref, bias_ref, o_ref):
    n = q_ref.shape[0]
    rows = n // GRID_W
    nkeys = NA_KH * GRID_W
    lane = lax.broadcasted_iota(jnp.int32, (1, LANE), 1)
    first = lane < NA_DH
    kc = kc_ref[...]
    vc = vc_ref[...]

    def body(jb, carry):
        items = []
        for j in range(NA_UNROLL):
            r = jb * NA_UNROLL + j
            r0 = jnp.clip(r - NA_KH // 2, 0, rows - NA_KH)
            var = r - r0
            qrows = pl.ds(pl.multiple_of(r * GRID_W, GRID_W), GRID_W)
            krows = pl.ds(pl.multiple_of(r0 * GRID_W, GRID_W), nkeys)
            qr = q_ref[qrows, :]
            kb = k_ref[krows, :]
            q2 = jnp.concatenate([jnp.where(first, qr, jnp.zeros_like(qr)),
                                  jnp.where(first, jnp.zeros_like(qr), qr)], axis=0)
            items.append((qrows, krows, var, _dot_nt(q2, kb), _dot_nt(q2, kc)))
        probs = []
        for qrows, krows, var, s_loc, s_ctx in items:
            s_loc = s_loc + jnp.concatenate([bias_ref[0, var], bias_ref[1, var]], axis=0)
            m = jnp.maximum(jnp.max(s_loc, axis=-1, keepdims=True), jnp.max(s_ctx, axis=-1, keepdims=True))
            p_loc = jnp.exp(s_loc - m)
            p_ctx = jnp.exp(s_ctx - m)
            l = jnp.sum(p_loc, axis=-1, keepdims=True) + jnp.sum(p_ctx, axis=-1, keepdims=True)
            probs.append((p_loc.astype(BF16), p_ctx.astype(BF16), l))
        for (qrows, krows, var, _, _), (p_loc, p_ctx, l) in zip(items, probs):
            o = (_dot(p_loc, v_ref[krows, :]) + _dot(p_ctx, vc)) / l
            o_ref[qrows, :] = jnp.where(first, o[:GRID_W], o[GRID_W:]).astype(o_ref.dtype)
        return carry

    lax.fori_loop(0, rows // NA_UNROLL, body, 0)


def _na_latent(nq, nk, nv, bias_tab, layer, *, seq, ctx_len, n_batch, ctx_blk0):
    t_all = nq.shape[0]
    npair = NA_HEADS // 2
    lat = pl.BlockSpec((seq, LANE), lambda b, p: (b, p))
    ctx = pl.BlockSpec((ctx_len, LANE), lambda b, p: (ctx_blk0 + b, p))
    bias = pl.BlockSpec((None, 2) + bias_tab.shape[2:], lambda b, p: (layer, p, 0, 0, 0))
    return pl.pallas_call(
        _na_kernel,
        grid=(n_batch, npair),
        in_specs=[lat, lat, lat, ctx, ctx, bias],
        out_specs=lat,
        out_shape=jax.ShapeDtypeStruct((t_all, NA_WIDTH), BF16),
        compiler_params=_cparams(("parallel", "arbitrary")),
        name="na_latent",
    )(nq, nk, nv, nk, nv, bias_tab)


def _na_ctx_kernel(q_ref, k_ref, v_ref, o_in_ref, o_ref):
    del o_in_ref
    lane = lax.broadcasted_iota(jnp.int32, (1, LANE), 1)
    first = lane < NA_DH
    q = q_ref[...]
    k = k_ref[...]
    v = v_ref[...]
    res = []
    for h in range(2):
        sel = first if h == 0 else jnp.logical_not(first)
        qh = jnp.where(sel, q, jnp.zeros_like(q))
        s = _dot_nt(qh, k)
        m = jnp.max(s, axis=-1, keepdims=True)
        p = jnp.exp(s - m)
        l = jnp.sum(p, axis=-1, keepdims=True)
        res.append(_dot(p.astype(BF16), v) / l)
    o_ref[...] = jnp.where(first, res[0], res[1]).astype(o_ref.dtype)


def _na_context(nq, nk, nv, o_prev, *, ctx_len, n_batch, ctx_blk0):
    npair = NA_HEADS // 2
    ctx = pl.BlockSpec((ctx_len, LANE), lambda b, p: (ctx_blk0 + b, p))
    return pl.pallas_call(
        _na_ctx_kernel,
        grid=(n_batch, npair),
        in_specs=[ctx, ctx, ctx, pl.BlockSpec(memory_space=pl.ANY)],
        out_specs=ctx,
        out_shape=jax.ShapeDtypeStruct(o_prev.shape, o_prev.dtype),
        input_output_aliases={3: 0},
        compiler_params=_cparams(("parallel", "arbitrary")),
        name="na_context",
    )(nq, nk, nv, o_prev)


CONV_HALO = 16
CONV_ROWS = 128


def _conv_kernel(*refs, has_prev):
    if has_prev:
        h_ref, w_ref, b_ref, lg_ref, lb_ref, _, o_ref, pad_ref, sh_ref = refs
    else:
        h_ref, w_ref, b_ref, lg_ref, lb_ref, o_ref, pad_ref, sh_ref = refs
    n = h_ref.shape[0]
    zeros = jnp.zeros((CONV_HALO, CONV_WIDTH), F32)
    pad_ref[0:CONV_HALO, :] = zeros
    pad_ref[CONV_HALO + n:CONV_HALO + n + CONV_HALO, :] = zeros
    pad_ref[CONV_HALO:CONV_HALO + n, :] = h_ref[...]
    span = n + 2 * CONV_HALO - SUBLANES
    for s in range(SUBLANES):
        sh_ref[s, 0:span, :] = pad_ref[s:s + span, :]
    w = w_ref[...]
    off = CONV_HALO - CONV_K // 2

    def chunk(cidx, carry):
        base = pl.multiple_of(cidx * CONV_ROWS, CONV_ROWS)
        acc = jnp.zeros((CONV_ROWS, CONV_WIDTH), F32) + b_ref[...]
        for j in range(CONV_K):
            s = (off + j) % SUBLANES
            acc = acc + sh_ref[s, pl.ds(base + (off + j - s), CONV_ROWS), :] * w[j:j + 1, :]
        mu = jnp.mean(acc, axis=-1, keepdims=True)
        xc = acc - mu
        var = jnp.mean(xc * xc, axis=-1, keepdims=True)
        y = xc * lax.rsqrt(var + EPS) * lg_ref[...] + lb_ref[...]
        o_ref[pl.ds(base, CONV_ROWS), :] = (y * _sigmoid(y)).astype(o_ref.dtype)
        return carry

    lax.fori_loop(0, n // CONV_ROWS, chunk, 0)


def _conv(hcv, w, b, lg, lb, layer, o_prev, *, seq, blk0, n_batch):
    t_all = hcv.shape[0]
    tok = pl.BlockSpec((seq, CONV_WIDTH), lambda i: (blk0 + i, 0))
    full = lambda a: pl.BlockSpec(a.shape, lambda i: (0,) * a.ndim)
    in_specs = [tok] + [_layer_spec(a, layer) for a in (w, b, lg, lb)]
    args = [hcv, w, b, lg, lb]
    aliases = {}
    if o_prev is not None:
        in_specs.append(pl.BlockSpec(memory_space=pl.ANY))
        args.append(o_prev)
        aliases = {5: 0}
    return pl.pallas_call(
        functools.partial(_conv_kernel, has_prev=o_prev is not None),
        grid=(n_batch,),
        in_specs=in_specs,
        out_specs=tok,
        out_shape=jax.ShapeDtypeStruct((t_all, CONV_WIDTH), BF16),
        scratch_shapes=[pltpu.VMEM((seq + 2 * CONV_HALO, CONV_WIDTH), F32),
                        pltpu.VMEM((SUBLANES, seq + 2 * CONV_HALO, CONV_WIDTH), F32)],
        input_output_aliases=aliases,
        compiler_params=_cparams(("parallel",)),
        name="conv",
    )(*args)


def _outproj_kernel(ogf_ref, ogb_ref, sg_ref, on_ref, oc_ref, x_ref, g2_ref, sh_ref, sc_ref, gn_ref, e_ref,
                    wo_ref, gffn_ref, rw_ref, xmid_ref, h2_ref, lg_ref):
    tile = x_ref.shape[0]
    parts = [pl.ds(p * (tile // ROW_PARTS), tile // ROW_PARTS) for p in range(ROW_PARTS)]
    e = e_ref[...]
    rw = rw_ref[...]

    ofs, mss = [], []
    for rows in parts:
        of = ogf_ref[rows, :] + ogb_ref[rows, :]
        sq_hi, sq_lo = _split(of * of)
        ofs.append(of)
        mss.append(_dot(jnp.concatenate([sq_hi, sq_lo], axis=1), e) * (1.0 / GLA_DV))
    ys = []
    for rows, of, ms in zip(parts, ofs, mss):
        og = of * lax.rsqrt(ms + EPS) * gn_ref[...] * sg_ref[rows, :].astype(F32)
        mix = jnp.concatenate([og.astype(BF16), on_ref[rows, :], oc_ref[rows, :]], axis=1)
        ys.append(_dot(mix, wo_ref[...]))
    for p, (rows, y) in enumerate(zip(parts, ys)):
        x = x_ref[rows, :] + g2_ref[...] * y
        xmid_ref[rows, :] = x
        ms2 = jnp.mean(x * x, axis=-1, keepdims=True)
        h2 = x * lax.rsqrt(ms2 + EPS) * gffn_ref[...]
        h2 = h2 * (1.0 + sc_ref[...]) + sh_ref[...]
        h_hi, h_lo = _split(h2)
        _store_subrows(h2_ref, _pack_rows(h2), 0, row0=p * (tile // ROW_PARTS))
        hw = _dot(h_hi, rw)
        lg_ref[rows, :] = hw[:, :LANE] + hw[:, LANE:] + _dot(h_lo, rw[:, :LANE])


def _outproj(o_gf, o_gb, sgate, o_n, o_c, x_all, mod5, layer, gnorm, e_mat, w_out, gffn, rw_split, *, n_tiles,
             n_lat_tiles, tiles_per_seq, n_batch):
    d = x_all.shape[1]
    t_out = n_tiles * TOK_TILE

    def bidx(i):
        return jnp.where(i < n_lat_tiles, i // tiles_per_seq, n_batch)

    def modspec(j):
        return pl.BlockSpec((None, None, None, 1, d), lambda i: (layer, bidx(i), j, 0, 0))

    tok = lambda w: pl.BlockSpec((TOK_TILE, w), lambda i: (i, 0))
    full = lambda a: pl.BlockSpec(a.shape, lambda i: (0,) * a.ndim)
    return pl.pallas_call(
        _outproj_kernel,
        grid=(n_tiles,),
        in_specs=[tok(GLA_WIDTH), tok(GLA_WIDTH), tok(GLA_WIDTH), tok(NA_WIDTH), tok(CONV_WIDTH), tok(d),
                  modspec(2), modspec(3), modspec(4), _layer_spec(gnorm, layer), full(e_mat),
                  _layer_spec(w_out, layer), _layer_spec(gffn, layer),
                  full(rw_split)],
        out_specs=[tok(d), pl.BlockSpec((TOK_TILE // 8, SUB_Y, 8, LANE), lambda i: (i, 0, 0, 0)), tok(LANE)],
        out_shape=[jax.ShapeDtypeStruct((t_out, d), F32),
                   jax.ShapeDtypeStruct((t_out // 8, SUB_X, 8, LANE), jnp.uint32),
                   jax.ShapeDtypeStruct((t_out, LANE), F32)],
        compiler_params=_cparams(("parallel",)),
        name="outproj",
    )(o_gf, o_gb, sgate, o_n, o_c, x_all, mod5, mod5, mod5, gnorm, e_mat, w_out, gffn, rw_split)


def _route_kernel(lg_ref, bias_ref, h2_in_ref, meta_ref, cnt_ref, h2w_ref, carry_ref):
    del h2_in_ref
    tile = lg_ref.shape[0]

    @pl.when(pl.program_id(0) == 0)
    def _():
        carry_ref[...] = jnp.zeros_like(carry_ref)

    lt = lg_ref[...].T
    aff = _sigmoid(lt[0:N_EXPERTS])
    sel = aff + bias_ref[...]
    s = [sel[e:e + 1] for e in range(N_EXPERTS)]
    a = [aff[e:e + 1] for e in range(N_EXPERTS)]

    def top2sum(v):
        best = v[0] + v[1]
        for i, j in PAIRS[1:]:
            best = jnp.maximum(best, v[i] + v[j])
        return best

    gs = [top2sum(s[4 * g:4 * g + 4]) for g in range(N_GROUPS)]
    gbest = jnp.zeros_like(gs[0], dtype=jnp.int32)
    gmax = gs[0]
    for g in range(1, N_GROUPS):
        upd = gs[g] > gmax
        gbest = jnp.where(upd, g, gbest)
        gmax = jnp.where(upd, gs[g], gmax)

    def pick(vals, j):
        out = vals[j]
        for g in range(1, N_GROUPS):
            out = jnp.where(gbest == g, vals[4 * g + j], out)
        return out

    sv = [pick(s, j) for j in range(EXPERTS_PER_GROUP)]
    av = [pick(a, j) for j in range(EXPERTS_PER_GROUP)]
    i1 = jnp.zeros_like(gbest)
    m1 = sv[0]
    for j in range(1, EXPERTS_PER_GROUP):
        upd = sv[j] > m1
        i1 = jnp.where(upd, j, i1)
        m1 = jnp.where(upd, sv[j], m1)
    i2 = jnp.full_like(gbest, -1)
    m2 = jnp.zeros_like(m1)
    for j in range(EXPERTS_PER_GROUP):
        upd = (i1 != j) & ((sv[j] > m2) | (i2 < 0))
        i2 = jnp.where(upd, j, i2)
        m2 = jnp.where(upd, sv[j], m2)
    ia = jnp.minimum(i1, i2)
    ib = jnp.maximum(i1, i2)
    pair = jnp.where(ia == 0, ib - 1, jnp.where(ia == 1, ib + 1, 5))
    cls = gbest * len(PAIRS) + pair

    def take(vals, idx):
        out = vals[0]
        for j in range(1, EXPERTS_PER_GROUP):
            out = jnp.where(idx == j, vals[j], out)
        return out

    w1 = take(av, i1)
    w2 = take(av, i2)
    tot = w1 + w2
    wa = jnp.where(i1 < i2, w1, w2) / tot
    wb = jnp.where(i1 < i2, w2, w1) / tot

    crow = lax.broadcasted_iota(jnp.int32, (32, tile), 0)
    oh = (crow == cls).astype(F32)
    us = lax.broadcasted_iota(jnp.int32, (tile, tile), 0)
    ut = lax.broadcasted_iota(jnp.int32, (tile, tile), 1)
    upper = (us < ut).astype(BF16)
    prefix = _dot(oh.astype(BF16), upper)
    carry = carry_ref[...]
    rank = jnp.sum(oh * (prefix + carry), axis=0, keepdims=True)
    carry_new = carry + jnp.sum(oh, axis=1, keepdims=True)
    carry_ref[...] = carry_new
    cnt_ref[...] = jnp.broadcast_to(carry_new, cnt_ref.shape)

    meta_ref[...] = jnp.zeros_like(meta_ref)
    meta_ref[0:1, :] = cls.astype(F32)
    meta_ref[1:2, :] = rank

    wrow = lax.broadcasted_iota(jnp.int32, (LANE, tile), 0)
    wmat = jnp.where(wrow == 0, wa, jnp.where(wrow == 1, wb, 0.0))
    wtok = lax.bitcast_convert_type(wmat.T, jnp.uint32)
    h2w_ref[:, 0, :, :] = wtok.reshape(tile // 8, 8, LANE)


def _route(logits, bias_col, h2_sub):
    t = logits.shape[0]
    nt = t // ROUTE_TILE
    meta, cnt, h2_sub = pl.pallas_call(
        _route_kernel,
        grid=(nt,),
        in_specs=[pl.BlockSpec((ROUTE_TILE, LANE), lambda i: (i, 0)),
                  pl.BlockSpec(bias_col.shape, lambda i: (0, 0)),
                  pl.BlockSpec(memory_space=pl.ANY)],
        out_specs=[pl.BlockSpec((8, ROUTE_TILE), lambda i: (0, i)),
                   pl.BlockSpec((32, LANE), lambda i: (0, 0)),
                   pl.BlockSpec((ROUTE_TILE // 8, 1, 8, LANE), lambda i: (i, SUB_Y, 0, 0))],
        out_shape=[jax.ShapeDtypeStruct((8, t), F32), jax.ShapeDtypeStruct((32, LANE), F32),
                   jax.ShapeDtypeStruct(h2_sub.shape, h2_sub.dtype)],
        scratch_shapes=[pltpu.VMEM((32, 1), F32)],
        input_output_aliases={2: 2},
        compiler_params=_cparams(("arbitrary",)),
        name="route",
    )(logits, bias_col, h2_sub)
    return meta, cnt, h2_sub


FF_TILE = 512


def _expert_kernel(ea_ref, eb_ref, nvalid_ref, xs_ref, wga_ref, wua_ref, wda_ref, wgb_ref, wub_ref, wdb_ref, y_ref):
    del ea_ref, eb_ref
    j = pl.program_id(0)
    nvalid = nvalid_ref[j]

    @pl.when(nvalid == 0)
    def _():
        y_ref[...] = jnp.zeros_like(y_ref)

    @pl.when(nvalid != 0)
    def _():
        rows = y_ref.shape[0] * 8
        live = lax.broadcasted_iota(jnp.int32, (rows, 1), 0) < nvalid
        x = jnp.where(live, _unpack_rows(_load_subrows(xs_ref, 0, SUB_Y)), 0.0).astype(BF16)
        ws = jnp.where(live, lax.bitcast_convert_type(xs_ref[:, SUB_Y, :, :].reshape(rows, LANE), F32), 0.0)
        ff = wga_ref.shape[1]

        items = [(w, f0) for w in ((wga_ref, wua_ref, wda_ref), (wgb_ref, wub_ref, wdb_ref))
                 for f0 in range(0, ff, FF_TILE)]

        def up(item):
            (wg_ref, wu_ref, _), f0 = item
            return _dot(x, wg_ref[:, f0:f0 + FF_TILE]), _dot(x, wu_ref[:, f0:f0 + FF_TILE])

        ups = [up(items[0]), up(items[1])]
        parts = []
        for c, ((_, _, wd_ref), f0) in enumerate(items):
            hg, hu = ups[c]
            hh = (hg * _sigmoid(hg) * hu).astype(BF16)
            if c + 2 < len(items):
                ups.append(up(items[c + 2]))
            parts.append(_dot(hh, wd_ref[f0:f0 + FF_TILE, :]))
        per = len(items) // 2
        ya = functools.reduce(lambda a, b: a + b, parts[:per])
        yb = functools.reduce(lambda a, b: a + b, parts[per:])
        _store_subrows(y_ref, _pack_rows(ya * ws[:, 0:1] + yb * ws[:, 1:2]), 0)


def _experts(xs_sub, ea, eb, nvalid, wg, wu, wd, layer):
    nb = xs_sub.shape[0] * 8 // MOE_BLK
    d, ff = wg.shape[2], wg.shape[3]
    wspec_in = lambda which: pl.BlockSpec((None, None, d, ff),
                                          lambda j, ea, eb, v: (layer, (ea, eb)[which][j], 0, 0))
    wspec_out = lambda which: pl.BlockSpec((None, None, ff, d),
                                           lambda j, ea, eb, v: (layer, (ea, eb)[which][j], 0, 0))
    grid_spec = pltpu.PrefetchScalarGridSpec(
        num_scalar_prefetch=3,
        grid=(nb,),
        in_specs=[pl.BlockSpec((MOE_BLK // 8, SUB_X, 8, LANE), lambda j, ea, eb, v: (j, 0, 0, 0)),
                  wspec_in(0), wspec_in(0), wspec_out(0), wspec_in(1), wspec_in(1), wspec_out(1)],
        out_specs=pl.BlockSpec((MOE_BLK // 8, SUB_Y, 8, LANE), lambda j, ea, eb, v: (j, 0, 0, 0)),
    )
    return pl.pallas_call(
        _expert_kernel,
        grid_spec=grid_spec,
        out_shape=jax.ShapeDtypeStruct((nb * MOE_BLK // 8, SUB_Y, 8, LANE), jnp.uint32),
        compiler_params=_cparams(("arbitrary",)),
        name="experts",
    )(ea, eb, nvalid, xs_sub, wg, wu, wd, wg, wu, wd)


def _subrow_index(dest, nsub):
    t = dest.shape[0]
    base = ((dest // 8) * (nsub * 8) + dest % 8).astype(F32).reshape(t // LANE, LANE)
    src = np.arange(LANE)
    grp, r = src // 8, src % 8
    sel = np.zeros((LANE, LANE * nsub), np.float32)
    off = np.zeros((LANE * nsub,), np.int32)
    for j in range(nsub):
        pos = grp * (nsub * 8) + j * 8 + r
        sel[src, pos] = 1.0
        off[pos] = j * 8
    idx = jnp.dot(base, jnp.asarray(sel), precision=lax.Precision.HIGHEST).astype(jnp.int32) + jnp.asarray(off)
    return idx.reshape(t * nsub)


def _moe(h2_sub, logits, bias_col, wg, wu, wd, layer):
    t = logits.shape[0]
    meta, cnt, h2_sub = _route(logits, bias_col, h2_sub)
    cls = meta[0].astype(jnp.int32)
    rank = meta[1].astype(jnp.int32)
    counts = cnt[:N_CLASSES, 0].astype(jnp.int32)
    padded = (counts + MOE_BLK - 1) // MOE_BLK * MOE_BLK
    pad_end = jnp.cumsum(padded)
    pad_start = pad_end - padded
    class_ids = jnp.arange(N_CLASSES, dtype=jnp.int32)
    dest = rank + jnp.sum(jnp.where(cls[:, None] == class_ids[None, :], pad_start[None, :], 0), axis=1)
    nb = t // MOE_BLK + N_CLASSES
    p_rows = nb * MOE_BLK
    blk_start = jnp.arange(nb, dtype=jnp.int32) * MOE_BLK
    valid = blk_start < pad_end[-1]
    blk_cls = jnp.sum((pad_end[None, :] <= blk_start[:, None]).astype(jnp.int32), axis=-1)
    last_cls = jnp.sum((pad_end <= pad_end[-1] - 1).astype(jnp.int32))
    blk_cls = jnp.minimum(jnp.where(valid, blk_cls, last_cls), N_CLASSES - 1)
    nvalid = jnp.where(valid, jnp.clip(pad_start[blk_cls] + counts[blk_cls] - blk_start, 0, MOE_BLK), 0)
    pair_a = jnp.array([p[0] for p in PAIRS], jnp.int32)
    pair_b = jnp.array([p[1] for p in PAIRS], jnp.int32)
    grp = blk_cls // len(PAIRS)
    ea = grp * EXPERTS_PER_GROUP + pair_a[blk_cls % len(PAIRS)]
    eb = grp * EXPERTS_PER_GROUP + pair_b[blk_cls % len(PAIRS)]

    xs = _sc_scatter(h2_sub.reshape(t * SUB_X, LANE), _subrow_index(dest, SUB_X), p_rows * SUB_X)
    ys = _experts(xs.reshape(p_rows // 8, SUB_X, 8, LANE), ea, eb, nvalid.astype(jnp.int32), wg, wu, wd, layer)
    y = _sc_gather(ys.reshape(p_rows * SUB_Y, LANE), _subrow_index(dest, SUB_Y))
    return y.reshape(t // 8, SUB_Y, 8, LANE)


def _final_kernel(x_ref, y_ref, g5_ref, gf_ref, o_ref):
    x = x_ref[...] + g5_ref[...] * _unpack_rows(_load_subrows(y_ref, 0, SUB_Y))
    ms = jnp.mean(x * x, axis=-1, keepdims=True)
    o_ref[...] = x * lax.rsqrt(ms + EPS) * gf_ref[...]


def _final(x_mid, y, mod5, layer, g_final, *, tiles_per_seq):
    t, d = x_mid.shape
    tok = pl.BlockSpec((TOK_TILE, d), lambda i: (i, 0))
    return pl.pallas_call(
        _final_kernel,
        grid=(t // TOK_TILE,),
        in_specs=[tok, pl.BlockSpec((TOK_TILE // 8, SUB_Y, 8, LANE), lambda i: (i, 0, 0, 0)),
                  pl.BlockSpec((None, None, None, 1, d), lambda i: (layer, i // tiles_per_seq, 5, 0, 0)),
                  pl.BlockSpec(g_final.shape, lambda i: (0, 0))],
        out_specs=tok,
        out_shape=jax.ShapeDtypeStruct((t, d), F32),
        compiler_params=_cparams(("parallel",)),
        name="final_norm",
    )(x_mid, y, mod5, g_final)


def _rope_tables(seq):
    t = jnp.arange(seq)
    row = (t // GRID_W).astype(F32)
    col = (t % GRID_W).astype(F32)
    half = GLA_DK // 2
    inv = ROPE_BASE ** (-jnp.arange(0, half, 2, dtype=F32) / half)
    ang = jnp.concatenate([row[:, None] * inv, col[:, None] * inv], axis=-1)
    cos = jnp.repeat(jnp.cos(ang), 2, axis=-1)
    sin = (jnp.sin(ang)[:, :, None] * jnp.array([-1.0, 1.0], F32)).reshape(seq, GLA_DK)
    cos = jnp.tile(cos, (1, GLA_HEADS))
    sin = jnp.tile(sin, (1, GLA_HEADS))
    padw = QK_PAD - GLA_QK
    cos = jnp.pad(cos, ((0, 0), (0, padw)), constant_values=1.0)
    sin = jnp.pad(sin, ((0, 0), (0, padw)))
    cos = jnp.concatenate([cos, jnp.ones((TOK_TILE, QK_PAD), F32)], axis=0)
    sin = jnp.concatenate([sin, jnp.zeros((TOK_TILE, QK_PAD), F32)], axis=0)
    return cos, sin


def _pad_last(w, n):
    return jnp.pad(w, [(0, 0)] * (w.ndim - 1) + [(0, n - w.shape[-1])])


def _in_weights(w_in):
    offs = np.cumsum([0, GLA_QK, GLA_QK, GLA_WIDTH, GLA_WIDTH, 2 * GLA_LOWRANK, NA_WIDTH, NA_WIDTH, NA_WIDTH,
                      2 * CONV_WIDTH])
    seg = [w_in[..., offs[i]:offs[i + 1]] for i in range(9)]
    cols = [_pad_last(seg[0], QK_PAD), _pad_last(seg[1], QK_PAD), _pad_last(seg[4], LANE), seg[2], seg[3], seg[5],
            seg[6], seg[7], seg[8]]
    w = jnp.concatenate(cols, axis=-1).astype(BF16)
    assert w.shape[-1] == IN_COLS_PAD
    return w


def _gate_weights(wa_f, ba_f, wa_b, ba_b):
    depth = wa_f.shape[0]
    zero = jnp.zeros((depth, GLA_LOWRANK, QK_PAD), F32)
    top = jnp.concatenate([_pad_last(wa_f, QK_PAD), zero], axis=-1)
    bot = jnp.concatenate([zero, _pad_last(wa_b, QK_PAD)], axis=-1)
    rest = jnp.zeros((depth, LANE - 2 * GLA_LOWRANK, 2 * QK_PAD), F32)
    wa = jnp.concatenate([top, bot, rest], axis=1)
    ba = jnp.concatenate([_pad_last(ba_f, QK_PAD), _pad_last(ba_b, QK_PAD)], axis=-1)[:, None, :]
    return wa, ba


def _na_bias_tables(rpb):
    cq = np.arange(GRID_W)
    c0 = np.clip(cq - NA_KW // 2, 0, GRID_W - NA_KW)
    kc = np.arange(GRID_W)
    valid = (kc[None, :] >= c0[:, None]) & (kc[None, :] < c0[:, None] + NA_KW)
    w = GRID_W
    padded = jnp.pad(rpb, [(0, 0)] * 3 + [(w - NA_KW, w - NA_KW)], constant_values=NEG_BIG)
    flat = jnp.tile(padded, (1, 1, 1, w))
    band = flat[..., w - 1:w - 1 + w * (2 * w - 2)].reshape(rpb.shape[:3] + (w, 2 * w - 2))[..., :w]
    band = jnp.where(valid, band, NEG_BIG)
    tab = jnp.stack([band[:, :, NA_KH - 1 - v:2 * NA_KH - 1 - v] for v in range(NA_KH)], axis=2)
    tab = jnp.transpose(tab, (0, 1, 2, 4, 3, 5))
    return tab.reshape(rpb.shape[:2] + (NA_KH, GRID_W, NA_KH * GRID_W)).astype(F32)


def _head_mean_matrix():
    h = np.arange(GLA_WIDTH) // GLA_DV
    e = (h[:, None] == h[None, :]).astype(np.float32)
    return jnp.asarray(np.concatenate([e, e], axis=0), dtype=BF16)


def kernel(x, c, ctx, c_ctx, w_mod, b_mod, g_mix, g_ffn, w_in, gla_wa_f, gla_ba_f, gla_wa_b, gla_ba_b, gla_g_norm,
           na_rpb, conv_w, conv_b, conv_ln_g, conv_ln_b, w_out, router_w, router_bias, w_gate, w_up, w_down,
           g_final):
    bsz, seq, d = x.shape
    ctx_len = ctx.shape[1]
    depth = w_mod.shape[0]
    t_lat, t_ctx = bsz * seq, bsz * ctx_len
    assert seq % TOK_TILE == 0 and t_ctx % TOK_TILE == 0 and seq % ctx_len == 0
    assert d == 2 * SUB_Y * LANE
    assert seq // GRID_W >= NA_KH and (seq // GRID_W) % NA_UNROLL == 0
    assert ctx_len % (GLA_CHUNK * GLA_STEPS) == 0 and seq % (GLA_CHUNK * GLA_STEPS) == 0
    tiles_per_seq = seq // TOK_TILE
    n_lat_tiles = t_lat // TOK_TILE
    n_all_tiles = (t_lat + t_ctx) // TOK_TILE
    ctx_blk0 = t_lat // ctx_len

    mod_rows = -(-(bsz + 1) // 8) * 8
    c_pad = jnp.zeros((mod_rows, d), F32).at[:bsz].set(c).at[bsz].set(c_ctx)
    mod = _modulation(c_pad, w_mod, b_mod)
    mod5 = mod.reshape(depth, mod_rows, 6, 1, d)

    cos_t, sin_t = _rope_tables(seq)
    e_mat = _head_mean_matrix()
    rw = jnp.pad(router_w, ((0, 0), (0, LANE - N_EXPERTS)))
    rw_hi = rw.astype(BF16)
    rw_lo = (rw - rw_hi.astype(F32)).astype(BF16)
    rw_split = jnp.concatenate([rw_hi, rw_lo], axis=1)
    bias_col = router_bias.reshape(N_EXPERTS, 1).astype(F32)
    zero_state = jnp.zeros((bsz, GLA_WIDTH, QK_PAD), F32)

    w_aug = _in_weights(w_in)
    wa_aug, ba_aug = _gate_weights(gla_wa_f, gla_ba_f, gla_wa_b, gla_ba_b)
    bias_tab = _na_bias_tables(na_rpb)
    row = lambda a: a[:, None, :]
    gmix, gffn, gnorm = row(g_mix), row(g_ffn), row(gla_g_norm)
    cb, clg, clb = row(conv_b), row(conv_ln_g), row(conv_ln_b)
    w_out_b = w_out.astype(BF16)
    wg_b, wu_b, wd_b = w_gate.astype(BF16), w_up.astype(BF16), w_down.astype(BF16)

    x_all = (x.reshape(t_lat, d), ctx.reshape(t_ctx, d))
    y_moe = None
    for l in range(depth):
        last = l == depth - 1
        x_all, (q, k, v, sgate, gf, gb, nq, nk, nv, hcv) = _inproj(
            x_all, y_moe, mod5, l, gmix, cos_t, sin_t, w_aug, wa_aug, ba_aug,
            n_lat_tiles=n_lat_tiles, tiles_per_seq=tiles_per_seq, n_batch=bsz, res_layer=l - 1)

        o_g, st_f, st_b = _gla(q, k, v, gf, gb, zero_state, zero_state, None, seq=ctx_len, blk0=ctx_blk0,
                               n_batch=bsz)
        o_g, _, _ = _gla(q, k, v, gf, gb, st_f, st_b, o_g, seq=seq, blk0=0, n_batch=bsz)

        o_n = _na_latent(nq, nk, nv, bias_tab, l, seq=seq, ctx_len=ctx_len, n_batch=bsz, ctx_blk0=ctx_blk0)
        o_c = _conv(hcv, conv_w, cb, clg, clb, l, None, seq=seq, blk0=0, n_batch=bsz)
        if not last:
            o_n = _na_context(nq, nk, nv, o_n, ctx_len=ctx_len, n_batch=bsz, ctx_blk0=ctx_blk0)
            o_c = _conv(hcv, conv_w, cb, clg, clb, l, o_c, seq=ctx_len, blk0=ctx_blk0, n_batch=bsz)

        n_tiles = n_lat_tiles if last else n_all_tiles
        x_mid, h2, logits = _outproj(
            o_g[0], o_g[1], sgate, o_n, o_c, x_all, mod5, l, gnorm, e_mat, w_out_b, gffn, rw_split,
            n_tiles=n_tiles, n_lat_tiles=n_lat_tiles, tiles_per_seq=tiles_per_seq, n_batch=bsz)
        y_moe = _moe(h2, logits, bias_col, wg_b, wu_b, wd_b, l)
        x_all = x_mid

    out = _final(x_all, y_moe, mod5, depth - 1, g_final.reshape(1, d), tiles_per_seq=tiles_per_seq)
    return out.reshape(bsz, seq, d)
```

```python
import functools

import numpy as np
import jax
import jax.numpy as jnp
from jax import lax
from jax.experimental import pallas as pl
from jax.experimental.pallas import tpu as pltpu
from jax.experimental.pallas import tpu_sc as plsc

GRID_W = 64
EPS = 1e-6
GLA_HEADS, GLA_DK, GLA_DV = 4, 48, 96
GLA_QK = GLA_HEADS * GLA_DK
GLA_WIDTH = GLA_HEADS * GLA_DV
GLA_LOWRANK = 16
GLA_TAU = 16.0
ROPE_BASE = 10000.0
NA_HEADS, NA_DH = 6, 64
NA_WIDTH = NA_HEADS * NA_DH
NA_KH, NA_KW = 8, 16
CONV_WIDTH, CONV_K = 256, 31
N_EXPERTS, N_GROUPS, EXPERTS_PER_GROUP = 16, 4, 4
PAIRS = ((0, 1), (0, 2), (0, 3), (1, 2), (1, 3), (2, 3))
N_CLASSES = N_GROUPS * len(PAIRS)

LANE = 128
SUBLANES = 8
QK_PAD = 256
C_Q, C_K = 0, 256
C_A, C_V, C_GATE = 512, 640, 1024
C_NQ, C_NK, C_NV, C_CONV = 1408, 1792, 2176, 2560
IN_COLS_PAD = 3072
MXU_N = 256

TOK_TILE = 1024
ROW_PARTS = 4
GLA_CHUNK = 64
GLA_SUB = 16
GLA_STEPS = 4
MOE_BLK = 256
ROUTE_TILE = 512
NA_UNROLL = 8
NEG_BIG = -1e30
VMEM_PER_CORE = 64 * 1024 * 1024
VMEM_LIMIT = VMEM_PER_CORE - 8 * 1024 * 1024

F32 = jnp.float32
BF16 = jnp.bfloat16


def _cparams(sem):
    return pltpu.CompilerParams(dimension_semantics=sem, vmem_limit_bytes=VMEM_LIMIT)


def _layer_spec(a, layer):
    return pl.BlockSpec((None,) + a.shape[1:], lambda *_: (layer,) + (0,) * (a.ndim - 1),
                        pipeline_mode=pl.Buffered(1))


def _dot(a, b):
    return jnp.dot(a, b, preferred_element_type=F32)


def _dot_nt(a, b):
    return lax.dot_general(a, b, (((1,), (1,)), ((), ())), preferred_element_type=F32)


def _split(a):
    hi = a.astype(BF16)
    lo = (a - hi.astype(F32)).astype(BF16)
    return hi, lo


def _sigmoid(x):
    return 1.0 / (1.0 + jnp.exp(-x))


SUB_X, SUB_Y = 5, 4
SC_WINDOW = 256


def _pack_bf16_pairs(a, b):
    ua = lax.bitcast_convert_type(a.astype(BF16).astype(F32), jnp.uint32)
    ub = lax.bitcast_convert_type(b.astype(BF16).astype(F32), jnp.uint32)
    return jnp.bitwise_or(jnp.right_shift(ua, jnp.uint32(16)), ub)


def _unpack_bf16_pairs(w):
    lo = lax.bitcast_convert_type(jnp.left_shift(w, jnp.uint32(16)), F32)
    hi = lax.bitcast_convert_type(jnp.bitwise_and(w, jnp.uint32(0xFFFF0000)), F32)
    return lo, hi


def _store_subrows(ref, val, j0, row0=0):
    r = val.shape[0]
    for j in range(val.shape[1] // LANE):
        ref[row0 // 8:(row0 + r) // 8, j0 + j, :, :] = val[:, j * LANE:(j + 1) * LANE].reshape(r // 8, 8, LANE)


def _load_subrows(ref, j0, n, row0=0, rows=None):
    r = ref.shape[0] * 8 if rows is None else rows
    return jnp.concatenate([ref[row0 // 8:(row0 + r) // 8, j0 + j, :, :].reshape(r, LANE) for j in range(n)],
                           axis=1)


def _pack_rows(x):
    half = x.shape[1] // 2
    return _pack_bf16_pairs(x[:, :half], x[:, half:])


def _unpack_rows(w):
    lo, hi = _unpack_bf16_pairs(w)
    return jnp.concatenate([lo, hi], axis=1)


def _sc_mesh():
    return plsc.VectorSubcoreMesh(core_axis_name="c", subcore_axis_name="s")


def _sc_scatter(src, idx, n_out):
    n, w = src.shape
    idx2 = idx.reshape(1, n)

    @pl.kernel(out_type=jax.ShapeDtypeStruct((n_out, w), src.dtype), mesh=_sc_mesh(), scratch_types=[])
    def scatter_kernel(x_hbm, i_hbm, o_hbm):
        def body(x_vmem, i_vmem):
            pltpu.sync_copy(x_vmem, o_hbm.at[i_vmem.at[0]])

        pltpu.emit_pipeline(
            body,
            grid=(n // SC_WINDOW,),
            in_specs=[pl.BlockSpec((SC_WINDOW, w), index_map=lambda i: (i, 0)),
                      pl.BlockSpec((1, SC_WINDOW), index_map=lambda i: (0, i))],
            out_specs=[],
            core_axis_name=("c", "s"),
            dimension_semantics=(pltpu.PARALLEL,),
        )(x_hbm, i_hbm)

    return scatter_kernel(src, idx2)


def _sc_gather(src, idx):
    n = idx.shape[0]
    w = src.shape[1]
    idx2 = idx.reshape(1, n)

    @pl.kernel(out_type=jax.ShapeDtypeStruct((n, w), src.dtype), mesh=_sc_mesh())
    def gather_kernel(x_hbm, i_hbm, o_hbm):
        def body(i_vmem, o_vmem):
            pltpu.sync_copy(x_hbm.at[i_vmem.at[0]], o_vmem)

        pltpu.emit_pipeline(
            body,
            grid=(n // SC_WINDOW,),
            in_specs=[pl.BlockSpec((1, SC_WINDOW), index_map=lambda i: (0, i))],
            out_specs=[pl.BlockSpec((SC_WINDOW, w), index_map=lambda i: (i, 0))],
            core_axis_name=("c", "s"),
            dimension_semantics=(pltpu.PARALLEL,),
        )(i_hbm, o_hbm)

    return gather_kernel(src, idx2)


def _mod_kernel(c_ref, w_ref, b_ref, o_ref):
    cv = c_ref[...]
    s = cv * _sigmoid(cv)
    s_hi, s_lo = _split(s)
    w_hi, w_lo = _split(w_ref[...])
    o_ref[...] = _dot(s_hi, w_hi) + _dot(s_lo, w_hi) + _dot(s_hi, w_lo) + b_ref[...]


def _modulation(c_pad, w_mod, b_mod):
    depth, d, six_d = w_mod.shape
    rows = c_pad.shape[0]
    nt = 1536
    return pl.pallas_call(
        _mod_kernel,
        grid=(depth, six_d // nt),
        in_specs=[
            pl.BlockSpec((rows, d), lambda l, j: (0, 0)),
            pl.BlockSpec((None, d, nt), lambda l, j: (l, 0, j)),
            pl.BlockSpec((None, 1, nt), lambda l, j: (l, 0, j)),
        ],
        out_specs=pl.BlockSpec((None, rows, nt), lambda l, j: (l, 0, j)),
        out_shape=jax.ShapeDtypeStruct((depth, rows, six_d), F32),
        compiler_params=_cparams(("arbitrary", "arbitrary")),
        name="modulation",
    )(c_pad, w_mod, b_mod.reshape(depth, 1, six_d))


def _inproj_kernel(*refs, has_res, n_lat_tiles):
    if has_res:
        x_ref, y_ref, g5_ref = refs[:3]
    else:
        x_ref, xctx_ref = refs[:2]
        is_lat = pl.program_id(0) < n_lat_tiles
    refs = refs[3 if has_res else 2:]
    (sh_ref, sc_ref, gmix_ref, cos_ref, sin_ref, w_ref, wa_ref, ba_ref) = refs[:8]
    (xnew_ref, q_ref, k_ref, v_ref, sg_ref, gf_ref, gb_ref, nq_ref, nk_ref, nv_ref, hcv_ref) = refs[8:]

    tile = x_ref.shape[0]
    part = tile // ROW_PARTS
    wa_hi, wa_lo = _split(wa_ref[...])
    even_lane = jnp.bitwise_and(lax.broadcasted_iota(jnp.int32, (1, LANE), 1), 1) == 0
    hbs = []
    for p in range(ROW_PARTS):
        rows = pl.ds(p * part, part)
        if has_res:
            x = x_ref[rows, :] + g5_ref[...] * _unpack_rows(_load_subrows(y_ref, 0, SUB_Y, row0=p * part, rows=part))
        else:
            x = jnp.where(is_lat, x_ref[rows, :], xctx_ref[rows, :])
        xnew_ref[rows, :] = x
        ms = jnp.mean(x * x, axis=-1, keepdims=True)
        h = x * lax.rsqrt(ms + EPS) * gmix_ref[...]
        h = h * (1.0 + sc_ref[...]) + sh_ref[...]
        hbs.append(h.astype(BF16))

    for p, hb in enumerate(hbs):
        rows = pl.ds(p * part, part)

        def proj(c0, n, hb=hb):
            return _dot(hb, w_ref[:, c0:c0 + n])

        cos = cos_ref[rows, :]
        sin = sin_ref[rows, :]

        def rope(t):
            halves = []
            for c0 in range(0, QK_PAD, LANE):
                th = t[:, c0:c0 + LANE]
                halves.append(jnp.where(even_lane, pltpu.roll(th, LANE - 1, 1), pltpu.roll(th, 1, 1)))
            return t * cos + jnp.concatenate(halves, axis=1) * sin

        mid = (C_A + C_CONV) // 2
        assert (mid - C_A) % MXU_N == 0 and C_NQ < mid < C_NK
        d1 = proj(C_A, mid - C_A)
        a_hi, a_lo = _split(d1[:, :LANE])
        q_ref[rows, :] = (rope(proj(C_Q, QK_PAD)) * (GLA_DK ** -0.5)).astype(BF16)
        k_ref[rows, :] = rope(proj(C_K, QK_PAD)).astype(BF16)

        z = _dot(a_hi, wa_hi) + _dot(a_lo, wa_hi) + _dot(a_hi, wa_lo) + ba_ref[...]
        logsig = jnp.minimum(z, 0.0) - jnp.log(1.0 + jnp.exp(-jnp.abs(z)))
        g = logsig * (1.0 / GLA_TAU)
        gf_ref[rows, :] = g[:, :QK_PAD]
        gb_ref[rows, :] = g[:, QK_PAD:]

        v_ref[rows, :] = d1[:, C_V - C_A:C_GATE - C_A].astype(BF16)
        gate = d1[:, C_GATE - C_A:C_NQ - C_A]
        sg_ref[rows, :] = (gate * _sigmoid(gate)).astype(BF16)
        d2 = proj(mid, C_CONV - mid)
        nq = jnp.concatenate([d1[:, C_NQ - C_A:], d2[:, :C_NK - mid]], axis=1)
        nq_ref[rows, :] = (nq * (NA_DH ** -0.5)).astype(BF16)
        nk_ref[rows, :] = d2[:, C_NK - mid:C_NV - mid].astype(BF16)
        nv_ref[rows, :] = d2[:, C_NV - mid:].astype(BF16)
        u = proj(C_CONV, 2 * CONV_WIDTH)
        hcv_ref[rows, :] = u[:, :CONV_WIDTH] * _sigmoid(u[:, CONV_WIDTH:])


def _inproj(x_all, y, mod5, layer, gmix, cos_t, sin_t, w_aug, wa_aug, ba_aug, *, n_lat_tiles, tiles_per_seq,
            n_batch, res_layer):
    has_res = y is not None
    d = x_all.shape[1] if has_res else x_all[0].shape[1]
    t_all = x_all.shape[0] if has_res else x_all[0].shape[0] + x_all[1].shape[0]
    nt = t_all // TOK_TILE

    def bidx(i):
        return jnp.where(i < n_lat_tiles, i // tiles_per_seq, n_batch)

    def ridx(i):
        return jnp.where(i < n_lat_tiles, i % tiles_per_seq, tiles_per_seq)

    def modspec(l, j):
        return pl.BlockSpec((None, None, None, 1, d), lambda i: (l, bidx(i), j, 0, 0))

    tok = lambda w: pl.BlockSpec((TOK_TILE, w), lambda i: (i, 0))
    full = lambda a: pl.BlockSpec(a.shape, lambda i: (0,) * a.ndim)

    if has_res:
        in_specs = [tok(d), pl.BlockSpec((TOK_TILE // 8, SUB_Y, 8, LANE), lambda i: (i, 0, 0, 0)),
                    modspec(res_layer, 5)]
        args = [x_all, y, mod5]
    else:
        in_specs = [pl.BlockSpec((TOK_TILE, d), lambda i: (jnp.minimum(i, n_lat_tiles - 1), 0)),
                    pl.BlockSpec((TOK_TILE, d), lambda i: (jnp.maximum(i - n_lat_tiles, 0), 0))]
        args = list(x_all)
    in_specs += [modspec(layer, 0), modspec(layer, 1), _layer_spec(gmix, layer),
                 pl.BlockSpec((TOK_TILE, QK_PAD), lambda i: (ridx(i), 0)),
                 pl.BlockSpec((TOK_TILE, QK_PAD), lambda i: (ridx(i), 0)),
                 _layer_spec(w_aug, layer), _layer_spec(wa_aug, layer), _layer_spec(ba_aug, layer)]
    args += [mod5, mod5, gmix, cos_t, sin_t, w_aug, wa_aug, ba_aug]

    out_widths = [(d, F32), (QK_PAD, BF16), (QK_PAD, BF16), (GLA_WIDTH, BF16), (GLA_WIDTH, BF16), (QK_PAD, F32),
                  (QK_PAD, F32), (NA_WIDTH, BF16), (NA_WIDTH, BF16), (NA_WIDTH, BF16), (CONV_WIDTH, F32)]
    out_specs = [tok(w) for w, _ in out_widths]
    out_shape = [jax.ShapeDtypeStruct((t_all, w), dt) for w, dt in out_widths]
    res = pl.pallas_call(
        functools.partial(_inproj_kernel, has_res=has_res, n_lat_tiles=n_lat_tiles),
        grid=(nt,),
        in_specs=in_specs,
        out_specs=out_specs,
        out_shape=out_shape,
        compiler_params=_cparams(("parallel",)),
        name="inproj",
    )(*args)
    return res[0], res[1:]


def _gla_masks():
    c, sub = GLA_CHUNK, GLA_SUB
    lane_qk = lax.broadcasted_iota(jnp.int32, (1, QK_PAD), 1)
    head_qk = ((lane_qk >= GLA_DK).astype(jnp.int32) + (lane_qk >= 2 * GLA_DK).astype(jnp.int32)
               + (lane_qk >= 3 * GLA_DK).astype(jnp.int32) + 4 * (lane_qk >= 4 * GLA_DK).astype(jnp.int32))
    row_h = jnp.right_shift(lax.broadcasted_iota(jnp.int32, (c, 1), 0), GLA_SUB.bit_length() - 1)
    hm = (row_h == head_qk).astype(F32)
    row_v = lax.broadcasted_iota(jnp.int32, (GLA_WIDTH, 1), 0)
    head_v = ((row_v >= GLA_DV).astype(jnp.int32) + (row_v >= 2 * GLA_DV).astype(jnp.int32)
              + (row_v >= 3 * GLA_DV).astype(jnp.int32))
    bd = (head_v == head_qk).astype(F32)
    lane_v = lax.broadcasted_iota(jnp.int32, (1, GLA_WIDTH), 1)
    vm = [((lane_v >= h * GLA_DV) & (lane_v < (h + 1) * GLA_DV)).astype(F32) for h in range(GLA_HEADS)]
    return hm, bd, vm


def _gla_steps(q, k, v, g, s_t, hm, bd, vm):
    c, sub = GLA_CHUNK, GLA_SUB
    nsub = c // sub
    dirs = (True, False)
    items = [(u, d) for u in range(GLA_STEPS) for d in range(2)]
    ri = lax.broadcasted_iota(jnp.int32, (c, c), 0)
    ci = lax.broadcasted_iota(jnp.int32, (c, c), 1)
    key_row = lax.broadcasted_iota(jnp.int32, (c, 1), 0)
    att_row = jnp.bitwise_and(ri, sub - 1)
    tri = [((ci <= ri) if fwd else (ci >= ri)).astype(BF16) for fwd in dirs]

    cums = {}
    for u, d in items:
        g_hi, g_lo = _split(g[u][d])
        cums[u, d] = _dot(tri[d], g_hi) + _dot(tri[d], g_lo)

    qe, kv, decay, atts = {}, {}, {}, {}
    for u, d in items:
        fwd = dirs[d]
        cum = cums[u, d]
        tot = cum[c - 1:c] if fwd else cum[0:1]
        qe[u, d] = (q[u][d] * jnp.exp(cum)).astype(BF16)
        k_end = (k[u][d] * jnp.exp(tot - cum)).astype(BF16)
        kv[u, d] = lax.dot_general(v[u][d], k_end, (((0,), (0,)), ((), ())), preferred_element_type=F32)
        decay[u, d] = jnp.exp(tot)
        att_d = []
        for i in range(nsub):
            lo, hi = i * sub, (i + 1) * sub
            if fwd:
                ref = cum[lo - 1:lo] if i > 0 else jnp.zeros((1, QK_PAD), F32)
                key_ok = key_row < hi
                causal = ci <= att_row + lo
            else:
                ref = cum[hi:hi + 1] if i < nsub - 1 else jnp.zeros((1, QK_PAD), F32)
                key_ok = key_row >= lo
                causal = ci >= att_row + lo
            qi = q[u][d][lo:hi] * jnp.exp(cum[lo:hi] - ref)
            qs = (jnp.concatenate([qi] * GLA_HEADS, axis=0) * hm).astype(BF16)
            ki = (k[u][d] * jnp.exp(jnp.where(key_ok, ref - cum, NEG_BIG))).astype(BF16)
            att = _dot_nt(qs, ki)
            att_d.append(jnp.where(causal, att, 0.0).astype(BF16))
        atts[u, d] = jnp.concatenate(att_d, axis=0)

    s = list(s_t)
    o_inter = {}
    for u, d in items:
        o_inter[u, d] = _dot_nt(qe[u, d], s[d].astype(BF16))
        s[d] = s[d] * decay[u, d] + bd * kv[u, d]

    outs = [[None, None] for _ in range(GLA_STEPS)]
    for u, d in items:
        r = _dot(atts[u, d], v[u][d])
        blocks = []
        for i in range(nsub):
            base = i * c
            oi = r[base:base + sub] * vm[0]
            for h in range(1, GLA_HEADS):
                oi = oi + r[base + h * sub:base + (h + 1) * sub] * vm[h]
            blocks.append(oi)
        outs[u][d] = o_inter[u, d] + jnp.concatenate(blocks, axis=0)
    return outs, s


def _gla_kernel(q_ref, k_ref, v_ref, gf_ref, gb_ref, sf0_ref, sb0_ref, of_ref, ob_ref, sf_ref, sb_ref):
    n = q_ref.shape[0]
    nc = n // GLA_CHUNK
    hm, bd, vm = _gla_masks()
    sf_ref[...] = sf0_ref[...]
    sb_ref[...] = sb0_ref[...]

    def body(j, carry):
        rows = [[pl.ds(pl.multiple_of(cidx * GLA_CHUNK, GLA_CHUNK), GLA_CHUNK)
                 for cidx in (j * GLA_STEPS + u, nc - 1 - (j * GLA_STEPS + u))] for u in range(GLA_STEPS)]
        q = [[q_ref[r, :].astype(F32) for r in ru] for ru in rows]
        k = [[k_ref[r, :].astype(F32) for r in ru] for ru in rows]
        v = [[v_ref[r, :] for r in ru] for ru in rows]
        g = [[gf_ref[ru[0], :], gb_ref[ru[1], :]] for ru in rows]
        outs, s_new = _gla_steps(q, k, v, g, [sf_ref[...], sb_ref[...]], hm, bd, vm)
        sf_ref[...] = s_new[0]
        sb_ref[...] = s_new[1]
        for u in range(GLA_STEPS):
            of_ref[rows[u][0], :] = outs[u][0]
            ob_ref[rows[u][1], :] = outs[u][1]
        return carry

    lax.fori_loop(0, nc // GLA_STEPS, body, 0)


def _gla(q, k, v, gf, gb, sf0, sb0, o_prev, *, seq, blk0, n_batch):
    t_all = q.shape[0]
    tokw = lambda w: pl.BlockSpec((seq, w), lambda b: (blk0 + b, 0))
    st = pl.BlockSpec((None, GLA_WIDTH, QK_PAD), lambda b: (b, 0, 0))
    in_specs = [tokw(QK_PAD), tokw(QK_PAD), tokw(GLA_WIDTH), tokw(QK_PAD), tokw(QK_PAD), st, st]
    args = [q, k, v, gf, gb, sf0, sb0]
    aliases = {}
    n_in = len(args)
    if o_prev is not None:
        in_specs += [pl.BlockSpec(memory_space=pl.ANY)] * 2
        args += list(o_prev)
        aliases = {n_in: 0, n_in + 1: 1}

    def kern(*refs):
        _gla_kernel(*refs[:n_in], *refs[len(args):])

    st_shape = jax.ShapeDtypeStruct((n_batch, GLA_WIDTH, QK_PAD), F32)
    o_shape = jax.ShapeDtypeStruct((t_all, GLA_WIDTH), F32)
    o_f, o_b, s_f, s_b = pl.pallas_call(
        kern,
        grid=(n_batch,),
        in_specs=in_specs,
        out_specs=[tokw(GLA_WIDTH), tokw(GLA_WIDTH), st, st],
        out_shape=[o_shape, o_shape, st_shape, st_shape],
        input_output_aliases=aliases,
        compiler_params=_cparams(("parallel",)),
        name="gla",
    )(*args)
    return (o_f, o_b), s_f, s_b


def _na_kernel(q_ref, k_ref, v_ref, kc_ref, vc_ref, bias_ref, o_ref):
    n = q_ref.shape[0]
    rows = n // GRID_W
    nkeys = NA_KH * GRID_W
    lane = lax.broadcasted_iota(jnp.int32, (1, LANE), 1)
    first = lane < NA_DH
    kc = kc_ref[...]
    vc = vc_ref[...]

    def body(jb, carry):
        items = []
        for j in range(NA_UNROLL):
            r = jb * NA_UNROLL + j
            r0 = jnp.clip(r - NA_KH // 2, 0, rows - NA_KH)
            var = r - r0
            qrows = pl.ds(pl.multiple_of(r * GRID_W, GRID_W), GRID_W)
            krows = pl.ds(pl.multiple_of(r0 * GRID_W, GRID_W), nkeys)
            qr = q_ref[qrows, :]
            kb = k_ref[krows, :]
            q2 = jnp.concatenate([jnp.where(first, qr, jnp.zeros_like(qr)),
                                  jnp.where(first, jnp.zeros_like(qr), qr)], axis=0)
            items.append((qrows, krows, var, _dot_nt(q2, kb), _dot_nt(q2, kc)))
        probs = []
        for qrows, krows, var, s_loc, s_ctx in items:
            s_loc = s_loc + jnp.concatenate([bias_ref[0, var], bias_ref[1, var]], axis=0)
            m = jnp.maximum(jnp.max(s_loc, axis=-1, keepdims=True), jnp.max(s_ctx, axis=-1, keepdims=True))
            p_loc = jnp.exp(s_loc - m)
            p_ctx = jnp.exp(s_ctx - m)
            l = jnp.sum(p_loc, axis=-1, keepdims=True) + jnp.sum(p_ctx, axis=-1, keepdims=True)
            probs.append((p_loc.astype(BF16), p_ctx.astype(BF16), l))
        for (qrows, krows, var, _, _), (p_loc, p_ctx, l) in zip(items, probs):
            o = (_dot(p_loc, v_ref[krows, :]) + _dot(p_ctx, vc)) / l
            o_ref[qrows, :] = jnp.where(first, o[:GRID_W], o[GRID_W:]).astype(o_ref.dtype)
        return carry

    lax.fori_loop(0, rows // NA_UNROLL, body, 0)


def _na_latent(nq, nk, nv, bias_tab, layer, *, seq, ctx_len, n_batch, ctx_blk0):
    t_all = nq.shape[0]
    npair = NA_HEADS // 2
    lat = pl.BlockSpec((seq, LANE), lambda b, p: (b, p))
    ctx = pl.BlockSpec((ctx_len, LANE), lambda b, p: (ctx_blk0 + b, p))
    bias = pl.BlockSpec((None, 2) + bias_tab.shape[2:], lambda b, p: (layer, p, 0, 0, 0))
    return pl.pallas_call(
        _na_kernel,
        grid=(n_batch, npair),
        in_specs=[lat, lat, lat, ctx, ctx, bias],
        out_specs=lat,
        out_shape=jax.ShapeDtypeStruct((t_all, NA_WIDTH), BF16),
        compiler_params=_cparams(("parallel", "arbitrary")),
        name="na_latent",
    )(nq, nk, nv, nk, nv, bias_tab)


def _na_ctx_kernel(q_ref, k_ref, v_ref, o_in_ref, o_ref):
    del o_in_ref
    lane = lax.broadcasted_iota(jnp.int32, (1, LANE), 1)
    first = lane < NA_DH
    q = q_ref[...]
    k = k_ref[...]
    v = v_ref[...]
    res = []
    for h in range(2):
        sel = first if h == 0 else jnp.logical_not(first)
        qh = jnp.where(sel, q, jnp.zeros_like(q))
        s = _dot_nt(qh, k)
        m = jnp.max(s, axis=-1, keepdims=True)
        p = jnp.exp(s - m)
        l = jnp.sum(p, axis=-1, keepdims=True)
        res.append(_dot(p.astype(BF16), v) / l)
    o_ref[...] = jnp.where(first, res[0], res[1]).astype(o_ref.dtype)


def _na_context(nq, nk, nv, o_prev, *, ctx_len, n_batch, ctx_blk0):
    npair = NA_HEADS // 2
    ctx = pl.BlockSpec((ctx_len, LANE), lambda b, p: (ctx_blk0 + b, p))
    return pl.pallas_call(
        _na_ctx_kernel,
        grid=(n_batch, npair),
        in_specs=[ctx, ctx, ctx, pl.BlockSpec(memory_space=pl.ANY)],
        out_specs=ctx,
        out_shape=jax.ShapeDtypeStruct(o_prev.shape, o_prev.dtype),
        input_output_aliases={3: 0},
        compiler_params=_cparams(("parallel", "arbitrary")),
        name="na_context",
    )(nq, nk, nv, o_prev)


CONV_HALO = 16
CONV_ROWS = 128


def _conv_kernel(*refs, has_prev):
    if has_prev:
        h_ref, w_ref, b_ref, lg_ref, lb_ref, _, o_ref, pad_ref, sh_ref = refs
    else:
        h_ref, w_ref, b_ref, lg_ref, lb_ref, o_ref, pad_ref, sh_ref = refs
    n = h_ref.shape[0]
    zeros = jnp.zeros((CONV_HALO, CONV_WIDTH), F32)
    pad_ref[0:CONV_HALO, :] = zeros
    pad_ref[CONV_HALO + n:CONV_HALO + n + CONV_HALO, :] = zeros
    pad_ref[CONV_HALO:CONV_HALO + n, :] = h_ref[...]
    span = n + 2 * CONV_HALO - SUBLANES
    for s in range(1, SUBLANES):
        sh_ref[s - 1, 0:span, :] = pad_ref[s:s + span, :]
    w = w_ref[...]
    off = CONV_HALO - CONV_K // 2

    def chunk(cidx, carry):
        base = pl.multiple_of(cidx * CONV_ROWS, CONV_ROWS)
        acc = jnp.zeros((CONV_ROWS, CONV_WIDTH), F32) + b_ref[...]
        for j in range(CONV_K):
            s = (off + j) % SUBLANES
            win = pl.ds(base + (off + j - s), CONV_ROWS)
            tap = pad_ref[win, :] if s == 0 else sh_ref[s - 1, win, :]
            acc = acc + tap * w[j:j + 1, :]
        mu = jnp.mean(acc, axis=-1, keepdims=True)
        xc = acc - mu
        var = jnp.mean(xc * xc, axis=-1, keepdims=True)
        y = xc * lax.rsqrt(var + EPS) * lg_ref[...] + lb_ref[...]
        o_ref[pl.ds(base, CONV_ROWS), :] = (y * _sigmoid(y)).astype(o_ref.dtype)
        return carry

    lax.fori_loop(0, n // CONV_ROWS, chunk, 0)


def _conv(hcv, w, b, lg, lb, layer, o_prev, *, seq, blk0, n_batch):
    t_all = hcv.shape[0]
    tok = pl.BlockSpec((seq, CONV_WIDTH), lambda i: (blk0 + i, 0))
    full = lambda a: pl.BlockSpec(a.shape, lambda i: (0,) * a.ndim)
    in_specs = [tok] + [_layer_spec(a, layer) for a in (w, b, lg, lb)]
    args = [hcv, w, b, lg, lb]
    aliases = {}
    if o_prev is not None:
        in_specs.append(pl.BlockSpec(memory_space=pl.ANY))
        args.append(o_prev)
        aliases = {5: 0}
    return pl.pallas_call(
        functools.partial(_conv_kernel, has_prev=o_prev is not None),
        grid=(n_batch,),
        in_specs=in_specs,
        out_specs=tok,
        out_shape=jax.ShapeDtypeStruct((t_all, CONV_WIDTH), BF16),
        scratch_shapes=[pltpu.VMEM((seq + 2 * CONV_HALO, CONV_WIDTH), F32),
                        pltpu.VMEM((SUBLANES - 1, seq + 2 * CONV_HALO, CONV_WIDTH), F32)],
        input_output_aliases=aliases,
        compiler_params=_cparams(("parallel",)),
        name="conv",
    )(*args)


def _outproj_kernel(ogf_ref, ogb_ref, sg_ref, on_ref, oc_ref, x_ref, g2_ref, sh_ref, sc_ref, gn_ref, e_ref,
                    wo_ref, gffn_ref, rw_ref, xmid_ref, h2_ref, lg_ref):
    tile = x_ref.shape[0]
    parts = [pl.ds(p * (tile // ROW_PARTS), tile // ROW_PARTS) for p in range(ROW_PARTS)]
    e = e_ref[...]
    rw = rw_ref[...]

    ofs, mss = [], []
    for rows in parts:
        of = ogf_ref[rows, :] + ogb_ref[rows, :]
        sq_hi, sq_lo = _split(of * of)
        ofs.append(of)
        mss.append(_dot(jnp.concatenate([sq_hi, sq_lo], axis=1), e) * (1.0 / GLA_DV))
    ys = []
    for rows, of, ms in zip(parts, ofs, mss):
        og = of * lax.rsqrt(ms + EPS) * gn_ref[...] * sg_ref[rows, :].astype(F32)
        mix = jnp.concatenate([og.astype(BF16), on_ref[rows, :], oc_ref[rows, :]], axis=1)
        ys.append(_dot(mix, wo_ref[...]))
    for p, (rows, y) in enumerate(zip(parts, ys)):
        x = x_ref[rows, :] + g2_ref[...] * y
        xmid_ref[rows, :] = x
        ms2 = jnp.mean(x * x, axis=-1, keepdims=True)
        h2 = x * lax.rsqrt(ms2 + EPS) * gffn_ref[...]
        h2 = h2 * (1.0 + sc_ref[...]) + sh_ref[...]
        h_hi, h_lo = _split(h2)
        _store_subrows(h2_ref, _pack_rows(h2), 0, row0=p * (tile // ROW_PARTS))
        hw = _dot(h_hi, rw)
        lg_ref[rows, :] = hw[:, :LANE] + hw[:, LANE:] + _dot(h_lo, rw[:, :LANE])


def _outproj(o_gf, o_gb, sgate, o_n, o_c, x_all, mod5, layer, gnorm, e_mat, w_out, gffn, rw_split, *, n_tiles,
             n_lat_tiles, tiles_per_seq, n_batch):
    d = x_all.shape[1]
    t_out = n_tiles * TOK_TILE

    def bidx(i):
        return jnp.where(i < n_lat_tiles, i // tiles_per_seq, n_batch)

    def modspec(j):
        return pl.BlockSpec((None, None, None, 1, d), lambda i: (layer, bidx(i), j, 0, 0))

    tok = lambda w: pl.BlockSpec((TOK_TILE, w), lambda i: (i, 0))
    full = lambda a: pl.BlockSpec(a.shape, lambda i: (0,) * a.ndim)
    return pl.pallas_call(
        _outproj_kernel,
        grid=(n_tiles,),
        in_specs=[tok(GLA_WIDTH), tok(GLA_WIDTH), tok(GLA_WIDTH), tok(NA_WIDTH), tok(CONV_WIDTH), tok(d),
                  modspec(2), modspec(3), modspec(4), _layer_spec(gnorm, layer), full(e_mat),
                  _layer_spec(w_out, layer), _layer_spec(gffn, layer),
                  full(rw_split)],
        out_specs=[tok(d), pl.BlockSpec((TOK_TILE // 8, SUB_Y, 8, LANE), lambda i: (i, 0, 0, 0)), tok(LANE)],
        out_shape=[jax.ShapeDtypeStruct((t_out, d), F32),
                   jax.ShapeDtypeStruct((t_out // 8, SUB_X, 8, LANE), jnp.uint32),
                   jax.ShapeDtypeStruct((t_out, LANE), F32)],
        compiler_params=_cparams(("parallel",)),
        name="outproj",
    )(o_gf, o_gb, sgate, o_n, o_c, x_all, mod5, mod5, mod5, gnorm, e_mat, w_out, gffn, rw_split)


def _route_kernel(lg_ref, bias_ref, h2_in_ref, meta_ref, cnt_ref, h2w_ref, carry_ref):
    del h2_in_ref
    tile = lg_ref.shape[0]

    @pl.when(pl.program_id(0) == 0)
    def _():
        carry_ref[...] = jnp.zeros_like(carry_ref)

    lt = lg_ref[...].T
    aff = _sigmoid(lt[0:N_EXPERTS])
    sel = aff + bias_ref[...]
    s = [sel[e:e + 1] for e in range(N_EXPERTS)]
    a = [aff[e:e + 1] for e in range(N_EXPERTS)]

    def top2sum(v):
        best = v[0] + v[1]
        for i, j in PAIRS[1:]:
            best = jnp.maximum(best, v[i] + v[j])
        return best

    gs = [top2sum(s[4 * g:4 * g + 4]) for g in range(N_GROUPS)]
    gbest = jnp.zeros_like(gs[0], dtype=jnp.int32)
    gmax = gs[0]
    for g in range(1, N_GROUPS):
        upd = gs[g] > gmax
        gbest = jnp.where(upd, g, gbest)
        gmax = jnp.where(upd, gs[g], gmax)

    def pick(vals, j):
        out = vals[j]
        for g in range(1, N_GROUPS):
            out = jnp.where(gbest == g, vals[4 * g + j], out)
        return out

    sv = [pick(s, j) for j in range(EXPERTS_PER_GROUP)]
    av = [pick(a, j) for j in range(EXPERTS_PER_GROUP)]
    i1 = jnp.zeros_like(gbest)
    m1 = sv[0]
    for j in range(1, EXPERTS_PER_GROUP):
        upd = sv[j] > m1
        i1 = jnp.where(upd, j, i1)
        m1 = jnp.where(upd, sv[j], m1)
    i2 = jnp.full_like(gbest, -1)
    m2 = jnp.zeros_like(m1)
    for j in range(EXPERTS_PER_GROUP):
        upd = (i1 != j) & ((sv[j] > m2) | (i2 < 0))
        i2 = jnp.where(upd, j, i2)
        m2 = jnp.where(upd, sv[j], m2)
    ia = jnp.minimum(i1, i2)
    ib = jnp.maximum(i1, i2)
    pair = jnp.where(ia == 0, ib - 1, jnp.where(ia == 1, ib + 1, 5))
    cls = gbest * len(PAIRS) + pair

    def take(vals, idx):
        out = vals[0]
        for j in range(1, EXPERTS_PER_GROUP):
            out = jnp.where(idx == j, vals[j], out)
        return out

    w1 = take(av, i1)
    w2 = take(av, i2)
    tot = w1 + w2
    wa = jnp.where(i1 < i2, w1, w2) / tot
    wb = jnp.where(i1 < i2, w2, w1) / tot

    crow = lax.broadcasted_iota(jnp.int32, (32, tile), 0)
    oh = (crow == cls).astype(F32)
    us = lax.broadcasted_iota(jnp.int32, (tile, tile), 0)
    ut = lax.broadcasted_iota(jnp.int32, (tile, tile), 1)
    upper = (us < ut).astype(BF16)
    prefix = _dot(oh.astype(BF16), upper)
    carry = carry_ref[...]
    rank = jnp.sum(oh * (prefix + carry), axis=0, keepdims=True)
    carry_new = carry + jnp.sum(oh, axis=1, keepdims=True)
    carry_ref[...] = carry_new
    cnt_ref[...] = jnp.broadcast_to(carry_new, cnt_ref.shape)

    meta_ref[...] = jnp.zeros_like(meta_ref)
    meta_ref[0:1, :] = cls.astype(F32)
    meta_ref[1:2, :] = rank

    wrow = lax.broadcasted_iota(jnp.int32, (LANE, tile), 0)
    wmat = jnp.where(wrow == 0, wa, jnp.where(wrow == 1, wb, 0.0))
    wtok = lax.bitcast_convert_type(wmat.T, jnp.uint32)
    h2w_ref[:, 0, :, :] = wtok.reshape(tile // 8, 8, LANE)


def _route(logits, bias_col, h2_sub):
    t = logits.shape[0]
    nt = t // ROUTE_TILE
    meta, cnt, h2_sub = pl.pallas_call(
        _route_kernel,
        grid=(nt,),
        in_specs=[pl.BlockSpec((ROUTE_TILE, LANE), lambda i: (i, 0)),
                  pl.BlockSpec(bias_col.shape, lambda i: (0, 0)),
                  pl.BlockSpec(memory_space=pl.ANY)],
        out_specs=[pl.BlockSpec((8, ROUTE_TILE), lambda i: (0, i)),
                   pl.BlockSpec((32, LANE), lambda i: (0, 0)),
                   pl.BlockSpec((ROUTE_TILE // 8, 1, 8, LANE), lambda i: (i, SUB_Y, 0, 0))],
        out_shape=[jax.ShapeDtypeStruct((8, t), F32), jax.ShapeDtypeStruct((32, LANE), F32),
                   jax.ShapeDtypeStruct(h2_sub.shape, h2_sub.dtype)],
        scratch_shapes=[pltpu.VMEM((32, 1), F32)],
        input_output_aliases={2: 2},
        compiler_params=_cparams(("arbitrary",)),
        name="route",
    )(logits, bias_col, h2_sub)
    return meta, cnt, h2_sub


FF_TILE = 512


def _expert_kernel(ea_ref, eb_ref, nvalid_ref, xs_ref, wga_ref, wua_ref, wda_ref, wgb_ref, wub_ref, wdb_ref, y_ref):
    del ea_ref, eb_ref
    j = pl.program_id(0)
    nvalid = nvalid_ref[j]

    @pl.when(nvalid == 0)
    def _():
        y_ref[...] = jnp.zeros_like(y_ref)

    blk = y_ref.shape[0] * 8
    half = blk // 2

    def run(rows):
        live = lax.broadcasted_iota(jnp.int32, (rows, 1), 0) < nvalid
        x = jnp.where(live, _unpack_rows(_load_subrows(xs_ref, 0, SUB_Y, rows=rows)), 0.0).astype(BF16)
        wsub = xs_ref[0:rows // 8, SUB_Y, :, :].reshape(rows, LANE)
        ws = jnp.where(live, lax.bitcast_convert_type(wsub, F32), 0.0)
        ff = wga_ref.shape[1]

        items = [(w, f0) for w in ((wga_ref, wua_ref, wda_ref), (wgb_ref, wub_ref, wdb_ref))
                 for f0 in range(0, ff, FF_TILE)]

        def up(item):
            (wg_ref, wu_ref, _), f0 = item
            return _dot(x, wg_ref[:, f0:f0 + FF_TILE]), _dot(x, wu_ref[:, f0:f0 + FF_TILE])

        ups = [up(items[0]), up(items[1])]
        parts = []
        for c, ((_, _, wd_ref), f0) in enumerate(items):
            hg, hu = ups[c]
            hh = (hg * _sigmoid(hg) * hu).astype(BF16)
            if c + 2 < len(items):
                ups.append(up(items[c + 2]))
            parts.append(_dot(hh, wd_ref[f0:f0 + FF_TILE, :]))
        per = len(items) // 2
        ya = functools.reduce(lambda a, b: a + b, parts[:per])
        yb = functools.reduce(lambda a, b: a + b, parts[per:])
        _store_subrows(y_ref, _pack_rows(ya * ws[:, 0:1] + yb * ws[:, 1:2]), 0)

    @pl.when(nvalid > half)
    def _():
        run(blk)

    @pl.when((nvalid != 0) & (nvalid <= half))
    def _():
        run(half)
        y_ref[half // 8:, :, :, :] = jnp.zeros(((blk - half) // 8, SUB_Y, 8, LANE), y_ref.dtype)


def _experts(xs_sub, ea, eb, nvalid, wg, wu, wd, layer):
    nb = xs_sub.shape[0] * 8 // MOE_BLK
    d, ff = wg.shape[2], wg.shape[3]
    wspec_in = lambda which: pl.BlockSpec((None, None, d, ff),
                                          lambda j, ea, eb, v: (layer, (ea, eb)[which][j], 0, 0))
    wspec_out = lambda which: pl.BlockSpec((None, None, ff, d),
                                           lambda j, ea, eb, v: (layer, (ea, eb)[which][j], 0, 0))
    grid_spec = pltpu.PrefetchScalarGridSpec(
        num_scalar_prefetch=3,
        grid=(nb,),
        in_specs=[pl.BlockSpec((MOE_BLK // 8, SUB_X, 8, LANE), lambda j, ea, eb, v: (j, 0, 0, 0)),
                  wspec_in(0), wspec_in(0), wspec_out(0), wspec_in(1), wspec_in(1), wspec_out(1)],
        out_specs=pl.BlockSpec((MOE_BLK // 8, SUB_Y, 8, LANE), lambda j, ea, eb, v: (j, 0, 0, 0)),
    )
    return pl.pallas_call(
        _expert_kernel,
        grid_spec=grid_spec,
        out_shape=jax.ShapeDtypeStruct((nb * MOE_BLK // 8, SUB_Y, 8, LANE), jnp.uint32),
        compiler_params=_cparams(("arbitrary",)),
        name="experts",
    )(ea, eb, nvalid, xs_sub, wg, wu, wd, wg, wu, wd)


def _subrow_index(dest, nsub):
    t = dest.shape[0]
    base = ((dest // 8) * (nsub * 8) + dest % 8).astype(F32).reshape(t // LANE, LANE)
    src = np.arange(LANE)
    grp, r = src // 8, src % 8
    sel = np.zeros((LANE, LANE * nsub), np.float32)
    off = np.zeros((LANE * nsub,), np.int32)
    for j in range(nsub):
        pos = grp * (nsub * 8) + j * 8 + r
        sel[src, pos] = 1.0
        off[pos] = j * 8
    idx = jnp.dot(base, jnp.asarray(sel), precision=lax.Precision.HIGHEST).astype(jnp.int32) + jnp.asarray(off)
    return idx.reshape(t * nsub)


def _moe(h2_sub, logits, bias_col, wg, wu, wd, layer):
    t = logits.shape[0]
    meta, cnt, h2_sub = _route(logits, bias_col, h2_sub)
    cls = meta[0].astype(jnp.int32)
    rank = meta[1].astype(jnp.int32)
    counts = cnt[:N_CLASSES, 0].astype(jnp.int32)
    padded = (counts + MOE_BLK - 1) // MOE_BLK * MOE_BLK
    pad_end = jnp.cumsum(padded)
    pad_start = pad_end - padded
    class_ids = jnp.arange(N_CLASSES, dtype=jnp.int32)
    dest = rank + jnp.sum(jnp.where(cls[:, None] == class_ids[None, :], pad_start[None, :], 0), axis=1)
    nb = t // MOE_BLK + N_CLASSES
    p_rows = nb * MOE_BLK
    blk_start = jnp.arange(nb, dtype=jnp.int32) * MOE_BLK
    valid = blk_start < pad_end[-1]
    blk_cls = jnp.sum((pad_end[None, :] <= blk_start[:, None]).astype(jnp.int32), axis=-1)
    last_cls = jnp.sum((pad_end <= pad_end[-1] - 1).astype(jnp.int32))
    blk_cls = jnp.minimum(jnp.where(valid, blk_cls, last_cls), N_CLASSES - 1)
    nvalid = jnp.where(valid, jnp.clip(pad_start[blk_cls] + counts[blk_cls] - blk_start, 0, MOE_BLK), 0)
    pair_a = jnp.array([p[0] for p in PAIRS], jnp.int32)
    pair_b = jnp.array([p[1] for p in PAIRS], jnp.int32)
    grp = blk_cls // len(PAIRS)
    ea = grp * EXPERTS_PER_GROUP + pair_a[blk_cls % len(PAIRS)]
    eb = grp * EXPERTS_PER_GROUP + pair_b[blk_cls % len(PAIRS)]

    xs = _sc_scatter(h2_sub.reshape(t * SUB_X, LANE), _subrow_index(dest, SUB_X), p_rows * SUB_X)
    ys = _experts(xs.reshape(p_rows // 8, SUB_X, 8, LANE), ea, eb, nvalid.astype(jnp.int32), wg, wu, wd, layer)
    y = _sc_gather(ys.reshape(p_rows * SUB_Y, LANE), _subrow_index(dest, SUB_Y))
    return y.reshape(t // 8, SUB_Y, 8, LANE)


def _final_kernel(x_ref, y_ref, g5_ref, gf_ref, o_ref):
    x = x_ref[...] + g5_ref[...] * _unpack_rows(_load_subrows(y_ref, 0, SUB_Y))
    ms = jnp.mean(x * x, axis=-1, keepdims=True)
    o_ref[...] = x * lax.rsqrt(ms + EPS) * gf_ref[...]


def _final(x_mid, y, mod5, layer, g_final, *, tiles_per_seq):
    t, d = x_mid.shape
    tok = pl.BlockSpec((TOK_TILE, d), lambda i: (i, 0))
    return pl.pallas_call(
        _final_kernel,
        grid=(t // TOK_TILE,),
        in_specs=[tok, pl.BlockSpec((TOK_TILE // 8, SUB_Y, 8, LANE), lambda i: (i, 0, 0, 0)),
                  pl.BlockSpec((None, None, None, 1, d), lambda i: (layer, i // tiles_per_seq, 5, 0, 0)),
                  pl.BlockSpec(g_final.shape, lambda i: (0, 0))],
        out_specs=tok,
        out_shape=jax.ShapeDtypeStruct((t, d), F32),
        compiler_params=_cparams(("parallel",)),
        name="final_norm",
    )(x_mid, y, mod5, g_final)


def _rope_tables(seq):
    t = jnp.arange(seq)
    row = (t // GRID_W).astype(F32)
    col = (t % GRID_W).astype(F32)
    half = GLA_DK // 2
    inv = ROPE_BASE ** (-jnp.arange(0, half, 2, dtype=F32) / half)
    ang = jnp.concatenate([row[:, None] * inv, col[:, None] * inv], axis=-1)
    cos = jnp.repeat(jnp.cos(ang), 2, axis=-1)
    sin = (jnp.sin(ang)[:, :, None] * jnp.array([-1.0, 1.0], F32)).reshape(seq, GLA_DK)
    cos = jnp.tile(cos, (1, GLA_HEADS))
    sin = jnp.tile(sin, (1, GLA_HEADS))
    padw = QK_PAD - GLA_QK
    cos = jnp.pad(cos, ((0, 0), (0, padw)), constant_values=1.0)
    sin = jnp.pad(sin, ((0, 0), (0, padw)))
    cos = jnp.concatenate([cos, jnp.ones((TOK_TILE, QK_PAD), F32)], axis=0)
    sin = jnp.concatenate([sin, jnp.zeros((TOK_TILE, QK_PAD), F32)], axis=0)
    return cos, sin


def _pad_last(w, n):
    return jnp.pad(w, [(0, 0)] * (w.ndim - 1) + [(0, n - w.shape[-1])])


def _in_weights(w_in):
    offs = np.cumsum([0, GLA_QK, GLA_QK, GLA_WIDTH, GLA_WIDTH, 2 * GLA_LOWRANK, NA_WIDTH, NA_WIDTH, NA_WIDTH,
                      2 * CONV_WIDTH])
    seg = [w_in[..., offs[i]:offs[i + 1]] for i in range(9)]
    cols = [_pad_last(seg[0], QK_PAD), _pad_last(seg[1], QK_PAD), _pad_last(seg[4], LANE), seg[2], seg[3], seg[5],
            seg[6], seg[7], seg[8]]
    w = jnp.concatenate(cols, axis=-1).astype(BF16)
    assert w.shape[-1] == IN_COLS_PAD
    return w


def _gate_weights(wa_f, ba_f, wa_b, ba_b):
    depth = wa_f.shape[0]
    zero = jnp.zeros((depth, GLA_LOWRANK, QK_PAD), F32)
    top = jnp.concatenate([_pad_last(wa_f, QK_PAD), zero], axis=-1)
    bot = jnp.concatenate([zero, _pad_last(wa_b, QK_PAD)], axis=-1)
    rest = jnp.zeros((depth, LANE - 2 * GLA_LOWRANK, 2 * QK_PAD), F32)
    wa = jnp.concatenate([top, bot, rest], axis=1)
    ba = jnp.concatenate([_pad_last(ba_f, QK_PAD), _pad_last(ba_b, QK_PAD)], axis=-1)[:, None, :]
    return wa, ba


def _na_bias_tables(rpb):
    cq = np.arange(GRID_W)
    c0 = np.clip(cq - NA_KW // 2, 0, GRID_W - NA_KW)
    kc = np.arange(GRID_W)
    valid = (kc[None, :] >= c0[:, None]) & (kc[None, :] < c0[:, None] + NA_KW)
    w = GRID_W
    padded = jnp.pad(rpb, [(0, 0)] * 3 + [(w - NA_KW, w - NA_KW)], constant_values=NEG_BIG)
    flat = jnp.tile(padded, (1, 1, 1, w))
    band = flat[..., w - 1:w - 1 + w * (2 * w - 2)].reshape(rpb.shape[:3] + (w, 2 * w - 2))[..., :w]
    band = jnp.where(valid, band, NEG_BIG)
    tab = jnp.stack([band[:, :, NA_KH - 1 - v:2 * NA_KH - 1 - v] for v in range(NA_KH)], axis=2)
    tab = jnp.transpose(tab, (0, 1, 2, 4, 3, 5))
    return tab.reshape(rpb.shape[:2] + (NA_KH, GRID_W, NA_KH * GRID_W)).astype(F32)


def _head_mean_matrix():
    h = np.arange(GLA_WIDTH) // GLA_DV
    e = (h[:, None] == h[None, :]).astype(np.float32)
    return jnp.asarray(np.concatenate([e, e], axis=0), dtype=BF16)


def kernel(x, c, ctx, c_ctx, w_mod, b_mod, g_mix, g_ffn, w_in, gla_wa_f, gla_ba_f, gla_wa_b, gla_ba_b, gla_g_norm,
           na_rpb, conv_w, conv_b, conv_ln_g, conv_ln_b, w_out, router_w, router_bias, w_gate, w_up, w_down,
           g_final):
    bsz, seq, d = x.shape
    ctx_len = ctx.shape[1]
    depth = w_mod.shape[0]
    t_lat, t_ctx = bsz * seq, bsz * ctx_len
    assert seq % TOK_TILE == 0 and t_ctx % TOK_TILE == 0 and seq % ctx_len == 0
    assert d == 2 * SUB_Y * LANE
    assert seq // GRID_W >= NA_KH and (seq // GRID_W) % NA_UNROLL == 0
    assert ctx_len % (GLA_CHUNK * GLA_STEPS) == 0 and seq % (GLA_CHUNK * GLA_STEPS) == 0
    tiles_per_seq = seq // TOK_TILE
    n_lat_tiles = t_lat // TOK_TILE
    n_all_tiles = (t_lat + t_ctx) // TOK_TILE
    ctx_blk0 = t_lat // ctx_len

    mod_rows = -(-(bsz + 1) // 8) * 8
    c_pad = jnp.zeros((mod_rows, d), F32).at[:bsz].set(c).at[bsz].set(c_ctx)
    mod = _modulation(c_pad, w_mod, b_mod)
    mod5 = mod.reshape(depth, mod_rows, 6, 1, d)

    cos_t, sin_t = _rope_tables(seq)
    e_mat = _head_mean_matrix()
    rw = jnp.pad(router_w, ((0, 0), (0, LANE - N_EXPERTS)))
    rw_hi = rw.astype(BF16)
    rw_lo = (rw - rw_hi.astype(F32)).astype(BF16)
    rw_split = jnp.concatenate([rw_hi, rw_lo], axis=1)
    bias_col = router_bias.reshape(N_EXPERTS, 1).astype(F32)
    zero_state = jnp.zeros((bsz, GLA_WIDTH, QK_PAD), F32)

    w_aug = _in_weights(w_in)
    wa_aug, ba_aug = _gate_weights(gla_wa_f, gla_ba_f, gla_wa_b, gla_ba_b)
    bias_tab = _na_bias_tables(na_rpb)
    row = lambda a: a[:, None, :]
    gmix, gffn, gnorm = row(g_mix), row(g_ffn), row(gla_g_norm)
    cb, clg, clb = row(conv_b), row(conv_ln_g), row(conv_ln_b)
    w_out_b = w_out.astype(BF16)
    wg_b, wu_b, wd_b = w_gate.astype(BF16), w_up.astype(BF16), w_down.astype(BF16)

    x_all = (x.reshape(t_lat, d), ctx.reshape(t_ctx, d))
    y_moe = None
    for l in range(depth):
        last = l == depth - 1
        x_all, (q, k, v, sgate, gf, gb, nq, nk, nv, hcv) = _inproj(
            x_all, y_moe, mod5, l, gmix, cos_t, sin_t, w_aug, wa_aug, ba_aug,
            n_lat_tiles=n_lat_tiles, tiles_per_seq=tiles_per_seq, n_batch=bsz, res_layer=l - 1)

        o_g, st_f, st_b = _gla(q, k, v, gf, gb, zero_state, zero_state, None, seq=ctx_len, blk0=ctx_blk0,
                               n_batch=bsz)
        o_g, _, _ = _gla(q, k, v, gf, gb, st_f, st_b, o_g, seq=seq, blk0=0, n_batch=bsz)

        o_n = _na_latent(nq, nk, nv, bias_tab, l, seq=seq, ctx_len=ctx_len, n_batch=bsz, ctx_blk0=ctx_blk0)
        o_c = _conv(hcv, conv_w, cb, clg, clb, l, None, seq=seq, blk0=0, n_batch=bsz)
        if not last:
            o_n = _na_context(nq, nk, nv, o_n, ctx_len=ctx_len, n_batch=bsz, ctx_blk0=ctx_blk0)
            o_c = _conv(hcv, conv_w, cb, clg, clb, l, o_c, seq=ctx_len, blk0=ctx_blk0, n_batch=bsz)

        n_tiles = n_lat_tiles if last else n_all_tiles
        x_mid, h2, logits = _outproj(
            o_g[0], o_g[1], sgate, o_n, o_c, x_all, mod5, l, gnorm, e_mat, w_out_b, gffn, rw_split,
            n_tiles=n_tiles, n_lat_tiles=n_lat_tiles, tiles_per_seq=tiles_per_seq, n_batch=bsz)
        y_moe = _moe(h2, logits, bias_col, wg_b, wu_b, wd_b, l)
        x_all = x_mid

    out = _final(x_all, y_moe, mod5, depth - 1, g_final.reshape(1, d), tiles_per_seq=tiles_per_seq)
    return out.reshape(bsz, seq, d)
```

```python
import functools

import numpy as np
import jax
import jax.numpy as jnp
from jax import lax
from jax.experimental import pallas as pl
from jax.experimental.pallas import tpu as pltpu
from jax.experimental.pallas import tpu_sc as plsc

GRID_W = 64
EPS = 1e-6
GLA_HEADS, GLA_DK, GLA_DV = 4, 48, 96
GLA_QK = GLA_HEADS * GLA_DK
GLA_WIDTH = GLA_HEADS * GLA_DV
GLA_LOWRANK = 16
GLA_TAU = 16.0
ROPE_BASE = 10000.0
NA_HEADS, NA_DH = 6, 64
NA_WIDTH = NA_HEADS * NA_DH
NA_KH, NA_KW = 8, 16
CONV_WIDTH, CONV_K = 256, 31
N_EXPERTS, N_GROUPS, EXPERTS_PER_GROUP = 16, 4, 4
PAIRS = ((0, 1), (0, 2), (0, 3), (1, 2), (1, 3), (2, 3))
N_CLASSES = N_GROUPS * len(PAIRS)

LANE = 128
SUBLANES = 8
QK_PAD = 256
C_Q, C_K = 0, 256
C_A, C_V, C_GATE = 512, 640, 1024
C_NQ, C_NK, C_NV, C_CONV = 1408, 1792, 2176, 2560
IN_COLS_PAD = 3072
MXU_N = 256

TOK_TILE = 1024
ROW_PARTS = 4
GLA_CHUNK = 64
GLA_SUB = 16
GLA_STEPS = 4
MOE_BLK = 256
ROUTE_TILE = 512
NA_UNROLL = 8
NEG_BIG = -1e30
VMEM_PER_CORE = 64 * 1024 * 1024
VMEM_LIMIT = VMEM_PER_CORE - 8 * 1024 * 1024

F32 = jnp.float32
BF16 = jnp.bfloat16


def _cparams(sem):
    return pltpu.CompilerParams(dimension_semantics=sem, vmem_limit_bytes=VMEM_LIMIT)


def _layer_spec(a, layer):
    return pl.BlockSpec((None,) + a.shape[1:], lambda *_: (layer,) + (0,) * (a.ndim - 1),
                        pipeline_mode=pl.Buffered(1))


def _dot(a, b):
    return jnp.dot(a, b, preferred_element_type=F32)


def _dot_nt(a, b):
    return lax.dot_general(a, b, (((1,), (1,)), ((), ())), preferred_element_type=F32)


def _split(a):
    hi = a.astype(BF16)
    lo = (a - hi.astype(F32)).astype(BF16)
    return hi, lo


def _sigmoid(x):
    return 1.0 / (1.0 + jnp.exp(-x))


SUB_X, SUB_Y = 5, 4
SC_WINDOW = 256


def _pack_bf16_pairs(a, b):
    ua = lax.bitcast_convert_type(a.astype(BF16).astype(F32), jnp.uint32)
    ub = lax.bitcast_convert_type(b.astype(BF16).astype(F32), jnp.uint32)
    return jnp.bitwise_or(jnp.right_shift(ua, jnp.uint32(16)), ub)


def _unpack_bf16_pairs(w):
    lo = lax.bitcast_convert_type(jnp.left_shift(w, jnp.uint32(16)), F32)
    hi = lax.bitcast_convert_type(jnp.bitwise_and(w, jnp.uint32(0xFFFF0000)), F32)
    return lo, hi


def _store_subrows(ref, val, j0, row0=0):
    r = val.shape[0]
    for j in range(val.shape[1] // LANE):
        ref[row0 // 8:(row0 + r) // 8, j0 + j, :, :] = val[:, j * LANE:(j + 1) * LANE].reshape(r // 8, 8, LANE)


def _load_subrows(ref, j0, n, row0=0, rows=None):
    r = ref.shape[0] * 8 if rows is None else rows
    return jnp.concatenate([ref[row0 // 8:(row0 + r) // 8, j0 + j, :, :].reshape(r, LANE) for j in range(n)],
                           axis=1)


def _pack_rows(x):
    half = x.shape[1] // 2
    return _pack_bf16_pairs(x[:, :half], x[:, half:])


def _unpack_rows(w):
    lo, hi = _unpack_bf16_pairs(w)
    return jnp.concatenate([lo, hi], axis=1)


def _sc_mesh():
    return plsc.VectorSubcoreMesh(core_axis_name="c", subcore_axis_name="s")


def _sc_scatter(src, idx, n_out):
    n, w = src.shape
    idx2 = idx.reshape(1, n)

    @pl.kernel(out_type=jax.ShapeDtypeStruct((n_out, w), src.dtype), mesh=_sc_mesh(), scratch_types=[])
    def scatter_kernel(x_hbm, i_hbm, o_hbm):
        def body(x_vmem, i_vmem):
            pltpu.sync_copy(x_vmem, o_hbm.at[i_vmem.at[0]])

        pltpu.emit_pipeline(
            body,
            grid=(n // SC_WINDOW,),
            in_specs=[pl.BlockSpec((SC_WINDOW, w), index_map=lambda i: (i, 0)),
                      pl.BlockSpec((1, SC_WINDOW), index_map=lambda i: (0, i))],
            out_specs=[],
            core_axis_name=("c", "s"),
            dimension_semantics=(pltpu.PARALLEL,),
        )(x_hbm, i_hbm)

    return scatter_kernel(src, idx2)


def _sc_gather(src, idx):
    n = idx.shape[0]
    w = src.shape[1]
    idx2 = idx.reshape(1, n)

    @pl.kernel(out_type=jax.ShapeDtypeStruct((n, w), src.dtype), mesh=_sc_mesh())
    def gather_kernel(x_hbm, i_hbm, o_hbm):
        def body(i_vmem, o_vmem):
            pltpu.sync_copy(x_hbm.at[i_vmem.at[0]], o_vmem)

        pltpu.emit_pipeline(
            body,
            grid=(n // SC_WINDOW,),
            in_specs=[pl.BlockSpec((1, SC_WINDOW), index_map=lambda i: (0, i))],
            out_specs=[pl.BlockSpec((SC_WINDOW, w), index_map=lambda i: (i, 0))],
            core_axis_name=("c", "s"),
            dimension_semantics=(pltpu.PARALLEL,),
        )(i_hbm, o_hbm)

    return gather_kernel(src, idx2)


def _mod_kernel(c_ref, w_ref, b_ref, o_ref):
    cv = c_ref[...]
    s = cv * _sigmoid(cv)
    s_hi, s_lo = _split(s)
    w_hi, w_lo = _split(w_ref[...])
    o_ref[...] = _dot(s_hi, w_hi) + _dot(s_lo, w_hi) + _dot(s_hi, w_lo) + b_ref[...]


def _modulation(c_pad, w_mod, b_mod):
    depth, d, six_d = w_mod.shape
    rows = c_pad.shape[0]
    nt = 1536
    return pl.pallas_call(
        _mod_kernel,
        grid=(depth, six_d // nt),
        in_specs=[
            pl.BlockSpec((rows, d), lambda l, j: (0, 0)),
            pl.BlockSpec((None, d, nt), lambda l, j: (l, 0, j)),
            pl.BlockSpec((None, 1, nt), lambda l, j: (l, 0, j)),
        ],
        out_specs=pl.BlockSpec((None, rows, nt), lambda l, j: (l, 0, j)),
        out_shape=jax.ShapeDtypeStruct((depth, rows, six_d), F32),
        compiler_params=_cparams(("arbitrary", "arbitrary")),
        name="modulation",
    )(c_pad, w_mod, b_mod.reshape(depth, 1, six_d))


def _inproj_kernel(*refs, has_res, n_lat_tiles):
    if has_res:
        x_ref, y_ref, g5_ref = refs[:3]
    else:
        x_ref, xctx_ref = refs[:2]
        is_lat = pl.program_id(0) < n_lat_tiles
    refs = refs[3 if has_res else 2:]
    (sh_ref, sc_ref, gmix_ref, cos_ref, sin_ref, w_ref, wa_ref, ba_ref) = refs[:8]
    (xnew_ref, q_ref, k_ref, v_ref, sg_ref, gf_ref, gb_ref, nq_ref, nk_ref, nv_ref, hcv_ref) = refs[8:]

    tile = x_ref.shape[0]
    part = tile // ROW_PARTS
    wa_hi, wa_lo = _split(wa_ref[...])
    even_lane = jnp.bitwise_and(lax.broadcasted_iota(jnp.int32, (1, LANE), 1), 1) == 0
    hbs = []
    for p in range(ROW_PARTS):
        rows = pl.ds(p * part, part)
        if has_res:
            x = x_ref[rows, :] + g5_ref[...] * _unpack_rows(_load_subrows(y_ref, 0, SUB_Y, row0=p * part, rows=part))
        else:
            x = jnp.where(is_lat, x_ref[rows, :], xctx_ref[rows, :])
        xnew_ref[rows, :] = x
        ms = jnp.mean(x * x, axis=-1, keepdims=True)
        h = x * lax.rsqrt(ms + EPS) * gmix_ref[...]
        h = h * (1.0 + sc_ref[...]) + sh_ref[...]
        hbs.append(h.astype(BF16))

    for p, hb in enumerate(hbs):
        rows = pl.ds(p * part, part)

        def proj(c0, n, hb=hb):
            return _dot(hb, w_ref[:, c0:c0 + n])

        cos = cos_ref[rows, :]
        sin = sin_ref[rows, :]

        def rope(t):
            halves = []
            for c0 in range(0, QK_PAD, LANE):
                th = t[:, c0:c0 + LANE]
                halves.append(jnp.where(even_lane, pltpu.roll(th, LANE - 1, 1), pltpu.roll(th, 1, 1)))
            return t * cos + jnp.concatenate(halves, axis=1) * sin

        mid = (C_A + C_CONV) // 2
        assert (mid - C_A) % MXU_N == 0 and C_NQ < mid < C_NK
        d1 = proj(C_A, mid - C_A)
        a_hi, a_lo = _split(d1[:, :LANE])
        q_ref[rows, :] = (rope(proj(C_Q, QK_PAD)) * (GLA_DK ** -0.5)).astype(BF16)
        k_ref[rows, :] = rope(proj(C_K, QK_PAD)).astype(BF16)

        z = _dot(a_hi, wa_hi) + _dot(a_lo, wa_hi) + _dot(a_hi, wa_lo) + ba_ref[...]
        logsig = jnp.minimum(z, 0.0) - jnp.log(1.0 + jnp.exp(-jnp.abs(z)))
        g = logsig * (1.0 / GLA_TAU)
        gf_ref[rows, :] = g[:, :QK_PAD]
        gb_ref[rows, :] = g[:, QK_PAD:]

        v_ref[rows, :] = d1[:, C_V - C_A:C_GATE - C_A].astype(BF16)
        gate = d1[:, C_GATE - C_A:C_NQ - C_A]
        sg_ref[rows, :] = (gate * _sigmoid(gate)).astype(BF16)
        d2 = proj(mid, C_CONV - mid)
        nq = jnp.concatenate([d1[:, C_NQ - C_A:], d2[:, :C_NK - mid]], axis=1)
        nq_ref[rows, :] = (nq * (NA_DH ** -0.5)).astype(BF16)
        nk_ref[rows, :] = d2[:, C_NK - mid:C_NV - mid].astype(BF16)
        nv_ref[rows, :] = d2[:, C_NV - mid:].astype(BF16)
        u = proj(C_CONV, 2 * CONV_WIDTH)
        hcv_ref[rows, :] = u[:, :CONV_WIDTH] * _sigmoid(u[:, CONV_WIDTH:])


def _inproj(x_all, y, mod5, layer, gmix, cos_t, sin_t, w_aug, wa_aug, ba_aug, *, n_lat_tiles, tiles_per_seq,
            n_batch, res_layer):
    has_res = y is not None
    d = x_all.shape[1] if has_res else x_all[0].shape[1]
    t_all = x_all.shape[0] if has_res else x_all[0].shape[0] + x_all[1].shape[0]
    nt = t_all // TOK_TILE

    def bidx(i):
        return jnp.where(i < n_lat_tiles, i // tiles_per_seq, n_batch)

    def ridx(i):
        return jnp.where(i < n_lat_tiles, i % tiles_per_seq, tiles_per_seq)

    def modspec(l, j):
        return pl.BlockSpec((None, None, None, 1, d), lambda i: (l, bidx(i), j, 0, 0))

    tok = lambda w: pl.BlockSpec((TOK_TILE, w), lambda i: (i, 0))
    full = lambda a: pl.BlockSpec(a.shape, lambda i: (0,) * a.ndim)

    if has_res:
        in_specs = [tok(d), pl.BlockSpec((TOK_TILE // 8, SUB_Y, 8, LANE), lambda i: (i, 0, 0, 0)),
                    modspec(res_layer, 5)]
        args = [x_all, y, mod5]
    else:
        in_specs = [pl.BlockSpec((TOK_TILE, d), lambda i: (jnp.minimum(i, n_lat_tiles - 1), 0)),
                    pl.BlockSpec((TOK_TILE, d), lambda i: (jnp.maximum(i - n_lat_tiles, 0), 0))]
        args = list(x_all)
    in_specs += [modspec(layer, 0), modspec(layer, 1), _layer_spec(gmix, layer),
                 pl.BlockSpec((TOK_TILE, QK_PAD), lambda i: (ridx(i), 0)),
                 pl.BlockSpec((TOK_TILE, QK_PAD), lambda i: (ridx(i), 0)),
                 _layer_spec(w_aug, layer), _layer_spec(wa_aug, layer), _layer_spec(ba_aug, layer)]
    args += [mod5, mod5, gmix, cos_t, sin_t, w_aug, wa_aug, ba_aug]

    out_widths = [(d, F32), (QK_PAD, BF16), (QK_PAD, BF16), (GLA_WIDTH, BF16), (GLA_WIDTH, BF16), (QK_PAD, F32),
                  (QK_PAD, F32), (NA_WIDTH, BF16), (NA_WIDTH, BF16), (NA_WIDTH, BF16), (CONV_WIDTH, F32)]
    out_specs = [tok(w) for w, _ in out_widths]
    out_shape = [jax.ShapeDtypeStruct((t_all, w), dt) for w, dt in out_widths]
    res = pl.pallas_call(
        functools.partial(_inproj_kernel, has_res=has_res, n_lat_tiles=n_lat_tiles),
        grid=(nt,),
        in_specs=in_specs,
        out_specs=out_specs,
        out_shape=out_shape,
        compiler_params=_cparams(("parallel",)),
        name="inproj",
    )(*args)
    return res[0], res[1:]


def _gla_masks():
    c, sub = GLA_CHUNK, GLA_SUB
    lane_qk = lax.broadcasted_iota(jnp.int32, (1, QK_PAD), 1)
    head_qk = ((lane_qk >= GLA_DK).astype(jnp.int32) + (lane_qk >= 2 * GLA_DK).astype(jnp.int32)
               + (lane_qk >= 3 * GLA_DK).astype(jnp.int32) + 4 * (lane_qk >= 4 * GLA_DK).astype(jnp.int32))
    row_h = jnp.right_shift(lax.broadcasted_iota(jnp.int32, (c, 1), 0), GLA_SUB.bit_length() - 1)
    hm = (row_h == head_qk).astype(F32)
    row_v = lax.broadcasted_iota(jnp.int32, (GLA_WIDTH, 1), 0)
    head_v = ((row_v >= GLA_DV).astype(jnp.int32) + (row_v >= 2 * GLA_DV).astype(jnp.int32)
              + (row_v >= 3 * GLA_DV).astype(jnp.int32))
    bd = (head_v == head_qk).astype(F32)
    lane_v = lax.broadcasted_iota(jnp.int32, (1, GLA_WIDTH), 1)
    vm = [((lane_v >= h * GLA_DV) & (lane_v < (h + 1) * GLA_DV)).astype(F32) for h in range(GLA_HEADS)]
    return hm, bd, vm


def _gla_steps(q, k, v, g, s_t, hm, bd, vm):
    c, sub = GLA_CHUNK, GLA_SUB
    nsub = c // sub
    dirs = (True, False)
    items = [(u, d) for u in range(GLA_STEPS) for d in range(2)]
    ri = lax.broadcasted_iota(jnp.int32, (c, c), 0)
    ci = lax.broadcasted_iota(jnp.int32, (c, c), 1)
    key_row = lax.broadcasted_iota(jnp.int32, (c, 1), 0)
    att_row = jnp.bitwise_and(ri, sub - 1)
    tri = [((ci <= ri) if fwd else (ci >= ri)).astype(BF16) for fwd in dirs]

    cums = {}
    for u, d in items:
        g_hi, g_lo = _split(g[u][d])
        cums[u, d] = _dot(tri[d], g_hi) + _dot(tri[d], g_lo)

    qe, kv, decay, atts = {}, {}, {}, {}
    for u, d in items:
        fwd = dirs[d]
        cum = cums[u, d]
        tot = cum[c - 1:c] if fwd else cum[0:1]
        qe[u, d] = (q[u][d] * jnp.exp(cum)).astype(BF16)
        k_end = (k[u][d] * jnp.exp(tot - cum)).astype(BF16)
        kv[u, d] = lax.dot_general(v[u][d], k_end, (((0,), (0,)), ((), ())), preferred_element_type=F32)
        decay[u, d] = jnp.exp(tot)
        att_d = []
        for i in range(nsub):
            lo, hi = i * sub, (i + 1) * sub
            if fwd:
                ref = cum[lo - 1:lo] if i > 0 else jnp.zeros((1, QK_PAD), F32)
                key_ok = key_row < hi
                causal = ci <= att_row + lo
            else:
                ref = cum[hi:hi + 1] if i < nsub - 1 else jnp.zeros((1, QK_PAD), F32)
                key_ok = key_row >= lo
                causal = ci >= att_row + lo
            qi = q[u][d][lo:hi] * jnp.exp(cum[lo:hi] - ref)
            qs = (jnp.concatenate([qi] * GLA_HEADS, axis=0) * hm).astype(BF16)
            ki = (k[u][d] * jnp.exp(jnp.where(key_ok, ref - cum, NEG_BIG))).astype(BF16)
            att = _dot_nt(qs, ki)
            att_d.append(jnp.where(causal, att, 0.0).astype(BF16))
        atts[u, d] = jnp.concatenate(att_d, axis=0)

    s = list(s_t)
    o_inter = {}
    for u, d in items:
        o_inter[u, d] = _dot_nt(qe[u, d], s[d].astype(BF16))
        s[d] = s[d] * decay[u, d] + bd * kv[u, d]

    outs = [[None, None] for _ in range(GLA_STEPS)]
    for u, d in items:
        r = _dot(atts[u, d], v[u][d])
        blocks = []
        for i in range(nsub):
            base = i * c
            oi = r[base:base + sub] * vm[0]
            for h in range(1, GLA_HEADS):
                oi = oi + r[base + h * sub:base + (h + 1) * sub] * vm[h]
            blocks.append(oi)
        outs[u][d] = o_inter[u, d] + jnp.concatenate(blocks, axis=0)
    return outs, s


def _gla_kernel(q_ref, k_ref, v_ref, gf_ref, gb_ref, sf0_ref, sb0_ref, of_ref, ob_ref, sf_ref, sb_ref):
    n = q_ref.shape[0]
    nc = n // GLA_CHUNK
    hm, bd, vm = _gla_masks()
    sf_ref[...] = sf0_ref[...]
    sb_ref[...] = sb0_ref[...]

    def body(j, carry):
        rows = [[pl.ds(pl.multiple_of(cidx * GLA_CHUNK, GLA_CHUNK), GLA_CHUNK)
                 for cidx in (j * GLA_STEPS + u, nc - 1 - (j * GLA_STEPS + u))] for u in range(GLA_STEPS)]
        q = [[q_ref[r, :].astype(F32) for r in ru] for ru in rows]
        k = [[k_ref[r, :].astype(F32) for r in ru] for ru in rows]
        v = [[v_ref[r, :] for r in ru] for ru in rows]
        g = [[gf_ref[ru[0], :], gb_ref[ru[1], :]] for ru in rows]
        outs, s_new = _gla_steps(q, k, v, g, [sf_ref[...], sb_ref[...]], hm, bd, vm)
        sf_ref[...] = s_new[0]
        sb_ref[...] = s_new[1]
        for u in range(GLA_STEPS):
            of_ref[rows[u][0], :] = outs[u][0]
            ob_ref[rows[u][1], :] = outs[u][1]
        return carry

    lax.fori_loop(0, nc // GLA_STEPS, body, 0)


def _gla(q, k, v, gf, gb, sf0, sb0, o_prev, *, seq, blk0, n_batch):
    t_all = q.shape[0]
    tokw = lambda w: pl.BlockSpec((seq, w), lambda b: (blk0 + b, 0))
    st = pl.BlockSpec((None, GLA_WIDTH, QK_PAD), lambda b: (b, 0, 0))
    in_specs = [tokw(QK_PAD), tokw(QK_PAD), tokw(GLA_WIDTH), tokw(QK_PAD), tokw(QK_PAD), st, st]
    args = [q, k, v, gf, gb, sf0, sb0]
    aliases = {}
    n_in = len(args)
    if o_prev is not None:
        in_specs += [pl.BlockSpec(memory_space=pl.ANY)] * 2
        args += list(o_prev)
        aliases = {n_in: 0, n_in + 1: 1}

    def kern(*refs):
        _gla_kernel(*refs[:n_in], *refs[len(args):])

    st_shape = jax.ShapeDtypeStruct((n_batch, GLA_WIDTH, QK_PAD), F32)
    o_shape = jax.ShapeDtypeStruct((t_all, GLA_WIDTH), F32)
    o_f, o_b, s_f, s_b = pl.pallas_call(
        kern,
        grid=(n_batch,),
        in_specs=in_specs,
        out_specs=[tokw(GLA_WIDTH), tokw(GLA_WIDTH), st, st],
        out_shape=[o_shape, o_shape, st_shape, st_shape],
        input_output_aliases=aliases,
        compiler_params=_cparams(("parallel",)),
        name="gla",
    )(*args)
    return (o_f, o_b), s_f, s_b


def _na_kernel(q_ref, k_ref, v_ref, kc_ref, vc_ref, bias_ref, o_ref):
    n = q_ref.shape[0]
    rows = n // GRID_W
    nkeys = NA_KH * GRID_W
    lane = lax.broadcasted_iota(jnp.int32, (1, LANE), 1)
    first = lane < NA_DH
    kc = kc_ref[...]
    vc = vc_ref[...]

    def body(jb, carry):
        items = []
        for j in range(NA_UNROLL):
            r = jb * NA_UNROLL + j
            r0 = jnp.clip(r - NA_KH // 2, 0, rows - NA_KH)
            var = r - r0
            qrows = pl.ds(pl.multiple_of(r * GRID_W, GRID_W), GRID_W)
            krows = pl.ds(pl.multiple_of(r0 * GRID_W, GRID_W), nkeys)
            qr = q_ref[qrows, :]
            kb = k_ref[krows, :]
            q2 = jnp.concatenate([jnp.where(first, qr, jnp.zeros_like(qr)),
                                  jnp.where(first, jnp.zeros_like(qr), qr)], axis=0)
            items.append((qrows, krows, var, _dot_nt(q2, kb), _dot_nt(q2, kc)))
        probs = []
        for qrows, krows, var, s_loc, s_ctx in items:
            s_loc = s_loc + jnp.concatenate([bias_ref[0, var], bias_ref[1, var]], axis=0)
            m = jnp.maximum(jnp.max(s_loc, axis=-1, keepdims=True), jnp.max(s_ctx, axis=-1, keepdims=True))
            p_loc = jnp.exp(s_loc - m)
            p_ctx = jnp.exp(s_ctx - m)
            l = jnp.sum(p_loc, axis=-1, keepdims=True) + jnp.sum(p_ctx, axis=-1, keepdims=True)
            probs.append((p_loc.astype(BF16), p_ctx.astype(BF16), l))
        for (qrows, krows, var, _, _), (p_loc, p_ctx, l) in zip(items, probs):
            o = (_dot(p_loc, v_ref[krows, :]) + _dot(p_ctx, vc)) / l
            o_ref[qrows, :] = jnp.where(first, o[:GRID_W], o[GRID_W:]).astype(o_ref.dtype)
        return carry

    lax.fori_loop(0, rows // NA_UNROLL, body, 0)


def _na_latent(nq, nk, nv, bias_tab, layer, *, seq, ctx_len, n_batch, ctx_blk0):
    t_all = nq.shape[0]
    npair = NA_HEADS // 2
    lat = pl.BlockSpec((seq, LANE), lambda b, p: (b, p))
    ctx = pl.BlockSpec((ctx_len, LANE), lambda b, p: (ctx_blk0 + b, p))
    bias = pl.BlockSpec((None, 2) + bias_tab.shape[2:], lambda b, p: (layer, p, 0, 0, 0))
    return pl.pallas_call(
        _na_kernel,
        grid=(n_batch, npair),
        in_specs=[lat, lat, lat, ctx, ctx, bias],
        out_specs=lat,
        out_shape=jax.ShapeDtypeStruct((t_all, NA_WIDTH), BF16),
        compiler_params=_cparams(("parallel", "arbitrary")),
        name="na_latent",
    )(nq, nk, nv, nk, nv, bias_tab)


def _na_ctx_kernel(q_ref, k_ref, v_ref, o_in_ref, o_ref):
    del o_in_ref
    lane = lax.broadcasted_iota(jnp.int32, (1, LANE), 1)
    first = lane < NA_DH
    q = q_ref[...]
    k = k_ref[...]
    v = v_ref[...]
    res = []
    for h in range(2):
        sel = first if h == 0 else jnp.logical_not(first)
        qh = jnp.where(sel, q, jnp.zeros_like(q))
        s = _dot_nt(qh, k)
        m = jnp.max(s, axis=-1, keepdims=True)
        p = jnp.exp(s - m)
        l = jnp.sum(p, axis=-1, keepdims=True)
        res.append(_dot(p.astype(BF16), v) / l)
    o_ref[...] = jnp.where(first, res[0], res[1]).astype(o_ref.dtype)


def _na_context(nq, nk, nv, o_prev, *, ctx_len, n_batch, ctx_blk0):
    npair = NA_HEADS // 2
    ctx = pl.BlockSpec((ctx_len, LANE), lambda b, p: (ctx_blk0 + b, p))
    return pl.pallas_call(
        _na_ctx_kernel,
        grid=(n_batch, npair),
        in_specs=[ctx, ctx, ctx, pl.BlockSpec(memory_space=pl.ANY)],
        out_specs=ctx,
        out_shape=jax.ShapeDtypeStruct(o_prev.shape, o_prev.dtype),
        input_output_aliases={3: 0},
        compiler_params=_cparams(("parallel", "arbitrary")),
        name="na_context",
    )(nq, nk, nv, o_prev)


CONV_HALO = 16
CONV_ROWS = 128


def _conv_kernel(*refs, has_prev):
    if has_prev:
        h_ref, w_ref, b_ref, lg_ref, lb_ref, _, o_ref, pad_ref, sh_ref = refs
    else:
        h_ref, w_ref, b_ref, lg_ref, lb_ref, o_ref, pad_ref, sh_ref = refs
    n = h_ref.shape[0]
    zeros = jnp.zeros((CONV_HALO, CONV_WIDTH), F32)
    pad_ref[0:CONV_HALO, :] = zeros
    pad_ref[CONV_HALO + n:CONV_HALO + n + CONV_HALO, :] = zeros
    pad_ref[CONV_HALO:CONV_HALO + n, :] = h_ref[...]
    span = n + 2 * CONV_HALO - SUBLANES
    for s in range(1, SUBLANES):
        sh_ref[s - 1, 0:span, :] = pad_ref[s:s + span, :]
    w = w_ref[...]
    off = CONV_HALO - CONV_K // 2

    def chunk(cidx, carry):
        base = pl.multiple_of(cidx * CONV_ROWS, CONV_ROWS)
        acc = jnp.zeros((CONV_ROWS, CONV_WIDTH), F32) + b_ref[...]
        for j in range(CONV_K):
            s = (off + j) % SUBLANES
            win = pl.ds(base + (off + j - s), CONV_ROWS)
            tap = pad_ref[win, :] if s == 0 else sh_ref[s - 1, win, :]
            acc = acc + tap * w[j:j + 1, :]
        mu = jnp.mean(acc, axis=-1, keepdims=True)
        xc = acc - mu
        var = jnp.mean(xc * xc, axis=-1, keepdims=True)
        y = xc * lax.rsqrt(var + EPS) * lg_ref[...] + lb_ref[...]
        o_ref[pl.ds(base, CONV_ROWS), :] = (y * _sigmoid(y)).astype(o_ref.dtype)
        return carry

    lax.fori_loop(0, n // CONV_ROWS, chunk, 0)


def _conv(hcv, w, b, lg, lb, layer, o_prev, *, seq, blk0, n_batch):
    t_all = hcv.shape[0]
    tok = pl.BlockSpec((seq, CONV_WIDTH), lambda i: (blk0 + i, 0))
    full = lambda a: pl.BlockSpec(a.shape, lambda i: (0,) * a.ndim)
    in_specs = [tok] + [_layer_spec(a, layer) for a in (w, b, lg, lb)]
    args = [hcv, w, b, lg, lb]
    aliases = {}
    if o_prev is not None:
        in_specs.append(pl.BlockSpec(memory_space=pl.ANY))
        args.append(o_prev)
        aliases = {5: 0}
    return pl.pallas_call(
        functools.partial(_conv_kernel, has_prev=o_prev is not None),
        grid=(n_batch,),
        in_specs=in_specs,
        out_specs=tok,
        out_shape=jax.ShapeDtypeStruct((t_all, CONV_WIDTH), BF16),
        scratch_shapes=[pltpu.VMEM((seq + 2 * CONV_HALO, CONV_WIDTH), F32),
                        pltpu.VMEM((SUBLANES - 1, seq + 2 * CONV_HALO, CONV_WIDTH), F32)],
        input_output_aliases=aliases,
        compiler_params=_cparams(("parallel",)),
        name="conv",
    )(*args)


def _outproj_kernel(ogf_ref, ogb_ref, sg_ref, on_ref, oc_ref, x_ref, g2_ref, sh_ref, sc_ref, gn_ref, e_ref,
                    wo_ref, gffn_ref, rw_ref, xmid_ref, h2_ref, lg_ref):
    tile = x_ref.shape[0]
    parts = [pl.ds(p * (tile // ROW_PARTS), tile // ROW_PARTS) for p in range(ROW_PARTS)]
    e = e_ref[...]
    rw = rw_ref[...]

    ofs, mss = [], []
    for rows in parts:
        of = ogf_ref[rows, :] + ogb_ref[rows, :]
        sq_hi, sq_lo = _split(of * of)
        ofs.append(of)
        mss.append(_dot(jnp.concatenate([sq_hi, sq_lo], axis=1), e) * (1.0 / GLA_DV))
    ys = []
    for rows, of, ms in zip(parts, ofs, mss):
        og = of * lax.rsqrt(ms + EPS) * gn_ref[...] * sg_ref[rows, :].astype(F32)
        mix = jnp.concatenate([og.astype(BF16), on_ref[rows, :], oc_ref[rows, :]], axis=1)
        ys.append(_dot(mix, wo_ref[...]))
    for p, (rows, y) in enumerate(zip(parts, ys)):
        x = x_ref[rows, :] + g2_ref[...] * y
        xmid_ref[rows, :] = x
        ms2 = jnp.mean(x * x, axis=-1, keepdims=True)
        h2 = x * lax.rsqrt(ms2 + EPS) * gffn_ref[...]
        h2 = h2 * (1.0 + sc_ref[...]) + sh_ref[...]
        h_hi, h_lo = _split(h2)
        _store_subrows(h2_ref, _pack_rows(h2), 0, row0=p * (tile // ROW_PARTS))
        hw = _dot(h_hi, rw)
        lg_ref[rows, :] = hw[:, :LANE] + hw[:, LANE:] + _dot(h_lo, rw[:, :LANE])


def _outproj(o_gf, o_gb, sgate, o_n, o_c, x_all, mod5, layer, gnorm, e_mat, w_out, gffn, rw_split, *, n_tiles,
             n_lat_tiles, tiles_per_seq, n_batch):
    d = x_all.shape[1]
    t_out = n_tiles * TOK_TILE

    def bidx(i):
        return jnp.where(i < n_lat_tiles, i // tiles_per_seq, n_batch)

    def modspec(j):
        return pl.BlockSpec((None, None, None, 1, d), lambda i: (layer, bidx(i), j, 0, 0))

    tok = lambda w: pl.BlockSpec((TOK_TILE, w), lambda i: (i, 0))
    full = lambda a: pl.BlockSpec(a.shape, lambda i: (0,) * a.ndim)
    return pl.pallas_call(
        _outproj_kernel,
        grid=(n_tiles,),
        in_specs=[tok(GLA_WIDTH), tok(GLA_WIDTH), tok(GLA_WIDTH), tok(NA_WIDTH), tok(CONV_WIDTH), tok(d),
                  modspec(2), modspec(3), modspec(4), _layer_spec(gnorm, layer), full(e_mat),
                  _layer_spec(w_out, layer), _layer_spec(gffn, layer),
                  full(rw_split)],
        out_specs=[tok(d), pl.BlockSpec((TOK_TILE // 8, SUB_Y, 8, LANE), lambda i: (i, 0, 0, 0)), tok(LANE)],
        out_shape=[jax.ShapeDtypeStruct((t_out, d), F32),
                   jax.ShapeDtypeStruct((t_out // 8, SUB_X, 8, LANE), jnp.uint32),
                   jax.ShapeDtypeStruct((t_out, LANE), F32)],
        compiler_params=_cparams(("parallel",)),
        name="outproj",
    )(o_gf, o_gb, sgate, o_n, o_c, x_all, mod5, mod5, mod5, gnorm, e_mat, w_out, gffn, rw_split)


def _route_kernel(lg_ref, bias_ref, h2_in_ref, meta_ref, cnt_ref, h2w_ref, carry_ref):
    del h2_in_ref
    tile = lg_ref.shape[0]

    @pl.when(pl.program_id(0) == 0)
    def _():
        carry_ref[...] = jnp.zeros_like(carry_ref)

    lt = lg_ref[...].T
    aff = _sigmoid(lt[0:N_EXPERTS])
    sel = aff + bias_ref[...]
    s = [sel[e:e + 1] for e in range(N_EXPERTS)]
    a = [aff[e:e + 1] for e in range(N_EXPERTS)]

    def top2sum(v):
        best = v[0] + v[1]
        for i, j in PAIRS[1:]:
            best = jnp.maximum(best, v[i] + v[j])
        return best

    gs = [top2sum(s[4 * g:4 * g + 4]) for g in range(N_GROUPS)]
    gbest = jnp.zeros_like(gs[0], dtype=jnp.int32)
    gmax = gs[0]
    for g in range(1, N_GROUPS):
        upd = gs[g] > gmax
        gbest = jnp.where(upd, g, gbest)
        gmax = jnp.where(upd, gs[g], gmax)

    def pick(vals, j):
        out = vals[j]
        for g in range(1, N_GROUPS):
            out = jnp.where(gbest == g, vals[4 * g + j], out)
        return out

    sv = [pick(s, j) for j in range(EXPERTS_PER_GROUP)]
    av = [pick(a, j) for j in range(EXPERTS_PER_GROUP)]
    i1 = jnp.zeros_like(gbest)
    m1 = sv[0]
    for j in range(1, EXPERTS_PER_GROUP):
        upd = sv[j] > m1
        i1 = jnp.where(upd, j, i1)
        m1 = jnp.where(upd, sv[j], m1)
    i2 = jnp.full_like(gbest, -1)
    m2 = jnp.zeros_like(m1)
    for j in range(EXPERTS_PER_GROUP):
        upd = (i1 != j) & ((sv[j] > m2) | (i2 < 0))
        i2 = jnp.where(upd, j, i2)
        m2 = jnp.where(upd, sv[j], m2)
    ia = jnp.minimum(i1, i2)
    ib = jnp.maximum(i1, i2)
    pair = jnp.where(ia == 0, ib - 1, jnp.where(ia == 1, ib + 1, 5))
    cls = gbest * len(PAIRS) + pair

    def take(vals, idx):
        out = vals[0]
        for j in range(1, EXPERTS_PER_GROUP):
            out = jnp.where(idx == j, vals[j], out)
        return out

    w1 = take(av, i1)
    w2 = take(av, i2)
    tot = w1 + w2
    wa = jnp.where(i1 < i2, w1, w2) / tot
    wb = jnp.where(i1 < i2, w2, w1) / tot

    crow = lax.broadcasted_iota(jnp.int32, (32, tile), 0)
    oh = (crow == cls).astype(F32)
    us = lax.broadcasted_iota(jnp.int32, (tile, tile), 0)
    ut = lax.broadcasted_iota(jnp.int32, (tile, tile), 1)
    upper = (us < ut).astype(BF16)
    prefix = _dot(oh.astype(BF16), upper)
    carry = carry_ref[...]
    rank = jnp.sum(oh * (prefix + carry), axis=0, keepdims=True)
    carry_new = carry + jnp.sum(oh, axis=1, keepdims=True)
    carry_ref[...] = carry_new
    cnt_ref[...] = jnp.broadcast_to(carry_new, cnt_ref.shape)

    meta_ref[...] = jnp.zeros_like(meta_ref)
    meta_ref[0:1, :] = cls.astype(F32)
    meta_ref[1:2, :] = rank

    wrow = lax.broadcasted_iota(jnp.int32, (LANE, tile), 0)
    wmat = jnp.where(wrow == 0, wa, jnp.where(wrow == 1, wb, 0.0))
    wtok = lax.bitcast_convert_type(wmat.T, jnp.uint32)
    h2w_ref[:, 0, :, :] = wtok.reshape(tile // 8, 8, LANE)


def _route(logits, bias_col, h2_sub):
    t = logits.shape[0]
    nt = t // ROUTE_TILE
    meta, cnt, h2_sub = pl.pallas_call(
        _route_kernel,
        grid=(nt,),
        in_specs=[pl.BlockSpec((ROUTE_TILE, LANE), lambda i: (i, 0)),
                  pl.BlockSpec(bias_col.shape, lambda i: (0, 0)),
                  pl.BlockSpec(memory_space=pl.ANY)],
        out_specs=[pl.BlockSpec((8, ROUTE_TILE), lambda i: (0, i)),
                   pl.BlockSpec((32, LANE), lambda i: (0, 0)),
                   pl.BlockSpec((ROUTE_TILE // 8, 1, 8, LANE), lambda i: (i, SUB_Y, 0, 0))],
        out_shape=[jax.ShapeDtypeStruct((8, t), F32), jax.ShapeDtypeStruct((32, LANE), F32),
                   jax.ShapeDtypeStruct(h2_sub.shape, h2_sub.dtype)],
        scratch_shapes=[pltpu.VMEM((32, 1), F32)],
        input_output_aliases={2: 2},
        compiler_params=_cparams(("arbitrary",)),
        name="route",
    )(logits, bias_col, h2_sub)
    return meta, cnt, h2_sub


FF_TILE = 512


def _expert_kernel(ea_ref, eb_ref, nvalid_ref, xs_ref, wga_ref, wua_ref, wda_ref, wgb_ref, wub_ref, wdb_ref,
                   *rest, cast_steps):
    del ea_ref, eb_ref
    j = pl.program_id(0)
    nvalid = nvalid_ref[j]
    nxt_in, (y_ref, *nxt_out) = (rest[:3], rest[3:]) if cast_steps else ((), rest)

    def cast_next():
        for src, dst in zip(nxt_in, nxt_out):
            dst[...] = src[...].astype(BF16)

    @pl.when(nvalid == 0)
    def _():
        cast_next()
        y_ref[...] = jnp.zeros_like(y_ref)

    @pl.when(nvalid != 0)
    def _():
        cast_next()
        rows = y_ref.shape[0] * 8
        live = lax.broadcasted_iota(jnp.int32, (rows, 1), 0) < nvalid
        x = jnp.where(live, _unpack_rows(_load_subrows(xs_ref, 0, SUB_Y)), 0.0).astype(BF16)
        ws = jnp.where(live, lax.bitcast_convert_type(xs_ref[:, SUB_Y, :, :].reshape(rows, LANE), F32), 0.0)
        ff = wga_ref.shape[1]

        items = [(w, f0) for w in ((wga_ref, wua_ref, wda_ref), (wgb_ref, wub_ref, wdb_ref))
                 for f0 in range(0, ff, FF_TILE)]

        def up(item):
            (wg_ref, wu_ref, _), f0 = item
            return _dot(x, wg_ref[:, f0:f0 + FF_TILE]), _dot(x, wu_ref[:, f0:f0 + FF_TILE])

        ups = [up(items[0]), up(items[1])]
        parts = []
        for c, ((_, _, wd_ref), f0) in enumerate(items):
            hg, hu = ups[c]
            hh = (hg * _sigmoid(hg) * hu).astype(BF16)
            if c + 2 < len(items):
                ups.append(up(items[c + 2]))
            parts.append(_dot(hh, wd_ref[f0:f0 + FF_TILE, :]))
        per = len(items) // 2
        ya = functools.reduce(lambda a, b: a + b, parts[:per])
        yb = functools.reduce(lambda a, b: a + b, parts[per:])
        _store_subrows(y_ref, _pack_rows(ya * ws[:, 0:1] + yb * ws[:, 1:2]), 0)


def _experts(xs_sub, ea, eb, nvalid, weights, next_f32):
    wg, wu, wd = weights
    nb = xs_sub.shape[0] * 8 // MOE_BLK
    d, ff = wg.shape[1], wg.shape[2]
    wspec_in = lambda which: pl.BlockSpec((None, d, ff), lambda j, ea, eb, v: ((ea, eb)[which][j], 0, 0))
    wspec_out = lambda which: pl.BlockSpec((None, ff, d), lambda j, ea, eb, v: ((ea, eb)[which][j], 0, 0))
    in_specs = [pl.BlockSpec((MOE_BLK // 8, SUB_X, 8, LANE), lambda j, ea, eb, v: (j, 0, 0, 0)),
                wspec_in(0), wspec_in(0), wspec_out(0), wspec_in(1), wspec_in(1), wspec_out(1)]
    out_specs = [pl.BlockSpec((MOE_BLK // 8, SUB_Y, 8, LANE), lambda j, ea, eb, v: (j, 0, 0, 0))]
    out_shape = [jax.ShapeDtypeStruct((nb * MOE_BLK // 8, SUB_Y, 8, LANE), jnp.uint32)]
    args = [ea, eb, nvalid, xs_sub, wg, wu, wd, wg, wu, wd]
    cast_steps = 0
    if next_f32 is not None:
        stacked, layer_rows, layer = next_f32
        cast_steps = 1 << (nb.bit_length() - 1)
        cast_rows = layer_rows // cast_steps
        assert cast_rows * cast_steps == layer_rows and cast_rows % 16 == 0
        step = lambda j: jnp.minimum(j, cast_steps - 1)
        for a in stacked:
            in_specs.append(pl.BlockSpec((cast_rows, a.shape[1]),
                                         lambda j, ea, eb, v: (layer * cast_steps + step(j), 0)))
            out_specs.append(pl.BlockSpec((cast_rows, a.shape[1]), lambda j, ea, eb, v: (step(j), 0)))
            out_shape.append(jax.ShapeDtypeStruct((layer_rows, a.shape[1]), BF16))
        args += list(stacked)
    grid_spec = pltpu.PrefetchScalarGridSpec(num_scalar_prefetch=3, grid=(nb,), in_specs=in_specs,
                                             out_specs=out_specs)
    res = pl.pallas_call(
        functools.partial(_expert_kernel, cast_steps=cast_steps),
        grid_spec=grid_spec,
        out_shape=out_shape,
        compiler_params=_cparams(("arbitrary",)),
        name="experts",
    )(*args)
    return res[0], tuple(res[1:])


def _subrow_index(dest, nsub):
    t = dest.shape[0]
    base = ((dest // 8) * (nsub * 8) + dest % 8).astype(F32).reshape(t // LANE, LANE)
    src = np.arange(LANE)
    grp, r = src // 8, src % 8
    sel = np.zeros((LANE, LANE * nsub), np.float32)
    off = np.zeros((LANE * nsub,), np.int32)
    for j in range(nsub):
        pos = grp * (nsub * 8) + j * 8 + r
        sel[src, pos] = 1.0
        off[pos] = j * 8
    idx = jnp.dot(base, jnp.asarray(sel), precision=lax.Precision.HIGHEST).astype(jnp.int32) + jnp.asarray(off)
    return idx.reshape(t * nsub)


def _moe(h2_sub, logits, bias_col, weights, next_f32):
    t = logits.shape[0]
    meta, cnt, h2_sub = _route(logits, bias_col, h2_sub)
    cls = meta[0].astype(jnp.int32)
    rank = meta[1].astype(jnp.int32)
    counts = cnt[:N_CLASSES, 0].astype(jnp.int32)
    padded = (counts + MOE_BLK - 1) // MOE_BLK * MOE_BLK
    pad_end = jnp.cumsum(padded)
    pad_start = pad_end - padded
    class_ids = jnp.arange(N_CLASSES, dtype=jnp.int32)
    dest = rank + jnp.sum(jnp.where(cls[:, None] == class_ids[None, :], pad_start[None, :], 0), axis=1)
    nb = t // MOE_BLK + N_CLASSES
    p_rows = nb * MOE_BLK
    blk_start = jnp.arange(nb, dtype=jnp.int32) * MOE_BLK
    valid = blk_start < pad_end[-1]
    blk_cls = jnp.sum((pad_end[None, :] <= blk_start[:, None]).astype(jnp.int32), axis=-1)
    last_cls = jnp.sum((pad_end <= pad_end[-1] - 1).astype(jnp.int32))
    blk_cls = jnp.minimum(jnp.where(valid, blk_cls, last_cls), N_CLASSES - 1)
    nvalid = jnp.where(valid, jnp.clip(pad_start[blk_cls] + counts[blk_cls] - blk_start, 0, MOE_BLK), 0)
    pair_a = jnp.array([p[0] for p in PAIRS], jnp.int32)
    pair_b = jnp.array([p[1] for p in PAIRS], jnp.int32)
    grp = blk_cls // len(PAIRS)
    ea = grp * EXPERTS_PER_GROUP + pair_a[blk_cls % len(PAIRS)]
    eb = grp * EXPERTS_PER_GROUP + pair_b[blk_cls % len(PAIRS)]

    xs = _sc_scatter(h2_sub.reshape(t * SUB_X, LANE), _subrow_index(dest, SUB_X), p_rows * SUB_X)
    ys, next_bf16 = _experts(xs.reshape(p_rows // 8, SUB_X, 8, LANE), ea, eb, nvalid.astype(jnp.int32), weights,
                             next_f32)
    y = _sc_gather(ys.reshape(p_rows * SUB_Y, LANE), _subrow_index(dest, SUB_Y))
    return y.reshape(t // 8, SUB_Y, 8, LANE), next_bf16


def _final_kernel(x_ref, y_ref, g5_ref, gf_ref, o_ref):
    x = x_ref[...] + g5_ref[...] * _unpack_rows(_load_subrows(y_ref, 0, SUB_Y))
    ms = jnp.mean(x * x, axis=-1, keepdims=True)
    o_ref[...] = x * lax.rsqrt(ms + EPS) * gf_ref[...]


def _final(x_mid, y, mod5, layer, g_final, *, tiles_per_seq):
    t, d = x_mid.shape
    tok = pl.BlockSpec((TOK_TILE, d), lambda i: (i, 0))
    return pl.pallas_call(
        _final_kernel,
        grid=(t // TOK_TILE,),
        in_specs=[tok, pl.BlockSpec((TOK_TILE // 8, SUB_Y, 8, LANE), lambda i: (i, 0, 0, 0)),
                  pl.BlockSpec((None, None, None, 1, d), lambda i: (layer, i // tiles_per_seq, 5, 0, 0)),
                  pl.BlockSpec(g_final.shape, lambda i: (0, 0))],
        out_specs=tok,
        out_shape=jax.ShapeDtypeStruct((t, d), F32),
        compiler_params=_cparams(("parallel",)),
        name="final_norm",
    )(x_mid, y, mod5, g_final)


def _rope_tables(seq):
    t = jnp.arange(seq)
    row = (t // GRID_W).astype(F32)
    col = (t % GRID_W).astype(F32)
    half = GLA_DK // 2
    inv = ROPE_BASE ** (-jnp.arange(0, half, 2, dtype=F32) / half)
    ang = jnp.concatenate([row[:, None] * inv, col[:, None] * inv], axis=-1)
    cos = jnp.repeat(jnp.cos(ang), 2, axis=-1)
    sin = (jnp.sin(ang)[:, :, None] * jnp.array([-1.0, 1.0], F32)).reshape(seq, GLA_DK)
    cos = jnp.tile(cos, (1, GLA_HEADS))
    sin = jnp.tile(sin, (1, GLA_HEADS))
    padw = QK_PAD - GLA_QK
    cos = jnp.pad(cos, ((0, 0), (0, padw)), constant_values=1.0)
    sin = jnp.pad(sin, ((0, 0), (0, padw)))
    cos = jnp.concatenate([cos, jnp.ones((TOK_TILE, QK_PAD), F32)], axis=0)
    sin = jnp.concatenate([sin, jnp.zeros((TOK_TILE, QK_PAD), F32)], axis=0)
    return cos, sin


def _pad_last(w, n):
    return jnp.pad(w, [(0, 0)] * (w.ndim - 1) + [(0, n - w.shape[-1])])


def _in_weights(w_in):
    offs = np.cumsum([0, GLA_QK, GLA_QK, GLA_WIDTH, GLA_WIDTH, 2 * GLA_LOWRANK, NA_WIDTH, NA_WIDTH, NA_WIDTH,
                      2 * CONV_WIDTH])
    seg = [w_in[..., offs[i]:offs[i + 1]] for i in range(9)]
    cols = [_pad_last(seg[0], QK_PAD), _pad_last(seg[1], QK_PAD), _pad_last(seg[4], LANE), seg[2], seg[3], seg[5],
            seg[6], seg[7], seg[8]]
    w = jnp.concatenate(cols, axis=-1).astype(BF16)
    assert w.shape[-1] == IN_COLS_PAD
    return w


def _gate_weights(wa_f, ba_f, wa_b, ba_b):
    depth = wa_f.shape[0]
    zero = jnp.zeros((depth, GLA_LOWRANK, QK_PAD), F32)
    top = jnp.concatenate([_pad_last(wa_f, QK_PAD), zero], axis=-1)
    bot = jnp.concatenate([zero, _pad_last(wa_b, QK_PAD)], axis=-1)
    rest = jnp.zeros((depth, LANE - 2 * GLA_LOWRANK, 2 * QK_PAD), F32)
    wa = jnp.concatenate([top, bot, rest], axis=1)
    ba = jnp.concatenate([_pad_last(ba_f, QK_PAD), _pad_last(ba_b, QK_PAD)], axis=-1)[:, None, :]
    return wa, ba


def _na_bias_tables(rpb):
    cq = np.arange(GRID_W)
    c0 = np.clip(cq - NA_KW // 2, 0, GRID_W - NA_KW)
    kc = np.arange(GRID_W)
    valid = (kc[None, :] >= c0[:, None]) & (kc[None, :] < c0[:, None] + NA_KW)
    w = GRID_W
    padded = jnp.pad(rpb, [(0, 0)] * 3 + [(w - NA_KW, w - NA_KW)], constant_values=NEG_BIG)
    flat = jnp.tile(padded, (1, 1, 1, w))
    band = flat[..., w - 1:w - 1 + w * (2 * w - 2)].reshape(rpb.shape[:3] + (w, 2 * w - 2))[..., :w]
    band = jnp.where(valid, band, NEG_BIG)
    tab = jnp.stack([band[:, :, NA_KH - 1 - v:2 * NA_KH - 1 - v] for v in range(NA_KH)], axis=2)
    tab = jnp.transpose(tab, (0, 1, 2, 4, 3, 5))
    return tab.reshape(rpb.shape[:2] + (NA_KH, GRID_W, NA_KH * GRID_W)).astype(F32)


def _head_mean_matrix():
    h = np.arange(GLA_WIDTH) // GLA_DV
    e = (h[:, None] == h[None, :]).astype(np.float32)
    return jnp.asarray(np.concatenate([e, e], axis=0), dtype=BF16)


def kernel(x, c, ctx, c_ctx, w_mod, b_mod, g_mix, g_ffn, w_in, gla_wa_f, gla_ba_f, gla_wa_b, gla_ba_b, gla_g_norm,
           na_rpb, conv_w, conv_b, conv_ln_g, conv_ln_b, w_out, router_w, router_bias, w_gate, w_up, w_down,
           g_final):
    bsz, seq, d = x.shape
    ctx_len = ctx.shape[1]
    depth = w_mod.shape[0]
    t_lat, t_ctx = bsz * seq, bsz * ctx_len
    assert seq % TOK_TILE == 0 and t_ctx % TOK_TILE == 0 and seq % ctx_len == 0
    assert d == 2 * SUB_Y * LANE
    assert seq // GRID_W >= NA_KH and (seq // GRID_W) % NA_UNROLL == 0
    assert ctx_len % (GLA_CHUNK * GLA_STEPS) == 0 and seq % (GLA_CHUNK * GLA_STEPS) == 0
    tiles_per_seq = seq // TOK_TILE
    n_lat_tiles = t_lat // TOK_TILE
    n_all_tiles = (t_lat + t_ctx) // TOK_TILE
    ctx_blk0 = t_lat // ctx_len

    mod_rows = -(-(bsz + 1) // 8) * 8
    c_pad = jnp.zeros((mod_rows, d), F32).at[:bsz].set(c).at[bsz].set(c_ctx)
    mod = _modulation(c_pad, w_mod, b_mod)
    mod5 = mod.reshape(depth, mod_rows, 6, 1, d)

    cos_t, sin_t = _rope_tables(seq)
    e_mat = _head_mean_matrix()
    rw = jnp.pad(router_w, ((0, 0), (0, LANE - N_EXPERTS)))
    rw_hi = rw.astype(BF16)
    rw_lo = (rw - rw_hi.astype(F32)).astype(BF16)
    rw_split = jnp.concatenate([rw_hi, rw_lo], axis=1)
    bias_col = router_bias.reshape(N_EXPERTS, 1).astype(F32)
    zero_state = jnp.zeros((bsz, GLA_WIDTH, QK_PAD), F32)

    w_aug = _in_weights(w_in)
    wa_aug, ba_aug = _gate_weights(gla_wa_f, gla_ba_f, gla_wa_b, gla_ba_b)
    bias_tab = _na_bias_tables(na_rpb)
    row = lambda a: a[:, None, :]
    gmix, gffn, gnorm = row(g_mix), row(g_ffn), row(gla_g_norm)
    cb, clg, clb = row(conv_b), row(conv_ln_g), row(conv_ln_b)
    w_out_b = w_out.astype(BF16)
    expert_f32 = (w_gate, w_up, w_down)
    weights = tuple(w[0].astype(BF16) for w in expert_f32)
    layer_rows = w_gate.shape[1] * w_gate.shape[2]
    expert_rows = tuple(w.reshape(depth * layer_rows, w.shape[-1]) for w in expert_f32)

    x_all = (x.reshape(t_lat, d), ctx.reshape(t_ctx, d))
    y_moe = None
    for l in range(depth):
        last = l == depth - 1
        x_all, (q, k, v, sgate, gf, gb, nq, nk, nv, hcv) = _inproj(
            x_all, y_moe, mod5, l, gmix, cos_t, sin_t, w_aug, wa_aug, ba_aug,
            n_lat_tiles=n_lat_tiles, tiles_per_seq=tiles_per_seq, n_batch=bsz, res_layer=l - 1)

        o_g, st_f, st_b = _gla(q, k, v, gf, gb, zero_state, zero_state, None, seq=ctx_len, blk0=ctx_blk0,
                               n_batch=bsz)
        o_g, _, _ = _gla(q, k, v, gf, gb, st_f, st_b, o_g, seq=seq, blk0=0, n_batch=bsz)

        o_n = _na_latent(nq, nk, nv, bias_tab, l, seq=seq, ctx_len=ctx_len, n_batch=bsz, ctx_blk0=ctx_blk0)
        o_c = _conv(hcv, conv_w, cb, clg, clb, l, None, seq=seq, blk0=0, n_batch=bsz)
        if not last:
            o_n = _na_context(nq, nk, nv, o_n, ctx_len=ctx_len, n_batch=bsz, ctx_blk0=ctx_blk0)
            o_c = _conv(hcv, conv_w, cb, clg, clb, l, o_c, seq=ctx_len, blk0=ctx_blk0, n_batch=bsz)

        n_tiles = n_lat_tiles if last else n_all_tiles
        x_mid, h2, logits = _outproj(
            o_g[0], o_g[1], sgate, o_n, o_c, x_all, mod5, l, gnorm, e_mat, w_out_b, gffn, rw_split,
            n_tiles=n_tiles, n_lat_tiles=n_lat_tiles, tiles_per_seq=tiles_per_seq, n_batch=bsz)
        next_f32 = None if last else (expert_rows, layer_rows, l + 1)
        y_moe, next_bf16 = _moe(h2, logits, bias_col, weights, next_f32)
        if not last:
            weights = tuple(nb.reshape(w.shape[1:]) for nb, w in zip(next_bf16, expert_f32))
        x_all = x_mid

    out = _final(x_all, y_moe, mod5, depth - 1, g_final.reshape(1, d), tiles_per_seq=tiles_per_seq)
    return out.reshape(bsz, seq, d)
```

```python
import functools

import numpy as np
import jax
import jax.numpy as jnp
from jax import lax
from jax.experimental import pallas as pl
from jax.experimental.pallas import tpu as pltpu
from jax.experimental.pallas import tpu_sc as plsc

GRID_W = 64
EPS = 1e-6
GLA_HEADS, GLA_DK, GLA_DV = 4, 48, 96
GLA_QK = GLA_HEADS * GLA_DK
GLA_WIDTH = GLA_HEADS * GLA_DV
GLA_LOWRANK = 16
GLA_TAU = 16.0
ROPE_BASE = 10000.0
NA_HEADS, NA_DH = 6, 64
NA_WIDTH = NA_HEADS * NA_DH
NA_KH, NA_KW = 8, 16
CONV_WIDTH, CONV_K = 256, 31
N_EXPERTS, N_GROUPS, EXPERTS_PER_GROUP = 16, 4, 4
PAIRS = ((0, 1), (0, 2), (0, 3), (1, 2), (1, 3), (2, 3))
N_CLASSES = N_GROUPS * len(PAIRS)

LANE = 128
SUBLANES = 8
QK_PAD = 256
C_Q, C_K = 0, 256
C_A, C_V, C_GATE = 512, 640, 1024
C_NQ, C_NK, C_NV, C_CONV = 1408, 1792, 2176, 2560
IN_COLS_PAD = 3072
MXU_N = 256

TOK_TILE = 1024
ROW_PARTS = 4
GLA_CHUNK = 64
GLA_SUB = 16
GLA_STEPS = 4
MOE_BLK = 256
ROUTE_TILE = 512
NA_UNROLL = 8
NEG_BIG = -1e30
VMEM_PER_CORE = 64 * 1024 * 1024
VMEM_LIMIT = VMEM_PER_CORE - 8 * 1024 * 1024

F32 = jnp.float32
BF16 = jnp.bfloat16


def _cparams(sem):
    return pltpu.CompilerParams(dimension_semantics=sem, vmem_limit_bytes=VMEM_LIMIT)


def _cast_specs(next_f32, n_steps, step_of):
    stacked, layer_rows, layer = next_f32
    cast_steps = 1 << (n_steps.bit_length() - 1)
    cast_rows = layer_rows // cast_steps
    assert cast_rows * cast_steps == layer_rows and cast_rows % 16 == 0
    step = lambda *g: jnp.minimum(step_of(*g), cast_steps - 1)
    in_specs = [pl.BlockSpec((cast_rows, a.shape[1]), lambda *g: (layer * cast_steps + step(*g), 0)) for a in stacked]
    out_specs = [pl.BlockSpec((cast_rows, a.shape[1]), lambda *g: (step(*g), 0)) for a in stacked]
    out_shape = [jax.ShapeDtypeStruct((layer_rows, a.shape[1]), BF16) for a in stacked]
    return in_specs, out_specs, out_shape


def _cast_slices(srcs, dsts):
    for src, dst in zip(srcs, dsts):
        dst[...] = src[...].astype(BF16)


def _layer_spec(a, layer):
    return pl.BlockSpec((None,) + a.shape[1:], lambda *_: (layer,) + (0,) * (a.ndim - 1),
                        pipeline_mode=pl.Buffered(1))


def _dot(a, b):
    return jnp.dot(a, b, preferred_element_type=F32)


def _dot_nt(a, b):
    return lax.dot_general(a, b, (((1,), (1,)), ((), ())), preferred_element_type=F32)


def _split(a):
    hi = a.astype(BF16)
    lo = (a - hi.astype(F32)).astype(BF16)
    return hi, lo


def _sigmoid(x):
    return 1.0 / (1.0 + jnp.exp(-x))


SUB_X, SUB_Y = 5, 4
SC_WINDOW = 256


def _pack_bf16_pairs(a, b):
    ua = lax.bitcast_convert_type(a.astype(BF16).astype(F32), jnp.uint32)
    ub = lax.bitcast_convert_type(b.astype(BF16).astype(F32), jnp.uint32)
    return jnp.bitwise_or(jnp.right_shift(ua, jnp.uint32(16)), ub)


def _unpack_bf16_pairs(w):
    lo = lax.bitcast_convert_type(jnp.left_shift(w, jnp.uint32(16)), F32)
    hi = lax.bitcast_convert_type(jnp.bitwise_and(w, jnp.uint32(0xFFFF0000)), F32)
    return lo, hi


def _store_subrows(ref, val, j0, row0=0):
    r = val.shape[0]
    for j in range(val.shape[1] // LANE):
        ref[row0 // 8:(row0 + r) // 8, j0 + j, :, :] = val[:, j * LANE:(j + 1) * LANE].reshape(r // 8, 8, LANE)


def _load_subrows(ref, j0, n, row0=0, rows=None):
    r = ref.shape[0] * 8 if rows is None else rows
    return jnp.concatenate([ref[row0 // 8:(row0 + r) // 8, j0 + j, :, :].reshape(r, LANE) for j in range(n)],
                           axis=1)


def _pack_rows(x):
    half = x.shape[1] // 2
    return _pack_bf16_pairs(x[:, :half], x[:, half:])


def _unpack_rows(w):
    lo, hi = _unpack_bf16_pairs(w)
    return jnp.concatenate([lo, hi], axis=1)


def _sc_mesh():
    return plsc.VectorSubcoreMesh(core_axis_name="c", subcore_axis_name="s")


def _sc_scatter(src, idx, n_out):
    n, w = src.shape
    idx2 = idx.reshape(1, n)

    @pl.kernel(out_type=jax.ShapeDtypeStruct((n_out, w), src.dtype), mesh=_sc_mesh(), scratch_types=[])
    def scatter_kernel(x_hbm, i_hbm, o_hbm):
        def body(x_vmem, i_vmem):
            pltpu.sync_copy(x_vmem, o_hbm.at[i_vmem.at[0]])

        pltpu.emit_pipeline(
            body,
            grid=(n // SC_WINDOW,),
            in_specs=[pl.BlockSpec((SC_WINDOW, w), index_map=lambda i: (i, 0)),
                      pl.BlockSpec((1, SC_WINDOW), index_map=lambda i: (0, i))],
            out_specs=[],
            core_axis_name=("c", "s"),
            dimension_semantics=(pltpu.PARALLEL,),
        )(x_hbm, i_hbm)

    return scatter_kernel(src, idx2)


def _sc_gather(src, idx):
    n = idx.shape[0]
    w = src.shape[1]
    idx2 = idx.reshape(1, n)

    @pl.kernel(out_type=jax.ShapeDtypeStruct((n, w), src.dtype), mesh=_sc_mesh())
    def gather_kernel(x_hbm, i_hbm, o_hbm):
        def body(i_vmem, o_vmem):
            pltpu.sync_copy(x_hbm.at[i_vmem.at[0]], o_vmem)

        pltpu.emit_pipeline(
            body,
            grid=(n // SC_WINDOW,),
            in_specs=[pl.BlockSpec((1, SC_WINDOW), index_map=lambda i: (0, i))],
            out_specs=[pl.BlockSpec((SC_WINDOW, w), index_map=lambda i: (i, 0))],
            core_axis_name=("c", "s"),
            dimension_semantics=(pltpu.PARALLEL,),
        )(i_hbm, o_hbm)

    return gather_kernel(src, idx2)


def _mod_kernel(c_ref, w_ref, b_ref, o_ref):
    cv = c_ref[...]
    s = cv * _sigmoid(cv)
    s_hi, s_lo = _split(s)
    w_hi, w_lo = _split(w_ref[...])
    o_ref[...] = _dot(s_hi, w_hi) + _dot(s_lo, w_hi) + _dot(s_hi, w_lo) + b_ref[...]


def _modulation(c_pad, w_mod, b_mod):
    depth, d, six_d = w_mod.shape
    rows = c_pad.shape[0]
    nt = 1536
    return pl.pallas_call(
        _mod_kernel,
        grid=(depth, six_d // nt),
        in_specs=[
            pl.BlockSpec((rows, d), lambda l, j: (0, 0)),
            pl.BlockSpec((None, d, nt), lambda l, j: (l, 0, j)),
            pl.BlockSpec((None, 1, nt), lambda l, j: (l, 0, j)),
        ],
        out_specs=pl.BlockSpec((None, rows, nt), lambda l, j: (l, 0, j)),
        out_shape=jax.ShapeDtypeStruct((depth, rows, six_d), F32),
        compiler_params=_cparams(("arbitrary", "arbitrary")),
        name="modulation",
    )(c_pad, w_mod, b_mod.reshape(depth, 1, six_d))


def _inproj_kernel(*refs, has_res, n_lat_tiles):
    if has_res:
        x_ref, y_ref, g5_ref = refs[:3]
    else:
        x_ref, xctx_ref = refs[:2]
        is_lat = pl.program_id(0) < n_lat_tiles
    refs = refs[3 if has_res else 2:]
    (sh_ref, sc_ref, gmix_ref, cos_ref, sin_ref, w_ref, wa_ref, ba_ref) = refs[:8]
    (xnew_ref, q_ref, k_ref, v_ref, sg_ref, gf_ref, gb_ref, nq_ref, nk_ref, nv_ref, hcv_ref) = refs[8:]

    tile = x_ref.shape[0]
    part = tile // ROW_PARTS
    wa_hi, wa_lo = _split(wa_ref[...])
    even_lane = jnp.bitwise_and(lax.broadcasted_iota(jnp.int32, (1, LANE), 1), 1) == 0
    hbs = []
    for p in range(ROW_PARTS):
        rows = pl.ds(p * part, part)
        if has_res:
            x = x_ref[rows, :] + g5_ref[...] * _unpack_rows(_load_subrows(y_ref, 0, SUB_Y, row0=p * part, rows=part))
        else:
            x = jnp.where(is_lat, x_ref[rows, :], xctx_ref[rows, :])
        xnew_ref[rows, :] = x
        ms = jnp.mean(x * x, axis=-1, keepdims=True)
        h = x * lax.rsqrt(ms + EPS) * gmix_ref[...]
        h = h * (1.0 + sc_ref[...]) + sh_ref[...]
        hbs.append(h.astype(BF16))

    for p, hb in enumerate(hbs):
        rows = pl.ds(p * part, part)

        def proj(c0, n, hb=hb):
            return _dot(hb, w_ref[:, c0:c0 + n])

        cos = cos_ref[rows, :]
        sin = sin_ref[rows, :]

        def rope(t):
            halves = []
            for c0 in range(0, QK_PAD, LANE):
                th = t[:, c0:c0 + LANE]
                halves.append(jnp.where(even_lane, pltpu.roll(th, LANE - 1, 1), pltpu.roll(th, 1, 1)))
            return t * cos + jnp.concatenate(halves, axis=1) * sin

        mid = (C_A + C_CONV) // 2
        assert (mid - C_A) % MXU_N == 0 and C_NQ < mid < C_NK
        d1 = proj(C_A, mid - C_A)
        a_hi, a_lo = _split(d1[:, :LANE])
        q_ref[rows, :] = (rope(proj(C_Q, QK_PAD)) * (GLA_DK ** -0.5)).astype(BF16)
        k_ref[rows, :] = rope(proj(C_K, QK_PAD)).astype(BF16)

        z = _dot(a_hi, wa_hi) + _dot(a_lo, wa_hi) + _dot(a_hi, wa_lo) + ba_ref[...]
        logsig = jnp.minimum(z, 0.0) - jnp.log(1.0 + jnp.exp(-jnp.abs(z)))
        g = logsig * (1.0 / GLA_TAU)
        gf_ref[rows, :] = g[:, :QK_PAD]
        gb_ref[rows, :] = g[:, QK_PAD:]

        v_ref[rows, :] = d1[:, C_V - C_A:C_GATE - C_A].astype(BF16)
        gate = d1[:, C_GATE - C_A:C_NQ - C_A]
        sg_ref[rows, :] = (gate * _sigmoid(gate)).astype(BF16)
        d2 = proj(mid, C_CONV - mid)
        nq = jnp.concatenate([d1[:, C_NQ - C_A:], d2[:, :C_NK - mid]], axis=1)
        nq_ref[rows, :] = (nq * (NA_DH ** -0.5)).astype(BF16)
        nk_ref[rows, :] = d2[:, C_NK - mid:C_NV - mid].astype(BF16)
        nv_ref[rows, :] = d2[:, C_NV - mid:].astype(BF16)
        u = proj(C_CONV, 2 * CONV_WIDTH)
        hcv_ref[rows, :] = u[:, :CONV_WIDTH] * _sigmoid(u[:, CONV_WIDTH:])


def _inproj(x_all, y, mod5, layer, gmix, cos_t, sin_t, w_aug, wa_aug, ba_aug, *, n_lat_tiles, tiles_per_seq,
            n_batch, res_layer):
    has_res = y is not None
    d = x_all.shape[1] if has_res else x_all[0].shape[1]
    t_all = x_all.shape[0] if has_res else x_all[0].shape[0] + x_all[1].shape[0]
    nt = t_all // TOK_TILE

    def bidx(i):
        return jnp.where(i < n_lat_tiles, i // tiles_per_seq, n_batch)

    def ridx(i):
        return jnp.where(i < n_lat_tiles, i % tiles_per_seq, tiles_per_seq)

    def modspec(l, j):
        return pl.BlockSpec((None, None, None, 1, d), lambda i: (l, bidx(i), j, 0, 0))

    tok = lambda w: pl.BlockSpec((TOK_TILE, w), lambda i: (i, 0))
    full = lambda a: pl.BlockSpec(a.shape, lambda i: (0,) * a.ndim)

    if has_res:
        in_specs = [tok(d), pl.BlockSpec((TOK_TILE // 8, SUB_Y, 8, LANE), lambda i: (i, 0, 0, 0)),
                    modspec(res_layer, 5)]
        args = [x_all, y, mod5]
    else:
        in_specs = [pl.BlockSpec((TOK_TILE, d), lambda i: (jnp.minimum(i, n_lat_tiles - 1), 0)),
                    pl.BlockSpec((TOK_TILE, d), lambda i: (jnp.maximum(i - n_lat_tiles, 0), 0))]
        args = list(x_all)
    in_specs += [modspec(layer, 0), modspec(layer, 1), _layer_spec(gmix, layer),
                 pl.BlockSpec((TOK_TILE, QK_PAD), lambda i: (ridx(i), 0)),
                 pl.BlockSpec((TOK_TILE, QK_PAD), lambda i: (ridx(i), 0)),
                 _layer_spec(w_aug, layer), _layer_spec(wa_aug, layer), _layer_spec(ba_aug, layer)]
    args += [mod5, mod5, gmix, cos_t, sin_t, w_aug, wa_aug, ba_aug]

    out_widths = [(d, F32), (QK_PAD, BF16), (QK_PAD, BF16), (GLA_WIDTH, BF16), (GLA_WIDTH, BF16), (QK_PAD, F32),
                  (QK_PAD, F32), (NA_WIDTH, BF16), (NA_WIDTH, BF16), (NA_WIDTH, BF16), (CONV_WIDTH, F32)]
    out_specs = [tok(w) for w, _ in out_widths]
    out_shape = [jax.ShapeDtypeStruct((t_all, w), dt) for w, dt in out_widths]
    res = pl.pallas_call(
        functools.partial(_inproj_kernel, has_res=has_res, n_lat_tiles=n_lat_tiles),
        grid=(nt,),
        in_specs=in_specs,
        out_specs=out_specs,
        out_shape=out_shape,
        compiler_params=_cparams(("parallel",)),
        name="inproj",
    )(*args)
    return res[0], res[1:]


def _gla_masks():
    c, sub = GLA_CHUNK, GLA_SUB
    lane_qk = lax.broadcasted_iota(jnp.int32, (1, QK_PAD), 1)
    head_qk = ((lane_qk >= GLA_DK).astype(jnp.int32) + (lane_qk >= 2 * GLA_DK).astype(jnp.int32)
               + (lane_qk >= 3 * GLA_DK).astype(jnp.int32) + 4 * (lane_qk >= 4 * GLA_DK).astype(jnp.int32))
    row_h = jnp.right_shift(lax.broadcasted_iota(jnp.int32, (c, 1), 0), GLA_SUB.bit_length() - 1)
    hm = (row_h == head_qk).astype(F32)
    row_v = lax.broadcasted_iota(jnp.int32, (GLA_WIDTH, 1), 0)
    head_v = ((row_v >= GLA_DV).astype(jnp.int32) + (row_v >= 2 * GLA_DV).astype(jnp.int32)
              + (row_v >= 3 * GLA_DV).astype(jnp.int32))
    bd = (head_v == head_qk).astype(F32)
    lane_v = lax.broadcasted_iota(jnp.int32, (1, GLA_WIDTH), 1)
    vm = [((lane_v >= h * GLA_DV) & (lane_v < (h + 1) * GLA_DV)).astype(F32) for h in range(GLA_HEADS)]
    return hm, bd, vm


def _gla_steps(q, k, v, g, s_t, hm, bd, vm):
    c, sub = GLA_CHUNK, GLA_SUB
    nsub = c // sub
    dirs = (True, False)
    items = [(u, d) for u in range(GLA_STEPS) for d in range(2)]
    ri = lax.broadcasted_iota(jnp.int32, (c, c), 0)
    ci = lax.broadcasted_iota(jnp.int32, (c, c), 1)
    key_row = lax.broadcasted_iota(jnp.int32, (c, 1), 0)
    att_row = jnp.bitwise_and(ri, sub - 1)
    tri = [((ci <= ri) if fwd else (ci >= ri)).astype(BF16) for fwd in dirs]

    cums = {}
    for u, d in items:
        g_hi, g_lo = _split(g[u][d])
        cums[u, d] = _dot(tri[d], g_hi) + _dot(tri[d], g_lo)

    qe, kv, decay, atts = {}, {}, {}, {}
    for u, d in items:
        fwd = dirs[d]
        cum = cums[u, d]
        tot = cum[c - 1:c] if fwd else cum[0:1]
        qe[u, d] = (q[u][d] * jnp.exp(cum)).astype(BF16)
        k_end = (k[u][d] * jnp.exp(tot - cum)).astype(BF16)
        kv[u, d] = lax.dot_general(v[u][d], k_end, (((0,), (0,)), ((), ())), preferred_element_type=F32)
        decay[u, d] = jnp.exp(tot)
        att_d = []
        for i in range(nsub):
            lo, hi = i * sub, (i + 1) * sub
            if fwd:
                ref = cum[lo - 1:lo] if i > 0 else jnp.zeros((1, QK_PAD), F32)
                key_ok = key_row < hi
                causal = ci <= att_row + lo
            else:
                ref = cum[hi:hi + 1] if i < nsub - 1 else jnp.zeros((1, QK_PAD), F32)
                key_ok = key_row >= lo
                causal = ci >= att_row + lo
            qi = q[u][d][lo:hi] * jnp.exp(cum[lo:hi] - ref)
            qs = (jnp.concatenate([qi] * GLA_HEADS, axis=0) * hm).astype(BF16)
            ki = (k[u][d] * jnp.exp(jnp.where(key_ok, ref - cum, NEG_BIG))).astype(BF16)
            att = _dot_nt(qs, ki)
            att_d.append(jnp.where(causal, att, 0.0).astype(BF16))
        atts[u, d] = jnp.concatenate(att_d, axis=0)

    s = list(s_t)
    o_inter = {}
    for u, d in items:
        o_inter[u, d] = _dot_nt(qe[u, d], s[d].astype(BF16))
        s[d] = s[d] * decay[u, d] + bd * kv[u, d]

    outs = [[None, None] for _ in range(GLA_STEPS)]
    for u, d in items:
        r = _dot(atts[u, d], v[u][d])
        blocks = []
        for i in range(nsub):
            base = i * c
            oi = r[base:base + sub] * vm[0]
            for h in range(1, GLA_HEADS):
                oi = oi + r[base + h * sub:base + (h + 1) * sub] * vm[h]
            blocks.append(oi)
        outs[u][d] = o_inter[u, d] + jnp.concatenate(blocks, axis=0)
    return outs, s


def _gla_kernel(q_ref, k_ref, v_ref, gf_ref, gb_ref, sf0_ref, sb0_ref, of_ref, ob_ref, sf_ref, sb_ref):
    n = q_ref.shape[0]
    nc = n // GLA_CHUNK
    hm, bd, vm = _gla_masks()
    sf_ref[...] = sf0_ref[...]
    sb_ref[...] = sb0_ref[...]

    def body(j, carry):
        rows = [[pl.ds(pl.multiple_of(cidx * GLA_CHUNK, GLA_CHUNK), GLA_CHUNK)
                 for cidx in (j * GLA_STEPS + u, nc - 1 - (j * GLA_STEPS + u))] for u in range(GLA_STEPS)]
        q = [[q_ref[r, :].astype(F32) for r in ru] for ru in rows]
        k = [[k_ref[r, :].astype(F32) for r in ru] for ru in rows]
        v = [[v_ref[r, :] for r in ru] for ru in rows]
        g = [[gf_ref[ru[0], :], gb_ref[ru[1], :]] for ru in rows]
        outs, s_new = _gla_steps(q, k, v, g, [sf_ref[...], sb_ref[...]], hm, bd, vm)
        sf_ref[...] = s_new[0]
        sb_ref[...] = s_new[1]
        for u in range(GLA_STEPS):
            of_ref[rows[u][0], :] = outs[u][0]
            ob_ref[rows[u][1], :] = outs[u][1]
        return carry

    lax.fori_loop(0, nc // GLA_STEPS, body, 0)


def _gla(q, k, v, gf, gb, sf0, sb0, o_prev, *, seq, blk0, n_batch):
    t_all = q.shape[0]
    tokw = lambda w: pl.BlockSpec((seq, w), lambda b: (blk0 + b, 0))
    st = pl.BlockSpec((None, GLA_WIDTH, QK_PAD), lambda b: (b, 0, 0))
    in_specs = [tokw(QK_PAD), tokw(QK_PAD), tokw(GLA_WIDTH), tokw(QK_PAD), tokw(QK_PAD), st, st]
    args = [q, k, v, gf, gb, sf0, sb0]
    aliases = {}
    n_in = len(args)
    if o_prev is not None:
        in_specs += [pl.BlockSpec(memory_space=pl.ANY)] * 2
        args += list(o_prev)
        aliases = {n_in: 0, n_in + 1: 1}

    def kern(*refs):
        _gla_kernel(*refs[:n_in], *refs[len(args):])

    st_shape = jax.ShapeDtypeStruct((n_batch, GLA_WIDTH, QK_PAD), F32)
    o_shape = jax.ShapeDtypeStruct((t_all, GLA_WIDTH), F32)
    o_f, o_b, s_f, s_b = pl.pallas_call(
        kern,
        grid=(n_batch,),
        in_specs=in_specs,
        out_specs=[tokw(GLA_WIDTH), tokw(GLA_WIDTH), st, st],
        out_shape=[o_shape, o_shape, st_shape, st_shape],
        input_output_aliases=aliases,
        compiler_params=_cparams(("parallel",)),
        name="gla",
    )(*args)
    return (o_f, o_b), s_f, s_b


def _na_kernel(q_ref, k_ref, v_ref, kc_ref, vc_ref, bias_ref, *rest, n_cast):
    cast_in, (o_ref, *cast_out) = rest[:n_cast], rest[n_cast:]
    _cast_slices(cast_in, cast_out)
    n = q_ref.shape[0]
    rows = n // GRID_W
    nkeys = NA_KH * GRID_W
    lane = lax.broadcasted_iota(jnp.int32, (1, LANE), 1)
    first = lane < NA_DH
    kc = kc_ref[...]
    vc = vc_ref[...]

    def body(jb, carry):
        items = []
        for j in range(NA_UNROLL):
            r = jb * NA_UNROLL + j
            r0 = jnp.clip(r - NA_KH // 2, 0, rows - NA_KH)
            var = r - r0
            qrows = pl.ds(pl.multiple_of(r * GRID_W, GRID_W), GRID_W)
            krows = pl.ds(pl.multiple_of(r0 * GRID_W, GRID_W), nkeys)
            qr = q_ref[qrows, :]
            kb = k_ref[krows, :]
            q2 = jnp.concatenate([jnp.where(first, qr, jnp.zeros_like(qr)),
                                  jnp.where(first, jnp.zeros_like(qr), qr)], axis=0)
            items.append((qrows, krows, var, _dot_nt(q2, kb), _dot_nt(q2, kc)))
        probs = []
        for qrows, krows, var, s_loc, s_ctx in items:
            s_loc = s_loc + jnp.concatenate([bias_ref[0, var], bias_ref[1, var]], axis=0)
            m = jnp.maximum(jnp.max(s_loc, axis=-1, keepdims=True), jnp.max(s_ctx, axis=-1, keepdims=True))
            p_loc = jnp.exp(s_loc - m)
            p_ctx = jnp.exp(s_ctx - m)
            l = jnp.sum(p_loc, axis=-1, keepdims=True) + jnp.sum(p_ctx, axis=-1, keepdims=True)
            probs.append((p_loc.astype(BF16), p_ctx.astype(BF16), l))
        for (qrows, krows, var, _, _), (p_loc, p_ctx, l) in zip(items, probs):
            o = (_dot(p_loc, v_ref[krows, :]) + _dot(p_ctx, vc)) / l
            o_ref[qrows, :] = jnp.where(first, o[:GRID_W], o[GRID_W:]).astype(o_ref.dtype)
        return carry

    lax.fori_loop(0, rows // NA_UNROLL, body, 0)


def _na_latent(nq, nk, nv, bias_tab, layer, cast_f32=None, *, seq, ctx_len, n_batch, ctx_blk0):
    t_all = nq.shape[0]
    npair = NA_HEADS // 2
    lat = pl.BlockSpec((seq, LANE), lambda b, p: (b, p))
    ctx = pl.BlockSpec((ctx_len, LANE), lambda b, p: (ctx_blk0 + b, p))
    bias = pl.BlockSpec((None, 2) + bias_tab.shape[2:], lambda b, p: (layer, p, 0, 0, 0))
    in_specs = [lat, lat, lat, ctx, ctx, bias]
    out_specs = [lat]
    out_shape = [jax.ShapeDtypeStruct((t_all, NA_WIDTH), BF16)]
    args = [nq, nk, nv, nk, nv, bias_tab]
    if cast_f32 is not None:
        c_in, c_out, c_shape = _cast_specs(cast_f32, n_batch * npair, lambda b, p: b * npair + p)
        in_specs += c_in
        out_specs += c_out
        out_shape += c_shape
        args += list(cast_f32[0])
    res = pl.pallas_call(
        functools.partial(_na_kernel, n_cast=0 if cast_f32 is None else len(cast_f32[0])),
        grid=(n_batch, npair),
        in_specs=in_specs,
        out_specs=out_specs,
        out_shape=out_shape,
        compiler_params=_cparams(("arbitrary", "arbitrary")),
        name="na_latent",
    )(*args)
    return res[0], tuple(res[1:])


def _na_ctx_kernel(q_ref, k_ref, v_ref, o_in_ref, o_ref):
    del o_in_ref
    lane = lax.broadcasted_iota(jnp.int32, (1, LANE), 1)
    first = lane < NA_DH
    q = q_ref[...]
    k = k_ref[...]
    v = v_ref[...]
    res = []
    for h in range(2):
        sel = first if h == 0 else jnp.logical_not(first)
        qh = jnp.where(sel, q, jnp.zeros_like(q))
        s = _dot_nt(qh, k)
        m = jnp.max(s, axis=-1, keepdims=True)
        p = jnp.exp(s - m)
        l = jnp.sum(p, axis=-1, keepdims=True)
        res.append(_dot(p.astype(BF16), v) / l)
    o_ref[...] = jnp.where(first, res[0], res[1]).astype(o_ref.dtype)


def _na_context(nq, nk, nv, o_prev, *, ctx_len, n_batch, ctx_blk0):
    npair = NA_HEADS // 2
    ctx = pl.BlockSpec((ctx_len, LANE), lambda b, p: (ctx_blk0 + b, p))
    return pl.pallas_call(
        _na_ctx_kernel,
        grid=(n_batch, npair),
        in_specs=[ctx, ctx, ctx, pl.BlockSpec(memory_space=pl.ANY)],
        out_specs=ctx,
        out_shape=jax.ShapeDtypeStruct(o_prev.shape, o_prev.dtype),
        input_output_aliases={3: 0},
        compiler_params=_cparams(("parallel", "arbitrary")),
        name="na_context",
    )(nq, nk, nv, o_prev)


CONV_HALO = 16
CONV_ROWS = 128


def _conv_kernel(*refs, has_prev):
    if has_prev:
        h_ref, w_ref, b_ref, lg_ref, lb_ref, _, o_ref, pad_ref, sh_ref = refs
    else:
        h_ref, w_ref, b_ref, lg_ref, lb_ref, o_ref, pad_ref, sh_ref = refs
    n = h_ref.shape[0]
    zeros = jnp.zeros((CONV_HALO, CONV_WIDTH), F32)
    pad_ref[0:CONV_HALO, :] = zeros
    pad_ref[CONV_HALO + n:CONV_HALO + n + CONV_HALO, :] = zeros
    pad_ref[CONV_HALO:CONV_HALO + n, :] = h_ref[...]
    span = n + 2 * CONV_HALO - SUBLANES
    for s in range(1, SUBLANES):
        sh_ref[s - 1, 0:span, :] = pad_ref[s:s + span, :]
    w = w_ref[...]
    off = CONV_HALO - CONV_K // 2

    def chunk(cidx, carry):
        base = pl.multiple_of(cidx * CONV_ROWS, CONV_ROWS)
        acc = jnp.zeros((CONV_ROWS, CONV_WIDTH), F32) + b_ref[...]
        for j in range(CONV_K):
            s = (off + j) % SUBLANES
            win = pl.ds(base + (off + j - s), CONV_ROWS)
            tap = pad_ref[win, :] if s == 0 else sh_ref[s - 1, win, :]
            acc = acc + tap * w[j:j + 1, :]
        mu = jnp.mean(acc, axis=-1, keepdims=True)
        xc = acc - mu
        var = jnp.mean(xc * xc, axis=-1, keepdims=True)
        y = xc * lax.rsqrt(var + EPS) * lg_ref[...] + lb_ref[...]
        o_ref[pl.ds(base, CONV_ROWS), :] = (y * _sigmoid(y)).astype(o_ref.dtype)
        return carry

    lax.fori_loop(0, n // CONV_ROWS, chunk, 0)


def _conv(hcv, w, b, lg, lb, layer, o_prev, *, seq, blk0, n_batch):
    t_all = hcv.shape[0]
    tok = pl.BlockSpec((seq, CONV_WIDTH), lambda i: (blk0 + i, 0))
    full = lambda a: pl.BlockSpec(a.shape, lambda i: (0,) * a.ndim)
    in_specs = [tok] + [_layer_spec(a, layer) for a in (w, b, lg, lb)]
    args = [hcv, w, b, lg, lb]
    aliases = {}
    if o_prev is not None:
        in_specs.append(pl.BlockSpec(memory_space=pl.ANY))
        args.append(o_prev)
        aliases = {5: 0}
    return pl.pallas_call(
        functools.partial(_conv_kernel, has_prev=o_prev is not None),
        grid=(n_batch,),
        in_specs=in_specs,
        out_specs=tok,
        out_shape=jax.ShapeDtypeStruct((t_all, CONV_WIDTH), BF16),
        scratch_shapes=[pltpu.VMEM((seq + 2 * CONV_HALO, CONV_WIDTH), F32),
                        pltpu.VMEM((SUBLANES - 1, seq + 2 * CONV_HALO, CONV_WIDTH), F32)],
        input_output_aliases=aliases,
        compiler_params=_cparams(("parallel",)),
        name="conv",
    )(*args)


def _outproj_kernel(ogf_ref, ogb_ref, sg_ref, on_ref, oc_ref, x_ref, g2_ref, sh_ref, sc_ref, gn_ref, e_ref,
                    wo_ref, gffn_ref, rw_ref, xmid_ref, h2_ref, lg_ref):
    tile = x_ref.shape[0]
    parts = [pl.ds(p * (tile // ROW_PARTS), tile // ROW_PARTS) for p in range(ROW_PARTS)]
    e = e_ref[...]
    rw = rw_ref[...]

    ofs, mss = [], []
    for rows in parts:
        of = ogf_ref[rows, :] + ogb_ref[rows, :]
        sq_hi, sq_lo = _split(of * of)
        ofs.append(of)
        mss.append(_dot(jnp.concatenate([sq_hi, sq_lo], axis=1), e) * (1.0 / GLA_DV))
    ys = []
    for rows, of, ms in zip(parts, ofs, mss):
        og = of * lax.rsqrt(ms + EPS) * gn_ref[...] * sg_ref[rows, :].astype(F32)
        mix = jnp.concatenate([og.astype(BF16), on_ref[rows, :], oc_ref[rows, :]], axis=1)
        ys.append(_dot(mix, wo_ref[...]))
    for p, (rows, y) in enumerate(zip(parts, ys)):
        x = x_ref[rows, :] + g2_ref[...] * y
        xmid_ref[rows, :] = x
        ms2 = jnp.mean(x * x, axis=-1, keepdims=True)
        h2 = x * lax.rsqrt(ms2 + EPS) * gffn_ref[...]
        h2 = h2 * (1.0 + sc_ref[...]) + sh_ref[...]
        h_hi, h_lo = _split(h2)
        _store_subrows(h2_ref, _pack_rows(h2), 0, row0=p * (tile // ROW_PARTS))
        hw = _dot(h_hi, rw)
        lg_ref[rows, :] = hw[:, :LANE] + hw[:, LANE:] + _dot(h_lo, rw[:, :LANE])


def _outproj(o_gf, o_gb, sgate, o_n, o_c, x_all, mod5, layer, gnorm, e_mat, w_out, gffn, rw_split, *, n_tiles,
             n_lat_tiles, tiles_per_seq, n_batch):
    d = x_all.shape[1]
    t_out = n_tiles * TOK_TILE

    def bidx(i):
        return jnp.where(i < n_lat_tiles, i // tiles_per_seq, n_batch)

    def modspec(j):
        return pl.BlockSpec((None, None, None, 1, d), lambda i: (layer, bidx(i), j, 0, 0))

    tok = lambda w: pl.BlockSpec((TOK_TILE, w), lambda i: (i, 0))
    full = lambda a: pl.BlockSpec(a.shape, lambda i: (0,) * a.ndim)
    return pl.pallas_call(
        _outproj_kernel,
        grid=(n_tiles,),
        in_specs=[tok(GLA_WIDTH), tok(GLA_WIDTH), tok(GLA_WIDTH), tok(NA_WIDTH), tok(CONV_WIDTH), tok(d),
                  modspec(2), modspec(3), modspec(4), _layer_spec(gnorm, layer), full(e_mat),
                  _layer_spec(w_out, layer), _layer_spec(gffn, layer),
                  full(rw_split)],
        out_specs=[tok(d), pl.BlockSpec((TOK_TILE // 8, SUB_Y, 8, LANE), lambda i: (i, 0, 0, 0)), tok(LANE)],
        out_shape=[jax.ShapeDtypeStruct((t_out, d), F32),
                   jax.ShapeDtypeStruct((t_out // 8, SUB_X, 8, LANE), jnp.uint32),
                   jax.ShapeDtypeStruct((t_out, LANE), F32)],
        compiler_params=_cparams(("parallel",)),
        name="outproj",
    )(o_gf, o_gb, sgate, o_n, o_c, x_all, mod5, mod5, mod5, gnorm, e_mat, w_out, gffn, rw_split)


def _route_kernel(lg_ref, bias_ref, h2_in_ref, meta_ref, cnt_ref, h2w_ref, carry_ref):
    del h2_in_ref
    tile = lg_ref.shape[0]

    @pl.when(pl.program_id(0) == 0)
    def _():
        carry_ref[...] = jnp.zeros_like(carry_ref)

    lt = lg_ref[...].T
    aff = _sigmoid(lt[0:N_EXPERTS])
    sel = aff + bias_ref[...]
    s = [sel[e:e + 1] for e in range(N_EXPERTS)]
    a = [aff[e:e + 1] for e in range(N_EXPERTS)]

    def top2sum(v):
        best = v[0] + v[1]
        for i, j in PAIRS[1:]:
            best = jnp.maximum(best, v[i] + v[j])
        return best

    gs = [top2sum(s[4 * g:4 * g + 4]) for g in range(N_GROUPS)]
    gbest = jnp.zeros_like(gs[0], dtype=jnp.int32)
    gmax = gs[0]
    for g in range(1, N_GROUPS):
        upd = gs[g] > gmax
        gbest = jnp.where(upd, g, gbest)
        gmax = jnp.where(upd, gs[g], gmax)

    def pick(vals, j):
        out = vals[j]
        for g in range(1, N_GROUPS):
            out = jnp.where(gbest == g, vals[4 * g + j], out)
        return out

    sv = [pick(s, j) for j in range(EXPERTS_PER_GROUP)]
    av = [pick(a, j) for j in range(EXPERTS_PER_GROUP)]
    i1 = jnp.zeros_like(gbest)
    m1 = sv[0]
    for j in range(1, EXPERTS_PER_GROUP):
        upd = sv[j] > m1
        i1 = jnp.where(upd, j, i1)
        m1 = jnp.where(upd, sv[j], m1)
    i2 = jnp.full_like(gbest, -1)
    m2 = jnp.zeros_like(m1)
    for j in range(EXPERTS_PER_GROUP):
        upd = (i1 != j) & ((sv[j] > m2) | (i2 < 0))
        i2 = jnp.where(upd, j, i2)
        m2 = jnp.where(upd, sv[j], m2)
    ia = jnp.minimum(i1, i2)
    ib = jnp.maximum(i1, i2)
    pair = jnp.where(ia == 0, ib - 1, jnp.where(ia == 1, ib + 1, 5))
    cls = gbest * len(PAIRS) + pair

    def take(vals, idx):
        out = vals[0]
        for j in range(1, EXPERTS_PER_GROUP):
            out = jnp.where(idx == j, vals[j], out)
        return out

    w1 = take(av, i1)
    w2 = take(av, i2)
    tot = w1 + w2
    wa = jnp.where(i1 < i2, w1, w2) / tot
    wb = jnp.where(i1 < i2, w2, w1) / tot

    crow = lax.broadcasted_iota(jnp.int32, (32, tile), 0)
    oh = (crow == cls).astype(F32)
    us = lax.broadcasted_iota(jnp.int32, (tile, tile), 0)
    ut = lax.broadcasted_iota(jnp.int32, (tile, tile), 1)
    upper = (us < ut).astype(BF16)
    prefix = _dot(oh.astype(BF16), upper)
    carry = carry_ref[...]
    rank = jnp.sum(oh * (prefix + carry), axis=0, keepdims=True)
    carry_new = carry + jnp.sum(oh, axis=1, keepdims=True)
    carry_ref[...] = carry_new
    cnt_ref[...] = jnp.broadcast_to(carry_new, cnt_ref.shape)

    meta_ref[...] = jnp.zeros_like(meta_ref)
    meta_ref[0:1, :] = cls.astype(F32)
    meta_ref[1:2, :] = rank

    wrow = lax.broadcasted_iota(jnp.int32, (LANE, tile), 0)
    wmat = jnp.where(wrow == 0, wa, jnp.where(wrow == 1, wb, 0.0))
    wtok = lax.bitcast_convert_type(wmat.T, jnp.uint32)
    h2w_ref[:, 0, :, :] = wtok.reshape(tile // 8, 8, LANE)


def _route(logits, bias_col, h2_sub):
    t = logits.shape[0]
    nt = t // ROUTE_TILE
    meta, cnt, h2_sub = pl.pallas_call(
        _route_kernel,
        grid=(nt,),
        in_specs=[pl.BlockSpec((ROUTE_TILE, LANE), lambda i: (i, 0)),
                  pl.BlockSpec(bias_col.shape, lambda i: (0, 0)),
                  pl.BlockSpec(memory_space=pl.ANY)],
        out_specs=[pl.BlockSpec((8, ROUTE_TILE), lambda i: (0, i)),
                   pl.BlockSpec((32, LANE), lambda i: (0, 0)),
                   pl.BlockSpec((ROUTE_TILE // 8, 1, 8, LANE), lambda i: (i, SUB_Y, 0, 0))],
        out_shape=[jax.ShapeDtypeStruct((8, t), F32), jax.ShapeDtypeStruct((32, LANE), F32),
                   jax.ShapeDtypeStruct(h2_sub.shape, h2_sub.dtype)],
        scratch_shapes=[pltpu.VMEM((32, 1), F32)],
        input_output_aliases={2: 2},
        compiler_params=_cparams(("arbitrary",)),
        name="route",
    )(logits, bias_col, h2_sub)
    return meta, cnt, h2_sub


FF_TILE = 512


def _expert_kernel(ea_ref, eb_ref, nvalid_ref, xs_ref, wga_ref, wua_ref, wda_ref, wgb_ref, wub_ref, wdb_ref,
                   *rest, n_cast):
    del ea_ref, eb_ref
    j = pl.program_id(0)
    nvalid = nvalid_ref[j]
    nxt_in, (y_ref, *nxt_out) = rest[:n_cast], rest[n_cast:]

    def cast_next():
        _cast_slices(nxt_in, nxt_out)

    @pl.when(nvalid == 0)
    def _():
        cast_next()
        y_ref[...] = jnp.zeros_like(y_ref)

    @pl.when(nvalid != 0)
    def _():
        cast_next()
        rows = y_ref.shape[0] * 8
        live = lax.broadcasted_iota(jnp.int32, (rows, 1), 0) < nvalid
        x = jnp.where(live, _unpack_rows(_load_subrows(xs_ref, 0, SUB_Y)), 0.0).astype(BF16)
        ws = jnp.where(live, lax.bitcast_convert_type(xs_ref[:, SUB_Y, :, :].reshape(rows, LANE), F32), 0.0)
        ff = wga_ref.shape[1]

        items = [(w, f0) for w in ((wga_ref, wua_ref, wda_ref), (wgb_ref, wub_ref, wdb_ref))
                 for f0 in range(0, ff, FF_TILE)]

        def up(item):
            (wg_ref, wu_ref, _), f0 = item
            return _dot(x, wg_ref[:, f0:f0 + FF_TILE]), _dot(x, wu_ref[:, f0:f0 + FF_TILE])

        ups = [up(items[0]), up(items[1])]
        parts = []
        for c, ((_, _, wd_ref), f0) in enumerate(items):
            hg, hu = ups[c]
            hh = (hg * _sigmoid(hg) * hu).astype(BF16)
            if c + 2 < len(items):
                ups.append(up(items[c + 2]))
            parts.append(_dot(hh, wd_ref[f0:f0 + FF_TILE, :]))
        per = len(items) // 2
        ya = functools.reduce(lambda a, b: a + b, parts[:per])
        yb = functools.reduce(lambda a, b: a + b, parts[per:])
        _store_subrows(y_ref, _pack_rows(ya * ws[:, 0:1] + yb * ws[:, 1:2]), 0)


def _experts(xs_sub, ea, eb, nvalid, weights, next_f32):
    wg, wu, wd = weights
    nb = xs_sub.shape[0] * 8 // MOE_BLK
    d, ff = wg.shape[1], wg.shape[2]
    wspec_in = lambda which: pl.BlockSpec((None, d, ff), lambda j, ea, eb, v: ((ea, eb)[which][j], 0, 0))
    wspec_out = lambda which: pl.BlockSpec((None, ff, d), lambda j, ea, eb, v: ((ea, eb)[which][j], 0, 0))
    in_specs = [pl.BlockSpec((MOE_BLK // 8, SUB_X, 8, LANE), lambda j, ea, eb, v: (j, 0, 0, 0)),
                wspec_in(0), wspec_in(0), wspec_out(0), wspec_in(1), wspec_in(1), wspec_out(1)]
    out_specs = [pl.BlockSpec((MOE_BLK // 8, SUB_Y, 8, LANE), lambda j, ea, eb, v: (j, 0, 0, 0))]
    out_shape = [jax.ShapeDtypeStruct((nb * MOE_BLK // 8, SUB_Y, 8, LANE), jnp.uint32)]
    args = [ea, eb, nvalid, xs_sub, wg, wu, wd, wg, wu, wd]
    if next_f32 is not None:
        c_in, c_out, c_shape = _cast_specs(next_f32, nb, lambda j, *_: j)
        in_specs += c_in
        out_specs += c_out
        out_shape += c_shape
        args += list(next_f32[0])
    grid_spec = pltpu.PrefetchScalarGridSpec(num_scalar_prefetch=3, grid=(nb,), in_specs=in_specs,
                                             out_specs=out_specs)
    res = pl.pallas_call(
        functools.partial(_expert_kernel, n_cast=0 if next_f32 is None else len(next_f32[0])),
        grid_spec=grid_spec,
        out_shape=out_shape,
        compiler_params=_cparams(("arbitrary",)),
        name="experts",
    )(*args)
    return res[0], tuple(res[1:])


def _subrow_index(dest, nsub):
    t = dest.shape[0]
    base = ((dest // 8) * (nsub * 8) + dest % 8).astype(F32).reshape(t // LANE, LANE)
    src = np.arange(LANE)
    grp, r = src // 8, src % 8
    sel = np.zeros((LANE, LANE * nsub), np.float32)
    off = np.zeros((LANE * nsub,), np.int32)
    for j in range(nsub):
        pos = grp * (nsub * 8) + j * 8 + r
        sel[src, pos] = 1.0
        off[pos] = j * 8
    idx = jnp.dot(base, jnp.asarray(sel), precision=lax.Precision.HIGHEST).astype(jnp.int32) + jnp.asarray(off)
    return idx.reshape(t * nsub)


def _moe(h2_sub, logits, bias_col, weights, next_f32):
    t = logits.shape[0]
    meta, cnt, h2_sub = _route(logits, bias_col, h2_sub)
    cls = meta[0].astype(jnp.int32)
    rank = meta[1].astype(jnp.int32)
    counts = cnt[:N_CLASSES, 0].astype(jnp.int32)
    padded = (counts + MOE_BLK - 1) // MOE_BLK * MOE_BLK
    pad_end = jnp.cumsum(padded)
    pad_start = pad_end - padded
    class_ids = jnp.arange(N_CLASSES, dtype=jnp.int32)
    dest = rank + jnp.sum(jnp.where(cls[:, None] == class_ids[None, :], pad_start[None, :], 0), axis=1)
    nb = t // MOE_BLK + N_CLASSES
    p_rows = nb * MOE_BLK
    blk_start = jnp.arange(nb, dtype=jnp.int32) * MOE_BLK
    valid = blk_start < pad_end[-1]
    blk_cls = jnp.sum((pad_end[None, :] <= blk_start[:, None]).astype(jnp.int32), axis=-1)
    last_cls = jnp.sum((pad_end <= pad_end[-1] - 1).astype(jnp.int32))
    blk_cls = jnp.minimum(jnp.where(valid, blk_cls, last_cls), N_CLASSES - 1)
    nvalid = jnp.where(valid, jnp.clip(pad_start[blk_cls] + counts[blk_cls] - blk_start, 0, MOE_BLK), 0)
    pair_a = jnp.array([p[0] for p in PAIRS], jnp.int32)
    pair_b = jnp.array([p[1] for p in PAIRS], jnp.int32)
    grp = blk_cls // len(PAIRS)
    ea = grp * EXPERTS_PER_GROUP + pair_a[blk_cls % len(PAIRS)]
    eb = grp * EXPERTS_PER_GROUP + pair_b[blk_cls % len(PAIRS)]

    xs = _sc_scatter(h2_sub.reshape(t * SUB_X, LANE), _subrow_index(dest, SUB_X), p_rows * SUB_X)
    ys, next_bf16 = _experts(xs.reshape(p_rows // 8, SUB_X, 8, LANE), ea, eb, nvalid.astype(jnp.int32), weights,
                             next_f32)
    y = _sc_gather(ys.reshape(p_rows * SUB_Y, LANE), _subrow_index(dest, SUB_Y))
    return y.reshape(t // 8, SUB_Y, 8, LANE), next_bf16


def _final_kernel(x_ref, y_ref, g5_ref, gf_ref, o_ref):
    x = x_ref[...] + g5_ref[...] * _unpack_rows(_load_subrows(y_ref, 0, SUB_Y))
    ms = jnp.mean(x * x, axis=-1, keepdims=True)
    o_ref[...] = x * lax.rsqrt(ms + EPS) * gf_ref[...]


def _final(x_mid, y, mod5, layer, g_final, *, tiles_per_seq):
    t, d = x_mid.shape
    tok = pl.BlockSpec((TOK_TILE, d), lambda i: (i, 0))
    return pl.pallas_call(
        _final_kernel,
        grid=(t // TOK_TILE,),
        in_specs=[tok, pl.BlockSpec((TOK_TILE // 8, SUB_Y, 8, LANE), lambda i: (i, 0, 0, 0)),
                  pl.BlockSpec((None, None, None, 1, d), lambda i: (layer, i // tiles_per_seq, 5, 0, 0)),
                  pl.BlockSpec(g_final.shape, lambda i: (0, 0))],
        out_specs=tok,
        out_shape=jax.ShapeDtypeStruct((t, d), F32),
        compiler_params=_cparams(("parallel",)),
        name="final_norm",
    )(x_mid, y, mod5, g_final)


def _rope_tables(seq):
    t = jnp.arange(seq)
    row = (t // GRID_W).astype(F32)
    col = (t % GRID_W).astype(F32)
    half = GLA_DK // 2
    inv = ROPE_BASE ** (-jnp.arange(0, half, 2, dtype=F32) / half)
    ang = jnp.concatenate([row[:, None] * inv, col[:, None] * inv], axis=-1)
    cos = jnp.repeat(jnp.cos(ang), 2, axis=-1)
    sin = (jnp.sin(ang)[:, :, None] * jnp.array([-1.0, 1.0], F32)).reshape(seq, GLA_DK)
    cos = jnp.tile(cos, (1, GLA_HEADS))
    sin = jnp.tile(sin, (1, GLA_HEADS))
    padw = QK_PAD - GLA_QK
    cos = jnp.pad(cos, ((0, 0), (0, padw)), constant_values=1.0)
    sin = jnp.pad(sin, ((0, 0), (0, padw)))
    cos = jnp.concatenate([cos, jnp.ones((TOK_TILE, QK_PAD), F32)], axis=0)
    sin = jnp.concatenate([sin, jnp.zeros((TOK_TILE, QK_PAD), F32)], axis=0)
    return cos, sin


def _pad_last(w, n):
    return jnp.pad(w, [(0, 0)] * (w.ndim - 1) + [(0, n - w.shape[-1])])


def _in_weights(w_in):
    offs = np.cumsum([0, GLA_QK, GLA_QK, GLA_WIDTH, GLA_WIDTH, 2 * GLA_LOWRANK, NA_WIDTH, NA_WIDTH, NA_WIDTH,
                      2 * CONV_WIDTH])
    seg = [w_in[..., offs[i]:offs[i + 1]] for i in range(9)]
    cols = [_pad_last(seg[0], QK_PAD), _pad_last(seg[1], QK_PAD), _pad_last(seg[4], LANE), seg[2], seg[3], seg[5],
            seg[6], seg[7], seg[8]]
    w = jnp.concatenate(cols, axis=-1).astype(BF16)
    assert w.shape[-1] == IN_COLS_PAD
    return w


def _gate_weights(wa_f, ba_f, wa_b, ba_b):
    depth = wa_f.shape[0]
    zero = jnp.zeros((depth, GLA_LOWRANK, QK_PAD), F32)
    top = jnp.concatenate([_pad_last(wa_f, QK_PAD), zero], axis=-1)
    bot = jnp.concatenate([zero, _pad_last(wa_b, QK_PAD)], axis=-1)
    rest = jnp.zeros((depth, LANE - 2 * GLA_LOWRANK, 2 * QK_PAD), F32)
    wa = jnp.concatenate([top, bot, rest], axis=1)
    ba = jnp.concatenate([_pad_last(ba_f, QK_PAD), _pad_last(ba_b, QK_PAD)], axis=-1)[:, None, :]
    return wa, ba


def _na_bias_tables(rpb):
    cq = np.arange(GRID_W)
    c0 = np.clip(cq - NA_KW // 2, 0, GRID_W - NA_KW)
    kc = np.arange(GRID_W)
    valid = (kc[None, :] >= c0[:, None]) & (kc[None, :] < c0[:, None] + NA_KW)
    w = GRID_W
    padded = jnp.pad(rpb, [(0, 0)] * 3 + [(w - NA_KW, w - NA_KW)], constant_values=NEG_BIG)
    flat = jnp.tile(padded, (1, 1, 1, w))
    band = flat[..., w - 1:w - 1 + w * (2 * w - 2)].reshape(rpb.shape[:3] + (w, 2 * w - 2))[..., :w]
    band = jnp.where(valid, band, NEG_BIG)
    tab = jnp.stack([band[:, :, NA_KH - 1 - v:2 * NA_KH - 1 - v] for v in range(NA_KH)], axis=2)
    tab = jnp.transpose(tab, (0, 1, 2, 4, 3, 5))
    return tab.reshape(rpb.shape[:2] + (NA_KH, GRID_W, NA_KH * GRID_W)).astype(F32)


def _head_mean_matrix():
    h = np.arange(GLA_WIDTH) // GLA_DV
    e = (h[:, None] == h[None, :]).astype(np.float32)
    return jnp.asarray(np.concatenate([e, e], axis=0), dtype=BF16)


def kernel(x, c, ctx, c_ctx, w_mod, b_mod, g_mix, g_ffn, w_in, gla_wa_f, gla_ba_f, gla_wa_b, gla_ba_b, gla_g_norm,
           na_rpb, conv_w, conv_b, conv_ln_g, conv_ln_b, w_out, router_w, router_bias, w_gate, w_up, w_down,
           g_final):
    bsz, seq, d = x.shape
    ctx_len = ctx.shape[1]
    depth = w_mod.shape[0]
    t_lat, t_ctx = bsz * seq, bsz * ctx_len
    assert seq % TOK_TILE == 0 and t_ctx % TOK_TILE == 0 and seq % ctx_len == 0
    assert d == 2 * SUB_Y * LANE
    assert seq // GRID_W >= NA_KH and (seq // GRID_W) % NA_UNROLL == 0
    assert ctx_len % (GLA_CHUNK * GLA_STEPS) == 0 and seq % (GLA_CHUNK * GLA_STEPS) == 0
    tiles_per_seq = seq // TOK_TILE
    n_lat_tiles = t_lat // TOK_TILE
    n_all_tiles = (t_lat + t_ctx) // TOK_TILE
    ctx_blk0 = t_lat // ctx_len

    mod_rows = -(-(bsz + 1) // 8) * 8
    c_pad = jnp.zeros((mod_rows, d), F32).at[:bsz].set(c).at[bsz].set(c_ctx)
    mod = _modulation(c_pad, w_mod, b_mod)
    mod5 = mod.reshape(depth, mod_rows, 6, 1, d)

    cos_t, sin_t = _rope_tables(seq)
    e_mat = _head_mean_matrix()
    rw = jnp.pad(router_w, ((0, 0), (0, LANE - N_EXPERTS)))
    rw_hi = rw.astype(BF16)
    rw_lo = (rw - rw_hi.astype(F32)).astype(BF16)
    rw_split = jnp.concatenate([rw_hi, rw_lo], axis=1)
    bias_col = router_bias.reshape(N_EXPERTS, 1).astype(F32)
    zero_state = jnp.zeros((bsz, GLA_WIDTH, QK_PAD), F32)

    w_aug = _in_weights(w_in)
    wa_aug, ba_aug = _gate_weights(gla_wa_f, gla_ba_f, gla_wa_b, gla_ba_b)
    bias_tab = _na_bias_tables(na_rpb)
    row = lambda a: a[:, None, :]
    gmix, gffn, gnorm = row(g_mix), row(g_ffn), row(gla_g_norm)
    cb, clg, clb = row(conv_b), row(conv_ln_g), row(conv_ln_b)
    w_out_b = w_out.astype(BF16)
    expert_f32 = (w_gate, w_up, w_down)
    layer_rows = w_gate.shape[1] * w_gate.shape[2]
    assert w_down.shape[1] * w_down.shape[2] == layer_rows
    expert_rows = tuple(w.reshape(depth * layer_rows, w.shape[-1]) for w in expert_f32)
    as_experts = lambda flat: tuple(a.reshape(w.shape[1:]) for a, w in zip(flat, expert_f32))

    x_all = (x.reshape(t_lat, d), ctx.reshape(t_ctx, d))
    y_moe = None
    for l in range(depth):
        last = l == depth - 1
        x_all, (q, k, v, sgate, gf, gb, nq, nk, nv, hcv) = _inproj(
            x_all, y_moe, mod5, l, gmix, cos_t, sin_t, w_aug, wa_aug, ba_aug,
            n_lat_tiles=n_lat_tiles, tiles_per_seq=tiles_per_seq, n_batch=bsz, res_layer=l - 1)

        o_g, st_f, st_b = _gla(q, k, v, gf, gb, zero_state, zero_state, None, seq=ctx_len, blk0=ctx_blk0,
                               n_batch=bsz)
        o_g, _, _ = _gla(q, k, v, gf, gb, st_f, st_b, o_g, seq=seq, blk0=0, n_batch=bsz)

        o_n, cast0 = _na_latent(nq, nk, nv, bias_tab, l, (expert_rows, layer_rows, 0) if l == 0 else None,
                                seq=seq, ctx_len=ctx_len, n_batch=bsz, ctx_blk0=ctx_blk0)
        if l == 0:
            weights = as_experts(cast0)
        o_c = _conv(hcv, conv_w, cb, clg, clb, l, None, seq=seq, blk0=0, n_batch=bsz)
        if not last:
            o_n = _na_context(nq, nk, nv, o_n, ctx_len=ctx_len, n_batch=bsz, ctx_blk0=ctx_blk0)
            o_c = _conv(hcv, conv_w, cb, clg, clb, l, o_c, seq=ctx_len, blk0=ctx_blk0, n_batch=bsz)

        n_tiles = n_lat_tiles if last else n_all_tiles
        x_mid, h2, logits = _outproj(
            o_g[0], o_g[1], sgate, o_n, o_c, x_all, mod5, l, gnorm, e_mat, w_out_b, gffn, rw_split,
            n_tiles=n_tiles, n_lat_tiles=n_lat_tiles, tiles_per_seq=tiles_per_seq, n_batch=bsz)
        next_f32 = None if last else (expert_rows, layer_rows, l + 1)
        y_moe, next_bf16 = _moe(h2, logits, bias_col, weights, next_f32)
        if not last:
            weights = as_experts(next_bf16)
        x_all = x_mid

    out = _final(x_all, y_moe, mod5, depth - 1, g_final.reshape(1, d), tiles_per_seq=tiles_per_seq)
    return out.reshape(bsz, seq, d)
```

```python
import functools

import numpy as np
import jax
import jax.numpy as jnp
from jax import lax
from jax.experimental import pallas as pl
from jax.experimental.pallas import tpu as pltpu
from jax.experimental.pallas import tpu_sc as plsc

GRID_W = 64
EPS = 1e-6
GLA_HEADS, GLA_DK, GLA_DV = 4, 48, 96
GLA_QK = GLA_HEADS * GLA_DK
GLA_WIDTH = GLA_HEADS * GLA_DV
GLA_LOWRANK = 16
GLA_TAU = 16.0
ROPE_BASE = 10000.0
NA_HEADS, NA_DH = 6, 64
NA_WIDTH = NA_HEADS * NA_DH
NA_KH, NA_KW = 8, 16
CONV_WIDTH, CONV_K = 256, 31
N_EXPERTS, N_GROUPS, EXPERTS_PER_GROUP = 16, 4, 4
PAIRS = ((0, 1), (0, 2), (0, 3), (1, 2), (1, 3), (2, 3))
N_CLASSES = N_GROUPS * len(PAIRS)

LANE = 128
SUBLANES = 8
QK_PAD = 256
C_Q, C_K = 0, 256
C_A, C_V, C_GATE = 512, 640, 1024
C_NQ, C_NK, C_NV, C_CONV = 1408, 1792, 2176, 2560
IN_COLS_PAD = 3072
MXU_N = 256

TOK_TILE = 1024
ROW_PARTS = 4
GLA_CHUNK = 64
GLA_SUB = 16
GLA_STEPS = 4
MOE_BLK = 256
ROUTE_TILE = 512
NA_UNROLL = 8
NEG_BIG = -1e30
VMEM_PER_CORE = 64 * 1024 * 1024
VMEM_LIMIT = VMEM_PER_CORE - 8 * 1024 * 1024

F32 = jnp.float32
BF16 = jnp.bfloat16


def _cparams(sem):
    return pltpu.CompilerParams(dimension_semantics=sem, vmem_limit_bytes=VMEM_LIMIT)


def _cast_specs(next_f32, n_steps, step_of):
    stacked, layer_rows, layer = next_f32
    cast_steps = 1 << (n_steps.bit_length() - 1)
    cast_rows = layer_rows // cast_steps
    assert cast_rows * cast_steps == layer_rows and cast_rows % 16 == 0
    step = lambda *g: jnp.minimum(step_of(*g), cast_steps - 1)
    in_specs = [pl.BlockSpec((cast_rows, a.shape[1]), lambda *g: (layer * cast_steps + step(*g), 0)) for a in stacked]
    out_specs = [pl.BlockSpec((cast_rows, a.shape[1]), lambda *g: (step(*g), 0)) for a in stacked]
    out_shape = [jax.ShapeDtypeStruct((layer_rows, a.shape[1]), BF16) for a in stacked]
    return in_specs, out_specs, out_shape


def _cast_slices(srcs, dsts):
    for src, dst in zip(srcs, dsts):
        dst[...] = src[...].astype(BF16)


def _layer_spec(a, layer):
    return pl.BlockSpec((None,) + a.shape[1:], lambda *_: (layer,) + (0,) * (a.ndim - 1),
                        pipeline_mode=pl.Buffered(1))


def _dot(a, b):
    return jnp.dot(a, b, preferred_element_type=F32)


def _dot_nt(a, b):
    return lax.dot_general(a, b, (((1,), (1,)), ((), ())), preferred_element_type=F32)


def _split(a):
    hi = a.astype(BF16)
    lo = (a - hi.astype(F32)).astype(BF16)
    return hi, lo


def _sigmoid(x):
    return 1.0 / (1.0 + jnp.exp(-x))


SUB_X, SUB_Y = 5, 4
SC_WINDOW = 256


def _pack_bf16_pairs(a, b):
    ua = lax.bitcast_convert_type(a.astype(BF16).astype(F32), jnp.uint32)
    ub = lax.bitcast_convert_type(b.astype(BF16).astype(F32), jnp.uint32)
    return jnp.bitwise_or(jnp.right_shift(ua, jnp.uint32(16)), ub)


def _unpack_bf16_pairs(w):
    lo = lax.bitcast_convert_type(jnp.left_shift(w, jnp.uint32(16)), F32)
    hi = lax.bitcast_convert_type(jnp.bitwise_and(w, jnp.uint32(0xFFFF0000)), F32)
    return lo, hi


def _store_subrows(ref, val, j0, row0=0):
    r = val.shape[0]
    for j in range(val.shape[1] // LANE):
        ref[row0 // 8:(row0 + r) // 8, j0 + j, :, :] = val[:, j * LANE:(j + 1) * LANE].reshape(r // 8, 8, LANE)


def _load_subrows(ref, j0, n, row0=0, rows=None):
    r = ref.shape[0] * 8 if rows is None else rows
    return jnp.concatenate([ref[row0 // 8:(row0 + r) // 8, j0 + j, :, :].reshape(r, LANE) for j in range(n)],
                           axis=1)


def _pack_rows(x):
    half = x.shape[1] // 2
    return _pack_bf16_pairs(x[:, :half], x[:, half:])


def _unpack_rows(w):
    lo, hi = _unpack_bf16_pairs(w)
    return jnp.concatenate([lo, hi], axis=1)


def _sc_mesh():
    return plsc.VectorSubcoreMesh(core_axis_name="c", subcore_axis_name="s")


def _sc_scatter(src, idx, n_out):
    n, w = src.shape
    idx2 = idx.reshape(1, n)

    @pl.kernel(out_type=jax.ShapeDtypeStruct((n_out, w), src.dtype), mesh=_sc_mesh(), scratch_types=[])
    def scatter_kernel(x_hbm, i_hbm, o_hbm):
        def body(x_vmem, i_vmem):
            pltpu.sync_copy(x_vmem, o_hbm.at[i_vmem.at[0]])

        pltpu.emit_pipeline(
            body,
            grid=(n // SC_WINDOW,),
            in_specs=[pl.BlockSpec((SC_WINDOW, w), index_map=lambda i: (i, 0)),
                      pl.BlockSpec((1, SC_WINDOW), index_map=lambda i: (0, i))],
            out_specs=[],
            core_axis_name=("c", "s"),
            dimension_semantics=(pltpu.PARALLEL,),
        )(x_hbm, i_hbm)

    return scatter_kernel(src, idx2)


def _sc_gather(src, idx):
    n = idx.shape[0]
    w = src.shape[1]
    idx2 = idx.reshape(1, n)

    @pl.kernel(out_type=jax.ShapeDtypeStruct((n, w), src.dtype), mesh=_sc_mesh())
    def gather_kernel(x_hbm, i_hbm, o_hbm):
        def body(i_vmem, o_vmem):
            pltpu.sync_copy(x_hbm.at[i_vmem.at[0]], o_vmem)

        pltpu.emit_pipeline(
            body,
            grid=(n // SC_WINDOW,),
            in_specs=[pl.BlockSpec((1, SC_WINDOW), index_map=lambda i: (0, i))],
            out_specs=[pl.BlockSpec((SC_WINDOW, w), index_map=lambda i: (i, 0))],
            core_axis_name=("c", "s"),
            dimension_semantics=(pltpu.PARALLEL,),
        )(i_hbm, o_hbm)

    return gather_kernel(src, idx2)


def _mod_kernel(c_ref, w_ref, b_ref, o_ref):
    cv = c_ref[...]
    s = cv * _sigmoid(cv)
    s_hi, s_lo = _split(s)
    w_hi, w_lo = _split(w_ref[...])
    o_ref[...] = _dot(s_hi, w_hi) + _dot(s_lo, w_hi) + _dot(s_hi, w_lo) + b_ref[...]


def _modulation(c_pad, w_mod, b_mod):
    depth, d, six_d = w_mod.shape
    rows = c_pad.shape[0]
    nt = 1536
    return pl.pallas_call(
        _mod_kernel,
        grid=(depth, six_d // nt),
        in_specs=[
            pl.BlockSpec((rows, d), lambda l, j: (0, 0)),
            pl.BlockSpec((None, d, nt), lambda l, j: (l, 0, j)),
            pl.BlockSpec((None, 1, nt), lambda l, j: (l, 0, j)),
        ],
        out_specs=pl.BlockSpec((None, rows, nt), lambda l, j: (l, 0, j)),
        out_shape=jax.ShapeDtypeStruct((depth, rows, six_d), F32),
        compiler_params=_cparams(("arbitrary", "arbitrary")),
        name="modulation",
    )(c_pad, w_mod, b_mod.reshape(depth, 1, six_d))


def _inproj_kernel(*refs, has_res, n_lat_tiles):
    if has_res:
        x_ref, y_ref, g5_ref = refs[:3]
    else:
        x_ref, xctx_ref = refs[:2]
        is_lat = pl.program_id(0) < n_lat_tiles
    refs = refs[3 if has_res else 2:]
    (sh_ref, sc_ref, gmix_ref, cos_ref, sin_ref, w_ref, wa_ref, ba_ref) = refs[:8]
    (xnew_ref, q_ref, k_ref, v_ref, sg_ref, gf_ref, gb_ref, nq_ref, nk_ref, nv_ref, hcv_ref) = refs[8:]

    tile = x_ref.shape[0]
    part = tile // ROW_PARTS
    wa_hi, wa_lo = _split(wa_ref[...])
    even_lane = jnp.bitwise_and(lax.broadcasted_iota(jnp.int32, (1, LANE), 1), 1) == 0
    hbs = []
    for p in range(ROW_PARTS):
        rows = pl.ds(p * part, part)
        if has_res:
            x = x_ref[rows, :] + g5_ref[...] * _unpack_rows(_load_subrows(y_ref, 0, SUB_Y, row0=p * part, rows=part))
        else:
            x = jnp.where(is_lat, x_ref[rows, :], xctx_ref[rows, :])
        xnew_ref[rows, :] = x
        ms = jnp.mean(x * x, axis=-1, keepdims=True)
        h = x * lax.rsqrt(ms + EPS) * gmix_ref[...]
        h = h * (1.0 + sc_ref[...]) + sh_ref[...]
        hbs.append(h.astype(BF16))

    for p, hb in enumerate(hbs):
        rows = pl.ds(p * part, part)

        def proj(c0, n, hb=hb):
            return _dot(hb, w_ref[:, c0:c0 + n])

        cos = cos_ref[rows, :]
        sin = sin_ref[rows, :]

        def rope(t):
            halves = []
            for c0 in range(0, QK_PAD, LANE):
                th = t[:, c0:c0 + LANE]
                halves.append(jnp.where(even_lane, pltpu.roll(th, LANE - 1, 1), pltpu.roll(th, 1, 1)))
            return t * cos + jnp.concatenate(halves, axis=1) * sin

        mid = (C_A + C_CONV) // 2
        assert (mid - C_A) % MXU_N == 0 and C_NQ < mid < C_NK
        d1 = proj(C_A, mid - C_A)
        a_hi, a_lo = _split(d1[:, :LANE])
        q_ref[rows, :] = (rope(proj(C_Q, QK_PAD)) * (GLA_DK ** -0.5)).astype(BF16)
        k_ref[rows, :] = rope(proj(C_K, QK_PAD)).astype(BF16)

        z = _dot(a_hi, wa_hi) + _dot(a_lo, wa_hi) + _dot(a_hi, wa_lo) + ba_ref[...]
        logsig = jnp.minimum(z, 0.0) - jnp.log(1.0 + jnp.exp(-jnp.abs(z)))
        g = logsig * (1.0 / GLA_TAU)
        gf_ref[rows, :] = g[:, :QK_PAD]
        gb_ref[rows, :] = g[:, QK_PAD:]

        v_ref[rows, :] = d1[:, C_V - C_A:C_GATE - C_A].astype(BF16)
        gate = d1[:, C_GATE - C_A:C_NQ - C_A]
        sg_ref[rows, :] = (gate * _sigmoid(gate)).astype(BF16)
        d2 = proj(mid, C_CONV - mid)
        nq = jnp.concatenate([d1[:, C_NQ - C_A:], d2[:, :C_NK - mid]], axis=1)
        nq_ref[rows, :] = (nq * (NA_DH ** -0.5)).astype(BF16)
        nk_ref[rows, :] = d2[:, C_NK - mid:C_NV - mid].astype(BF16)
        nv_ref[rows, :] = d2[:, C_NV - mid:].astype(BF16)
        u = proj(C_CONV, 2 * CONV_WIDTH)
        hcv_ref[rows, :] = u[:, :CONV_WIDTH] * _sigmoid(u[:, CONV_WIDTH:])


def _inproj(x_all, y, mod5, layer, gmix, cos_t, sin_t, w_aug, wa_aug, ba_aug, *, n_lat_tiles, tiles_per_seq,
            n_batch, res_layer):
    has_res = y is not None
    d = x_all.shape[1] if has_res else x_all[0].shape[1]
    t_all = x_all.shape[0] if has_res else x_all[0].shape[0] + x_all[1].shape[0]
    nt = t_all // TOK_TILE

    def bidx(i):
        return jnp.where(i < n_lat_tiles, i // tiles_per_seq, n_batch)

    def ridx(i):
        return jnp.where(i < n_lat_tiles, i % tiles_per_seq, tiles_per_seq)

    def modspec(l, j):
        return pl.BlockSpec((None, None, None, 1, d), lambda i: (l, bidx(i), j, 0, 0))

    tok = lambda w: pl.BlockSpec((TOK_TILE, w), lambda i: (i, 0))
    full = lambda a: pl.BlockSpec(a.shape, lambda i: (0,) * a.ndim)

    if has_res:
        in_specs = [tok(d), pl.BlockSpec((TOK_TILE // 8, SUB_Y, 8, LANE), lambda i: (i, 0, 0, 0)),
                    modspec(res_layer, 5)]
        args = [x_all, y, mod5]
    else:
        in_specs = [pl.BlockSpec((TOK_TILE, d), lambda i: (jnp.minimum(i, n_lat_tiles - 1), 0)),
                    pl.BlockSpec((TOK_TILE, d), lambda i: (jnp.maximum(i - n_lat_tiles, 0), 0))]
        args = list(x_all)
    in_specs += [modspec(layer, 0), modspec(layer, 1), _layer_spec(gmix, layer),
                 pl.BlockSpec((TOK_TILE, QK_PAD), lambda i: (ridx(i), 0)),
                 pl.BlockSpec((TOK_TILE, QK_PAD), lambda i: (ridx(i), 0)),
                 _layer_spec(w_aug, layer), _layer_spec(wa_aug, layer), _layer_spec(ba_aug, layer)]
    args += [mod5, mod5, gmix, cos_t, sin_t, w_aug, wa_aug, ba_aug]

    out_widths = [(d, F32), (QK_PAD, BF16), (QK_PAD, BF16), (GLA_WIDTH, BF16), (GLA_WIDTH, BF16), (QK_PAD, F32),
                  (QK_PAD, F32), (NA_WIDTH, BF16), (NA_WIDTH, BF16), (NA_WIDTH, BF16), (CONV_WIDTH, F32)]
    out_specs = [tok(w) for w, _ in out_widths]
    out_shape = [jax.ShapeDtypeStruct((t_all, w), dt) for w, dt in out_widths]
    res = pl.pallas_call(
        functools.partial(_inproj_kernel, has_res=has_res, n_lat_tiles=n_lat_tiles),
        grid=(nt,),
        in_specs=in_specs,
        out_specs=out_specs,
        out_shape=out_shape,
        compiler_params=_cparams(("parallel",)),
        name="inproj",
    )(*args)
    return res[0], res[1:]


def _gla_masks():
    c, sub = GLA_CHUNK, GLA_SUB
    lane_qk = lax.broadcasted_iota(jnp.int32, (1, QK_PAD), 1)
    head_qk = ((lane_qk >= GLA_DK).astype(jnp.int32) + (lane_qk >= 2 * GLA_DK).astype(jnp.int32)
               + (lane_qk >= 3 * GLA_DK).astype(jnp.int32) + 4 * (lane_qk >= 4 * GLA_DK).astype(jnp.int32))
    row_h = jnp.right_shift(lax.broadcasted_iota(jnp.int32, (c, 1), 0), GLA_SUB.bit_length() - 1)
    hm = (row_h == head_qk).astype(F32)
    row_v = lax.broadcasted_iota(jnp.int32, (GLA_WIDTH, 1), 0)
    head_v = ((row_v >= GLA_DV).astype(jnp.int32) + (row_v >= 2 * GLA_DV).astype(jnp.int32)
              + (row_v >= 3 * GLA_DV).astype(jnp.int32))
    bd = (head_v == head_qk).astype(F32)
    lane_v = lax.broadcasted_iota(jnp.int32, (1, GLA_WIDTH), 1)
    vm = [((lane_v >= h * GLA_DV) & (lane_v < (h + 1) * GLA_DV)).astype(F32) for h in range(GLA_HEADS)]
    return hm, bd, vm


def _gla_steps(q, k, v, g, s_t, hm, bd, vm):
    c, sub = GLA_CHUNK, GLA_SUB
    nsub = c // sub
    dirs = (True, False)
    items = [(u, d) for u in range(GLA_STEPS) for d in range(2)]
    ri = lax.broadcasted_iota(jnp.int32, (c, c), 0)
    ci = lax.broadcasted_iota(jnp.int32, (c, c), 1)
    key_row = lax.broadcasted_iota(jnp.int32, (c, 1), 0)
    att_row = jnp.bitwise_and(ri, sub - 1)
    tri = [((ci <= ri) if fwd else (ci >= ri)).astype(BF16) for fwd in dirs]

    cums = {}
    for u, d in items:
        g_hi, g_lo = _split(g[u][d])
        cums[u, d] = _dot(tri[d], g_hi) + _dot(tri[d], g_lo)

    qe, kv, decay, atts = {}, {}, {}, {}
    for u, d in items:
        fwd = dirs[d]
        cum = cums[u, d]
        tot = cum[c - 1:c] if fwd else cum[0:1]
        qe[u, d] = (q[u][d] * jnp.exp(cum)).astype(BF16)
        k_end = (k[u][d] * jnp.exp(tot - cum)).astype(BF16)
        kv[u, d] = lax.dot_general(v[u][d], k_end, (((0,), (0,)), ((), ())), preferred_element_type=F32)
        decay[u, d] = jnp.exp(tot)
        att_d = []
        for i in range(nsub):
            lo, hi = i * sub, (i + 1) * sub
            if fwd:
                ref = cum[lo - 1:lo] if i > 0 else jnp.zeros((1, QK_PAD), F32)
                key_ok = key_row < hi
                causal = ci <= att_row + lo
            else:
                ref = cum[hi:hi + 1] if i < nsub - 1 else jnp.zeros((1, QK_PAD), F32)
                key_ok = key_row >= lo
                causal = ci >= att_row + lo
            qi = q[u][d][lo:hi] * jnp.exp(cum[lo:hi] - ref)
            qs = (jnp.concatenate([qi] * GLA_HEADS, axis=0) * hm).astype(BF16)
            ki = (k[u][d] * jnp.exp(jnp.where(key_ok, ref - cum, NEG_BIG))).astype(BF16)
            att = _dot_nt(qs, ki)
            att_d.append(jnp.where(causal, att, 0.0).astype(BF16))
        atts[u, d] = jnp.concatenate(att_d, axis=0)

    s = list(s_t)
    o_inter = {}
    for u, d in items:
        o_inter[u, d] = _dot_nt(qe[u, d], s[d].astype(BF16))
        s[d] = s[d] * decay[u, d] + bd * kv[u, d]

    outs = [[None, None] for _ in range(GLA_STEPS)]
    for u, d in items:
        r = _dot(atts[u, d], v[u][d])
        blocks = []
        for i in range(nsub):
            base = i * c
            oi = r[base:base + sub] * vm[0]
            for h in range(1, GLA_HEADS):
                oi = oi + r[base + h * sub:base + (h + 1) * sub] * vm[h]
            blocks.append(oi)
        outs[u][d] = o_inter[u, d] + jnp.concatenate(blocks, axis=0)
    return outs, s


def _gla_kernel(q_ref, k_ref, v_ref, gf_ref, gb_ref, sf0_ref, sb0_ref, of_ref, ob_ref, sf_ref, sb_ref):
    n = q_ref.shape[0]
    nc = n // GLA_CHUNK
    hm, bd, vm = _gla_masks()
    sf_ref[...] = sf0_ref[...]
    sb_ref[...] = sb0_ref[...]

    def body(j, carry):
        rows = [[pl.ds(pl.multiple_of(cidx * GLA_CHUNK, GLA_CHUNK), GLA_CHUNK)
                 for cidx in (j * GLA_STEPS + u, nc - 1 - (j * GLA_STEPS + u))] for u in range(GLA_STEPS)]
        q = [[q_ref[r, :].astype(F32) for r in ru] for ru in rows]
        k = [[k_ref[r, :].astype(F32) for r in ru] for ru in rows]
        v = [[v_ref[r, :] for r in ru] for ru in rows]
        g = [[gf_ref[ru[0], :], gb_ref[ru[1], :]] for ru in rows]
        outs, s_new = _gla_steps(q, k, v, g, [sf_ref[...], sb_ref[...]], hm, bd, vm)
        sf_ref[...] = s_new[0]
        sb_ref[...] = s_new[1]
        for u in range(GLA_STEPS):
            of_ref[rows[u][0], :] = outs[u][0]
            ob_ref[rows[u][1], :] = outs[u][1]
        return carry

    lax.fori_loop(0, nc // GLA_STEPS, body, 0)


def _gla(q, k, v, gf, gb, sf0, sb0, o_prev, *, seq, blk0, n_batch):
    t_all = q.shape[0]
    tokw = lambda w: pl.BlockSpec((seq, w), lambda b: (blk0 + b, 0))
    st = pl.BlockSpec((None, GLA_WIDTH, QK_PAD), lambda b: (b, 0, 0))
    in_specs = [tokw(QK_PAD), tokw(QK_PAD), tokw(GLA_WIDTH), tokw(QK_PAD), tokw(QK_PAD), st, st]
    args = [q, k, v, gf, gb, sf0, sb0]
    aliases = {}
    n_in = len(args)
    if o_prev is not None:
        in_specs += [pl.BlockSpec(memory_space=pl.ANY)] * 2
        args += list(o_prev)
        aliases = {n_in: 0, n_in + 1: 1}

    def kern(*refs):
        _gla_kernel(*refs[:n_in], *refs[len(args):])

    st_shape = jax.ShapeDtypeStruct((n_batch, GLA_WIDTH, QK_PAD), F32)
    o_shape = jax.ShapeDtypeStruct((t_all, GLA_WIDTH), F32)
    o_f, o_b, s_f, s_b = pl.pallas_call(
        kern,
        grid=(n_batch,),
        in_specs=in_specs,
        out_specs=[tokw(GLA_WIDTH), tokw(GLA_WIDTH), st, st],
        out_shape=[o_shape, o_shape, st_shape, st_shape],
        input_output_aliases=aliases,
        compiler_params=_cparams(("parallel",)),
        name="gla",
    )(*args)
    return (o_f, o_b), s_f, s_b


def _na_kernel(q_ref, k_ref, v_ref, kc_ref, vc_ref, bias_ref, *rest, n_cast):
    cast_in, (o_ref, *cast_out) = rest[:n_cast], rest[n_cast:]
    _cast_slices(cast_in, cast_out)
    n = q_ref.shape[0]
    rows = n // GRID_W
    nkeys = NA_KH * GRID_W
    lane = lax.broadcasted_iota(jnp.int32, (1, LANE), 1)
    first = lane < NA_DH
    kc = kc_ref[...]
    vc = vc_ref[...]

    def body(jb, carry):
        items = []
        for j in range(NA_UNROLL):
            r = jb * NA_UNROLL + j
            r0 = jnp.clip(r - NA_KH // 2, 0, rows - NA_KH)
            var = r - r0
            qrows = pl.ds(pl.multiple_of(r * GRID_W, GRID_W), GRID_W)
            krows = pl.ds(pl.multiple_of(r0 * GRID_W, GRID_W), nkeys)
            qr = q_ref[qrows, :]
            kb = k_ref[krows, :]
            q2 = jnp.concatenate([jnp.where(first, qr, jnp.zeros_like(qr)),
                                  jnp.where(first, jnp.zeros_like(qr), qr)], axis=0)
            items.append((qrows, krows, var, _dot_nt(q2, kb), _dot_nt(q2, kc)))
        probs = []
        for qrows, krows, var, s_loc, s_ctx in items:
            s_loc = s_loc + jnp.concatenate([bias_ref[0, var], bias_ref[1, var]], axis=0)
            m = jnp.maximum(jnp.max(s_loc, axis=-1, keepdims=True), jnp.max(s_ctx, axis=-1, keepdims=True))
            p_loc = jnp.exp(s_loc - m)
            p_ctx = jnp.exp(s_ctx - m)
            l = jnp.sum(p_loc, axis=-1, keepdims=True) + jnp.sum(p_ctx, axis=-1, keepdims=True)
            probs.append((p_loc.astype(BF16), p_ctx.astype(BF16), l))
        for (qrows, krows, var, _, _), (p_loc, p_ctx, l) in zip(items, probs):
            o = (_dot(p_loc, v_ref[krows, :]) + _dot(p_ctx, vc)) / l
            o_ref[qrows, :] = jnp.where(first, o[:GRID_W], o[GRID_W:]).astype(o_ref.dtype)
        return carry

    lax.fori_loop(0, rows // NA_UNROLL, body, 0)


def _na_latent(nq, nk, nv, bias_tab, layer, cast_f32=None, *, seq, ctx_len, n_batch, ctx_blk0):
    t_all = nq.shape[0]
    npair = NA_HEADS // 2
    lat = pl.BlockSpec((seq, LANE), lambda b, p: (b, p))
    ctx = pl.BlockSpec((ctx_len, LANE), lambda b, p: (ctx_blk0 + b, p))
    bias = pl.BlockSpec((None, 2) + bias_tab.shape[2:], lambda b, p: (layer, p, 0, 0, 0))
    in_specs = [lat, lat, lat, ctx, ctx, bias]
    out_specs = [lat]
    out_shape = [jax.ShapeDtypeStruct((t_all, NA_WIDTH), BF16)]
    args = [nq, nk, nv, nk, nv, bias_tab]
    if cast_f32 is not None:
        c_in, c_out, c_shape = _cast_specs(cast_f32, n_batch * npair, lambda b, p: b * npair + p)
        in_specs += c_in
        out_specs += c_out
        out_shape += c_shape
        args += list(cast_f32[0])
    res = pl.pallas_call(
        functools.partial(_na_kernel, n_cast=0 if cast_f32 is None else len(cast_f32[0])),
        grid=(n_batch, npair),
        in_specs=in_specs,
        out_specs=out_specs,
        out_shape=out_shape,
        compiler_params=_cparams(("arbitrary", "arbitrary")),
        name="na_latent",
    )(*args)
    return res[0], tuple(res[1:])


def _na_ctx_kernel(q_ref, k_ref, v_ref, o_in_ref, o_ref):
    del o_in_ref
    lane = lax.broadcasted_iota(jnp.int32, (1, LANE), 1)
    first = lane < NA_DH
    q = q_ref[...]
    k = k_ref[...]
    v = v_ref[...]
    res = []
    for h in range(2):
        sel = first if h == 0 else jnp.logical_not(first)
        qh = jnp.where(sel, q, jnp.zeros_like(q))
        s = _dot_nt(qh, k)
        m = jnp.max(s, axis=-1, keepdims=True)
        p = jnp.exp(s - m)
        l = jnp.sum(p, axis=-1, keepdims=True)
        res.append(_dot(p.astype(BF16), v) / l)
    o_ref[...] = jnp.where(first, res[0], res[1]).astype(o_ref.dtype)


def _na_context(nq, nk, nv, o_prev, *, ctx_len, n_batch, ctx_blk0):
    npair = NA_HEADS // 2
    ctx = pl.BlockSpec((ctx_len, LANE), lambda b, p: (ctx_blk0 + b, p))
    return pl.pallas_call(
        _na_ctx_kernel,
        grid=(n_batch, npair),
        in_specs=[ctx, ctx, ctx, pl.BlockSpec(memory_space=pl.ANY)],
        out_specs=ctx,
        out_shape=jax.ShapeDtypeStruct(o_prev.shape, o_prev.dtype),
        input_output_aliases={3: 0},
        compiler_params=_cparams(("parallel", "arbitrary")),
        name="na_context",
    )(nq, nk, nv, o_prev)


CONV_HALO = 16
CONV_ROWS = 128


def _conv_kernel(*refs, has_prev):
    if has_prev:
        h_ref, w_ref, b_ref, lg_ref, lb_ref, _, o_ref, pad_ref, sh_ref = refs
    else:
        h_ref, w_ref, b_ref, lg_ref, lb_ref, o_ref, pad_ref, sh_ref = refs
    n = h_ref.shape[0]
    zeros = jnp.zeros((CONV_HALO, CONV_WIDTH), F32)
    pad_ref[0:CONV_HALO, :] = zeros
    pad_ref[CONV_HALO + n:CONV_HALO + n + CONV_HALO, :] = zeros
    pad_ref[CONV_HALO:CONV_HALO + n, :] = h_ref[...]
    span = n + 2 * CONV_HALO - SUBLANES
    for s in range(1, SUBLANES):
        sh_ref[s - 1, 0:span, :] = pad_ref[s:s + span, :]
    w = w_ref[...]
    off = CONV_HALO - CONV_K // 2

    def chunk(cidx, carry):
        base = pl.multiple_of(cidx * CONV_ROWS, CONV_ROWS)
        acc = jnp.zeros((CONV_ROWS, CONV_WIDTH), F32) + b_ref[...]
        for j in range(CONV_K):
            s = (off + j) % SUBLANES
            win = pl.ds(base + (off + j - s), CONV_ROWS)
            tap = pad_ref[win, :] if s == 0 else sh_ref[s - 1, win, :]
            acc = acc + tap * w[j:j + 1, :]
        mu = jnp.mean(acc, axis=-1, keepdims=True)
        xc = acc - mu
        var = jnp.mean(xc * xc, axis=-1, keepdims=True)
        y = xc * lax.rsqrt(var + EPS) * lg_ref[...] + lb_ref[...]
        o_ref[pl.ds(base, CONV_ROWS), :] = (y * _sigmoid(y)).astype(o_ref.dtype)
        return carry

    lax.fori_loop(0, n // CONV_ROWS, chunk, 0)


def _conv(hcv, w, b, lg, lb, layer, o_prev, *, seq, blk0, n_batch):
    t_all = hcv.shape[0]
    tok = pl.BlockSpec((seq, CONV_WIDTH), lambda i: (blk0 + i, 0))
    full = lambda a: pl.BlockSpec(a.shape, lambda i: (0,) * a.ndim)
    in_specs = [tok] + [_layer_spec(a, layer) for a in (w, b, lg, lb)]
    args = [hcv, w, b, lg, lb]
    aliases = {}
    if o_prev is not None:
        in_specs.append(pl.BlockSpec(memory_space=pl.ANY))
        args.append(o_prev)
        aliases = {5: 0}
    return pl.pallas_call(
        functools.partial(_conv_kernel, has_prev=o_prev is not None),
        grid=(n_batch,),
        in_specs=in_specs,
        out_specs=tok,
        out_shape=jax.ShapeDtypeStruct((t_all, CONV_WIDTH), BF16),
        scratch_shapes=[pltpu.VMEM((seq + 2 * CONV_HALO, CONV_WIDTH), F32),
                        pltpu.VMEM((SUBLANES - 1, seq + 2 * CONV_HALO, CONV_WIDTH), F32)],
        input_output_aliases=aliases,
        compiler_params=_cparams(("parallel",)),
        name="conv",
    )(*args)


def _outproj_kernel(ogf_ref, ogb_ref, sg_ref, on_ref, oc_ref, x_ref, g2_ref, sh_ref, sc_ref, gn_ref, e_ref,
                    wo_ref, gffn_ref, rw_ref, xmid_ref, h2_ref, lg_ref):
    tile = x_ref.shape[0]
    parts = [pl.ds(p * (tile // ROW_PARTS), tile // ROW_PARTS) for p in range(ROW_PARTS)]
    e = e_ref[...]
    rw = rw_ref[...]

    ofs, mss = [], []
    for rows in parts:
        of = ogf_ref[rows, :] + ogb_ref[rows, :]
        sq_hi, sq_lo = _split(of * of)
        ofs.append(of)
        mss.append(_dot(jnp.concatenate([sq_hi, sq_lo], axis=1), e) * (1.0 / GLA_DV))
    ys = []
    for rows, of, ms in zip(parts, ofs, mss):
        og = of * lax.rsqrt(ms + EPS) * gn_ref[...] * sg_ref[rows, :].astype(F32)
        mix = jnp.concatenate([og.astype(BF16), on_ref[rows, :], oc_ref[rows, :]], axis=1)
        ys.append(_dot(mix, wo_ref[...]))
    for p, (rows, y) in enumerate(zip(parts, ys)):
        x = x_ref[rows, :] + g2_ref[...] * y
        xmid_ref[rows, :] = x
        ms2 = jnp.mean(x * x, axis=-1, keepdims=True)
        h2 = x * lax.rsqrt(ms2 + EPS) * gffn_ref[...]
        h2 = h2 * (1.0 + sc_ref[...]) + sh_ref[...]
        h_hi, h_lo = _split(h2)
        _store_subrows(h2_ref, _pack_rows(h2), 0, row0=p * (tile // ROW_PARTS))
        hw = _dot(h_hi, rw)
        lg_ref[rows, :] = hw[:, :LANE] + hw[:, LANE:] + _dot(h_lo, rw[:, :LANE])


def _outproj(o_gf, o_gb, sgate, o_n, o_c, x_all, mod5, layer, gnorm, e_mat, w_out, gffn, rw_split, *, n_tiles,
             n_lat_tiles, tiles_per_seq, n_batch):
    d = x_all.shape[1]
    t_out = n_tiles * TOK_TILE

    def bidx(i):
        return jnp.where(i < n_lat_tiles, i // tiles_per_seq, n_batch)

    def modspec(j):
        return pl.BlockSpec((None, None, None, 1, d), lambda i: (layer, bidx(i), j, 0, 0))

    tok = lambda w: pl.BlockSpec((TOK_TILE, w), lambda i: (i, 0))
    full = lambda a: pl.BlockSpec(a.shape, lambda i: (0,) * a.ndim)
    return pl.pallas_call(
        _outproj_kernel,
        grid=(n_tiles,),
        in_specs=[tok(GLA_WIDTH), tok(GLA_WIDTH), tok(GLA_WIDTH), tok(NA_WIDTH), tok(CONV_WIDTH), tok(d),
                  modspec(2), modspec(3), modspec(4), _layer_spec(gnorm, layer), full(e_mat),
                  _layer_spec(w_out, layer), _layer_spec(gffn, layer),
                  full(rw_split)],
        out_specs=[tok(d), pl.BlockSpec((TOK_TILE // 8, SUB_Y, 8, LANE), lambda i: (i, 0, 0, 0)), tok(LANE)],
        out_shape=[jax.ShapeDtypeStruct((t_out, d), F32),
                   jax.ShapeDtypeStruct((t_out // 8, SUB_X, 8, LANE), jnp.uint32),
                   jax.ShapeDtypeStruct((t_out, LANE), F32)],
        compiler_params=_cparams(("parallel",)),
        name="outproj",
    )(o_gf, o_gb, sgate, o_n, o_c, x_all, mod5, mod5, mod5, gnorm, e_mat, w_out, gffn, rw_split)


def _route_kernel(lg_ref, bias_ref, h2_in_ref, meta_ref, cnt_ref, h2w_ref, carry_ref):
    del h2_in_ref
    tile = lg_ref.shape[0]

    @pl.when(pl.program_id(0) == 0)
    def _():
        carry_ref[...] = jnp.zeros_like(carry_ref)

    lt = lg_ref[...].T
    aff = _sigmoid(lt[0:N_EXPERTS])
    sel = aff + bias_ref[...]
    s = [sel[e:e + 1] for e in range(N_EXPERTS)]
    a = [aff[e:e + 1] for e in range(N_EXPERTS)]

    def top2sum(v):
        best = v[0] + v[1]
        for i, j in PAIRS[1:]:
            best = jnp.maximum(best, v[i] + v[j])
        return best

    gs = [top2sum(s[4 * g:4 * g + 4]) for g in range(N_GROUPS)]
    gbest = jnp.zeros_like(gs[0], dtype=jnp.int32)
    gmax = gs[0]
    for g in range(1, N_GROUPS):
        upd = gs[g] > gmax
        gbest = jnp.where(upd, g, gbest)
        gmax = jnp.where(upd, gs[g], gmax)

    def pick(vals, j):
        out = vals[j]
        for g in range(1, N_GROUPS):
            out = jnp.where(gbest == g, vals[4 * g + j], out)
        return out

    sv = [pick(s, j) for j in range(EXPERTS_PER_GROUP)]
    av = [pick(a, j) for j in range(EXPERTS_PER_GROUP)]
    i1 = jnp.zeros_like(gbest)
    m1 = sv[0]
    for j in range(1, EXPERTS_PER_GROUP):
        upd = sv[j] > m1
        i1 = jnp.where(upd, j, i1)
        m1 = jnp.where(upd, sv[j], m1)
    i2 = jnp.full_like(gbest, -1)
    m2 = jnp.zeros_like(m1)
    for j in range(EXPERTS_PER_GROUP):
        upd = (i1 != j) & ((sv[j] > m2) | (i2 < 0))
        i2 = jnp.where(upd, j, i2)
        m2 = jnp.where(upd, sv[j], m2)
    ia = jnp.minimum(i1, i2)
    ib = jnp.maximum(i1, i2)
    pair = jnp.where(ia == 0, ib - 1, jnp.where(ia == 1, ib + 1, 5))
    cls = gbest * len(PAIRS) + pair

    def take(vals, idx):
        out = vals[0]
        for j in range(1, EXPERTS_PER_GROUP):
            out = jnp.where(idx == j, vals[j], out)
        return out

    w1 = take(av, i1)
    w2 = take(av, i2)
    tot = w1 + w2
    wa = jnp.where(i1 < i2, w1, w2) / tot
    wb = jnp.where(i1 < i2, w2, w1) / tot

    crow = lax.broadcasted_iota(jnp.int32, (32, tile), 0)
    oh = (crow == cls).astype(F32)
    us = lax.broadcasted_iota(jnp.int32, (tile, tile), 0)
    ut = lax.broadcasted_iota(jnp.int32, (tile, tile), 1)
    upper = (us < ut).astype(BF16)
    prefix = _dot(oh.astype(BF16), upper)
    carry = carry_ref[...]
    rank = jnp.sum(oh * (prefix + carry), axis=0, keepdims=True)
    carry_new = carry + jnp.sum(oh, axis=1, keepdims=True)
    carry_ref[...] = carry_new
    cnt_ref[...] = jnp.broadcast_to(carry_new, cnt_ref.shape)

    meta_ref[...] = jnp.zeros_like(meta_ref)
    meta_ref[0:1, :] = cls.astype(F32)
    meta_ref[1:2, :] = rank

    wrow = lax.broadcasted_iota(jnp.int32, (LANE, tile), 0)
    wmat = jnp.where(wrow == 0, wa, jnp.where(wrow == 1, wb, 0.0))
    wtok = lax.bitcast_convert_type(wmat.T, jnp.uint32)
    h2w_ref[:, 0, :, :] = wtok.reshape(tile // 8, 8, LANE)


def _route(logits, bias_col, h2_sub):
    t = logits.shape[0]
    nt = t // ROUTE_TILE
    meta, cnt, h2_sub = pl.pallas_call(
        _route_kernel,
        grid=(nt,),
        in_specs=[pl.BlockSpec((ROUTE_TILE, LANE), lambda i: (i, 0)),
                  pl.BlockSpec(bias_col.shape, lambda i: (0, 0)),
                  pl.BlockSpec(memory_space=pl.ANY)],
        out_specs=[pl.BlockSpec((8, ROUTE_TILE), lambda i: (0, i)),
                   pl.BlockSpec((32, LANE), lambda i: (0, 0)),
                   pl.BlockSpec((ROUTE_TILE // 8, 1, 8, LANE), lambda i: (i, SUB_Y, 0, 0))],
        out_shape=[jax.ShapeDtypeStruct((8, t), F32), jax.ShapeDtypeStruct((32, LANE), F32),
                   jax.ShapeDtypeStruct(h2_sub.shape, h2_sub.dtype)],
        scratch_shapes=[pltpu.VMEM((32, 1), F32)],
        input_output_aliases={2: 2},
        compiler_params=_cparams(("arbitrary",)),
        name="route",
    )(logits, bias_col, h2_sub)
    return meta, cnt, h2_sub


FF_TILE = 512


def _expert_kernel(ea_ref, eb_ref, nvalid_ref, xs_ref, wga_ref, wua_ref, wda_ref, wgb_ref, wub_ref, wdb_ref,
                   *rest, n_cast):
    del ea_ref, eb_ref
    j = pl.program_id(0)
    nvalid = nvalid_ref[j]
    nxt_in, (y_ref, *nxt_out) = rest[:n_cast], rest[n_cast:]

    def cast_next():
        _cast_slices(nxt_in, nxt_out)

    @pl.when(nvalid == 0)
    def _():
        cast_next()
        y_ref[...] = jnp.zeros_like(y_ref)

    @pl.when(nvalid != 0)
    def _():
        cast_next()
        rows = y_ref.shape[0] * 8
        live = lax.broadcasted_iota(jnp.int32, (rows, 1), 0) < nvalid
        x = jnp.where(live, _unpack_rows(_load_subrows(xs_ref, 0, SUB_Y)), 0.0).astype(BF16)
        ws = jnp.where(live, lax.bitcast_convert_type(xs_ref[:, SUB_Y, :, :].reshape(rows, LANE), F32), 0.0)
        ff = wga_ref.shape[1]

        items = [(w, f0) for w in ((wga_ref, wua_ref, wda_ref), (wgb_ref, wub_ref, wdb_ref))
                 for f0 in range(0, ff, FF_TILE)]

        def up(item):
            (wg_ref, wu_ref, _), f0 = item
            return _dot(x, wg_ref[:, f0:f0 + FF_TILE]), _dot(x, wu_ref[:, f0:f0 + FF_TILE])

        ups = [up(items[0]), up(items[1])]
        parts = []
        for c, ((_, _, wd_ref), f0) in enumerate(items):
            hg, hu = ups[c]
            hh = (hg * _sigmoid(hg) * hu).astype(BF16)
            if c + 2 < len(items):
                ups.append(up(items[c + 2]))
            parts.append(_dot(hh, wd_ref[f0:f0 + FF_TILE, :]))
        per = len(items) // 2
        ya = functools.reduce(lambda a, b: a + b, parts[:per])
        yb = functools.reduce(lambda a, b: a + b, parts[per:])
        _store_subrows(y_ref, _pack_rows(ya * ws[:, 0:1] + yb * ws[:, 1:2]), 0)


def _experts(xs_sub, ea, eb, nvalid, weights, next_f32):
    wg, wu, wd = weights
    nb = xs_sub.shape[0] * 8 // MOE_BLK
    d, ff = wg.shape[1], wg.shape[2]
    wspec_in = lambda which: pl.BlockSpec((None, d, ff), lambda j, ea, eb, v: ((ea, eb)[which][j], 0, 0))
    wspec_out = lambda which: pl.BlockSpec((None, ff, d), lambda j, ea, eb, v: ((ea, eb)[which][j], 0, 0))
    in_specs = [pl.BlockSpec((MOE_BLK // 8, SUB_X, 8, LANE), lambda j, ea, eb, v: (j, 0, 0, 0)),
                wspec_in(0), wspec_in(0), wspec_out(0), wspec_in(1), wspec_in(1), wspec_out(1)]
    out_specs = [pl.BlockSpec((MOE_BLK // 8, SUB_Y, 8, LANE), lambda j, ea, eb, v: (j, 0, 0, 0))]
    out_shape = [jax.ShapeDtypeStruct((nb * MOE_BLK // 8, SUB_Y, 8, LANE), jnp.uint32)]
    args = [ea, eb, nvalid, xs_sub, wg, wu, wd, wg, wu, wd]
    if next_f32 is not None:
        c_in, c_out, c_shape = _cast_specs(next_f32, nb, lambda j, *_: j)
        in_specs += c_in
        out_specs += c_out
        out_shape += c_shape
        args += list(next_f32[0])
    grid_spec = pltpu.PrefetchScalarGridSpec(num_scalar_prefetch=3, grid=(nb,), in_specs=in_specs,
                                             out_specs=out_specs)
    res = pl.pallas_call(
        functools.partial(_expert_kernel, n_cast=0 if next_f32 is None else len(next_f32[0])),
        grid_spec=grid_spec,
        out_shape=out_shape,
        compiler_params=_cparams(("arbitrary",)),
        name="experts",
    )(*args)
    return res[0], tuple(res[1:])


def _subrow_index(dest, nsub):
    t = dest.shape[0]
    base = ((dest // 8) * (nsub * 8) + dest % 8).astype(F32).reshape(t // LANE, LANE)
    src = np.arange(LANE)
    grp, r = src // 8, src % 8
    sel = np.zeros((LANE, LANE * nsub), np.float32)
    off = np.zeros((LANE * nsub,), np.int32)
    for j in range(nsub):
        pos = grp * (nsub * 8) + j * 8 + r
        sel[src, pos] = 1.0
        off[pos] = j * 8
    idx = jnp.dot(base, jnp.asarray(sel), precision=lax.Precision.HIGHEST).astype(jnp.int32) + jnp.asarray(off)
    return idx.reshape(t * nsub)


def _moe(h2_sub, logits, bias_col, weights, next_f32):
    t = logits.shape[0]
    meta, cnt, h2_sub = _route(logits, bias_col, h2_sub)
    cls = meta[0].astype(jnp.int32)
    rank = meta[1].astype(jnp.int32)
    counts = cnt[:N_CLASSES, 0].astype(jnp.int32)
    padded = (counts + MOE_BLK - 1) // MOE_BLK * MOE_BLK
    pad_end = jnp.cumsum(padded)
    pad_start = pad_end - padded
    class_ids = jnp.arange(N_CLASSES, dtype=jnp.int32)
    dest = rank + jnp.sum(jnp.where(cls[None, :] == class_ids[:, None], pad_start[:, None], 0), axis=0)
    nb = t // MOE_BLK + N_CLASSES
    p_rows = nb * MOE_BLK
    blk_start = jnp.arange(nb, dtype=jnp.int32) * MOE_BLK
    valid = blk_start < pad_end[-1]
    blk_cls = jnp.sum((pad_end[None, :] <= blk_start[:, None]).astype(jnp.int32), axis=-1)
    last_cls = jnp.sum((pad_end <= pad_end[-1] - 1).astype(jnp.int32))
    blk_cls = jnp.minimum(jnp.where(valid, blk_cls, last_cls), N_CLASSES - 1)
    nvalid = jnp.where(valid, jnp.clip(pad_start[blk_cls] + counts[blk_cls] - blk_start, 0, MOE_BLK), 0)
    pair_a = jnp.array([p[0] for p in PAIRS], jnp.int32)
    pair_b = jnp.array([p[1] for p in PAIRS], jnp.int32)
    grp = blk_cls // len(PAIRS)
    ea = grp * EXPERTS_PER_GROUP + pair_a[blk_cls % len(PAIRS)]
    eb = grp * EXPERTS_PER_GROUP + pair_b[blk_cls % len(PAIRS)]

    xs = _sc_scatter(h2_sub.reshape(t * SUB_X, LANE), _subrow_index(dest, SUB_X), p_rows * SUB_X)
    ys, next_bf16 = _experts(xs.reshape(p_rows // 8, SUB_X, 8, LANE), ea, eb, nvalid.astype(jnp.int32), weights,
                             next_f32)
    y = _sc_gather(ys.reshape(p_rows * SUB_Y, LANE), _subrow_index(dest, SUB_Y))
    return y.reshape(t // 8, SUB_Y, 8, LANE), next_bf16


def _final_kernel(x_ref, y_ref, g5_ref, gf_ref, o_ref):
    x = x_ref[...] + g5_ref[...] * _unpack_rows(_load_subrows(y_ref, 0, SUB_Y))
    ms = jnp.mean(x * x, axis=-1, keepdims=True)
    o_ref[...] = x * lax.rsqrt(ms + EPS) * gf_ref[...]


def _final(x_mid, y, mod5, layer, g_final, *, tiles_per_seq):
    t, d = x_mid.shape
    tok = pl.BlockSpec((TOK_TILE, d), lambda i: (i, 0))
    return pl.pallas_call(
        _final_kernel,
        grid=(t // TOK_TILE,),
        in_specs=[tok, pl.BlockSpec((TOK_TILE // 8, SUB_Y, 8, LANE), lambda i: (i, 0, 0, 0)),
                  pl.BlockSpec((None, None, None, 1, d), lambda i: (layer, i // tiles_per_seq, 5, 0, 0)),
                  pl.BlockSpec(g_final.shape, lambda i: (0, 0))],
        out_specs=tok,
        out_shape=jax.ShapeDtypeStruct((t, d), F32),
        compiler_params=_cparams(("parallel",)),
        name="final_norm",
    )(x_mid, y, mod5, g_final)


def _rope_tables(seq):
    t = jnp.arange(seq)
    row = (t // GRID_W).astype(F32)
    col = (t % GRID_W).astype(F32)
    half = GLA_DK // 2
    inv = ROPE_BASE ** (-jnp.arange(0, half, 2, dtype=F32) / half)
    ang = jnp.concatenate([row[:, None] * inv, col[:, None] * inv], axis=-1)
    cos = jnp.repeat(jnp.cos(ang), 2, axis=-1)
    sin = (jnp.sin(ang)[:, :, None] * jnp.array([-1.0, 1.0], F32)).reshape(seq, GLA_DK)
    cos = jnp.tile(cos, (1, GLA_HEADS))
    sin = jnp.tile(sin, (1, GLA_HEADS))
    padw = QK_PAD - GLA_QK
    cos = jnp.pad(cos, ((0, 0), (0, padw)), constant_values=1.0)
    sin = jnp.pad(sin, ((0, 0), (0, padw)))
    cos = jnp.concatenate([cos, jnp.ones((TOK_TILE, QK_PAD), F32)], axis=0)
    sin = jnp.concatenate([sin, jnp.zeros((TOK_TILE, QK_PAD), F32)], axis=0)
    return cos, sin


def _pad_last(w, n):
    return jnp.pad(w, [(0, 0)] * (w.ndim - 1) + [(0, n - w.shape[-1])])


def _in_weights(w_in):
    offs = np.cumsum([0, GLA_QK, GLA_QK, GLA_WIDTH, GLA_WIDTH, 2 * GLA_LOWRANK, NA_WIDTH, NA_WIDTH, NA_WIDTH,
                      2 * CONV_WIDTH])
    seg = [w_in[..., offs[i]:offs[i + 1]] for i in range(9)]
    cols = [_pad_last(seg[0], QK_PAD), _pad_last(seg[1], QK_PAD), _pad_last(seg[4], LANE), seg[2], seg[3], seg[5],
            seg[6], seg[7], seg[8]]
    w = jnp.concatenate(cols, axis=-1).astype(BF16)
    assert w.shape[-1] == IN_COLS_PAD
    return w


def _gate_weights(wa_f, ba_f, wa_b, ba_b):
    depth = wa_f.shape[0]
    zero = jnp.zeros((depth, GLA_LOWRANK, QK_PAD), F32)
    top = jnp.concatenate([_pad_last(wa_f, QK_PAD), zero], axis=-1)
    bot = jnp.concatenate([zero, _pad_last(wa_b, QK_PAD)], axis=-1)
    rest = jnp.zeros((depth, LANE - 2 * GLA_LOWRANK, 2 * QK_PAD), F32)
    wa = jnp.concatenate([top, bot, rest], axis=1)
    ba = jnp.concatenate([_pad_last(ba_f, QK_PAD), _pad_last(ba_b, QK_PAD)], axis=-1)[:, None, :]
    return wa, ba


def _na_bias_tables(rpb):
    cq = np.arange(GRID_W)
    c0 = np.clip(cq - NA_KW // 2, 0, GRID_W - NA_KW)
    kc = np.arange(GRID_W)
    valid = (kc[None, :] >= c0[:, None]) & (kc[None, :] < c0[:, None] + NA_KW)
    w = GRID_W
    padded = jnp.pad(rpb, [(0, 0)] * 3 + [(w - NA_KW, w - NA_KW)], constant_values=NEG_BIG)
    flat = jnp.tile(padded, (1, 1, 1, w))
    band = flat[..., w - 1:w - 1 + w * (2 * w - 2)].reshape(rpb.shape[:3] + (w, 2 * w - 2))[..., :w]
    band = jnp.transpose(jnp.where(valid, band, NEG_BIG), (0, 1, 3, 2, 4))
    tab = jnp.stack([band[:, :, :, NA_KH - 1 - v:2 * NA_KH - 1 - v] for v in range(NA_KH)], axis=2)
    return tab.reshape(rpb.shape[:2] + (NA_KH, GRID_W, NA_KH * GRID_W)).astype(F32)


def _head_mean_matrix():
    h = np.arange(GLA_WIDTH) // GLA_DV
    e = (h[:, None] == h[None, :]).astype(np.float32)
    return jnp.asarray(np.concatenate([e, e], axis=0), dtype=BF16)


def kernel(x, c, ctx, c_ctx, w_mod, b_mod, g_mix, g_ffn, w_in, gla_wa_f, gla_ba_f, gla_wa_b, gla_ba_b, gla_g_norm,
           na_rpb, conv_w, conv_b, conv_ln_g, conv_ln_b, w_out, router_w, router_bias, w_gate, w_up, w_down,
           g_final):
    bsz, seq, d = x.shape
    ctx_len = ctx.shape[1]
    depth = w_mod.shape[0]
    t_lat, t_ctx = bsz * seq, bsz * ctx_len
    assert seq % TOK_TILE == 0 and t_ctx % TOK_TILE == 0 and seq % ctx_len == 0
    assert d == 2 * SUB_Y * LANE
    assert seq // GRID_W >= NA_KH and (seq // GRID_W) % NA_UNROLL == 0
    assert ctx_len % (GLA_CHUNK * GLA_STEPS) == 0 and seq % (GLA_CHUNK * GLA_STEPS) == 0
    tiles_per_seq = seq // TOK_TILE
    n_lat_tiles = t_lat // TOK_TILE
    n_all_tiles = (t_lat + t_ctx) // TOK_TILE
    ctx_blk0 = t_lat // ctx_len

    mod_rows = -(-(bsz + 1) // 8) * 8
    c_pad = jnp.zeros((mod_rows, d), F32).at[:bsz].set(c).at[bsz].set(c_ctx)
    mod = _modulation(c_pad, w_mod, b_mod)
    mod5 = mod.reshape(depth, mod_rows, 6, 1, d)

    cos_t, sin_t = _rope_tables(seq)
    e_mat = _head_mean_matrix()
    rw = jnp.pad(router_w, ((0, 0), (0, LANE - N_EXPERTS)))
    rw_hi = rw.astype(BF16)
    rw_lo = (rw - rw_hi.astype(F32)).astype(BF16)
    rw_split = jnp.concatenate([rw_hi, rw_lo], axis=1)
    bias_col = router_bias.reshape(N_EXPERTS, 1).astype(F32)
    zero_state = jnp.zeros((bsz, GLA_WIDTH, QK_PAD), F32)

    w_aug = _in_weights(w_in)
    wa_aug, ba_aug = _gate_weights(gla_wa_f, gla_ba_f, gla_wa_b, gla_ba_b)
    bias_tab = _na_bias_tables(na_rpb)
    row = lambda a: a[:, None, :]
    gmix, gffn, gnorm = row(g_mix), row(g_ffn), row(gla_g_norm)
    cb, clg, clb = row(conv_b), row(conv_ln_g), row(conv_ln_b)
    w_out_b = w_out.astype(BF16)
    expert_f32 = (w_gate, w_up, w_down)
    layer_rows = w_gate.shape[1] * w_gate.shape[2]
    assert w_down.shape[1] * w_down.shape[2] == layer_rows
    expert_rows = tuple(w.reshape(depth * layer_rows, w.shape[-1]) for w in expert_f32)
    as_experts = lambda flat: tuple(a.reshape(w.shape[1:]) for a, w in zip(flat, expert_f32))

    x_all = (x.reshape(t_lat, d), ctx.reshape(t_ctx, d))
    y_moe = None
    for l in range(depth):
        last = l == depth - 1
        x_all, (q, k, v, sgate, gf, gb, nq, nk, nv, hcv) = _inproj(
            x_all, y_moe, mod5, l, gmix, cos_t, sin_t, w_aug, wa_aug, ba_aug,
            n_lat_tiles=n_lat_tiles, tiles_per_seq=tiles_per_seq, n_batch=bsz, res_layer=l - 1)

        o_g, st_f, st_b = _gla(q, k, v, gf, gb, zero_state, zero_state, None, seq=ctx_len, blk0=ctx_blk0,
                               n_batch=bsz)
        o_g, _, _ = _gla(q, k, v, gf, gb, st_f, st_b, o_g, seq=seq, blk0=0, n_batch=bsz)

        o_n, cast0 = _na_latent(nq, nk, nv, bias_tab, l, (expert_rows, layer_rows, 0) if l == 0 else None,
                                seq=seq, ctx_len=ctx_len, n_batch=bsz, ctx_blk0=ctx_blk0)
        if l == 0:
            weights = as_experts(cast0)
        o_c = _conv(hcv, conv_w, cb, clg, clb, l, None, seq=seq, blk0=0, n_batch=bsz)
        if not last:
            o_n = _na_context(nq, nk, nv, o_n, ctx_len=ctx_len, n_batch=bsz, ctx_blk0=ctx_blk0)
            o_c = _conv(hcv, conv_w, cb, clg, clb, l, o_c, seq=ctx_len, blk0=ctx_blk0, n_batch=bsz)

        n_tiles = n_lat_tiles if last else n_all_tiles
        x_mid, h2, logits = _outproj(
            o_g[0], o_g[1], sgate, o_n, o_c, x_all, mod5, l, gnorm, e_mat, w_out_b, gffn, rw_split,
            n_tiles=n_tiles, n_lat_tiles=n_lat_tiles, tiles_per_seq=tiles_per_seq, n_batch=bsz)
        next_f32 = None if last else (expert_rows, layer_rows, l + 1)
        y_moe, next_bf16 = _moe(h2, logits, bias_col, weights, next_f32)
        if not last:
            weights = as_experts(next_bf16)
        x_all = x_mid

    out = _final(x_all, y_moe, mod5, depth - 1, g_final.reshape(1, d), tiles_per_seq=tiles_per_seq)
    return out.reshape(bsz, seq, d)
```

```python
import functools

import numpy as np
import jax
import jax.numpy as jnp
from jax import lax
from jax.experimental import pallas as pl
from jax.experimental.pallas import tpu as pltpu
from jax.experimental.pallas import tpu_sc as plsc

GRID_W = 64
EPS = 1e-6
GLA_HEADS, GLA_DK, GLA_DV = 4, 48, 96
GLA_QK = GLA_HEADS * GLA_DK
GLA_WIDTH = GLA_HEADS * GLA_DV
GLA_LOWRANK = 16
GLA_TAU = 16.0
ROPE_BASE = 10000.0
NA_HEADS, NA_DH = 6, 64
NA_WIDTH = NA_HEADS * NA_DH
NA_KH, NA_KW = 8, 16
CONV_WIDTH, CONV_K = 256, 31
N_EXPERTS, N_GROUPS, EXPERTS_PER_GROUP = 16, 4, 4
PAIRS = ((0, 1), (0, 2), (0, 3), (1, 2), (1, 3), (2, 3))
N_CLASSES = N_GROUPS * len(PAIRS)

LANE = 128
SUBLANES = 8
QK_PAD = 256
C_Q, C_K = 0, 256
C_A, C_V, C_GATE = 512, 640, 1024
C_NQ, C_NK, C_NV, C_CONV = 1408, 1792, 2176, 2560
IN_COLS_PAD = 3072
MXU_N = 256

TOK_TILE = 1024
ROW_PARTS = 4
GLA_CHUNK = 64
GLA_SUB = 16
GLA_STEPS = 4
MOE_BLK = 256
NA_UNROLL = 8
NEG_BIG = -1e30
VMEM_PER_CORE = 64 * 1024 * 1024
VMEM_LIMIT = VMEM_PER_CORE - 8 * 1024 * 1024

F32 = jnp.float32
BF16 = jnp.bfloat16


def _cparams(sem):
    return pltpu.CompilerParams(dimension_semantics=sem, vmem_limit_bytes=VMEM_LIMIT)


def _cast_specs(next_f32, n_steps, step_of):
    stacked, layer_rows, layer = next_f32
    cast_steps = 1 << (n_steps.bit_length() - 1)
    cast_rows = layer_rows // cast_steps
    assert cast_rows * cast_steps == layer_rows and cast_rows % 16 == 0
    step = lambda *g: jnp.minimum(step_of(*g), cast_steps - 1)
    in_specs = [pl.BlockSpec((cast_rows, a.shape[1]), lambda *g: (layer * cast_steps + step(*g), 0)) for a in stacked]
    out_specs = [pl.BlockSpec((cast_rows, a.shape[1]), lambda *g: (step(*g), 0)) for a in stacked]
    out_shape = [jax.ShapeDtypeStruct((layer_rows, a.shape[1]), BF16) for a in stacked]
    return in_specs, out_specs, out_shape


def _cast_slices(srcs, dsts):
    for src, dst in zip(srcs, dsts):
        dst[...] = src[...].astype(BF16)


def _layer_spec(a, layer):
    return pl.BlockSpec((None,) + a.shape[1:], lambda *_: (layer,) + (0,) * (a.ndim - 1),
                        pipeline_mode=pl.Buffered(1))


def _dot(a, b):
    return jnp.dot(a, b, preferred_element_type=F32)


def _dot_nt(a, b):
    return lax.dot_general(a, b, (((1,), (1,)), ((), ())), preferred_element_type=F32)


def _split(a):
    hi = a.astype(BF16)
    lo = (a - hi.astype(F32)).astype(BF16)
    return hi, lo


def _sigmoid(x):
    return 1.0 / (1.0 + jnp.exp(-x))


SUB_X, SUB_Y = 5, 4
SC_WINDOW = 256


def _pack_bf16_pairs(a, b):
    ua = lax.bitcast_convert_type(a.astype(BF16).astype(F32), jnp.uint32)
    ub = lax.bitcast_convert_type(b.astype(BF16).astype(F32), jnp.uint32)
    return jnp.bitwise_or(jnp.right_shift(ua, jnp.uint32(16)), ub)


def _unpack_bf16_pairs(w):
    lo = lax.bitcast_convert_type(jnp.left_shift(w, jnp.uint32(16)), F32)
    hi = lax.bitcast_convert_type(jnp.bitwise_and(w, jnp.uint32(0xFFFF0000)), F32)
    return lo, hi


def _store_subrows(ref, val, j0, row0=0):
    r = val.shape[0]
    for j in range(val.shape[1] // LANE):
        ref[row0 // 8:(row0 + r) // 8, j0 + j, :, :] = val[:, j * LANE:(j + 1) * LANE].reshape(r // 8, 8, LANE)


def _load_subrows(ref, j0, n, row0=0, rows=None):
    r = ref.shape[0] * 8 if rows is None else rows
    return jnp.concatenate([ref[row0 // 8:(row0 + r) // 8, j0 + j, :, :].reshape(r, LANE) for j in range(n)],
                           axis=1)


def _pack_rows(x):
    half = x.shape[1] // 2
    return _pack_bf16_pairs(x[:, :half], x[:, half:])


def _unpack_rows(w):
    lo, hi = _unpack_bf16_pairs(w)
    return jnp.concatenate([lo, hi], axis=1)


def _sc_mesh():
    return plsc.VectorSubcoreMesh(core_axis_name="c", subcore_axis_name="s")


def _sc_scatter(src, idx, n_out):
    n, w = src.shape
    idx2 = idx.reshape(1, n)

    @pl.kernel(out_type=jax.ShapeDtypeStruct((n_out, w), src.dtype), mesh=_sc_mesh(), scratch_types=[])
    def scatter_kernel(x_hbm, i_hbm, o_hbm):
        def body(x_vmem, i_vmem):
            pltpu.sync_copy(x_vmem, o_hbm.at[i_vmem.at[0]])

        pltpu.emit_pipeline(
            body,
            grid=(n // SC_WINDOW,),
            in_specs=[pl.BlockSpec((SC_WINDOW, w), index_map=lambda i: (i, 0)),
                      pl.BlockSpec((1, SC_WINDOW), index_map=lambda i: (0, i))],
            out_specs=[],
            core_axis_name=("c", "s"),
            dimension_semantics=(pltpu.PARALLEL,),
        )(x_hbm, i_hbm)

    return scatter_kernel(src, idx2)


def _sc_gather(src, idx):
    n = idx.shape[0]
    w = src.shape[1]
    idx2 = idx.reshape(1, n)

    @pl.kernel(out_type=jax.ShapeDtypeStruct((n, w), src.dtype), mesh=_sc_mesh())
    def gather_kernel(x_hbm, i_hbm, o_hbm):
        def body(i_vmem, o_vmem):
            pltpu.sync_copy(x_hbm.at[i_vmem.at[0]], o_vmem)

        pltpu.emit_pipeline(
            body,
            grid=(n // SC_WINDOW,),
            in_specs=[pl.BlockSpec((1, SC_WINDOW), index_map=lambda i: (0, i))],
            out_specs=[pl.BlockSpec((SC_WINDOW, w), index_map=lambda i: (i, 0))],
            core_axis_name=("c", "s"),
            dimension_semantics=(pltpu.PARALLEL,),
        )(i_hbm, o_hbm)

    return gather_kernel(src, idx2)


def _mod_kernel(c_ref, w_ref, b_ref, o_ref):
    cv = c_ref[...]
    s = cv * _sigmoid(cv)
    s_hi, s_lo = _split(s)
    w_hi, w_lo = _split(w_ref[...])
    o_ref[...] = _dot(s_hi, w_hi) + _dot(s_lo, w_hi) + _dot(s_hi, w_lo) + b_ref[...]


def _modulation(c_pad, w_mod, b_mod):
    depth, d, six_d = w_mod.shape
    rows = c_pad.shape[0]
    nt = 1536
    return pl.pallas_call(
        _mod_kernel,
        grid=(depth, six_d // nt),
        in_specs=[
            pl.BlockSpec((rows, d), lambda l, j: (0, 0)),
            pl.BlockSpec((None, d, nt), lambda l, j: (l, 0, j)),
            pl.BlockSpec((None, 1, nt), lambda l, j: (l, 0, j)),
        ],
        out_specs=pl.BlockSpec((None, rows, nt), lambda l, j: (l, 0, j)),
        out_shape=jax.ShapeDtypeStruct((depth, rows, six_d), F32),
        compiler_params=_cparams(("arbitrary", "arbitrary")),
        name="modulation",
    )(c_pad, w_mod, b_mod.reshape(depth, 1, six_d))


def _inproj_kernel(*refs, has_res, n_lat_tiles):
    if has_res:
        x_ref, y_ref, g5_ref = refs[:3]
    else:
        x_ref, xctx_ref = refs[:2]
        is_lat = pl.program_id(0) < n_lat_tiles
    refs = refs[3 if has_res else 2:]
    (sh_ref, sc_ref, gmix_ref, cos_ref, sin_ref, w_ref, wa_ref, ba_ref) = refs[:8]
    (xnew_ref, q_ref, k_ref, v_ref, sg_ref, gf_ref, gb_ref, nq_ref, nk_ref, nv_ref, hcv_ref) = refs[8:]

    tile = x_ref.shape[0]
    part = tile // ROW_PARTS
    wa_hi, wa_lo = _split(wa_ref[...])
    even_lane = jnp.bitwise_and(lax.broadcasted_iota(jnp.int32, (1, LANE), 1), 1) == 0
    hbs = []
    for p in range(ROW_PARTS):
        rows = pl.ds(p * part, part)
        if has_res:
            x = x_ref[rows, :] + g5_ref[...] * _unpack_rows(_load_subrows(y_ref, 0, SUB_Y, row0=p * part, rows=part))
        else:
            x = jnp.where(is_lat, x_ref[rows, :], xctx_ref[rows, :])
        xnew_ref[rows, :] = x
        ms = jnp.mean(x * x, axis=-1, keepdims=True)
        h = x * lax.rsqrt(ms + EPS) * gmix_ref[...]
        h = h * (1.0 + sc_ref[...]) + sh_ref[...]
        hbs.append(h.astype(BF16))

    for p, hb in enumerate(hbs):
        rows = pl.ds(p * part, part)

        def proj(c0, n, hb=hb):
            return _dot(hb, w_ref[:, c0:c0 + n])

        cos = cos_ref[rows, :]
        sin = sin_ref[rows, :]

        def rope(t):
            halves = []
            for c0 in range(0, QK_PAD, LANE):
                th = t[:, c0:c0 + LANE]
                halves.append(jnp.where(even_lane, pltpu.roll(th, LANE - 1, 1), pltpu.roll(th, 1, 1)))
            return t * cos + jnp.concatenate(halves, axis=1) * sin

        mid = (C_A + C_CONV) // 2
        assert (mid - C_A) % MXU_N == 0 and C_NQ < mid < C_NK
        d1 = proj(C_A, mid - C_A)
        a_hi, a_lo = _split(d1[:, :LANE])
        q_ref[rows, :] = (rope(proj(C_Q, QK_PAD)) * (GLA_DK ** -0.5)).astype(BF16)
        k_ref[rows, :] = rope(proj(C_K, QK_PAD)).astype(BF16)

        z = _dot(a_hi, wa_hi) + _dot(a_lo, wa_hi) + _dot(a_hi, wa_lo) + ba_ref[...]
        logsig = jnp.minimum(z, 0.0) - jnp.log(1.0 + jnp.exp(-jnp.abs(z)))
        g = logsig * (1.0 / GLA_TAU)
        gf_ref[rows, :] = g[:, :QK_PAD]
        gb_ref[rows, :] = g[:, QK_PAD:]

        v_ref[rows, :] = d1[:, C_V - C_A:C_GATE - C_A].astype(BF16)
        gate = d1[:, C_GATE - C_A:C_NQ - C_A]
        sg_ref[rows, :] = (gate * _sigmoid(gate)).astype(BF16)
        d2 = proj(mid, C_CONV - mid)
        nq = jnp.concatenate([d1[:, C_NQ - C_A:], d2[:, :C_NK - mid]], axis=1)
        nq_ref[rows, :] = (nq * (NA_DH ** -0.5)).astype(BF16)
        nk_ref[rows, :] = d2[:, C_NK - mid:C_NV - mid].astype(BF16)
        nv_ref[rows, :] = d2[:, C_NV - mid:].astype(BF16)
        u = proj(C_CONV, 2 * CONV_WIDTH)
        hcv_ref[rows, :] = u[:, :CONV_WIDTH] * _sigmoid(u[:, CONV_WIDTH:])


def _inproj(x_all, y, mod5, layer, gmix, cos_t, sin_t, w_aug, wa_aug, ba_aug, *, n_lat_tiles, tiles_per_seq,
            n_batch, res_layer):
    has_res = y is not None
    d = x_all.shape[1] if has_res else x_all[0].shape[1]
    t_all = x_all.shape[0] if has_res else x_all[0].shape[0] + x_all[1].shape[0]
    nt = t_all // TOK_TILE

    def bidx(i):
        return jnp.where(i < n_lat_tiles, i // tiles_per_seq, n_batch)

    def ridx(i):
        return jnp.where(i < n_lat_tiles, i % tiles_per_seq, tiles_per_seq)

    def modspec(l, j):
        return pl.BlockSpec((None, None, None, 1, d), lambda i: (l, bidx(i), j, 0, 0))

    tok = lambda w: pl.BlockSpec((TOK_TILE, w), lambda i: (i, 0))
    full = lambda a: pl.BlockSpec(a.shape, lambda i: (0,) * a.ndim)

    if has_res:
        in_specs = [tok(d), pl.BlockSpec((TOK_TILE // 8, SUB_Y, 8, LANE), lambda i: (i, 0, 0, 0)),
                    modspec(res_layer, 5)]
        args = [x_all, y, mod5]
    else:
        in_specs = [pl.BlockSpec((TOK_TILE, d), lambda i: (jnp.minimum(i, n_lat_tiles - 1), 0)),
                    pl.BlockSpec((TOK_TILE, d), lambda i: (jnp.maximum(i - n_lat_tiles, 0), 0))]
        args = list(x_all)
    in_specs += [modspec(layer, 0), modspec(layer, 1), _layer_spec(gmix, layer),
                 pl.BlockSpec((TOK_TILE, QK_PAD), lambda i: (ridx(i), 0)),
                 pl.BlockSpec((TOK_TILE, QK_PAD), lambda i: (ridx(i), 0)),
                 _layer_spec(w_aug, layer), _layer_spec(wa_aug, layer), _layer_spec(ba_aug, layer)]
    args += [mod5, mod5, gmix, cos_t, sin_t, w_aug, wa_aug, ba_aug]

    out_widths = [(d, F32), (QK_PAD, BF16), (QK_PAD, BF16), (GLA_WIDTH, BF16), (GLA_WIDTH, BF16), (QK_PAD, F32),
                  (QK_PAD, F32), (NA_WIDTH, BF16), (NA_WIDTH, BF16), (NA_WIDTH, BF16), (CONV_WIDTH, F32)]
    out_specs = [tok(w) for w, _ in out_widths]
    out_shape = [jax.ShapeDtypeStruct((t_all, w), dt) for w, dt in out_widths]
    res = pl.pallas_call(
        functools.partial(_inproj_kernel, has_res=has_res, n_lat_tiles=n_lat_tiles),
        grid=(nt,),
        in_specs=in_specs,
        out_specs=out_specs,
        out_shape=out_shape,
        compiler_params=_cparams(("parallel",)),
        name="inproj",
    )(*args)
    return res[0], res[1:]


def _gla_masks():
    c, sub = GLA_CHUNK, GLA_SUB
    lane_qk = lax.broadcasted_iota(jnp.int32, (1, QK_PAD), 1)
    head_qk = ((lane_qk >= GLA_DK).astype(jnp.int32) + (lane_qk >= 2 * GLA_DK).astype(jnp.int32)
               + (lane_qk >= 3 * GLA_DK).astype(jnp.int32) + 4 * (lane_qk >= 4 * GLA_DK).astype(jnp.int32))
    row_h = jnp.right_shift(lax.broadcasted_iota(jnp.int32, (c, 1), 0), GLA_SUB.bit_length() - 1)
    hm = (row_h == head_qk).astype(F32)
    row_v = lax.broadcasted_iota(jnp.int32, (GLA_WIDTH, 1), 0)
    head_v = ((row_v >= GLA_DV).astype(jnp.int32) + (row_v >= 2 * GLA_DV).astype(jnp.int32)
              + (row_v >= 3 * GLA_DV).astype(jnp.int32))
    bd = (head_v == head_qk).astype(F32)
    lane_v = lax.broadcasted_iota(jnp.int32, (1, GLA_WIDTH), 1)
    vm = [((lane_v >= h * GLA_DV) & (lane_v < (h + 1) * GLA_DV)).astype(F32) for h in range(GLA_HEADS)]
    return hm, bd, vm


def _gla_steps(q, k, v, g, s_t, hm, bd, vm):
    c, sub = GLA_CHUNK, GLA_SUB
    nsub = c // sub
    dirs = (True, False)
    items = [(u, d) for u in range(GLA_STEPS) for d in range(2)]
    ri = lax.broadcasted_iota(jnp.int32, (c, c), 0)
    ci = lax.broadcasted_iota(jnp.int32, (c, c), 1)
    key_row = lax.broadcasted_iota(jnp.int32, (c, 1), 0)
    att_row = jnp.bitwise_and(ri, sub - 1)
    tri = [((ci <= ri) if fwd else (ci >= ri)).astype(BF16) for fwd in dirs]

    cums = {}
    for u, d in items:
        g_hi, g_lo = _split(g[u][d])
        cums[u, d] = _dot(tri[d], g_hi) + _dot(tri[d], g_lo)

    qe, kv, decay, atts = {}, {}, {}, {}
    for u, d in items:
        fwd = dirs[d]
        cum = cums[u, d]
        tot = cum[c - 1:c] if fwd else cum[0:1]
        qe[u, d] = (q[u][d] * jnp.exp(cum)).astype(BF16)
        k_end = (k[u][d] * jnp.exp(tot - cum)).astype(BF16)
        kv[u, d] = lax.dot_general(v[u][d], k_end, (((0,), (0,)), ((), ())), preferred_element_type=F32)
        decay[u, d] = jnp.exp(tot)
        att_d = []
        for i in range(nsub):
            lo, hi = i * sub, (i + 1) * sub
            if fwd:
                ref = cum[lo - 1:lo] if i > 0 else jnp.zeros((1, QK_PAD), F32)
                key_ok = key_row < hi
                causal = ci <= att_row + lo
            else:
                ref = cum[hi:hi + 1] if i < nsub - 1 else jnp.zeros((1, QK_PAD), F32)
                key_ok = key_row >= lo
                causal = ci >= att_row + lo
            qi = q[u][d][lo:hi] * jnp.exp(cum[lo:hi] - ref)
            qs = (jnp.concatenate([qi] * GLA_HEADS, axis=0) * hm).astype(BF16)
            ki = (k[u][d] * jnp.exp(jnp.where(key_ok, ref - cum, NEG_BIG))).astype(BF16)
            att = _dot_nt(qs, ki)
            att_d.append(jnp.where(causal, att, 0.0).astype(BF16))
        atts[u, d] = jnp.concatenate(att_d, axis=0)

    s = list(s_t)
    o_inter = {}
    for u, d in items:
        o_inter[u, d] = _dot_nt(qe[u, d], s[d].astype(BF16))
        s[d] = s[d] * decay[u, d] + bd * kv[u, d]

    outs = [[None, None] for _ in range(GLA_STEPS)]
    for u, d in items:
        r = _dot(atts[u, d], v[u][d])
        blocks = []
        for i in range(nsub):
            base = i * c
            oi = r[base:base + sub] * vm[0]
            for h in range(1, GLA_HEADS):
                oi = oi + r[base + h * sub:base + (h + 1) * sub] * vm[h]
            blocks.append(oi)
        outs[u][d] = o_inter[u, d] + jnp.concatenate(blocks, axis=0)
    return outs, s


def _gla_kernel(q_ref, k_ref, v_ref, gf_ref, gb_ref, sf0_ref, sb0_ref, of_ref, ob_ref, sf_ref, sb_ref):
    n = q_ref.shape[0]
    nc = n // GLA_CHUNK
    hm, bd, vm = _gla_masks()
    sf_ref[...] = sf0_ref[...]
    sb_ref[...] = sb0_ref[...]

    def body(j, carry):
        rows = [[pl.ds(pl.multiple_of(cidx * GLA_CHUNK, GLA_CHUNK), GLA_CHUNK)
                 for cidx in (j * GLA_STEPS + u, nc - 1 - (j * GLA_STEPS + u))] for u in range(GLA_STEPS)]
        q = [[q_ref[r, :].astype(F32) for r in ru] for ru in rows]
        k = [[k_ref[r, :].astype(F32) for r in ru] for ru in rows]
        v = [[v_ref[r, :] for r in ru] for ru in rows]
        g = [[gf_ref[ru[0], :], gb_ref[ru[1], :]] for ru in rows]
        outs, s_new = _gla_steps(q, k, v, g, [sf_ref[...], sb_ref[...]], hm, bd, vm)
        sf_ref[...] = s_new[0]
        sb_ref[...] = s_new[1]
        for u in range(GLA_STEPS):
            of_ref[rows[u][0], :] = outs[u][0]
            ob_ref[rows[u][1], :] = outs[u][1]
        return carry

    lax.fori_loop(0, nc // GLA_STEPS, body, 0)


def _gla(q, k, v, gf, gb, sf0, sb0, o_prev, *, seq, blk0, n_batch):
    t_all = q.shape[0]
    tokw = lambda w: pl.BlockSpec((seq, w), lambda b: (blk0 + b, 0))
    st = pl.BlockSpec((None, GLA_WIDTH, QK_PAD), lambda b: (b, 0, 0))
    in_specs = [tokw(QK_PAD), tokw(QK_PAD), tokw(GLA_WIDTH), tokw(QK_PAD), tokw(QK_PAD), st, st]
    args = [q, k, v, gf, gb, sf0, sb0]
    aliases = {}
    n_in = len(args)
    if o_prev is not None:
        in_specs += [pl.BlockSpec(memory_space=pl.ANY)] * 2
        args += list(o_prev)
        aliases = {n_in: 0, n_in + 1: 1}

    def kern(*refs):
        _gla_kernel(*refs[:n_in], *refs[len(args):])

    st_shape = jax.ShapeDtypeStruct((n_batch, GLA_WIDTH, QK_PAD), F32)
    o_shape = jax.ShapeDtypeStruct((t_all, GLA_WIDTH), F32)
    o_f, o_b, s_f, s_b = pl.pallas_call(
        kern,
        grid=(n_batch,),
        in_specs=in_specs,
        out_specs=[tokw(GLA_WIDTH), tokw(GLA_WIDTH), st, st],
        out_shape=[o_shape, o_shape, st_shape, st_shape],
        input_output_aliases=aliases,
        compiler_params=_cparams(("parallel",)),
        name="gla",
    )(*args)
    return (o_f, o_b), s_f, s_b


def _na_kernel(q_ref, k_ref, v_ref, kc_ref, vc_ref, bias_ref, *rest, n_cast):
    cast_in, (o_ref, *cast_out) = rest[:n_cast], rest[n_cast:]
    _cast_slices(cast_in, cast_out)
    n = q_ref.shape[0]
    rows = n // GRID_W
    nkeys = NA_KH * GRID_W
    lane = lax.broadcasted_iota(jnp.int32, (1, LANE), 1)
    first = lane < NA_DH
    kc = kc_ref[...]
    vc = vc_ref[...]

    def body(jb, carry):
        items = []
        for j in range(NA_UNROLL):
            r = jb * NA_UNROLL + j
            r0 = jnp.clip(r - NA_KH // 2, 0, rows - NA_KH)
            var = r - r0
            qrows = pl.ds(pl.multiple_of(r * GRID_W, GRID_W), GRID_W)
            krows = pl.ds(pl.multiple_of(r0 * GRID_W, GRID_W), nkeys)
            qr = q_ref[qrows, :]
            kb = k_ref[krows, :]
            q2 = jnp.concatenate([jnp.where(first, qr, jnp.zeros_like(qr)),
                                  jnp.where(first, jnp.zeros_like(qr), qr)], axis=0)
            items.append((qrows, krows, var, _dot_nt(q2, kb), _dot_nt(q2, kc)))
        probs = []
        for qrows, krows, var, s_loc, s_ctx in items:
            s_loc = s_loc + jnp.concatenate([bias_ref[0, var], bias_ref[1, var]], axis=0)
            m = jnp.maximum(jnp.max(s_loc, axis=-1, keepdims=True), jnp.max(s_ctx, axis=-1, keepdims=True))
            p_loc = jnp.exp(s_loc - m)
            p_ctx = jnp.exp(s_ctx - m)
            l = jnp.sum(p_loc, axis=-1, keepdims=True) + jnp.sum(p_ctx, axis=-1, keepdims=True)
            probs.append((p_loc.astype(BF16), p_ctx.astype(BF16), l))
        for (qrows, krows, var, _, _), (p_loc, p_ctx, l) in zip(items, probs):
            o = (_dot(p_loc, v_ref[krows, :]) + _dot(p_ctx, vc)) / l
            o_ref[qrows, :] = jnp.where(first, o[:GRID_W], o[GRID_W:]).astype(o_ref.dtype)
        return carry

    lax.fori_loop(0, rows // NA_UNROLL, body, 0)


def _na_latent(nq, nk, nv, bias_tab, layer, cast_f32=None, *, seq, ctx_len, n_batch, ctx_blk0):
    t_all = nq.shape[0]
    npair = NA_HEADS // 2
    lat = pl.BlockSpec((seq, LANE), lambda b, p: (b, p))
    ctx = pl.BlockSpec((ctx_len, LANE), lambda b, p: (ctx_blk0 + b, p))
    bias = pl.BlockSpec((None, 2) + bias_tab.shape[2:], lambda b, p: (layer, p, 0, 0, 0))
    in_specs = [lat, lat, lat, ctx, ctx, bias]
    out_specs = [lat]
    out_shape = [jax.ShapeDtypeStruct((t_all, NA_WIDTH), BF16)]
    args = [nq, nk, nv, nk, nv, bias_tab]
    if cast_f32 is not None:
        c_in, c_out, c_shape = _cast_specs(cast_f32, n_batch * npair, lambda b, p: b * npair + p)
        in_specs += c_in
        out_specs += c_out
        out_shape += c_shape
        args += list(cast_f32[0])
    res = pl.pallas_call(
        functools.partial(_na_kernel, n_cast=0 if cast_f32 is None else len(cast_f32[0])),
        grid=(n_batch, npair),
        in_specs=in_specs,
        out_specs=out_specs,
        out_shape=out_shape,
        compiler_params=_cparams(("arbitrary", "arbitrary")),
        name="na_latent",
    )(*args)
    return res[0], tuple(res[1:])


def _na_ctx_kernel(q_ref, k_ref, v_ref, o_in_ref, o_ref):
    del o_in_ref
    lane = lax.broadcasted_iota(jnp.int32, (1, LANE), 1)
    first = lane < NA_DH
    q = q_ref[...]
    k = k_ref[...]
    v = v_ref[...]
    res = []
    for h in range(2):
        sel = first if h == 0 else jnp.logical_not(first)
        qh = jnp.where(sel, q, jnp.zeros_like(q))
        s = _dot_nt(qh, k)
        m = jnp.max(s, axis=-1, keepdims=True)
        p = jnp.exp(s - m)
        l = jnp.sum(p, axis=-1, keepdims=True)
        res.append(_dot(p.astype(BF16), v) / l)
    o_ref[...] = jnp.where(first, res[0], res[1]).astype(o_ref.dtype)


def _na_context(nq, nk, nv, o_prev, *, ctx_len, n_batch, ctx_blk0):
    npair = NA_HEADS // 2
    ctx = pl.BlockSpec((ctx_len, LANE), lambda b, p: (ctx_blk0 + b, p))
    return pl.pallas_call(
        _na_ctx_kernel,
        grid=(n_batch, npair),
        in_specs=[ctx, ctx, ctx, pl.BlockSpec(memory_space=pl.ANY)],
        out_specs=ctx,
        out_shape=jax.ShapeDtypeStruct(o_prev.shape, o_prev.dtype),
        input_output_aliases={3: 0},
        compiler_params=_cparams(("parallel", "arbitrary")),
        name="na_context",
    )(nq, nk, nv, o_prev)


CONV_HALO = 16
CONV_ROWS = 128


def _conv_kernel(*refs, has_prev):
    if has_prev:
        h_ref, w_ref, b_ref, lg_ref, lb_ref, _, o_ref, pad_ref, sh_ref = refs
    else:
        h_ref, w_ref, b_ref, lg_ref, lb_ref, o_ref, pad_ref, sh_ref = refs
    n = h_ref.shape[0]
    zeros = jnp.zeros((CONV_HALO, CONV_WIDTH), F32)
    pad_ref[0:CONV_HALO, :] = zeros
    pad_ref[CONV_HALO + n:CONV_HALO + n + CONV_HALO, :] = zeros
    pad_ref[CONV_HALO:CONV_HALO + n, :] = h_ref[...]
    span = n + 2 * CONV_HALO - SUBLANES
    for s in range(1, SUBLANES):
        sh_ref[s - 1, 0:span, :] = pad_ref[s:s + span, :]
    w = w_ref[...]
    off = CONV_HALO - CONV_K // 2

    def chunk(cidx, carry):
        base = pl.multiple_of(cidx * CONV_ROWS, CONV_ROWS)
        acc = jnp.zeros((CONV_ROWS, CONV_WIDTH), F32) + b_ref[...]
        for j in range(CONV_K):
            s = (off + j) % SUBLANES
            win = pl.ds(base + (off + j - s), CONV_ROWS)
            tap = pad_ref[win, :] if s == 0 else sh_ref[s - 1, win, :]
            acc = acc + tap * w[j:j + 1, :]
        mu = jnp.mean(acc, axis=-1, keepdims=True)
        xc = acc - mu
        var = jnp.mean(xc * xc, axis=-1, keepdims=True)
        y = xc * lax.rsqrt(var + EPS) * lg_ref[...] + lb_ref[...]
        o_ref[pl.ds(base, CONV_ROWS), :] = (y * _sigmoid(y)).astype(o_ref.dtype)
        return carry

    lax.fori_loop(0, n // CONV_ROWS, chunk, 0)


def _conv(hcv, w, b, lg, lb, layer, o_prev, *, seq, blk0, n_batch):
    t_all = hcv.shape[0]
    tok = pl.BlockSpec((seq, CONV_WIDTH), lambda i: (blk0 + i, 0))
    full = lambda a: pl.BlockSpec(a.shape, lambda i: (0,) * a.ndim)
    in_specs = [tok] + [_layer_spec(a, layer) for a in (w, b, lg, lb)]
    args = [hcv, w, b, lg, lb]
    aliases = {}
    if o_prev is not None:
        in_specs.append(pl.BlockSpec(memory_space=pl.ANY))
        args.append(o_prev)
        aliases = {5: 0}
    return pl.pallas_call(
        functools.partial(_conv_kernel, has_prev=o_prev is not None),
        grid=(n_batch,),
        in_specs=in_specs,
        out_specs=tok,
        out_shape=jax.ShapeDtypeStruct((t_all, CONV_WIDTH), BF16),
        scratch_shapes=[pltpu.VMEM((seq + 2 * CONV_HALO, CONV_WIDTH), F32),
                        pltpu.VMEM((SUBLANES - 1, seq + 2 * CONV_HALO, CONV_WIDTH), F32)],
        input_output_aliases=aliases,
        compiler_params=_cparams(("parallel",)),
        name="conv",
    )(*args)


def _outproj_kernel(ogf_ref, ogb_ref, sg_ref, on_ref, oc_ref, x_ref, g2_ref, sh_ref, sc_ref, gn_ref, e_ref,
                    wo_ref, gffn_ref, rw_ref, rbias_ref, xmid_ref, h2_ref, meta_ref, cnt_ref, carry_ref):
    tile = x_ref.shape[0]
    part = tile // ROW_PARTS
    parts = [pl.ds(p * part, part) for p in range(ROW_PARTS)]
    e = e_ref[...]
    rw = rw_ref[...]

    @pl.when(pl.program_id(0) == 0)
    def _():
        carry_ref[...] = jnp.zeros_like(carry_ref)

    carry = carry_ref[...]
    meta_ref[...] = jnp.zeros_like(meta_ref)

    ofs, mss = [], []
    for rows in parts:
        of = ogf_ref[rows, :] + ogb_ref[rows, :]
        sq_hi, sq_lo = _split(of * of)
        ofs.append(of)
        mss.append(_dot(jnp.concatenate([sq_hi, sq_lo], axis=1), e) * (1.0 / GLA_DV))
    ys = []
    for rows, of, ms in zip(parts, ofs, mss):
        og = of * lax.rsqrt(ms + EPS) * gn_ref[...] * sg_ref[rows, :].astype(F32)
        mix = jnp.concatenate([og.astype(BF16), on_ref[rows, :], oc_ref[rows, :]], axis=1)
        ys.append(_dot(mix, wo_ref[...]))
    for p, (rows, y) in enumerate(zip(parts, ys)):
        x = x_ref[rows, :] + g2_ref[...] * y
        xmid_ref[rows, :] = x
        ms2 = jnp.mean(x * x, axis=-1, keepdims=True)
        h2 = x * lax.rsqrt(ms2 + EPS) * gffn_ref[...]
        h2 = h2 * (1.0 + sc_ref[...]) + sh_ref[...]
        h_hi, h_lo = _split(h2)
        _store_subrows(h2_ref, _pack_rows(h2), 0, row0=p * part)
        hw = _dot(h_hi, rw)
        logits = hw[:, :LANE] + hw[:, LANE:] + _dot(h_lo, rw[:, :LANE])
        cls, rank, wtok, carry = _route_tokens(logits, rbias_ref[...], carry)
        meta_ref[0:1, rows] = cls
        meta_ref[1:2, rows] = rank
        _store_subrows(h2_ref, wtok, SUB_Y, row0=p * part)
    carry_ref[...] = carry
    cnt_ref[...] = jnp.broadcast_to(carry, cnt_ref.shape)


def _outproj(o_gf, o_gb, sgate, o_n, o_c, x_all, mod5, layer, gnorm, e_mat, w_out, gffn, rw_split, bias_col, *,
             n_tiles, n_lat_tiles, tiles_per_seq, n_batch):
    d = x_all.shape[1]
    t_out = n_tiles * TOK_TILE

    def bidx(i):
        return jnp.where(i < n_lat_tiles, i // tiles_per_seq, n_batch)

    def modspec(j):
        return pl.BlockSpec((None, None, None, 1, d), lambda i: (layer, bidx(i), j, 0, 0))

    tok = lambda w: pl.BlockSpec((TOK_TILE, w), lambda i: (i, 0))
    full = lambda a: pl.BlockSpec(a.shape, lambda i: (0,) * a.ndim)
    return pl.pallas_call(
        _outproj_kernel,
        grid=(n_tiles,),
        in_specs=[tok(GLA_WIDTH), tok(GLA_WIDTH), tok(GLA_WIDTH), tok(NA_WIDTH), tok(CONV_WIDTH), tok(d),
                  modspec(2), modspec(3), modspec(4), _layer_spec(gnorm, layer), full(e_mat),
                  _layer_spec(w_out, layer), _layer_spec(gffn, layer),
                  full(rw_split), full(bias_col)],
        out_specs=[tok(d), pl.BlockSpec((TOK_TILE // 8, SUB_X, 8, LANE), lambda i: (i, 0, 0, 0)),
                   pl.BlockSpec((8, TOK_TILE), lambda i: (0, i)), pl.BlockSpec((32, LANE), lambda i: (0, 0))],
        out_shape=[jax.ShapeDtypeStruct((t_out, d), F32),
                   jax.ShapeDtypeStruct((t_out // 8, SUB_X, 8, LANE), jnp.uint32),
                   jax.ShapeDtypeStruct((8, t_out), F32), jax.ShapeDtypeStruct((32, LANE), F32)],
        scratch_shapes=[pltpu.VMEM((32, 1), F32)],
        compiler_params=_cparams(("arbitrary",)),
        name="outproj",
    )(o_gf, o_gb, sgate, o_n, o_c, x_all, mod5, mod5, mod5, gnorm, e_mat, w_out, gffn, rw_split, bias_col)


def _route_tokens(lg, bias, carry):
    tile = lg.shape[0]
    lt = lg.T
    aff = _sigmoid(lt[0:N_EXPERTS])
    sel = aff + bias
    s = [sel[e:e + 1] for e in range(N_EXPERTS)]
    a = [aff[e:e + 1] for e in range(N_EXPERTS)]

    def top2sum(v):
        best = v[0] + v[1]
        for i, j in PAIRS[1:]:
            best = jnp.maximum(best, v[i] + v[j])
        return best

    gs = [top2sum(s[4 * g:4 * g + 4]) for g in range(N_GROUPS)]
    gbest = jnp.zeros_like(gs[0], dtype=jnp.int32)
    gmax = gs[0]
    for g in range(1, N_GROUPS):
        upd = gs[g] > gmax
        gbest = jnp.where(upd, g, gbest)
        gmax = jnp.where(upd, gs[g], gmax)

    def pick(vals, j):
        out = vals[j]
        for g in range(1, N_GROUPS):
            out = jnp.where(gbest == g, vals[4 * g + j], out)
        return out

    sv = [pick(s, j) for j in range(EXPERTS_PER_GROUP)]
    av = [pick(a, j) for j in range(EXPERTS_PER_GROUP)]
    i1 = jnp.zeros_like(gbest)
    m1 = sv[0]
    for j in range(1, EXPERTS_PER_GROUP):
        upd = sv[j] > m1
        i1 = jnp.where(upd, j, i1)
        m1 = jnp.where(upd, sv[j], m1)
    i2 = jnp.full_like(gbest, -1)
    m2 = jnp.zeros_like(m1)
    for j in range(EXPERTS_PER_GROUP):
        upd = (i1 != j) & ((sv[j] > m2) | (i2 < 0))
        i2 = jnp.where(upd, j, i2)
        m2 = jnp.where(upd, sv[j], m2)
    ia = jnp.minimum(i1, i2)
    ib = jnp.maximum(i1, i2)
    pair = jnp.where(ia == 0, ib - 1, jnp.where(ia == 1, ib + 1, 5))
    cls = gbest * len(PAIRS) + pair

    def take(vals, idx):
        out = vals[0]
        for j in range(1, EXPERTS_PER_GROUP):
            out = jnp.where(idx == j, vals[j], out)
        return out

    w1 = take(av, i1)
    w2 = take(av, i2)
    tot = w1 + w2
    wa = jnp.where(i1 < i2, w1, w2) / tot
    wb = jnp.where(i1 < i2, w2, w1) / tot

    crow = lax.broadcasted_iota(jnp.int32, (32, tile), 0)
    oh = (crow == cls).astype(F32)
    us = lax.broadcasted_iota(jnp.int32, (tile, tile), 0)
    ut = lax.broadcasted_iota(jnp.int32, (tile, tile), 1)
    upper = (us < ut).astype(BF16)
    prefix = _dot(oh.astype(BF16), upper)
    rank = jnp.sum(oh * (prefix + carry), axis=0, keepdims=True)
    carry_new = carry + jnp.sum(oh, axis=1, keepdims=True)

    wrow = lax.broadcasted_iota(jnp.int32, (LANE, tile), 0)
    wmat = jnp.where(wrow == 0, wa, jnp.where(wrow == 1, wb, 0.0))
    wtok = lax.bitcast_convert_type(wmat.T, jnp.uint32)
    return cls.astype(F32), rank, wtok, carry_new


FF_TILE = 512


def _expert_kernel(ea_ref, eb_ref, nvalid_ref, xs_ref, wga_ref, wua_ref, wda_ref, wgb_ref, wub_ref, wdb_ref,
                   *rest, n_cast):
    del ea_ref, eb_ref
    j = pl.program_id(0)
    nvalid = nvalid_ref[j]
    nxt_in, (y_ref, *nxt_out) = rest[:n_cast], rest[n_cast:]

    def cast_next():
        _cast_slices(nxt_in, nxt_out)

    @pl.when(nvalid == 0)
    def _():
        cast_next()
        y_ref[...] = jnp.zeros_like(y_ref)

    @pl.when(nvalid != 0)
    def _():
        cast_next()
        rows = y_ref.shape[0] * 8
        live = lax.broadcasted_iota(jnp.int32, (rows, 1), 0) < nvalid
        x = jnp.where(live, _unpack_rows(_load_subrows(xs_ref, 0, SUB_Y)), 0.0).astype(BF16)
        ws = jnp.where(live, lax.bitcast_convert_type(xs_ref[:, SUB_Y, :, :].reshape(rows, LANE), F32), 0.0)
        ff = wga_ref.shape[1]

        items = [(w, f0) for w in ((wga_ref, wua_ref, wda_ref), (wgb_ref, wub_ref, wdb_ref))
                 for f0 in range(0, ff, FF_TILE)]

        def up(item):
            (wg_ref, wu_ref, _), f0 = item
            return _dot(x, wg_ref[:, f0:f0 + FF_TILE]), _dot(x, wu_ref[:, f0:f0 + FF_TILE])

        ups = [up(items[0]), up(items[1])]
        parts = []
        for c, ((_, _, wd_ref), f0) in enumerate(items):
            hg, hu = ups[c]
            hh = (hg * _sigmoid(hg) * hu).astype(BF16)
            if c + 2 < len(items):
                ups.append(up(items[c + 2]))
            parts.append(_dot(hh, wd_ref[f0:f0 + FF_TILE, :]))
        per = len(items) // 2
        ya = functools.reduce(lambda a, b: a + b, parts[:per])
        yb = functools.reduce(lambda a, b: a + b, parts[per:])
        _store_subrows(y_ref, _pack_rows(ya * ws[:, 0:1] + yb * ws[:, 1:2]), 0)


def _experts(xs_sub, ea, eb, nvalid, weights, next_f32):
    wg, wu, wd = weights
    nb = xs_sub.shape[0] * 8 // MOE_BLK
    d, ff = wg.shape[1], wg.shape[2]
    wspec_in = lambda which: pl.BlockSpec((None, d, ff), lambda j, ea, eb, v: ((ea, eb)[which][j], 0, 0))
    wspec_out = lambda which: pl.BlockSpec((None, ff, d), lambda j, ea, eb, v: ((ea, eb)[which][j], 0, 0))
    in_specs = [pl.BlockSpec((MOE_BLK // 8, SUB_X, 8, LANE), lambda j, ea, eb, v: (j, 0, 0, 0)),
                wspec_in(0), wspec_in(0), wspec_out(0), wspec_in(1), wspec_in(1), wspec_out(1)]
    out_specs = [pl.BlockSpec((MOE_BLK // 8, SUB_Y, 8, LANE), lambda j, ea, eb, v: (j, 0, 0, 0))]
    out_shape = [jax.ShapeDtypeStruct((nb * MOE_BLK // 8, SUB_Y, 8, LANE), jnp.uint32)]
    args = [ea, eb, nvalid, xs_sub, wg, wu, wd, wg, wu, wd]
    if next_f32 is not None:
        c_in, c_out, c_shape = _cast_specs(next_f32, nb, lambda j, *_: j)
        in_specs += c_in
        out_specs += c_out
        out_shape += c_shape
        args += list(next_f32[0])
    grid_spec = pltpu.PrefetchScalarGridSpec(num_scalar_prefetch=3, grid=(nb,), in_specs=in_specs,
                                             out_specs=out_specs)
    res = pl.pallas_call(
        functools.partial(_expert_kernel, n_cast=0 if next_f32 is None else len(next_f32[0])),
        grid_spec=grid_spec,
        out_shape=out_shape,
        compiler_params=_cparams(("arbitrary",)),
        name="experts",
    )(*args)
    return res[0], tuple(res[1:])


def _subrow_index(dest, nsub):
    t = dest.shape[0]
    base = ((dest // 8) * (nsub * 8) + dest % 8).astype(F32).reshape(t // LANE, LANE)
    src = np.arange(LANE)
    grp, r = src // 8, src % 8
    sel = np.zeros((LANE, LANE * nsub), np.float32)
    off = np.zeros((LANE * nsub,), np.int32)
    for j in range(nsub):
        pos = grp * (nsub * 8) + j * 8 + r
        sel[src, pos] = 1.0
        off[pos] = j * 8
    idx = jnp.dot(base, jnp.asarray(sel), precision=lax.Precision.HIGHEST).astype(jnp.int32) + jnp.asarray(off)
    return idx.reshape(t * nsub)


def _moe(h2_sub, meta, cnt, weights, next_f32):
    t = meta.shape[1]
    cls = meta[0].astype(jnp.int32)
    rank = meta[1].astype(jnp.int32)
    counts = cnt[:N_CLASSES, 0].astype(jnp.int32)
    padded = (counts + MOE_BLK - 1) // MOE_BLK * MOE_BLK
    pad_end = jnp.cumsum(padded)
    pad_start = pad_end - padded
    class_ids = jnp.arange(N_CLASSES, dtype=jnp.int32)
    dest = rank + jnp.sum(jnp.where(cls[:, None] == class_ids[None, :], pad_start[None, :], 0), axis=1)
    nb = t // MOE_BLK + N_CLASSES
    p_rows = nb * MOE_BLK
    blk_start = jnp.arange(nb, dtype=jnp.int32) * MOE_BLK
    valid = blk_start < pad_end[-1]
    blk_cls = jnp.sum((pad_end[None, :] <= blk_start[:, None]).astype(jnp.int32), axis=-1)
    last_cls = jnp.sum((pad_end <= pad_end[-1] - 1).astype(jnp.int32))
    blk_cls = jnp.minimum(jnp.where(valid, blk_cls, last_cls), N_CLASSES - 1)
    nvalid = jnp.where(valid, jnp.clip(pad_start[blk_cls] + counts[blk_cls] - blk_start, 0, MOE_BLK), 0)
    pair_a = jnp.array([p[0] for p in PAIRS], jnp.int32)
    pair_b = jnp.array([p[1] for p in PAIRS], jnp.int32)
    grp = blk_cls // len(PAIRS)
    ea = grp * EXPERTS_PER_GROUP + pair_a[blk_cls % len(PAIRS)]
    eb = grp * EXPERTS_PER_GROUP + pair_b[blk_cls % len(PAIRS)]

    xs = _sc_scatter(h2_sub.reshape(t * SUB_X, LANE), _subrow_index(dest, SUB_X), p_rows * SUB_X)
    ys, next_bf16 = _experts(xs.reshape(p_rows // 8, SUB_X, 8, LANE), ea, eb, nvalid.astype(jnp.int32), weights,
                             next_f32)
    y = _sc_gather(ys.reshape(p_rows * SUB_Y, LANE), _subrow_index(dest, SUB_Y))
    return y.reshape(t // 8, SUB_Y, 8, LANE), next_bf16


def _final_kernel(x_ref, y_ref, g5_ref, gf_ref, o_ref):
    x = x_ref[...] + g5_ref[...] * _unpack_rows(_load_subrows(y_ref, 0, SUB_Y))
    ms = jnp.mean(x * x, axis=-1, keepdims=True)
    o_ref[...] = x * lax.rsqrt(ms + EPS) * gf_ref[...]


def _final(x_mid, y, mod5, layer, g_final, *, tiles_per_seq):
    t, d = x_mid.shape
    tok = pl.BlockSpec((TOK_TILE, d), lambda i: (i, 0))
    return pl.pallas_call(
        _final_kernel,
        grid=(t // TOK_TILE,),
        in_specs=[tok, pl.BlockSpec((TOK_TILE // 8, SUB_Y, 8, LANE), lambda i: (i, 0, 0, 0)),
                  pl.BlockSpec((None, None, None, 1, d), lambda i: (layer, i // tiles_per_seq, 5, 0, 0)),
                  pl.BlockSpec(g_final.shape, lambda i: (0, 0))],
        out_specs=tok,
        out_shape=jax.ShapeDtypeStruct((t, d), F32),
        compiler_params=_cparams(("parallel",)),
        name="final_norm",
    )(x_mid, y, mod5, g_final)


def _rope_tables(seq):
    t = jnp.arange(seq)
    row = (t // GRID_W).astype(F32)
    col = (t % GRID_W).astype(F32)
    half = GLA_DK // 2
    inv = ROPE_BASE ** (-jnp.arange(0, half, 2, dtype=F32) / half)
    ang = jnp.concatenate([row[:, None] * inv, col[:, None] * inv], axis=-1)
    cos = jnp.repeat(jnp.cos(ang), 2, axis=-1)
    sin = (jnp.sin(ang)[:, :, None] * jnp.array([-1.0, 1.0], F32)).reshape(seq, GLA_DK)
    cos = jnp.tile(cos, (1, GLA_HEADS))
    sin = jnp.tile(sin, (1, GLA_HEADS))
    padw = QK_PAD - GLA_QK
    cos = jnp.pad(cos, ((0, 0), (0, padw)), constant_values=1.0)
    sin = jnp.pad(sin, ((0, 0), (0, padw)))
    cos = jnp.concatenate([cos, jnp.ones((TOK_TILE, QK_PAD), F32)], axis=0)
    sin = jnp.concatenate([sin, jnp.zeros((TOK_TILE, QK_PAD), F32)], axis=0)
    return cos, sin


def _pad_last(w, n):
    return jnp.pad(w, [(0, 0)] * (w.ndim - 1) + [(0, n - w.shape[-1])])


def _in_weights(w_in):
    offs = np.cumsum([0, GLA_QK, GLA_QK, GLA_WIDTH, GLA_WIDTH, 2 * GLA_LOWRANK, NA_WIDTH, NA_WIDTH, NA_WIDTH,
                      2 * CONV_WIDTH])
    seg = [w_in[..., offs[i]:offs[i + 1]] for i in range(9)]
    cols = [_pad_last(seg[0], QK_PAD), _pad_last(seg[1], QK_PAD), _pad_last(seg[4], LANE), seg[2], seg[3], seg[5],
            seg[6], seg[7], seg[8]]
    w = jnp.concatenate(cols, axis=-1).astype(BF16)
    assert w.shape[-1] == IN_COLS_PAD
    return w


def _gate_weights(wa_f, ba_f, wa_b, ba_b):
    depth = wa_f.shape[0]
    zero = jnp.zeros((depth, GLA_LOWRANK, QK_PAD), F32)
    top = jnp.concatenate([_pad_last(wa_f, QK_PAD), zero], axis=-1)
    bot = jnp.concatenate([zero, _pad_last(wa_b, QK_PAD)], axis=-1)
    rest = jnp.zeros((depth, LANE - 2 * GLA_LOWRANK, 2 * QK_PAD), F32)
    wa = jnp.concatenate([top, bot, rest], axis=1)
    ba = jnp.concatenate([_pad_last(ba_f, QK_PAD), _pad_last(ba_b, QK_PAD)], axis=-1)[:, None, :]
    return wa, ba


def _na_bias_tables(rpb):
    cq = np.arange(GRID_W)
    c0 = np.clip(cq - NA_KW // 2, 0, GRID_W - NA_KW)
    kc = np.arange(GRID_W)
    valid = (kc[None, :] >= c0[:, None]) & (kc[None, :] < c0[:, None] + NA_KW)
    w = GRID_W
    padded = jnp.pad(rpb, [(0, 0)] * 3 + [(w - NA_KW, w - NA_KW)], constant_values=NEG_BIG)
    flat = jnp.tile(padded, (1, 1, 1, w))
    band = flat[..., w - 1:w - 1 + w * (2 * w - 2)].reshape(rpb.shape[:3] + (w, 2 * w - 2))[..., :w]
    band = jnp.where(valid, band, NEG_BIG)
    tab = jnp.stack([band[:, :, NA_KH - 1 - v:2 * NA_KH - 1 - v] for v in range(NA_KH)], axis=2)
    tab = jnp.transpose(tab, (0, 1, 2, 4, 3, 5))
    return tab.reshape(rpb.shape[:2] + (NA_KH, GRID_W, NA_KH * GRID_W)).astype(F32)


def _head_mean_matrix():
    h = np.arange(GLA_WIDTH) // GLA_DV
    e = (h[:, None] == h[None, :]).astype(np.float32)
    return jnp.asarray(np.concatenate([e, e], axis=0), dtype=BF16)


def kernel(x, c, ctx, c_ctx, w_mod, b_mod, g_mix, g_ffn, w_in, gla_wa_f, gla_ba_f, gla_wa_b, gla_ba_b, gla_g_norm,
           na_rpb, conv_w, conv_b, conv_ln_g, conv_ln_b, w_out, router_w, router_bias, w_gate, w_up, w_down,
           g_final):
    bsz, seq, d = x.shape
    ctx_len = ctx.shape[1]
    depth = w_mod.shape[0]
    t_lat, t_ctx = bsz * seq, bsz * ctx_len
    assert seq % TOK_TILE == 0 and t_ctx % TOK_TILE == 0 and seq % ctx_len == 0
    assert d == 2 * SUB_Y * LANE
    assert seq // GRID_W >= NA_KH and (seq // GRID_W) % NA_UNROLL == 0
    assert ctx_len % (GLA_CHUNK * GLA_STEPS) == 0 and seq % (GLA_CHUNK * GLA_STEPS) == 0
    tiles_per_seq = seq // TOK_TILE
    n_lat_tiles = t_lat // TOK_TILE
    n_all_tiles = (t_lat + t_ctx) // TOK_TILE
    ctx_blk0 = t_lat // ctx_len

    mod_rows = -(-(bsz + 1) // 8) * 8
    c_pad = jnp.zeros((mod_rows, d), F32).at[:bsz].set(c).at[bsz].set(c_ctx)
    mod = _modulation(c_pad, w_mod, b_mod)
    mod5 = mod.reshape(depth, mod_rows, 6, 1, d)

    cos_t, sin_t = _rope_tables(seq)
    e_mat = _head_mean_matrix()
    rw = jnp.pad(router_w, ((0, 0), (0, LANE - N_EXPERTS)))
    rw_hi = rw.astype(BF16)
    rw_lo = (rw - rw_hi.astype(F32)).astype(BF16)
    rw_split = jnp.concatenate([rw_hi, rw_lo], axis=1)
    bias_col = router_bias.reshape(N_EXPERTS, 1).astype(F32)
    zero_state = jnp.zeros((bsz, GLA_WIDTH, QK_PAD), F32)

    w_aug = _in_weights(w_in)
    wa_aug, ba_aug = _gate_weights(gla_wa_f, gla_ba_f, gla_wa_b, gla_ba_b)
    bias_tab = _na_bias_tables(na_rpb)
    row = lambda a: a[:, None, :]
    gmix, gffn, gnorm = row(g_mix), row(g_ffn), row(gla_g_norm)
    cb, clg, clb = row(conv_b), row(conv_ln_g), row(conv_ln_b)
    w_out_b = w_out.astype(BF16)
    expert_f32 = (w_gate, w_up, w_down)
    layer_rows = w_gate.shape[1] * w_gate.shape[2]
    assert w_down.shape[1] * w_down.shape[2] == layer_rows
    expert_rows = tuple(w.reshape(depth * layer_rows, w.shape[-1]) for w in expert_f32)
    as_experts = lambda flat: tuple(a.reshape(w.shape[1:]) for a, w in zip(flat, expert_f32))

    x_all = (x.reshape(t_lat, d), ctx.reshape(t_ctx, d))
    y_moe = None
    for l in range(depth):
        last = l == depth - 1
        x_all, (q, k, v, sgate, gf, gb, nq, nk, nv, hcv) = _inproj(
            x_all, y_moe, mod5, l, gmix, cos_t, sin_t, w_aug, wa_aug, ba_aug,
            n_lat_tiles=n_lat_tiles, tiles_per_seq=tiles_per_seq, n_batch=bsz, res_layer=l - 1)

        o_g, st_f, st_b = _gla(q, k, v, gf, gb, zero_state, zero_state, None, seq=ctx_len, blk0=ctx_blk0,
                               n_batch=bsz)
        o_g, _, _ = _gla(q, k, v, gf, gb, st_f, st_b, o_g, seq=seq, blk0=0, n_batch=bsz)

        o_n, cast0 = _na_latent(nq, nk, nv, bias_tab, l, (expert_rows, layer_rows, 0) if l == 0 else None,
                                seq=seq, ctx_len=ctx_len, n_batch=bsz, ctx_blk0=ctx_blk0)
        if l == 0:
            weights = as_experts(cast0)
        o_c = _conv(hcv, conv_w, cb, clg, clb, l, None, seq=seq, blk0=0, n_batch=bsz)
        if not last:
            o_n = _na_context(nq, nk, nv, o_n, ctx_len=ctx_len, n_batch=bsz, ctx_blk0=ctx_blk0)
            o_c = _conv(hcv, conv_w, cb, clg, clb, l, o_c, seq=ctx_len, blk0=ctx_blk0, n_batch=bsz)

        n_tiles = n_lat_tiles if last else n_all_tiles
        x_mid, h2, meta, cnt = _outproj(
            o_g[0], o_g[1], sgate, o_n, o_c, x_all, mod5, l, gnorm, e_mat, w_out_b, gffn, rw_split, bias_col,
            n_tiles=n_tiles, n_lat_tiles=n_lat_tiles, tiles_per_seq=tiles_per_seq, n_batch=bsz)
        next_f32 = None if last else (expert_rows, layer_rows, l + 1)
        y_moe, next_bf16 = _moe(h2, meta, cnt, weights, next_f32)
        if not last:
            weights = as_experts(next_bf16)
        x_all = x_mid

    out = _final(x_all, y_moe, mod5, depth - 1, g_final.reshape(1, d), tiles_per_seq=tiles_per_seq)
    return out.reshape(bsz, seq, d)
```

```python
import functools

import numpy as np
import jax
import jax.numpy as jnp
from jax import lax
from jax.experimental import pallas as pl
from jax.experimental.pallas import tpu as pltpu
from jax.experimental.pallas import tpu_sc as plsc

GRID_W = 64
EPS = 1e-6
GLA_HEADS, GLA_DK, GLA_DV = 4, 48, 96
GLA_QK = GLA_HEADS * GLA_DK
GLA_WIDTH = GLA_HEADS * GLA_DV
GLA_LOWRANK = 16
GLA_TAU = 16.0
ROPE_BASE = 10000.0
NA_HEADS, NA_DH = 6, 64
NA_WIDTH = NA_HEADS * NA_DH
NA_KH, NA_KW = 8, 16
CONV_WIDTH, CONV_K = 256, 31
N_EXPERTS, N_GROUPS, EXPERTS_PER_GROUP = 16, 4, 4
PAIRS = ((0, 1), (0, 2), (0, 3), (1, 2), (1, 3), (2, 3))
N_CLASSES = N_GROUPS * len(PAIRS)

LANE = 128
SUBLANES = 8
QK_PAD = 256
C_Q, C_K = 0, 256
C_A, C_V, C_GATE = 512, 640, 1024
C_NQ, C_NK, C_NV, C_CONV = 1408, 1792, 2176, 2560
IN_COLS_PAD = 3072
MXU_N = 256

TOK_TILE = 1024
ROW_PARTS = 4
GLA_CHUNK = 64
GLA_SUB = 16
GLA_STEPS = 4
MOE_BLK = 256
NA_UNROLL = 16
NEG_BIG = -1e30
VMEM_PER_CORE = 64 * 1024 * 1024
VMEM_LIMIT = VMEM_PER_CORE - 8 * 1024 * 1024

F32 = jnp.float32
BF16 = jnp.bfloat16


def _cparams(sem):
    return pltpu.CompilerParams(dimension_semantics=sem, vmem_limit_bytes=VMEM_LIMIT)


def _cast_specs(next_f32, n_steps, step_of):
    stacked, layer_rows, layer = next_f32
    cast_steps = 1 << (n_steps.bit_length() - 1)
    cast_rows = layer_rows // cast_steps
    assert cast_rows * cast_steps == layer_rows and cast_rows % 16 == 0
    step = lambda *g: jnp.minimum(step_of(*g), cast_steps - 1)
    in_specs = [pl.BlockSpec((cast_rows, a.shape[1]), lambda *g: (layer * cast_steps + step(*g), 0)) for a in stacked]
    out_specs = [pl.BlockSpec((cast_rows, a.shape[1]), lambda *g: (step(*g), 0)) for a in stacked]
    out_shape = [jax.ShapeDtypeStruct((layer_rows, a.shape[1]), BF16) for a in stacked]
    return in_specs, out_specs, out_shape


def _cast_slices(srcs, dsts):
    for src, dst in zip(srcs, dsts):
        dst[...] = src[...].astype(BF16)


def _layer_spec(a, layer):
    return pl.BlockSpec((None,) + a.shape[1:], lambda *_: (layer,) + (0,) * (a.ndim - 1),
                        pipeline_mode=pl.Buffered(1))


def _dot(a, b):
    return jnp.dot(a, b, preferred_element_type=F32)


def _dot_nt(a, b):
    return lax.dot_general(a, b, (((1,), (1,)), ((), ())), preferred_element_type=F32)


def _split(a):
    hi = a.astype(BF16)
    lo = (a - hi.astype(F32)).astype(BF16)
    return hi, lo


def _sigmoid(x):
    return 1.0 / (1.0 + jnp.exp(-x))


SUB_X, SUB_Y = 5, 4
SC_WINDOW = 256


def _pack_bf16_pairs(a, b):
    ua = lax.bitcast_convert_type(a.astype(BF16).astype(F32), jnp.uint32)
    ub = lax.bitcast_convert_type(b.astype(BF16).astype(F32), jnp.uint32)
    return jnp.bitwise_or(jnp.right_shift(ua, jnp.uint32(16)), ub)


def _unpack_bf16_pairs(w):
    lo = lax.bitcast_convert_type(jnp.left_shift(w, jnp.uint32(16)), F32)
    hi = lax.bitcast_convert_type(jnp.bitwise_and(w, jnp.uint32(0xFFFF0000)), F32)
    return lo, hi


def _store_subrows(ref, val, j0, row0=0):
    r = val.shape[0]
    for j in range(val.shape[1] // LANE):
        ref[row0 // 8:(row0 + r) // 8, j0 + j, :, :] = val[:, j * LANE:(j + 1) * LANE].reshape(r // 8, 8, LANE)


def _load_subrows(ref, j0, n, row0=0, rows=None):
    r = ref.shape[0] * 8 if rows is None else rows
    return jnp.concatenate([ref[row0 // 8:(row0 + r) // 8, j0 + j, :, :].reshape(r, LANE) for j in range(n)],
                           axis=1)


def _pack_rows(x):
    half = x.shape[1] // 2
    return _pack_bf16_pairs(x[:, :half], x[:, half:])


def _unpack_rows(w):
    lo, hi = _unpack_bf16_pairs(w)
    return jnp.concatenate([lo, hi], axis=1)


def _sc_mesh():
    return plsc.VectorSubcoreMesh(core_axis_name="c", subcore_axis_name="s")


def _sc_scatter(src, idx, n_out):
    n, w = src.shape
    idx2 = idx.reshape(1, n)

    @pl.kernel(out_type=jax.ShapeDtypeStruct((n_out, w), src.dtype), mesh=_sc_mesh(), scratch_types=[])
    def scatter_kernel(x_hbm, i_hbm, o_hbm):
        def body(x_vmem, i_vmem):
            pltpu.sync_copy(x_vmem, o_hbm.at[i_vmem.at[0]])

        pltpu.emit_pipeline(
            body,
            grid=(n // SC_WINDOW,),
            in_specs=[pl.BlockSpec((SC_WINDOW, w), index_map=lambda i: (i, 0)),
                      pl.BlockSpec((1, SC_WINDOW), index_map=lambda i: (0, i))],
            out_specs=[],
            core_axis_name=("c", "s"),
            dimension_semantics=(pltpu.PARALLEL,),
        )(x_hbm, i_hbm)

    return scatter_kernel(src, idx2)


def _sc_gather(src, idx):
    n = idx.shape[0]
    w = src.shape[1]
    idx2 = idx.reshape(1, n)

    @pl.kernel(out_type=jax.ShapeDtypeStruct((n, w), src.dtype), mesh=_sc_mesh())
    def gather_kernel(x_hbm, i_hbm, o_hbm):
        def body(i_vmem, o_vmem):
            pltpu.sync_copy(x_hbm.at[i_vmem.at[0]], o_vmem)

        pltpu.emit_pipeline(
            body,
            grid=(n // SC_WINDOW,),
            in_specs=[pl.BlockSpec((1, SC_WINDOW), index_map=lambda i: (0, i))],
            out_specs=[pl.BlockSpec((SC_WINDOW, w), index_map=lambda i: (i, 0))],
            core_axis_name=("c", "s"),
            dimension_semantics=(pltpu.PARALLEL,),
        )(i_hbm, o_hbm)

    return gather_kernel(src, idx2)


def _mod_kernel(c_ref, w_ref, b_ref, o_ref):
    cv = c_ref[...]
    s = cv * _sigmoid(cv)
    s_hi, s_lo = _split(s)
    w_hi, w_lo = _split(w_ref[...])
    o_ref[...] = _dot(s_hi, w_hi) + _dot(s_lo, w_hi) + _dot(s_hi, w_lo) + b_ref[...]


def _modulation(c_pad, w_mod, b_mod):
    depth, d, six_d = w_mod.shape
    rows = c_pad.shape[0]
    nt = 1536
    return pl.pallas_call(
        _mod_kernel,
        grid=(depth, six_d // nt),
        in_specs=[
            pl.BlockSpec((rows, d), lambda l, j: (0, 0)),
            pl.BlockSpec((None, d, nt), lambda l, j: (l, 0, j)),
            pl.BlockSpec((None, 1, nt), lambda l, j: (l, 0, j)),
        ],
        out_specs=pl.BlockSpec((None, rows, nt), lambda l, j: (l, 0, j)),
        out_shape=jax.ShapeDtypeStruct((depth, rows, six_d), F32),
        compiler_params=_cparams(("arbitrary", "arbitrary")),
        name="modulation",
    )(c_pad, w_mod, b_mod.reshape(depth, 1, six_d))


def _inproj_kernel(*refs, has_res, n_lat_tiles):
    if has_res:
        x_ref, y_ref, g5_ref = refs[:3]
    else:
        x_ref, xctx_ref = refs[:2]
        is_lat = pl.program_id(0) < n_lat_tiles
    refs = refs[3 if has_res else 2:]
    (sh_ref, sc_ref, gmix_ref, cos_ref, sin_ref, w_ref, wa_ref, ba_ref) = refs[:8]
    (xnew_ref, q_ref, k_ref, v_ref, sg_ref, gf_ref, gb_ref, nq_ref, nk_ref, nv_ref, hcv_ref) = refs[8:]

    tile = x_ref.shape[0]
    part = tile // ROW_PARTS
    wa_hi, wa_lo = _split(wa_ref[...])
    even_lane = jnp.bitwise_and(lax.broadcasted_iota(jnp.int32, (1, LANE), 1), 1) == 0
    hbs = []
    for p in range(ROW_PARTS):
        rows = pl.ds(p * part, part)
        if has_res:
            x = x_ref[rows, :] + g5_ref[...] * _unpack_rows(_load_subrows(y_ref, 0, SUB_Y, row0=p * part, rows=part))
        else:
            x = jnp.where(is_lat, x_ref[rows, :], xctx_ref[rows, :])
        xnew_ref[rows, :] = x
        ms = jnp.mean(x * x, axis=-1, keepdims=True)
        h = x * lax.rsqrt(ms + EPS) * gmix_ref[...]
        h = h * (1.0 + sc_ref[...]) + sh_ref[...]
        hbs.append(h.astype(BF16))

    for p, hb in enumerate(hbs):
        rows = pl.ds(p * part, part)

        def proj(c0, n, hb=hb):
            return _dot(hb, w_ref[:, c0:c0 + n])

        cos = cos_ref[rows, :]
        sin = sin_ref[rows, :]

        def rope(t):
            halves = []
            for c0 in range(0, QK_PAD, LANE):
                th = t[:, c0:c0 + LANE]
                halves.append(jnp.where(even_lane, pltpu.roll(th, LANE - 1, 1), pltpu.roll(th, 1, 1)))
            return t * cos + jnp.concatenate(halves, axis=1) * sin

        mid = (C_A + C_CONV) // 2
        assert (mid - C_A) % MXU_N == 0 and C_NQ < mid < C_NK
        d1 = proj(C_A, mid - C_A)
        a_hi, a_lo = _split(d1[:, :LANE])
        q_ref[rows, :] = (rope(proj(C_Q, QK_PAD)) * (GLA_DK ** -0.5)).astype(BF16)
        k_ref[rows, :] = rope(proj(C_K, QK_PAD)).astype(BF16)

        z = _dot(a_hi, wa_hi) + _dot(a_lo, wa_hi) + _dot(a_hi, wa_lo) + ba_ref[...]
        logsig = jnp.minimum(z, 0.0) - jnp.log(1.0 + jnp.exp(-jnp.abs(z)))
        g = logsig * (1.0 / GLA_TAU)
        gf_ref[rows, :] = g[:, :QK_PAD]
        gb_ref[rows, :] = g[:, QK_PAD:]

        v_ref[rows, :] = d1[:, C_V - C_A:C_GATE - C_A].astype(BF16)
        gate = d1[:, C_GATE - C_A:C_NQ - C_A]
        sg_ref[rows, :] = (gate * _sigmoid(gate)).astype(BF16)
        d2 = proj(mid, C_CONV - mid)
        nq = jnp.concatenate([d1[:, C_NQ - C_A:], d2[:, :C_NK - mid]], axis=1)
        nq_ref[rows, :] = (nq * (NA_DH ** -0.5)).astype(BF16)
        nk_ref[rows, :] = d2[:, C_NK - mid:C_NV - mid].astype(BF16)
        nv_ref[rows, :] = d2[:, C_NV - mid:].astype(BF16)
        u = proj(C_CONV, 2 * CONV_WIDTH)
        hcv_ref[rows, :] = u[:, :CONV_WIDTH] * _sigmoid(u[:, CONV_WIDTH:])


def _inproj(x_all, y, mod5, layer, gmix, cos_t, sin_t, w_aug, wa_aug, ba_aug, *, n_lat_tiles, tiles_per_seq,
            n_batch, res_layer):
    has_res = y is not None
    d = x_all.shape[1] if has_res else x_all[0].shape[1]
    t_all = x_all.shape[0] if has_res else x_all[0].shape[0] + x_all[1].shape[0]
    nt = t_all // TOK_TILE

    def bidx(i):
        return jnp.where(i < n_lat_tiles, i // tiles_per_seq, n_batch)

    def ridx(i):
        return jnp.where(i < n_lat_tiles, i % tiles_per_seq, tiles_per_seq)

    def modspec(l, j):
        return pl.BlockSpec((None, None, None, 1, d), lambda i: (l, bidx(i), j, 0, 0))

    tok = lambda w: pl.BlockSpec((TOK_TILE, w), lambda i: (i, 0))
    full = lambda a: pl.BlockSpec(a.shape, lambda i: (0,) * a.ndim)

    if has_res:
        in_specs = [tok(d), pl.BlockSpec((TOK_TILE // 8, SUB_Y, 8, LANE), lambda i: (i, 0, 0, 0)),
                    modspec(res_layer, 5)]
        args = [x_all, y, mod5]
    else:
        in_specs = [pl.BlockSpec((TOK_TILE, d), lambda i: (jnp.minimum(i, n_lat_tiles - 1), 0)),
                    pl.BlockSpec((TOK_TILE, d), lambda i: (jnp.maximum(i - n_lat_tiles, 0), 0))]
        args = list(x_all)
    in_specs += [modspec(layer, 0), modspec(layer, 1), _layer_spec(gmix, layer),
                 pl.BlockSpec((TOK_TILE, QK_PAD), lambda i: (ridx(i), 0)),
                 pl.BlockSpec((TOK_TILE, QK_PAD), lambda i: (ridx(i), 0)),
                 _layer_spec(w_aug, layer), _layer_spec(wa_aug, layer), _layer_spec(ba_aug, layer)]
    args += [mod5, mod5, gmix, cos_t, sin_t, w_aug, wa_aug, ba_aug]

    out_widths = [(d, F32), (QK_PAD, BF16), (QK_PAD, BF16), (GLA_WIDTH, BF16), (GLA_WIDTH, BF16), (QK_PAD, F32),
                  (QK_PAD, F32), (NA_WIDTH, BF16), (NA_WIDTH, BF16), (NA_WIDTH, BF16), (CONV_WIDTH, F32)]
    out_specs = [tok(w) for w, _ in out_widths]
    out_shape = [jax.ShapeDtypeStruct((t_all, w), dt) for w, dt in out_widths]
    res = pl.pallas_call(
        functools.partial(_inproj_kernel, has_res=has_res, n_lat_tiles=n_lat_tiles),
        grid=(nt,),
        in_specs=in_specs,
        out_specs=out_specs,
        out_shape=out_shape,
        compiler_params=_cparams(("parallel",)),
        name="inproj",
    )(*args)
    return res[0], res[1:]


def _gla_masks():
    c, sub = GLA_CHUNK, GLA_SUB
    lane_qk = lax.broadcasted_iota(jnp.int32, (1, QK_PAD), 1)
    head_qk = ((lane_qk >= GLA_DK).astype(jnp.int32) + (lane_qk >= 2 * GLA_DK).astype(jnp.int32)
               + (lane_qk >= 3 * GLA_DK).astype(jnp.int32) + 4 * (lane_qk >= 4 * GLA_DK).astype(jnp.int32))
    row_h = jnp.right_shift(lax.broadcasted_iota(jnp.int32, (c, 1), 0), GLA_SUB.bit_length() - 1)
    hm = (row_h == head_qk).astype(F32)
    row_v = lax.broadcasted_iota(jnp.int32, (GLA_WIDTH, 1), 0)
    head_v = ((row_v >= GLA_DV).astype(jnp.int32) + (row_v >= 2 * GLA_DV).astype(jnp.int32)
              + (row_v >= 3 * GLA_DV).astype(jnp.int32))
    bd = (head_v == head_qk).astype(F32)
    lane_v = lax.broadcasted_iota(jnp.int32, (1, GLA_WIDTH), 1)
    vm = [((lane_v >= h * GLA_DV) & (lane_v < (h + 1) * GLA_DV)).astype(F32) for h in range(GLA_HEADS)]
    return hm, bd, vm


def _gla_steps(q, k, v, g, s_t, hm, bd, vm):
    c, sub = GLA_CHUNK, GLA_SUB
    nsub = c // sub
    dirs = (True, False)
    items = [(u, d) for u in range(GLA_STEPS) for d in range(2)]
    ri = lax.broadcasted_iota(jnp.int32, (c, c), 0)
    ci = lax.broadcasted_iota(jnp.int32, (c, c), 1)
    key_row = lax.broadcasted_iota(jnp.int32, (c, 1), 0)
    att_row = jnp.bitwise_and(ri, sub - 1)
    tri = [((ci <= ri) if fwd else (ci >= ri)).astype(BF16) for fwd in dirs]

    cums = {}
    for u, d in items:
        g_hi, g_lo = _split(g[u][d])
        cums[u, d] = _dot(tri[d], g_hi) + _dot(tri[d], g_lo)

    qe, kv, decay, atts = {}, {}, {}, {}
    for u, d in items:
        fwd = dirs[d]
        cum = cums[u, d]
        tot = cum[c - 1:c] if fwd else cum[0:1]
        qe[u, d] = (q[u][d] * jnp.exp(cum)).astype(BF16)
        k_end = (k[u][d] * jnp.exp(tot - cum)).astype(BF16)
        kv[u, d] = lax.dot_general(v[u][d], k_end, (((0,), (0,)), ((), ())), preferred_element_type=F32)
        decay[u, d] = jnp.exp(tot)
        att_d = []
        for i in range(nsub):
            lo, hi = i * sub, (i + 1) * sub
            if fwd:
                ref = cum[lo - 1:lo] if i > 0 else jnp.zeros((1, QK_PAD), F32)
                key_ok = key_row < hi
                causal = ci <= att_row + lo
            else:
                ref = cum[hi:hi + 1] if i < nsub - 1 else jnp.zeros((1, QK_PAD), F32)
                key_ok = key_row >= lo
                causal = ci >= att_row + lo
            qi = q[u][d][lo:hi] * jnp.exp(cum[lo:hi] - ref)
            qs = (jnp.concatenate([qi] * GLA_HEADS, axis=0) * hm).astype(BF16)
            ki = (k[u][d] * jnp.exp(jnp.where(key_ok, ref - cum, NEG_BIG))).astype(BF16)
            att = _dot_nt(qs, ki)
            att_d.append(jnp.where(causal, att, 0.0).astype(BF16))
        atts[u, d] = jnp.concatenate(att_d, axis=0)

    s = list(s_t)
    o_inter = {}
    for u, d in items:
        o_inter[u, d] = _dot_nt(qe[u, d], s[d].astype(BF16))
        s[d] = s[d] * decay[u, d] + bd * kv[u, d]

    outs = [[None, None] for _ in range(GLA_STEPS)]
    for u, d in items:
        r = _dot(atts[u, d], v[u][d])
        blocks = []
        for i in range(nsub):
            base = i * c
            oi = r[base:base + sub] * vm[0]
            for h in range(1, GLA_HEADS):
                oi = oi + r[base + h * sub:base + (h + 1) * sub] * vm[h]
            blocks.append(oi)
        outs[u][d] = o_inter[u, d] + jnp.concatenate(blocks, axis=0)
    return outs, s


def _gla_kernel(q_ref, k_ref, v_ref, gf_ref, gb_ref, sf0_ref, sb0_ref, of_ref, ob_ref, sf_ref, sb_ref):
    n = q_ref.shape[0]
    nc = n // GLA_CHUNK
    hm, bd, vm = _gla_masks()
    sf_ref[...] = sf0_ref[...]
    sb_ref[...] = sb0_ref[...]

    def body(j, carry):
        rows = [[pl.ds(pl.multiple_of(cidx * GLA_CHUNK, GLA_CHUNK), GLA_CHUNK)
                 for cidx in (j * GLA_STEPS + u, nc - 1 - (j * GLA_STEPS + u))] for u in range(GLA_STEPS)]
        q = [[q_ref[r, :].astype(F32) for r in ru] for ru in rows]
        k = [[k_ref[r, :].astype(F32) for r in ru] for ru in rows]
        v = [[v_ref[r, :] for r in ru] for ru in rows]
        g = [[gf_ref[ru[0], :], gb_ref[ru[1], :]] for ru in rows]
        outs, s_new = _gla_steps(q, k, v, g, [sf_ref[...], sb_ref[...]], hm, bd, vm)
        sf_ref[...] = s_new[0]
        sb_ref[...] = s_new[1]
        for u in range(GLA_STEPS):
            of_ref[rows[u][0], :] = outs[u][0]
            ob_ref[rows[u][1], :] = outs[u][1]
        return carry

    lax.fori_loop(0, nc // GLA_STEPS, body, 0)


def _gla(q, k, v, gf, gb, sf0, sb0, o_prev, *, seq, blk0, n_batch):
    t_all = q.shape[0]
    tokw = lambda w: pl.BlockSpec((seq, w), lambda b: (blk0 + b, 0))
    st = pl.BlockSpec((None, GLA_WIDTH, QK_PAD), lambda b: (b, 0, 0))
    in_specs = [tokw(QK_PAD), tokw(QK_PAD), tokw(GLA_WIDTH), tokw(QK_PAD), tokw(QK_PAD), st, st]
    args = [q, k, v, gf, gb, sf0, sb0]
    aliases = {}
    n_in = len(args)
    if o_prev is not None:
        in_specs += [pl.BlockSpec(memory_space=pl.ANY)] * 2
        args += list(o_prev)
        aliases = {n_in: 0, n_in + 1: 1}

    def kern(*refs):
        _gla_kernel(*refs[:n_in], *refs[len(args):])

    st_shape = jax.ShapeDtypeStruct((n_batch, GLA_WIDTH, QK_PAD), F32)
    o_shape = jax.ShapeDtypeStruct((t_all, GLA_WIDTH), F32)
    o_f, o_b, s_f, s_b = pl.pallas_call(
        kern,
        grid=(n_batch,),
        in_specs=in_specs,
        out_specs=[tokw(GLA_WIDTH), tokw(GLA_WIDTH), st, st],
        out_shape=[o_shape, o_shape, st_shape, st_shape],
        input_output_aliases=aliases,
        compiler_params=_cparams(("parallel",)),
        name="gla",
    )(*args)
    return (o_f, o_b), s_f, s_b


def _na_kernel(q_ref, k_ref, v_ref, kc_ref, vc_ref, bias_ref, *rest, n_cast):
    cast_in, (o_ref, *cast_out) = rest[:n_cast], rest[n_cast:]
    _cast_slices(cast_in, cast_out)
    n = q_ref.shape[0]
    rows = n // GRID_W
    nkeys = NA_KH * GRID_W
    lane = lax.broadcasted_iota(jnp.int32, (1, LANE), 1)
    first = lane < NA_DH
    kc = kc_ref[...]
    vc = vc_ref[...]

    def body(jb, carry):
        items = []
        for j in range(NA_UNROLL):
            r = jb * NA_UNROLL + j
            r0 = jnp.clip(r - NA_KH // 2, 0, rows - NA_KH)
            var = r - r0
            qrows = pl.ds(pl.multiple_of(r * GRID_W, GRID_W), GRID_W)
            krows = pl.ds(pl.multiple_of(r0 * GRID_W, GRID_W), nkeys)
            qr = q_ref[qrows, :]
            kb = k_ref[krows, :]
            q2 = jnp.concatenate([jnp.where(first, qr, jnp.zeros_like(qr)),
                                  jnp.where(first, jnp.zeros_like(qr), qr)], axis=0)
            items.append((qrows, krows, var, _dot_nt(q2, kb), _dot_nt(q2, kc)))
        probs = []
        for qrows, krows, var, s_loc, s_ctx in items:
            s_loc = s_loc + jnp.concatenate([bias_ref[0, var], bias_ref[1, var]], axis=0)
            m = jnp.maximum(jnp.max(s_loc, axis=-1, keepdims=True), jnp.max(s_ctx, axis=-1, keepdims=True))
            p_loc = jnp.exp(s_loc - m)
            p_ctx = jnp.exp(s_ctx - m)
            l = jnp.sum(p_loc, axis=-1, keepdims=True) + jnp.sum(p_ctx, axis=-1, keepdims=True)
            probs.append((p_loc.astype(BF16), p_ctx.astype(BF16), l))
        for (qrows, krows, var, _, _), (p_loc, p_ctx, l) in zip(items, probs):
            o = (_dot(p_loc, v_ref[krows, :]) + _dot(p_ctx, vc)) / l
            o_ref[qrows, :] = jnp.where(first, o[:GRID_W], o[GRID_W:]).astype(o_ref.dtype)
        return carry

    lax.fori_loop(0, rows // NA_UNROLL, body, 0)


def _na_latent(nq, nk, nv, bias_tab, layer, cast_f32=None, *, seq, ctx_len, n_batch, ctx_blk0):
    t_all = nq.shape[0]
    npair = NA_HEADS // 2
    lat = pl.BlockSpec((seq, LANE), lambda b, p: (b, p))
    ctx = pl.BlockSpec((ctx_len, LANE), lambda b, p: (ctx_blk0 + b, p))
    bias = pl.BlockSpec((None, 2) + bias_tab.shape[2:], lambda b, p: (layer, p, 0, 0, 0))
    in_specs = [lat, lat, lat, ctx, ctx, bias]
    out_specs = [lat]
    out_shape = [jax.ShapeDtypeStruct((t_all, NA_WIDTH), BF16)]
    args = [nq, nk, nv, nk, nv, bias_tab]
    if cast_f32 is not None:
        c_in, c_out, c_shape = _cast_specs(cast_f32, n_batch * npair, lambda b, p: b * npair + p)
        in_specs += c_in
        out_specs += c_out
        out_shape += c_shape
        args += list(cast_f32[0])
    res = pl.pallas_call(
        functools.partial(_na_kernel, n_cast=0 if cast_f32 is None else len(cast_f32[0])),
        grid=(n_batch, npair),
        in_specs=in_specs,
        out_specs=out_specs,
        out_shape=out_shape,
        compiler_params=_cparams(("arbitrary", "arbitrary")),
        name="na_latent",
    )(*args)
    return res[0], tuple(res[1:])


def _na_ctx_kernel(q_ref, k_ref, v_ref, o_in_ref, o_ref):
    del o_in_ref
    lane = lax.broadcasted_iota(jnp.int32, (1, LANE), 1)
    first = lane < NA_DH
    q = q_ref[...]
    k = k_ref[...]
    v = v_ref[...]
    res = []
    for h in range(2):
        sel = first if h == 0 else jnp.logical_not(first)
        qh = jnp.where(sel, q, jnp.zeros_like(q))
        s = _dot_nt(qh, k)
        m = jnp.max(s, axis=-1, keepdims=True)
        p = jnp.exp(s - m)
        l = jnp.sum(p, axis=-1, keepdims=True)
        res.append(_dot(p.astype(BF16), v) / l)
    o_ref[...] = jnp.where(first, res[0], res[1]).astype(o_ref.dtype)


def _na_context(nq, nk, nv, o_prev, *, ctx_len, n_batch, ctx_blk0):
    npair = NA_HEADS // 2
    ctx = pl.BlockSpec((ctx_len, LANE), lambda b, p: (ctx_blk0 + b, p))
    return pl.pallas_call(
        _na_ctx_kernel,
        grid=(n_batch, npair),
        in_specs=[ctx, ctx, ctx, pl.BlockSpec(memory_space=pl.ANY)],
        out_specs=ctx,
        out_shape=jax.ShapeDtypeStruct(o_prev.shape, o_prev.dtype),
        input_output_aliases={3: 0},
        compiler_params=_cparams(("parallel", "arbitrary")),
        name="na_context",
    )(nq, nk, nv, o_prev)


CONV_HALO = 16
CONV_ROWS = 128


def _conv_kernel(*refs, has_prev):
    if has_prev:
        h_ref, w_ref, b_ref, lg_ref, lb_ref, _, o_ref, pad_ref, sh_ref = refs
    else:
        h_ref, w_ref, b_ref, lg_ref, lb_ref, o_ref, pad_ref, sh_ref = refs
    n = h_ref.shape[0]
    zeros = jnp.zeros((CONV_HALO, CONV_WIDTH), F32)
    pad_ref[0:CONV_HALO, :] = zeros
    pad_ref[CONV_HALO + n:CONV_HALO + n + CONV_HALO, :] = zeros
    pad_ref[CONV_HALO:CONV_HALO + n, :] = h_ref[...]
    span = n + 2 * CONV_HALO - SUBLANES
    for s in range(1, SUBLANES):
        sh_ref[s - 1, 0:span, :] = pad_ref[s:s + span, :]
    w = w_ref[...]
    off = CONV_HALO - CONV_K // 2

    def chunk(cidx, carry):
        base = pl.multiple_of(cidx * CONV_ROWS, CONV_ROWS)
        acc = jnp.zeros((CONV_ROWS, CONV_WIDTH), F32) + b_ref[...]
        for j in range(CONV_K):
            s = (off + j) % SUBLANES
            win = pl.ds(base + (off + j - s), CONV_ROWS)
            tap = pad_ref[win, :] if s == 0 else sh_ref[s - 1, win, :]
            acc = acc + tap * w[j:j + 1, :]
        mu = jnp.mean(acc, axis=-1, keepdims=True)
        xc = acc - mu
        var = jnp.mean(xc * xc, axis=-1, keepdims=True)
        y = xc * lax.rsqrt(var + EPS) * lg_ref[...] + lb_ref[...]
        o_ref[pl.ds(base, CONV_ROWS), :] = (y * _sigmoid(y)).astype(o_ref.dtype)
        return carry

    lax.fori_loop(0, n // CONV_ROWS, chunk, 0)


def _conv(hcv, w, b, lg, lb, layer, o_prev, *, seq, blk0, n_batch):
    t_all = hcv.shape[0]
    tok = pl.BlockSpec((seq, CONV_WIDTH), lambda i: (blk0 + i, 0))
    full = lambda a: pl.BlockSpec(a.shape, lambda i: (0,) * a.ndim)
    in_specs = [tok] + [_layer_spec(a, layer) for a in (w, b, lg, lb)]
    args = [hcv, w, b, lg, lb]
    aliases = {}
    if o_prev is not None:
        in_specs.append(pl.BlockSpec(memory_space=pl.ANY))
        args.append(o_prev)
        aliases = {5: 0}
    return pl.pallas_call(
        functools.partial(_conv_kernel, has_prev=o_prev is not None),
        grid=(n_batch,),
        in_specs=in_specs,
        out_specs=tok,
        out_shape=jax.ShapeDtypeStruct((t_all, CONV_WIDTH), BF16),
        scratch_shapes=[pltpu.VMEM((seq + 2 * CONV_HALO, CONV_WIDTH), F32),
                        pltpu.VMEM((SUBLANES - 1, seq + 2 * CONV_HALO, CONV_WIDTH), F32)],
        input_output_aliases=aliases,
        compiler_params=_cparams(("parallel",)),
        name="conv",
    )(*args)


def _outproj_kernel(ogf_ref, ogb_ref, sg_ref, on_ref, oc_ref, x_ref, g2_ref, sh_ref, sc_ref, gn_ref, e_ref,
                    wo_ref, gffn_ref, rw_ref, rbias_ref, xmid_ref, h2_ref, meta_ref, cnt_ref, carry_ref):
    tile = x_ref.shape[0]
    part = tile // ROW_PARTS
    parts = [pl.ds(p * part, part) for p in range(ROW_PARTS)]
    e = e_ref[...]
    rw = rw_ref[...]

    @pl.when(pl.program_id(0) == 0)
    def _():
        carry_ref[...] = jnp.zeros_like(carry_ref)

    carry = carry_ref[...]
    meta_ref[...] = jnp.zeros_like(meta_ref)

    ofs, mss = [], []
    for rows in parts:
        of = ogf_ref[rows, :] + ogb_ref[rows, :]
        sq_hi, sq_lo = _split(of * of)
        ofs.append(of)
        mss.append(_dot(jnp.concatenate([sq_hi, sq_lo], axis=1), e) * (1.0 / GLA_DV))
    ys = []
    for rows, of, ms in zip(parts, ofs, mss):
        og = of * lax.rsqrt(ms + EPS) * gn_ref[...] * sg_ref[rows, :].astype(F32)
        mix = jnp.concatenate([og.astype(BF16), on_ref[rows, :], oc_ref[rows, :]], axis=1)
        ys.append(_dot(mix, wo_ref[...]))
    for p, (rows, y) in enumerate(zip(parts, ys)):
        x = x_ref[rows, :] + g2_ref[...] * y
        xmid_ref[rows, :] = x
        ms2 = jnp.mean(x * x, axis=-1, keepdims=True)
        h2 = x * lax.rsqrt(ms2 + EPS) * gffn_ref[...]
        h2 = h2 * (1.0 + sc_ref[...]) + sh_ref[...]
        h_hi, h_lo = _split(h2)
        _store_subrows(h2_ref, _pack_rows(h2), 0, row0=p * part)
        hw = _dot(h_hi, rw)
        logits = hw[:, :LANE] + hw[:, LANE:] + _dot(h_lo, rw[:, :LANE])
        cls, rank, wtok, carry = _route_tokens(logits, rbias_ref[...], carry)
        meta_ref[0:1, rows] = cls
        meta_ref[1:2, rows] = rank
        _store_subrows(h2_ref, wtok, SUB_Y, row0=p * part)
    carry_ref[...] = carry
    cnt_ref[...] = jnp.broadcast_to(carry, cnt_ref.shape)


def _outproj(o_gf, o_gb, sgate, o_n, o_c, x_all, mod5, layer, gnorm, e_mat, w_out, gffn, rw_split, bias_col, *,
             n_tiles, n_lat_tiles, tiles_per_seq, n_batch):
    d = x_all.shape[1]
    t_out = n_tiles * TOK_TILE

    def bidx(i):
        return jnp.where(i < n_lat_tiles, i // tiles_per_seq, n_batch)

    def modspec(j):
        return pl.BlockSpec((None, None, None, 1, d), lambda i: (layer, bidx(i), j, 0, 0))

    tok = lambda w: pl.BlockSpec((TOK_TILE, w), lambda i: (i, 0))
    full = lambda a: pl.BlockSpec(a.shape, lambda i: (0,) * a.ndim)
    return pl.pallas_call(
        _outproj_kernel,
        grid=(n_tiles,),
        in_specs=[tok(GLA_WIDTH), tok(GLA_WIDTH), tok(GLA_WIDTH), tok(NA_WIDTH), tok(CONV_WIDTH), tok(d),
                  modspec(2), modspec(3), modspec(4), _layer_spec(gnorm, layer), full(e_mat),
                  _layer_spec(w_out, layer), _layer_spec(gffn, layer),
                  full(rw_split), full(bias_col)],
        out_specs=[tok(d), pl.BlockSpec((TOK_TILE // 8, SUB_X, 8, LANE), lambda i: (i, 0, 0, 0)),
                   pl.BlockSpec((8, TOK_TILE), lambda i: (0, i)), pl.BlockSpec((32, LANE), lambda i: (0, 0))],
        out_shape=[jax.ShapeDtypeStruct((t_out, d), F32),
                   jax.ShapeDtypeStruct((t_out // 8, SUB_X, 8, LANE), jnp.uint32),
                   jax.ShapeDtypeStruct((8, t_out), F32), jax.ShapeDtypeStruct((32, LANE), F32)],
        scratch_shapes=[pltpu.VMEM((32, 1), F32)],
        compiler_params=_cparams(("arbitrary",)),
        name="outproj",
    )(o_gf, o_gb, sgate, o_n, o_c, x_all, mod5, mod5, mod5, gnorm, e_mat, w_out, gffn, rw_split, bias_col)


def _route_tokens(lg, bias, carry):
    tile = lg.shape[0]
    lt = lg.T
    aff = _sigmoid(lt[0:N_EXPERTS])
    sel = aff + bias
    s = [sel[e:e + 1] for e in range(N_EXPERTS)]
    a = [aff[e:e + 1] for e in range(N_EXPERTS)]

    def top2sum(v):
        best = v[0] + v[1]
        for i, j in PAIRS[1:]:
            best = jnp.maximum(best, v[i] + v[j])
        return best

    gs = [top2sum(s[4 * g:4 * g + 4]) for g in range(N_GROUPS)]
    gbest = jnp.zeros_like(gs[0], dtype=jnp.int32)
    gmax = gs[0]
    for g in range(1, N_GROUPS):
        upd = gs[g] > gmax
        gbest = jnp.where(upd, g, gbest)
        gmax = jnp.where(upd, gs[g], gmax)

    def pick(vals, j):
        out = vals[j]
        for g in range(1, N_GROUPS):
            out = jnp.where(gbest == g, vals[4 * g + j], out)
        return out

    sv = [pick(s, j) for j in range(EXPERTS_PER_GROUP)]
    av = [pick(a, j) for j in range(EXPERTS_PER_GROUP)]
    i1 = jnp.zeros_like(gbest)
    m1 = sv[0]
    for j in range(1, EXPERTS_PER_GROUP):
        upd = sv[j] > m1
        i1 = jnp.where(upd, j, i1)
        m1 = jnp.where(upd, sv[j], m1)
    i2 = jnp.full_like(gbest, -1)
    m2 = jnp.zeros_like(m1)
    for j in range(EXPERTS_PER_GROUP):
        upd = (i1 != j) & ((sv[j] > m2) | (i2 < 0))
        i2 = jnp.where(upd, j, i2)
        m2 = jnp.where(upd, sv[j], m2)
    ia = jnp.minimum(i1, i2)
    ib = jnp.maximum(i1, i2)
    pair = jnp.where(ia == 0, ib - 1, jnp.where(ia == 1, ib + 1, 5))
    cls = gbest * len(PAIRS) + pair

    def take(vals, idx):
        out = vals[0]
        for j in range(1, EXPERTS_PER_GROUP):
            out = jnp.where(idx == j, vals[j], out)
        return out

    w1 = take(av, i1)
    w2 = take(av, i2)
    tot = w1 + w2
    wa = jnp.where(i1 < i2, w1, w2) / tot
    wb = jnp.where(i1 < i2, w2, w1) / tot

    crow = lax.broadcasted_iota(jnp.int32, (32, tile), 0)
    oh = (crow == cls).astype(F32)
    us = lax.broadcasted_iota(jnp.int32, (tile, tile), 0)
    ut = lax.broadcasted_iota(jnp.int32, (tile, tile), 1)
    upper = (us < ut).astype(BF16)
    prefix = _dot(oh.astype(BF16), upper)
    rank = jnp.sum(oh * (prefix + carry), axis=0, keepdims=True)
    carry_new = carry + jnp.sum(oh, axis=1, keepdims=True)

    wrow = lax.broadcasted_iota(jnp.int32, (LANE, tile), 0)
    wmat = jnp.where(wrow == 0, wa, jnp.where(wrow == 1, wb, 0.0))
    wtok = lax.bitcast_convert_type(wmat.T, jnp.uint32)
    return cls.astype(F32), rank, wtok, carry_new


FF_TILE = 512


def _expert_kernel(ea_ref, eb_ref, nvalid_ref, xs_ref, wga_ref, wua_ref, wda_ref, wgb_ref, wub_ref, wdb_ref,
                   *rest, n_cast):
    del ea_ref, eb_ref
    j = pl.program_id(0)
    nvalid = nvalid_ref[j]
    nxt_in, (y_ref, *nxt_out) = rest[:n_cast], rest[n_cast:]

    def cast_next():
        _cast_slices(nxt_in, nxt_out)

    @pl.when(nvalid == 0)
    def _():
        cast_next()
        y_ref[...] = jnp.zeros_like(y_ref)

    @pl.when(nvalid != 0)
    def _():
        cast_next()
        rows = y_ref.shape[0] * 8
        live = lax.broadcasted_iota(jnp.int32, (rows, 1), 0) < nvalid
        x = jnp.where(live, _unpack_rows(_load_subrows(xs_ref, 0, SUB_Y)), 0.0).astype(BF16)
        ws = jnp.where(live, lax.bitcast_convert_type(xs_ref[:, SUB_Y, :, :].reshape(rows, LANE), F32), 0.0)
        ff = wga_ref.shape[1]

        items = [(w, f0) for w in ((wga_ref, wua_ref, wda_ref), (wgb_ref, wub_ref, wdb_ref))
                 for f0 in range(0, ff, FF_TILE)]

        def up(item):
            (wg_ref, wu_ref, _), f0 = item
            return _dot(x, wg_ref[:, f0:f0 + FF_TILE]), _dot(x, wu_ref[:, f0:f0 + FF_TILE])

        ups = [up(items[0]), up(items[1])]
        parts = []
        for c, ((_, _, wd_ref), f0) in enumerate(items):
            hg, hu = ups[c]
            hh = (hg * _sigmoid(hg) * hu).astype(BF16)
            if c + 2 < len(items):
                ups.append(up(items[c + 2]))
            parts.append(_dot(hh, wd_ref[f0:f0 + FF_TILE, :]))
        per = len(items) // 2
        ya = functools.reduce(lambda a, b: a + b, parts[:per])
        yb = functools.reduce(lambda a, b: a + b, parts[per:])
        _store_subrows(y_ref, _pack_rows(ya * ws[:, 0:1] + yb * ws[:, 1:2]), 0)


def _experts(xs_sub, ea, eb, nvalid, weights, next_f32):
    wg, wu, wd = weights
    nb = xs_sub.shape[0] * 8 // MOE_BLK
    d, ff = wg.shape[1], wg.shape[2]
    wspec_in = lambda which: pl.BlockSpec((None, d, ff), lambda j, ea, eb, v: ((ea, eb)[which][j], 0, 0))
    wspec_out = lambda which: pl.BlockSpec((None, ff, d), lambda j, ea, eb, v: ((ea, eb)[which][j], 0, 0))
    in_specs = [pl.BlockSpec((MOE_BLK // 8, SUB_X, 8, LANE), lambda j, ea, eb, v: (j, 0, 0, 0)),
                wspec_in(0), wspec_in(0), wspec_out(0), wspec_in(1), wspec_in(1), wspec_out(1)]
    out_specs = [pl.BlockSpec((MOE_BLK // 8, SUB_Y, 8, LANE), lambda j, ea, eb, v: (j, 0, 0, 0))]
    out_shape = [jax.ShapeDtypeStruct((nb * MOE_BLK // 8, SUB_Y, 8, LANE), jnp.uint32)]
    args = [ea, eb, nvalid, xs_sub, wg, wu, wd, wg, wu, wd]
    if next_f32 is not None:
        c_in, c_out, c_shape = _cast_specs(next_f32, nb, lambda j, *_: j)
        in_specs += c_in
        out_specs += c_out
        out_shape += c_shape
        args += list(next_f32[0])
    grid_spec = pltpu.PrefetchScalarGridSpec(num_scalar_prefetch=3, grid=(nb,), in_specs=in_specs,
                                             out_specs=out_specs)
    res = pl.pallas_call(
        functools.partial(_expert_kernel, n_cast=0 if next_f32 is None else len(next_f32[0])),
        grid_spec=grid_spec,
        out_shape=out_shape,
        compiler_params=_cparams(("arbitrary",)),
        name="experts",
    )(*args)
    return res[0], tuple(res[1:])


def _subrow_index(dest, nsub):
    t = dest.shape[0]
    base = ((dest // 8) * (nsub * 8) + dest % 8).astype(F32).reshape(t // LANE, LANE)
    src = np.arange(LANE)
    grp, r = src // 8, src % 8
    sel = np.zeros((LANE, LANE * nsub), np.float32)
    off = np.zeros((LANE * nsub,), np.int32)
    for j in range(nsub):
        pos = grp * (nsub * 8) + j * 8 + r
        sel[src, pos] = 1.0
        off[pos] = j * 8
    idx = jnp.dot(base, jnp.asarray(sel), precision=lax.Precision.HIGHEST).astype(jnp.int32) + jnp.asarray(off)
    return idx.reshape(t * nsub)


def _moe(h2_sub, meta, cnt, weights, next_f32):
    t = meta.shape[1]
    cls = meta[0].astype(jnp.int32)
    rank = meta[1].astype(jnp.int32)
    counts = cnt[:N_CLASSES, 0].astype(jnp.int32)
    padded = (counts + MOE_BLK - 1) // MOE_BLK * MOE_BLK
    pad_end = jnp.cumsum(padded)
    pad_start = pad_end - padded
    class_ids = jnp.arange(N_CLASSES, dtype=jnp.int32)
    dest = rank + jnp.sum(jnp.where(cls[:, None] == class_ids[None, :], pad_start[None, :], 0), axis=1)
    nb = t // MOE_BLK + N_CLASSES
    p_rows = nb * MOE_BLK
    blk_start = jnp.arange(nb, dtype=jnp.int32) * MOE_BLK
    valid = blk_start < pad_end[-1]
    blk_cls = jnp.sum((pad_end[None, :] <= blk_start[:, None]).astype(jnp.int32), axis=-1)
    last_cls = jnp.sum((pad_end <= pad_end[-1] - 1).astype(jnp.int32))
    blk_cls = jnp.minimum(jnp.where(valid, blk_cls, last_cls), N_CLASSES - 1)
    nvalid = jnp.where(valid, jnp.clip(pad_start[blk_cls] + counts[blk_cls] - blk_start, 0, MOE_BLK), 0)
    pair_a = jnp.array([p[0] for p in PAIRS], jnp.int32)
    pair_b = jnp.array([p[1] for p in PAIRS], jnp.int32)
    grp = blk_cls // len(PAIRS)
    ea = grp * EXPERTS_PER_GROUP + pair_a[blk_cls % len(PAIRS)]
    eb = grp * EXPERTS_PER_GROUP + pair_b[blk_cls % len(PAIRS)]

    xs = _sc_scatter(h2_sub.reshape(t * SUB_X, LANE), _subrow_index(dest, SUB_X), p_rows * SUB_X)
    ys, next_bf16 = _experts(xs.reshape(p_rows // 8, SUB_X, 8, LANE), ea, eb, nvalid.astype(jnp.int32), weights,
                             next_f32)
    y = _sc_gather(ys.reshape(p_rows * SUB_Y, LANE), _subrow_index(dest, SUB_Y))
    return y.reshape(t // 8, SUB_Y, 8, LANE), next_bf16


def _final_kernel(x_ref, y_ref, g5_ref, gf_ref, o_ref):
    x = x_ref[...] + g5_ref[...] * _unpack_rows(_load_subrows(y_ref, 0, SUB_Y))
    ms = jnp.mean(x * x, axis=-1, keepdims=True)
    o_ref[...] = x * lax.rsqrt(ms + EPS) * gf_ref[...]


def _final(x_mid, y, mod5, layer, g_final, *, tiles_per_seq):
    t, d = x_mid.shape
    tok = pl.BlockSpec((TOK_TILE, d), lambda i: (i, 0))
    return pl.pallas_call(
        _final_kernel,
        grid=(t // TOK_TILE,),
        in_specs=[tok, pl.BlockSpec((TOK_TILE // 8, SUB_Y, 8, LANE), lambda i: (i, 0, 0, 0)),
                  pl.BlockSpec((None, None, None, 1, d), lambda i: (layer, i // tiles_per_seq, 5, 0, 0)),
                  pl.BlockSpec(g_final.shape, lambda i: (0, 0))],
        out_specs=tok,
        out_shape=jax.ShapeDtypeStruct((t, d), F32),
        compiler_params=_cparams(("parallel",)),
        name="final_norm",
    )(x_mid, y, mod5, g_final)


def _rope_tables(seq):
    t = jnp.arange(seq)
    row = (t // GRID_W).astype(F32)
    col = (t % GRID_W).astype(F32)
    half = GLA_DK // 2
    inv = ROPE_BASE ** (-jnp.arange(0, half, 2, dtype=F32) / half)
    ang = jnp.concatenate([row[:, None] * inv, col[:, None] * inv], axis=-1)
    cos = jnp.repeat(jnp.cos(ang), 2, axis=-1)
    sin = (jnp.sin(ang)[:, :, None] * jnp.array([-1.0, 1.0], F32)).reshape(seq, GLA_DK)
    cos = jnp.tile(cos, (1, GLA_HEADS))
    sin = jnp.tile(sin, (1, GLA_HEADS))
    padw = QK_PAD - GLA_QK
    cos = jnp.pad(cos, ((0, 0), (0, padw)), constant_values=1.0)
    sin = jnp.pad(sin, ((0, 0), (0, padw)))
    cos = jnp.concatenate([cos, jnp.ones((TOK_TILE, QK_PAD), F32)], axis=0)
    sin = jnp.concatenate([sin, jnp.zeros((TOK_TILE, QK_PAD), F32)], axis=0)
    return cos, sin


def _pad_last(w, n):
    return jnp.pad(w, [(0, 0)] * (w.ndim - 1) + [(0, n - w.shape[-1])])


def _in_weights(w_in):
    offs = np.cumsum([0, GLA_QK, GLA_QK, GLA_WIDTH, GLA_WIDTH, 2 * GLA_LOWRANK, NA_WIDTH, NA_WIDTH, NA_WIDTH,
                      2 * CONV_WIDTH])
    seg = [w_in[..., offs[i]:offs[i + 1]] for i in range(9)]
    cols = [_pad_last(seg[0], QK_PAD), _pad_last(seg[1], QK_PAD), _pad_last(seg[4], LANE), seg[2], seg[3], seg[5],
            seg[6], seg[7], seg[8]]
    w = jnp.concatenate(cols, axis=-1).astype(BF16)
    assert w.shape[-1] == IN_COLS_PAD
    return w


def _gate_weights(wa_f, ba_f, wa_b, ba_b):
    depth = wa_f.shape[0]
    zero = jnp.zeros((depth, GLA_LOWRANK, QK_PAD), F32)
    top = jnp.concatenate([_pad_last(wa_f, QK_PAD), zero], axis=-1)
    bot = jnp.concatenate([zero, _pad_last(wa_b, QK_PAD)], axis=-1)
    rest = jnp.zeros((depth, LANE - 2 * GLA_LOWRANK, 2 * QK_PAD), F32)
    wa = jnp.concatenate([top, bot, rest], axis=1)
    ba = jnp.concatenate([_pad_last(ba_f, QK_PAD), _pad_last(ba_b, QK_PAD)], axis=-1)[:, None, :]
    return wa, ba


def _na_bias_tables(rpb):
    cq = np.arange(GRID_W)
    c0 = np.clip(cq - NA_KW // 2, 0, GRID_W - NA_KW)
    kc = np.arange(GRID_W)
    valid = (kc[None, :] >= c0[:, None]) & (kc[None, :] < c0[:, None] + NA_KW)
    w = GRID_W
    padded = jnp.pad(rpb, [(0, 0)] * 3 + [(w - NA_KW, w - NA_KW)], constant_values=NEG_BIG)
    flat = jnp.tile(padded, (1, 1, 1, w))
    band = flat[..., w - 1:w - 1 + w * (2 * w - 2)].reshape(rpb.shape[:3] + (w, 2 * w - 2))[..., :w]
    band = jnp.where(valid, band, NEG_BIG)
    tab = jnp.stack([band[:, :, NA_KH - 1 - v:2 * NA_KH - 1 - v] for v in range(NA_KH)], axis=2)
    tab = jnp.transpose(tab, (0, 1, 2, 4, 3, 5))
    return tab.reshape(rpb.shape[:2] + (NA_KH, GRID_W, NA_KH * GRID_W)).astype(F32)


def _head_mean_matrix():
    h = np.arange(GLA_WIDTH) // GLA_DV
    e = (h[:, None] == h[None, :]).astype(np.float32)
    return jnp.asarray(np.concatenate([e, e], axis=0), dtype=BF16)


def kernel(x, c, ctx, c_ctx, w_mod, b_mod, g_mix, g_ffn, w_in, gla_wa_f, gla_ba_f, gla_wa_b, gla_ba_b, gla_g_norm,
           na_rpb, conv_w, conv_b, conv_ln_g, conv_ln_b, w_out, router_w, router_bias, w_gate, w_up, w_down,
           g_final):
    bsz, seq, d = x.shape
    ctx_len = ctx.shape[1]
    depth = w_mod.shape[0]
    t_lat, t_ctx = bsz * seq, bsz * ctx_len
    assert seq % TOK_TILE == 0 and t_ctx % TOK_TILE == 0 and seq % ctx_len == 0
    assert d == 2 * SUB_Y * LANE
    assert seq // GRID_W >= NA_KH and (seq // GRID_W) % NA_UNROLL == 0
    assert ctx_len % (GLA_CHUNK * GLA_STEPS) == 0 and seq % (GLA_CHUNK * GLA_STEPS) == 0
    tiles_per_seq = seq // TOK_TILE
    n_lat_tiles = t_lat // TOK_TILE
    n_all_tiles = (t_lat + t_ctx) // TOK_TILE
    ctx_blk0 = t_lat // ctx_len

    mod_rows = -(-(bsz + 1) // 8) * 8
    c_pad = jnp.zeros((mod_rows, d), F32).at[:bsz].set(c).at[bsz].set(c_ctx)
    mod = _modulation(c_pad, w_mod, b_mod)
    mod5 = mod.reshape(depth, mod_rows, 6, 1, d)

    cos_t, sin_t = _rope_tables(seq)
    e_mat = _head_mean_matrix()
    rw = jnp.pad(router_w, ((0, 0), (0, LANE - N_EXPERTS)))
    rw_hi = rw.astype(BF16)
    rw_lo = (rw - rw_hi.astype(F32)).astype(BF16)
    rw_split = jnp.concatenate([rw_hi, rw_lo], axis=1)
    bias_col = router_bias.reshape(N_EXPERTS, 1).astype(F32)
    zero_state = jnp.zeros((bsz, GLA_WIDTH, QK_PAD), F32)

    w_aug = _in_weights(w_in)
    wa_aug, ba_aug = _gate_weights(gla_wa_f, gla_ba_f, gla_wa_b, gla_ba_b)
    bias_tab = _na_bias_tables(na_rpb)
    row = lambda a: a[:, None, :]
    gmix, gffn, gnorm = row(g_mix), row(g_ffn), row(gla_g_norm)
    cb, clg, clb = row(conv_b), row(conv_ln_g), row(conv_ln_b)
    w_out_b = w_out.astype(BF16)
    expert_f32 = (w_gate, w_up, w_down)
    layer_rows = w_gate.shape[1] * w_gate.shape[2]
    assert w_down.shape[1] * w_down.shape[2] == layer_rows
    expert_rows = tuple(w.reshape(depth * layer_rows, w.shape[-1]) for w in expert_f32)
    as_experts = lambda flat: tuple(a.reshape(w.shape[1:]) for a, w in zip(flat, expert_f32))

    x_all = (x.reshape(t_lat, d), ctx.reshape(t_ctx, d))
    y_moe = None
    for l in range(depth):
        last = l == depth - 1
        x_all, (q, k, v, sgate, gf, gb, nq, nk, nv, hcv) = _inproj(
            x_all, y_moe, mod5, l, gmix, cos_t, sin_t, w_aug, wa_aug, ba_aug,
            n_lat_tiles=n_lat_tiles, tiles_per_seq=tiles_per_seq, n_batch=bsz, res_layer=l - 1)

        o_g, st_f, st_b = _gla(q, k, v, gf, gb, zero_state, zero_state, None, seq=ctx_len, blk0=ctx_blk0,
                               n_batch=bsz)
        o_g, _, _ = _gla(q, k, v, gf, gb, st_f, st_b, o_g, seq=seq, blk0=0, n_batch=bsz)

        o_n, cast0 = _na_latent(nq, nk, nv, bias_tab, l, (expert_rows, layer_rows, 0) if l == 0 else None,
                                seq=seq, ctx_len=ctx_len, n_batch=bsz, ctx_blk0=ctx_blk0)
        if l == 0:
            weights = as_experts(cast0)
        o_c = _conv(hcv, conv_w, cb, clg, clb, l, None, seq=seq, blk0=0, n_batch=bsz)
        if not last:
            o_n = _na_context(nq, nk, nv, o_n, ctx_len=ctx_len, n_batch=bsz, ctx_blk0=ctx_blk0)
            o_c = _conv(hcv, conv_w, cb, clg, clb, l, o_c, seq=ctx_len, blk0=ctx_blk0, n_batch=bsz)

        n_tiles = n_lat_tiles if last else n_all_tiles
        x_mid, h2, meta, cnt = _outproj(
            o_g[0], o_g[1], sgate, o_n, o_c, x_all, mod5, l, gnorm, e_mat, w_out_b, gffn, rw_split, bias_col,
            n_tiles=n_tiles, n_lat_tiles=n_lat_tiles, tiles_per_seq=tiles_per_seq, n_batch=bsz)
        next_f32 = None if last else (expert_rows, layer_rows, l + 1)
        y_moe, next_bf16 = _moe(h2, meta, cnt, weights, next_f32)
        if not last:
            weights = as_experts(next_bf16)
        x_all = x_mid

    out = _final(x_all, y_moe, mod5, depth - 1, g_final.reshape(1, d), tiles_per_seq=tiles_per_seq)
    return out.reshape(bsz, seq, d)
```

```python
import functools

import numpy as np
import jax
import jax.numpy as jnp
from jax import lax
from jax.experimental import pallas as pl
from jax.experimental.pallas import tpu as pltpu
from jax.experimental.pallas import tpu_sc as plsc

GRID_W = 64
EPS = 1e-6
GLA_HEADS, GLA_DK, GLA_DV = 4, 48, 96
GLA_QK = GLA_HEADS * GLA_DK
GLA_WIDTH = GLA_HEADS * GLA_DV
GLA_LOWRANK = 16
GLA_TAU = 16.0
ROPE_BASE = 10000.0
NA_HEADS, NA_DH = 6, 64
NA_WIDTH = NA_HEADS * NA_DH
NA_KH, NA_KW = 8, 16
CONV_WIDTH, CONV_K = 256, 31
N_EXPERTS, N_GROUPS, EXPERTS_PER_GROUP = 16, 4, 4
PAIRS = ((0, 1), (0, 2), (0, 3), (1, 2), (1, 3), (2, 3))
N_CLASSES = N_GROUPS * len(PAIRS)

LANE = 128
SUBLANES = 8
QK_PAD = 256
C_Q, C_K = 0, 256
C_A, C_V, C_GATE = 512, 640, 1024
C_NQ, C_NK, C_NV, C_CONV = 1408, 1792, 2176, 2560
IN_COLS_PAD = 3072
MXU_N = 256

TOK_TILE = 1024
ROW_PARTS = 4
GLA_CHUNK = 64
GLA_SUB = 16
GLA_STEPS = 4
MOE_BLK = 256
NA_UNROLL = 8
NEG_BIG = -1e30
VMEM_PER_CORE = 64 * 1024 * 1024
VMEM_LIMIT = VMEM_PER_CORE - 8 * 1024 * 1024

F32 = jnp.float32
BF16 = jnp.bfloat16


def _cparams(sem):
    return pltpu.CompilerParams(dimension_semantics=sem, vmem_limit_bytes=VMEM_LIMIT)


def _cast_specs(next_f32, n_steps, step_of):
    stacked, layer_rows, layer = next_f32
    cast_steps = 1 << (n_steps.bit_length() - 1)
    cast_rows = layer_rows // cast_steps
    assert cast_rows * cast_steps == layer_rows and cast_rows % 16 == 0
    step = lambda *g: jnp.minimum(step_of(*g), cast_steps - 1)
    in_specs = [pl.BlockSpec((cast_rows, a.shape[1]), lambda *g: (layer * cast_steps + step(*g), 0)) for a in stacked]
    out_specs = [pl.BlockSpec((cast_rows, a.shape[1]), lambda *g: (step(*g), 0)) for a in stacked]
    out_shape = [jax.ShapeDtypeStruct((layer_rows, a.shape[1]), BF16) for a in stacked]
    return in_specs, out_specs, out_shape


def _cast_slices(srcs, dsts):
    for src, dst in zip(srcs, dsts):
        dst[...] = src[...].astype(BF16)


def _layer_spec(a, layer):
    return pl.BlockSpec((None,) + a.shape[1:], lambda *_: (layer,) + (0,) * (a.ndim - 1),
                        pipeline_mode=pl.Buffered(1))


def _dot(a, b):
    return jnp.dot(a, b, preferred_element_type=F32)


def _dot_nt(a, b):
    return lax.dot_general(a, b, (((1,), (1,)), ((), ())), preferred_element_type=F32)


def _split(a):
    hi = a.astype(BF16)
    lo = (a - hi.astype(F32)).astype(BF16)
    return hi, lo


def _sigmoid(x):
    return 1.0 / (1.0 + jnp.exp(-x))


SUB_X, SUB_Y = 5, 4
SC_WINDOW = 256


def _pack_bf16_pairs(a, b):
    ua = lax.bitcast_convert_type(a.astype(BF16).astype(F32), jnp.uint32)
    ub = lax.bitcast_convert_type(b.astype(BF16).astype(F32), jnp.uint32)
    return jnp.bitwise_or(jnp.right_shift(ua, jnp.uint32(16)), ub)


def _unpack_bf16_pairs(w):
    lo = lax.bitcast_convert_type(jnp.left_shift(w, jnp.uint32(16)), F32)
    hi = lax.bitcast_convert_type(jnp.bitwise_and(w, jnp.uint32(0xFFFF0000)), F32)
    return lo, hi


def _store_subrows(ref, val, j0, row0=0):
    r = val.shape[0]
    for j in range(val.shape[1] // LANE):
        ref[row0 // 8:(row0 + r) // 8, j0 + j, :, :] = val[:, j * LANE:(j + 1) * LANE].reshape(r // 8, 8, LANE)


def _load_subrows(ref, j0, n, row0=0, rows=None):
    r = ref.shape[0] * 8 if rows is None else rows
    return jnp.concatenate([ref[row0 // 8:(row0 + r) // 8, j0 + j, :, :].reshape(r, LANE) for j in range(n)],
                           axis=1)


def _pack_rows(x):
    half = x.shape[1] // 2
    return _pack_bf16_pairs(x[:, :half], x[:, half:])


def _unpack_rows(w):
    lo, hi = _unpack_bf16_pairs(w)
    return jnp.concatenate([lo, hi], axis=1)


def _sc_mesh():
    return plsc.VectorSubcoreMesh(core_axis_name="c", subcore_axis_name="s")


def _sc_scatter(src, idx, n_out):
    n, w = src.shape
    idx2 = idx.reshape(1, n)

    @pl.kernel(out_type=jax.ShapeDtypeStruct((n_out, w), src.dtype), mesh=_sc_mesh(), scratch_types=[])
    def scatter_kernel(x_hbm, i_hbm, o_hbm):
        def body(x_vmem, i_vmem):
            pltpu.sync_copy(x_vmem, o_hbm.at[i_vmem.at[0]])

        pltpu.emit_pipeline(
            body,
            grid=(n // SC_WINDOW,),
            in_specs=[pl.BlockSpec((SC_WINDOW, w), index_map=lambda i: (i, 0)),
                      pl.BlockSpec((1, SC_WINDOW), index_map=lambda i: (0, i))],
            out_specs=[],
            core_axis_name=("c", "s"),
            dimension_semantics=(pltpu.PARALLEL,),
        )(x_hbm, i_hbm)

    return scatter_kernel(src, idx2)


def _sc_gather(src, idx):
    n = idx.shape[0]
    w = src.shape[1]
    idx2 = idx.reshape(1, n)

    @pl.kernel(out_type=jax.ShapeDtypeStruct((n, w), src.dtype), mesh=_sc_mesh())
    def gather_kernel(x_hbm, i_hbm, o_hbm):
        def body(i_vmem, o_vmem):
            pltpu.sync_copy(x_hbm.at[i_vmem.at[0]], o_vmem)

        pltpu.emit_pipeline(
            body,
            grid=(n // SC_WINDOW,),
            in_specs=[pl.BlockSpec((1, SC_WINDOW), index_map=lambda i: (0, i))],
            out_specs=[pl.BlockSpec((SC_WINDOW, w), index_map=lambda i: (i, 0))],
            core_axis_name=("c", "s"),
            dimension_semantics=(pltpu.PARALLEL,),
        )(i_hbm, o_hbm)

    return gather_kernel(src, idx2)


def _mod_kernel(c_ref, w_ref, b_ref, o_ref):
    cv = c_ref[...]
    s = cv * _sigmoid(cv)
    s_hi, s_lo = _split(s)
    w_hi, w_lo = _split(w_ref[...])
    o_ref[...] = _dot(s_hi, w_hi) + _dot(s_lo, w_hi) + _dot(s_hi, w_lo) + b_ref[...]


def _modulation(c_pad, w_mod, b_mod):
    depth, d, six_d = w_mod.shape
    rows = c_pad.shape[0]
    nt = 1536
    return pl.pallas_call(
        _mod_kernel,
        grid=(depth, six_d // nt),
        in_specs=[
            pl.BlockSpec((rows, d), lambda l, j: (0, 0)),
            pl.BlockSpec((None, d, nt), lambda l, j: (l, 0, j)),
            pl.BlockSpec((None, 1, nt), lambda l, j: (l, 0, j)),
        ],
        out_specs=pl.BlockSpec((None, rows, nt), lambda l, j: (l, 0, j)),
        out_shape=jax.ShapeDtypeStruct((depth, rows, six_d), F32),
        compiler_params=_cparams(("arbitrary", "arbitrary")),
        name="modulation",
    )(c_pad, w_mod, b_mod.reshape(depth, 1, six_d))


def _inproj_kernel(*refs, has_res, n_lat_tiles, n_alias):
    if has_res:
        x_ref, y_ref, g5_ref = refs[:3]
    else:
        x_ref, xctx_ref = refs[:2]
        is_lat = pl.program_id(0) < n_lat_tiles
    refs = refs[3 if has_res else 2:]
    (sh_ref, sc_ref, gmix_ref, cos_ref, sin_ref, w_ref, wa_ref, ba_ref) = refs[:8]
    (xnew_ref, q_ref, k_ref, v_ref, sg_ref, gf_ref, gb_ref, nq_ref, nk_ref, nv_ref, hcv_ref) = refs[8 + n_alias:]

    tile = x_ref.shape[0]
    part = tile // ROW_PARTS
    wa_hi, wa_lo = _split(wa_ref[...])
    even_lane = jnp.bitwise_and(lax.broadcasted_iota(jnp.int32, (1, LANE), 1), 1) == 0
    hbs = []
    for p in range(ROW_PARTS):
        rows = pl.ds(p * part, part)
        if has_res:
            x = x_ref[rows, :] + g5_ref[...] * _unpack_rows(_load_subrows(y_ref, 0, SUB_Y, row0=p * part, rows=part))
        else:
            x = jnp.where(is_lat, x_ref[rows, :], xctx_ref[rows, :])
        xnew_ref[rows, :] = x
        ms = jnp.mean(x * x, axis=-1, keepdims=True)
        h = x * lax.rsqrt(ms + EPS) * gmix_ref[...]
        h = h * (1.0 + sc_ref[...]) + sh_ref[...]
        hbs.append(h.astype(BF16))

    for p, hb in enumerate(hbs):
        rows = pl.ds(p * part, part)

        def proj(c0, n, hb=hb):
            return _dot(hb, w_ref[:, c0:c0 + n])

        cos = cos_ref[rows, :]
        sin = sin_ref[rows, :]

        def rope(t):
            halves = []
            for c0 in range(0, QK_PAD, LANE):
                th = t[:, c0:c0 + LANE]
                halves.append(jnp.where(even_lane, pltpu.roll(th, LANE - 1, 1), pltpu.roll(th, 1, 1)))
            return t * cos + jnp.concatenate(halves, axis=1) * sin

        mid = (C_A + C_CONV) // 2
        assert (mid - C_A) % MXU_N == 0 and C_NQ < mid < C_NK
        d1 = proj(C_A, mid - C_A)
        a_hi, a_lo = _split(d1[:, :LANE])
        q_ref[rows, :] = (rope(proj(C_Q, QK_PAD)) * (GLA_DK ** -0.5)).astype(BF16)
        k_ref[rows, :] = rope(proj(C_K, QK_PAD)).astype(BF16)

        z = _dot(a_hi, wa_hi) + _dot(a_lo, wa_hi) + _dot(a_hi, wa_lo) + ba_ref[...]
        logsig = jnp.minimum(z, 0.0) - jnp.log(1.0 + jnp.exp(-jnp.abs(z)))
        g = logsig * (1.0 / GLA_TAU)
        gf_ref[rows, :] = g[:, :QK_PAD]
        gb_ref[rows, :] = g[:, QK_PAD:]

        v_ref[rows, :] = d1[:, C_V - C_A:C_GATE - C_A].astype(BF16)
        gate = d1[:, C_GATE - C_A:C_NQ - C_A]
        sg_ref[rows, :] = (gate * _sigmoid(gate)).astype(BF16)
        d2 = proj(mid, C_CONV - mid)
        nq = jnp.concatenate([d1[:, C_NQ - C_A:], d2[:, :C_NK - mid]], axis=1)
        nq_ref[rows, :] = (nq * (NA_DH ** -0.5)).astype(BF16)
        nk_ref[rows, :] = d2[:, C_NK - mid:C_NV - mid].astype(BF16)
        nv_ref[rows, :] = d2[:, C_NV - mid:].astype(BF16)
        u = proj(C_CONV, 2 * CONV_WIDTH)
        hcv_ref[rows, :] = u[:, :CONV_WIDTH] * _sigmoid(u[:, CONV_WIDTH:])


def _inproj(x_all, y, mod5, layer, gmix, cos_t, sin_t, w_aug, wa_aug, ba_aug, *, n_lat_tiles, tiles_per_seq,
            n_batch, res_layer, tile0=0, n_call_tiles=None, prev=None):
    has_res = y is not None
    d = x_all.shape[1] if has_res else x_all[0].shape[1]
    t_all = x_all.shape[0] if has_res else x_all[0].shape[0] + x_all[1].shape[0]
    nt = t_all // TOK_TILE if n_call_tiles is None else n_call_tiles

    def bidx(i):
        i = i + tile0
        return jnp.where(i < n_lat_tiles, i // tiles_per_seq, n_batch)

    def ridx(i):
        i = i + tile0
        return jnp.where(i < n_lat_tiles, i % tiles_per_seq, tiles_per_seq)

    def modspec(l, j):
        return pl.BlockSpec((None, None, None, 1, d), lambda i: (l, bidx(i), j, 0, 0))

    tok = lambda w: pl.BlockSpec((TOK_TILE, w), lambda i: (i + tile0, 0))
    full = lambda a: pl.BlockSpec(a.shape, lambda i: (0,) * a.ndim)

    if has_res:
        in_specs = [tok(d), pl.BlockSpec((TOK_TILE // 8, SUB_Y, 8, LANE), lambda i: (i, 0, 0, 0)),
                    modspec(res_layer, 5)]
        args = [x_all, y, mod5]
    else:
        in_specs = [pl.BlockSpec((TOK_TILE, d), lambda i: (jnp.minimum(i, n_lat_tiles - 1), 0)),
                    pl.BlockSpec((TOK_TILE, d), lambda i: (jnp.maximum(i - n_lat_tiles, 0), 0))]
        args = list(x_all)
    in_specs += [modspec(layer, 0), modspec(layer, 1), _layer_spec(gmix, layer),
                 pl.BlockSpec((TOK_TILE, QK_PAD), lambda i: (ridx(i), 0)),
                 pl.BlockSpec((TOK_TILE, QK_PAD), lambda i: (ridx(i), 0)),
                 _layer_spec(w_aug, layer), _layer_spec(wa_aug, layer), _layer_spec(ba_aug, layer)]
    args += [mod5, mod5, gmix, cos_t, sin_t, w_aug, wa_aug, ba_aug]

    out_widths = [(d, F32), (QK_PAD, BF16), (QK_PAD, BF16), (GLA_WIDTH, BF16), (GLA_WIDTH, BF16), (QK_PAD, F32),
                  (QK_PAD, F32), (NA_WIDTH, BF16), (NA_WIDTH, BF16), (NA_WIDTH, BF16), (CONV_WIDTH, F32)]
    out_specs = [tok(w) for w, _ in out_widths]
    out_shape = [jax.ShapeDtypeStruct((t_all, w), dt) for w, dt in out_widths]
    aliases = {}
    if prev is not None:
        aliases = {len(args) + n: n for n in range(len(prev))}
        in_specs += [pl.BlockSpec(memory_space=pl.ANY)] * len(prev)
        args += list(prev)
    res = pl.pallas_call(
        functools.partial(_inproj_kernel, has_res=has_res, n_lat_tiles=n_lat_tiles, n_alias=len(aliases)),
        grid=(nt,),
        in_specs=in_specs,
        out_specs=out_specs,
        out_shape=out_shape,
        input_output_aliases=aliases,
        compiler_params=_cparams(("parallel",)),
        name="inproj",
    )(*args)
    return res[0], res[1:]


def _gla_masks():
    c, sub = GLA_CHUNK, GLA_SUB
    lane_qk = lax.broadcasted_iota(jnp.int32, (1, QK_PAD), 1)
    head_qk = ((lane_qk >= GLA_DK).astype(jnp.int32) + (lane_qk >= 2 * GLA_DK).astype(jnp.int32)
               + (lane_qk >= 3 * GLA_DK).astype(jnp.int32) + 4 * (lane_qk >= 4 * GLA_DK).astype(jnp.int32))
    row_h = jnp.right_shift(lax.broadcasted_iota(jnp.int32, (c, 1), 0), GLA_SUB.bit_length() - 1)
    hm = (row_h == head_qk).astype(F32)
    row_v = lax.broadcasted_iota(jnp.int32, (GLA_WIDTH, 1), 0)
    head_v = ((row_v >= GLA_DV).astype(jnp.int32) + (row_v >= 2 * GLA_DV).astype(jnp.int32)
              + (row_v >= 3 * GLA_DV).astype(jnp.int32))
    bd = (head_v == head_qk).astype(F32)
    lane_v = lax.broadcasted_iota(jnp.int32, (1, GLA_WIDTH), 1)
    vm = [((lane_v >= h * GLA_DV) & (lane_v < (h + 1) * GLA_DV)).astype(F32) for h in range(GLA_HEADS)]
    return hm, bd, vm


def _gla_steps(q, k, v, g, s_t, hm, bd, vm):
    c, sub = GLA_CHUNK, GLA_SUB
    nsub = c // sub
    dirs = (True, False)
    items = [(u, d) for u in range(GLA_STEPS) for d in range(2)]
    ri = lax.broadcasted_iota(jnp.int32, (c, c), 0)
    ci = lax.broadcasted_iota(jnp.int32, (c, c), 1)
    key_row = lax.broadcasted_iota(jnp.int32, (c, 1), 0)
    att_row = jnp.bitwise_and(ri, sub - 1)
    tri = [((ci <= ri) if fwd else (ci >= ri)).astype(BF16) for fwd in dirs]

    cums = {}
    for u, d in items:
        g_hi, g_lo = _split(g[u][d])
        cums[u, d] = _dot(tri[d], g_hi) + _dot(tri[d], g_lo)

    qe, kv, decay, atts = {}, {}, {}, {}
    for u, d in items:
        fwd = dirs[d]
        cum = cums[u, d]
        tot = cum[c - 1:c] if fwd else cum[0:1]
        qe[u, d] = (q[u][d] * jnp.exp(cum)).astype(BF16)
        k_end = (k[u][d] * jnp.exp(tot - cum)).astype(BF16)
        kv[u, d] = lax.dot_general(v[u][d], k_end, (((0,), (0,)), ((), ())), preferred_element_type=F32)
        decay[u, d] = jnp.exp(tot)
        att_d = []
        for i in range(nsub):
            lo, hi = i * sub, (i + 1) * sub
            if fwd:
                ref = cum[lo - 1:lo] if i > 0 else jnp.zeros((1, QK_PAD), F32)
                key_ok = key_row < hi
                causal = ci <= att_row + lo
            else:
                ref = cum[hi:hi + 1] if i < nsub - 1 else jnp.zeros((1, QK_PAD), F32)
                key_ok = key_row >= lo
                causal = ci >= att_row + lo
            qi = q[u][d][lo:hi] * jnp.exp(cum[lo:hi] - ref)
            qs = (jnp.concatenate([qi] * GLA_HEADS, axis=0) * hm).astype(BF16)
            ki = (k[u][d] * jnp.exp(jnp.where(key_ok, ref - cum, NEG_BIG))).astype(BF16)
            att = _dot_nt(qs, ki)
            att_d.append(jnp.where(causal, att, 0.0).astype(BF16))
        atts[u, d] = jnp.concatenate(att_d, axis=0)

    s = list(s_t)
    o_inter = {}
    for u, d in items:
        o_inter[u, d] = _dot_nt(qe[u, d], s[d].astype(BF16))
        s[d] = s[d] * decay[u, d] + bd * kv[u, d]

    outs = [[None, None] for _ in range(GLA_STEPS)]
    for u, d in items:
        r = _dot(atts[u, d], v[u][d])
        blocks = []
        for i in range(nsub):
            base = i * c
            oi = r[base:base + sub] * vm[0]
            for h in range(1, GLA_HEADS):
                oi = oi + r[base + h * sub:base + (h + 1) * sub] * vm[h]
            blocks.append(oi)
        outs[u][d] = o_inter[u, d] + jnp.concatenate(blocks, axis=0)
    return outs, s


def _gla_kernel(q_ref, k_ref, v_ref, gf_ref, gb_ref, sf0_ref, sb0_ref, of_ref, ob_ref, sf_ref, sb_ref):
    n = q_ref.shape[0]
    nc = n // GLA_CHUNK
    hm, bd, vm = _gla_masks()
    sf_ref[...] = sf0_ref[...]
    sb_ref[...] = sb0_ref[...]

    def body(j, carry):
        rows = [[pl.ds(pl.multiple_of(cidx * GLA_CHUNK, GLA_CHUNK), GLA_CHUNK)
                 for cidx in (j * GLA_STEPS + u, nc - 1 - (j * GLA_STEPS + u))] for u in range(GLA_STEPS)]
        q = [[q_ref[r, :].astype(F32) for r in ru] for ru in rows]
        k = [[k_ref[r, :].astype(F32) for r in ru] for ru in rows]
        v = [[v_ref[r, :] for r in ru] for ru in rows]
        g = [[gf_ref[ru[0], :], gb_ref[ru[1], :]] for ru in rows]
        outs, s_new = _gla_steps(q, k, v, g, [sf_ref[...], sb_ref[...]], hm, bd, vm)
        sf_ref[...] = s_new[0]
        sb_ref[...] = s_new[1]
        for u in range(GLA_STEPS):
            of_ref[rows[u][0], :] = outs[u][0]
            ob_ref[rows[u][1], :] = outs[u][1]
        return carry

    lax.fori_loop(0, nc // GLA_STEPS, body, 0)


def _gla(q, k, v, gf, gb, sf0, sb0, o_prev, *, seq, blk0, n_batch):
    t_all = q.shape[0]
    tokw = lambda w: pl.BlockSpec((seq, w), lambda b: (blk0 + b, 0))
    st = pl.BlockSpec((None, GLA_WIDTH, QK_PAD), lambda b: (b, 0, 0))
    in_specs = [tokw(QK_PAD), tokw(QK_PAD), tokw(GLA_WIDTH), tokw(QK_PAD), tokw(QK_PAD), st, st]
    args = [q, k, v, gf, gb, sf0, sb0]
    aliases = {}
    n_in = len(args)
    if o_prev is not None:
        in_specs += [pl.BlockSpec(memory_space=pl.ANY)] * 2
        args += list(o_prev)
        aliases = {n_in: 0, n_in + 1: 1}

    def kern(*refs):
        _gla_kernel(*refs[:n_in], *refs[len(args):])

    st_shape = jax.ShapeDtypeStruct((n_batch, GLA_WIDTH, QK_PAD), F32)
    o_shape = jax.ShapeDtypeStruct((t_all, GLA_WIDTH), F32)
    o_f, o_b, s_f, s_b = pl.pallas_call(
        kern,
        grid=(n_batch,),
        in_specs=in_specs,
        out_specs=[tokw(GLA_WIDTH), tokw(GLA_WIDTH), st, st],
        out_shape=[o_shape, o_shape, st_shape, st_shape],
        input_output_aliases=aliases,
        compiler_params=_cparams(("parallel",)),
        name="gla",
    )(*args)
    return (o_f, o_b), s_f, s_b


def _na_kernel(q_ref, k_ref, v_ref, kc_ref, vc_ref, bias_ref, *rest, n_cast):
    cast_in, (o_ref, *cast_out) = rest[:n_cast], rest[n_cast:]
    _cast_slices(cast_in, cast_out)
    n = q_ref.shape[0]
    rows = n // GRID_W
    nkeys = NA_KH * GRID_W
    lane = lax.broadcasted_iota(jnp.int32, (1, LANE), 1)
    first = lane < NA_DH
    kc = kc_ref[...]
    vc = vc_ref[...]

    def body(jb, carry):
        items = []
        for j in range(NA_UNROLL):
            r = jb * NA_UNROLL + j
            r0 = jnp.clip(r - NA_KH // 2, 0, rows - NA_KH)
            var = r - r0
            qrows = pl.ds(pl.multiple_of(r * GRID_W, GRID_W), GRID_W)
            krows = pl.ds(pl.multiple_of(r0 * GRID_W, GRID_W), nkeys)
            qr = q_ref[qrows, :]
            kb = k_ref[krows, :]
            q2 = jnp.concatenate([jnp.where(first, qr, jnp.zeros_like(qr)),
                                  jnp.where(first, jnp.zeros_like(qr), qr)], axis=0)
            items.append((qrows, krows, var, _dot_nt(q2, kb), _dot_nt(q2, kc)))
        probs = []
        for qrows, krows, var, s_loc, s_ctx in items:
            s_loc = s_loc + jnp.concatenate([bias_ref[0, var], bias_ref[1, var]], axis=0)
            m = jnp.maximum(jnp.max(s_loc, axis=-1, keepdims=True), jnp.max(s_ctx, axis=-1, keepdims=True))
            p_loc = jnp.exp(s_loc - m)
            p_ctx = jnp.exp(s_ctx - m)
            l = jnp.sum(p_loc, axis=-1, keepdims=True) + jnp.sum(p_ctx, axis=-1, keepdims=True)
            probs.append((p_loc.astype(BF16), p_ctx.astype(BF16), l))
        for (qrows, krows, var, _, _), (p_loc, p_ctx, l) in zip(items, probs):
            o = (_dot(p_loc, v_ref[krows, :]) + _dot(p_ctx, vc)) / l
            o_ref[qrows, :] = jnp.where(first, o[:GRID_W], o[GRID_W:]).astype(o_ref.dtype)
        return carry

    lax.fori_loop(0, rows // NA_UNROLL, body, 0)


def _na_latent(nq, nk, nv, bias_tab, layer, cast_f32=None, *, seq, ctx_len, n_batch, ctx_blk0):
    t_all = nq.shape[0]
    npair = NA_HEADS // 2
    lat = pl.BlockSpec((seq, LANE), lambda b, p: (b, p))
    ctx = pl.BlockSpec((ctx_len, LANE), lambda b, p: (ctx_blk0 + b, p))
    bias = pl.BlockSpec((None, 2) + bias_tab.shape[2:], lambda b, p: (layer, p, 0, 0, 0))
    in_specs = [lat, lat, lat, ctx, ctx, bias]
    out_specs = [lat]
    out_shape = [jax.ShapeDtypeStruct((t_all, NA_WIDTH), BF16)]
    args = [nq, nk, nv, nk, nv, bias_tab]
    if cast_f32 is not None:
        c_in, c_out, c_shape = _cast_specs(cast_f32, n_batch * npair, lambda b, p: b * npair + p)
        in_specs += c_in
        out_specs += c_out
        out_shape += c_shape
        args += list(cast_f32[0])
    res = pl.pallas_call(
        functools.partial(_na_kernel, n_cast=0 if cast_f32 is None else len(cast_f32[0])),
        grid=(n_batch, npair),
        in_specs=in_specs,
        out_specs=out_specs,
        out_shape=out_shape,
        compiler_params=_cparams(("arbitrary", "arbitrary")),
        name="na_latent",
    )(*args)
    return res[0], tuple(res[1:])


def _na_ctx_kernel(q_ref, k_ref, v_ref, o_in_ref, o_ref):
    del o_in_ref
    lane = lax.broadcasted_iota(jnp.int32, (1, LANE), 1)
    first = lane < NA_DH
    q = q_ref[...]
    k = k_ref[...]
    v = v_ref[...]
    res = []
    for h in range(2):
        sel = first if h == 0 else jnp.logical_not(first)
        qh = jnp.where(sel, q, jnp.zeros_like(q))
        s = _dot_nt(qh, k)
        m = jnp.max(s, axis=-1, keepdims=True)
        p = jnp.exp(s - m)
        l = jnp.sum(p, axis=-1, keepdims=True)
        res.append(_dot(p.astype(BF16), v) / l)
    o_ref[...] = jnp.where(first, res[0], res[1]).astype(o_ref.dtype)


def _na_context(nq, nk, nv, o_prev, *, ctx_len, n_batch, ctx_blk0):
    npair = NA_HEADS // 2
    ctx = pl.BlockSpec((ctx_len, LANE), lambda b, p: (ctx_blk0 + b, p))
    return pl.pallas_call(
        _na_ctx_kernel,
        grid=(n_batch, npair),
        in_specs=[ctx, ctx, ctx, pl.BlockSpec(memory_space=pl.ANY)],
        out_specs=ctx,
        out_shape=jax.ShapeDtypeStruct(o_prev.shape, o_prev.dtype),
        input_output_aliases={3: 0},
        compiler_params=_cparams(("parallel", "arbitrary")),
        name="na_context",
    )(nq, nk, nv, o_prev)


CONV_HALO = 16
CONV_ROWS = 128


def _conv_kernel(*refs, has_prev):
    if has_prev:
        h_ref, w_ref, b_ref, lg_ref, lb_ref, _, o_ref, pad_ref, sh_ref = refs
    else:
        h_ref, w_ref, b_ref, lg_ref, lb_ref, o_ref, pad_ref, sh_ref = refs
    n = h_ref.shape[0]
    zeros = jnp.zeros((CONV_HALO, CONV_WIDTH), F32)
    pad_ref[0:CONV_HALO, :] = zeros
    pad_ref[CONV_HALO + n:CONV_HALO + n + CONV_HALO, :] = zeros
    pad_ref[CONV_HALO:CONV_HALO + n, :] = h_ref[...]
    span = n + 2 * CONV_HALO - SUBLANES
    for s in range(1, SUBLANES):
        sh_ref[s - 1, 0:span, :] = pad_ref[s:s + span, :]
    w = w_ref[...]
    off = CONV_HALO - CONV_K // 2

    def chunk(cidx, carry):
        base = pl.multiple_of(cidx * CONV_ROWS, CONV_ROWS)
        acc = jnp.zeros((CONV_ROWS, CONV_WIDTH), F32) + b_ref[...]
        for j in range(CONV_K):
            s = (off + j) % SUBLANES
            win = pl.ds(base + (off + j - s), CONV_ROWS)
            tap = pad_ref[win, :] if s == 0 else sh_ref[s - 1, win, :]
            acc = acc + tap * w[j:j + 1, :]
        mu = jnp.mean(acc, axis=-1, keepdims=True)
        xc = acc - mu
        var = jnp.mean(xc * xc, axis=-1, keepdims=True)
        y = xc * lax.rsqrt(var + EPS) * lg_ref[...] + lb_ref[...]
        o_ref[pl.ds(base, CONV_ROWS), :] = (y * _sigmoid(y)).astype(o_ref.dtype)
        return carry

    lax.fori_loop(0, n // CONV_ROWS, chunk, 0)


def _conv(hcv, w, b, lg, lb, layer, o_prev, *, seq, blk0, n_batch):
    t_all = hcv.shape[0]
    tok = pl.BlockSpec((seq, CONV_WIDTH), lambda i: (blk0 + i, 0))
    full = lambda a: pl.BlockSpec(a.shape, lambda i: (0,) * a.ndim)
    in_specs = [tok] + [_layer_spec(a, layer) for a in (w, b, lg, lb)]
    args = [hcv, w, b, lg, lb]
    aliases = {}
    if o_prev is not None:
        in_specs.append(pl.BlockSpec(memory_space=pl.ANY))
        args.append(o_prev)
        aliases = {5: 0}
    return pl.pallas_call(
        functools.partial(_conv_kernel, has_prev=o_prev is not None),
        grid=(n_batch,),
        in_specs=in_specs,
        out_specs=tok,
        out_shape=jax.ShapeDtypeStruct((t_all, CONV_WIDTH), BF16),
        scratch_shapes=[pltpu.VMEM((seq + 2 * CONV_HALO, CONV_WIDTH), F32),
                        pltpu.VMEM((SUBLANES - 1, seq + 2 * CONV_HALO, CONV_WIDTH), F32)],
        input_output_aliases=aliases,
        compiler_params=_cparams(("parallel",)),
        name="conv",
    )(*args)


def _outproj_kernel(ogf_ref, ogb_ref, sg_ref, on_ref, oc_ref, x_ref, g2_ref, sh_ref, sc_ref, gn_ref, e_ref,
                    wo_ref, gffn_ref, rw_ref, rbias_ref, xmid_ref, h2_ref, meta_ref, cnt_ref, carry_ref):
    tile = x_ref.shape[0]
    part = tile // ROW_PARTS
    parts = [pl.ds(p * part, part) for p in range(ROW_PARTS)]
    e = e_ref[...]
    rw = rw_ref[...]

    @pl.when(pl.program_id(0) == 0)
    def _():
        carry_ref[...] = jnp.zeros_like(carry_ref)

    carry = carry_ref[...]
    meta_ref[...] = jnp.zeros_like(meta_ref)

    ofs, mss = [], []
    for rows in parts:
        of = ogf_ref[rows, :] + ogb_ref[rows, :]
        sq_hi, sq_lo = _split(of * of)
        ofs.append(of)
        mss.append(_dot(jnp.concatenate([sq_hi, sq_lo], axis=1), e) * (1.0 / GLA_DV))
    ys = []
    for rows, of, ms in zip(parts, ofs, mss):
        og = of * lax.rsqrt(ms + EPS) * gn_ref[...] * sg_ref[rows, :].astype(F32)
        mix = jnp.concatenate([og.astype(BF16), on_ref[rows, :], oc_ref[rows, :]], axis=1)
        ys.append(_dot(mix, wo_ref[...]))
    for p, (rows, y) in enumerate(zip(parts, ys)):
        x = x_ref[rows, :] + g2_ref[...] * y
        xmid_ref[rows, :] = x
        ms2 = jnp.mean(x * x, axis=-1, keepdims=True)
        h2 = x * lax.rsqrt(ms2 + EPS) * gffn_ref[...]
        h2 = h2 * (1.0 + sc_ref[...]) + sh_ref[...]
        h_hi, h_lo = _split(h2)
        _store_subrows(h2_ref, _pack_rows(h2), 0, row0=p * part)
        hw = _dot(h_hi, rw)
        logits = hw[:, :LANE] + hw[:, LANE:] + _dot(h_lo, rw[:, :LANE])
        cls, rank, wtok, carry = _route_tokens(logits, rbias_ref[...], carry)
        meta_ref[0:1, rows] = cls
        meta_ref[1:2, rows] = rank
        _store_subrows(h2_ref, wtok, SUB_Y, row0=p * part)
    carry_ref[...] = carry
    cnt_ref[...] = jnp.broadcast_to(carry, cnt_ref.shape)


def _outproj(o_gf, o_gb, sgate, o_n, o_c, x_all, mod5, layer, gnorm, e_mat, w_out, gffn, rw_split, bias_col, *,
             n_tiles, n_lat_tiles, tiles_per_seq, n_batch):
    d = x_all.shape[1]
    t_out = n_tiles * TOK_TILE

    def bidx(i):
        return jnp.where(i < n_lat_tiles, i // tiles_per_seq, n_batch)

    def modspec(j):
        return pl.BlockSpec((None, None, None, 1, d), lambda i: (layer, bidx(i), j, 0, 0))

    tok = lambda w: pl.BlockSpec((TOK_TILE, w), lambda i: (i, 0))
    full = lambda a: pl.BlockSpec(a.shape, lambda i: (0,) * a.ndim)
    return pl.pallas_call(
        _outproj_kernel,
        grid=(n_tiles,),
        in_specs=[tok(GLA_WIDTH), tok(GLA_WIDTH), tok(GLA_WIDTH), tok(NA_WIDTH), tok(CONV_WIDTH), tok(d),
                  modspec(2), modspec(3), modspec(4), _layer_spec(gnorm, layer), full(e_mat),
                  _layer_spec(w_out, layer), _layer_spec(gffn, layer),
                  full(rw_split), full(bias_col)],
        out_specs=[tok(d), pl.BlockSpec((TOK_TILE // 8, SUB_X, 8, LANE), lambda i: (i, 0, 0, 0)),
                   pl.BlockSpec((8, TOK_TILE), lambda i: (0, i)), pl.BlockSpec((32, LANE), lambda i: (0, 0))],
        out_shape=[jax.ShapeDtypeStruct((t_out, d), F32),
                   jax.ShapeDtypeStruct((t_out // 8, SUB_X, 8, LANE), jnp.uint32),
                   jax.ShapeDtypeStruct((8, t_out), F32), jax.ShapeDtypeStruct((32, LANE), F32)],
        scratch_shapes=[pltpu.VMEM((32, 1), F32)],
        compiler_params=_cparams(("arbitrary",)),
        name="outproj",
    )(o_gf, o_gb, sgate, o_n, o_c, x_all, mod5, mod5, mod5, gnorm, e_mat, w_out, gffn, rw_split, bias_col)


def _route_tokens(lg, bias, carry):
    tile = lg.shape[0]
    lt = lg.T
    aff = _sigmoid(lt[0:N_EXPERTS])
    sel = aff + bias
    s = [sel[e:e + 1] for e in range(N_EXPERTS)]
    a = [aff[e:e + 1] for e in range(N_EXPERTS)]

    def top2sum(v):
        best = v[0] + v[1]
        for i, j in PAIRS[1:]:
            best = jnp.maximum(best, v[i] + v[j])
        return best

    gs = [top2sum(s[4 * g:4 * g + 4]) for g in range(N_GROUPS)]
    gbest = jnp.zeros_like(gs[0], dtype=jnp.int32)
    gmax = gs[0]
    for g in range(1, N_GROUPS):
        upd = gs[g] > gmax
        gbest = jnp.where(upd, g, gbest)
        gmax = jnp.where(upd, gs[g], gmax)

    def pick(vals, j):
        out = vals[j]
        for g in range(1, N_GROUPS):
            out = jnp.where(gbest == g, vals[4 * g + j], out)
        return out

    sv = [pick(s, j) for j in range(EXPERTS_PER_GROUP)]
    av = [pick(a, j) for j in range(EXPERTS_PER_GROUP)]
    i1 = jnp.zeros_like(gbest)
    m1 = sv[0]
    for j in range(1, EXPERTS_PER_GROUP):
        upd = sv[j] > m1
        i1 = jnp.where(upd, j, i1)
        m1 = jnp.where(upd, sv[j], m1)
    i2 = jnp.full_like(gbest, -1)
    m2 = jnp.zeros_like(m1)
    for j in range(EXPERTS_PER_GROUP):
        upd = (i1 != j) & ((sv[j] > m2) | (i2 < 0))
        i2 = jnp.where(upd, j, i2)
        m2 = jnp.where(upd, sv[j], m2)
    ia = jnp.minimum(i1, i2)
    ib = jnp.maximum(i1, i2)
    pair = jnp.where(ia == 0, ib - 1, jnp.where(ia == 1, ib + 1, 5))
    cls = gbest * len(PAIRS) + pair

    def take(vals, idx):
        out = vals[0]
        for j in range(1, EXPERTS_PER_GROUP):
            out = jnp.where(idx == j, vals[j], out)
        return out

    w1 = take(av, i1)
    w2 = take(av, i2)
    tot = w1 + w2
    wa = jnp.where(i1 < i2, w1, w2) / tot
    wb = jnp.where(i1 < i2, w2, w1) / tot

    crow = lax.broadcasted_iota(jnp.int32, (32, tile), 0)
    oh = (crow == cls).astype(F32)
    us = lax.broadcasted_iota(jnp.int32, (tile, tile), 0)
    ut = lax.broadcasted_iota(jnp.int32, (tile, tile), 1)
    upper = (us < ut).astype(BF16)
    prefix = _dot(oh.astype(BF16), upper)
    rank = jnp.sum(oh * (prefix + carry), axis=0, keepdims=True)
    carry_new = carry + jnp.sum(oh, axis=1, keepdims=True)

    wrow = lax.broadcasted_iota(jnp.int32, (LANE, tile), 0)
    wmat = jnp.where(wrow == 0, wa, jnp.where(wrow == 1, wb, 0.0))
    wtok = lax.bitcast_convert_type(wmat.T, jnp.uint32)
    return cls.astype(F32), rank, wtok, carry_new


FF_TILE = 512


def _expert_kernel(ea_ref, eb_ref, nvalid_ref, xs_ref, wga_ref, wua_ref, wda_ref, wgb_ref, wub_ref, wdb_ref,
                   *rest, n_cast):
    del ea_ref, eb_ref
    j = pl.program_id(0)
    nvalid = nvalid_ref[j]
    nxt_in, (y_ref, *nxt_out) = rest[:n_cast], rest[n_cast:]

    def cast_next():
        _cast_slices(nxt_in, nxt_out)

    @pl.when(nvalid == 0)
    def _():
        cast_next()
        y_ref[...] = jnp.zeros_like(y_ref)

    @pl.when(nvalid != 0)
    def _():
        cast_next()
        rows = y_ref.shape[0] * 8
        live = lax.broadcasted_iota(jnp.int32, (rows, 1), 0) < nvalid
        x = jnp.where(live, _unpack_rows(_load_subrows(xs_ref, 0, SUB_Y)), 0.0).astype(BF16)
        ws = jnp.where(live, lax.bitcast_convert_type(xs_ref[:, SUB_Y, :, :].reshape(rows, LANE), F32), 0.0)
        ff = wga_ref.shape[1]

        items = [(w, f0) for w in ((wga_ref, wua_ref, wda_ref), (wgb_ref, wub_ref, wdb_ref))
                 for f0 in range(0, ff, FF_TILE)]

        def up(item):
            (wg_ref, wu_ref, _), f0 = item
            return _dot(x, wg_ref[:, f0:f0 + FF_TILE]), _dot(x, wu_ref[:, f0:f0 + FF_TILE])

        ups = [up(items[0]), up(items[1])]
        parts = []
        for c, ((_, _, wd_ref), f0) in enumerate(items):
            hg, hu = ups[c]
            hh = (hg * _sigmoid(hg) * hu).astype(BF16)
            if c + 2 < len(items):
                ups.append(up(items[c + 2]))
            parts.append(_dot(hh, wd_ref[f0:f0 + FF_TILE, :]))
        per = len(items) // 2
        ya = functools.reduce(lambda a, b: a + b, parts[:per])
        yb = functools.reduce(lambda a, b: a + b, parts[per:])
        _store_subrows(y_ref, _pack_rows(ya * ws[:, 0:1] + yb * ws[:, 1:2]), 0)


def _experts(xs_sub, ea, eb, nvalid, weights, next_f32):
    wg, wu, wd = weights
    nb = xs_sub.shape[0] * 8 // MOE_BLK
    d, ff = wg.shape[1], wg.shape[2]
    wspec_in = lambda which: pl.BlockSpec((None, d, ff), lambda j, ea, eb, v: ((ea, eb)[which][j], 0, 0))
    wspec_out = lambda which: pl.BlockSpec((None, ff, d), lambda j, ea, eb, v: ((ea, eb)[which][j], 0, 0))
    in_specs = [pl.BlockSpec((MOE_BLK // 8, SUB_X, 8, LANE), lambda j, ea, eb, v: (j, 0, 0, 0)),
                wspec_in(0), wspec_in(0), wspec_out(0), wspec_in(1), wspec_in(1), wspec_out(1)]
    out_specs = [pl.BlockSpec((MOE_BLK // 8, SUB_Y, 8, LANE), lambda j, ea, eb, v: (j, 0, 0, 0))]
    out_shape = [jax.ShapeDtypeStruct((nb * MOE_BLK // 8, SUB_Y, 8, LANE), jnp.uint32)]
    args = [ea, eb, nvalid, xs_sub, wg, wu, wd, wg, wu, wd]
    if next_f32 is not None:
        c_in, c_out, c_shape = _cast_specs(next_f32, nb, lambda j, *_: j)
        in_specs += c_in
        out_specs += c_out
        out_shape += c_shape
        args += list(next_f32[0])
    grid_spec = pltpu.PrefetchScalarGridSpec(num_scalar_prefetch=3, grid=(nb,), in_specs=in_specs,
                                             out_specs=out_specs)
    res = pl.pallas_call(
        functools.partial(_expert_kernel, n_cast=0 if next_f32 is None else len(next_f32[0])),
        grid_spec=grid_spec,
        out_shape=out_shape,
        compiler_params=_cparams(("arbitrary",)),
        name="experts",
    )(*args)
    return res[0], tuple(res[1:])


def _subrow_index(dest, nsub):
    t = dest.shape[0]
    base = ((dest // 8) * (nsub * 8) + dest % 8).astype(F32).reshape(t // LANE, LANE)
    src = np.arange(LANE)
    grp, r = src // 8, src % 8
    sel = np.zeros((LANE, LANE * nsub), np.float32)
    off = np.zeros((LANE * nsub,), np.int32)
    for j in range(nsub):
        pos = grp * (nsub * 8) + j * 8 + r
        sel[src, pos] = 1.0
        off[pos] = j * 8
    idx = jnp.dot(base, jnp.asarray(sel), precision=lax.Precision.HIGHEST).astype(jnp.int32) + jnp.asarray(off)
    return idx.reshape(t * nsub)


def _moe(h2_sub, meta, cnt, weights, next_f32, split=()):
    t = meta.shape[1]
    cls = meta[0].astype(jnp.int32)
    rank = meta[1].astype(jnp.int32)
    counts = cnt[:N_CLASSES, 0].astype(jnp.int32)
    padded = (counts + MOE_BLK - 1) // MOE_BLK * MOE_BLK
    pad_end = jnp.cumsum(padded)
    pad_start = pad_end - padded
    class_ids = jnp.arange(N_CLASSES, dtype=jnp.int32)
    dest = rank + jnp.sum(jnp.where(cls[:, None] == class_ids[None, :], pad_start[None, :], 0), axis=1)
    nb = t // MOE_BLK + N_CLASSES
    p_rows = nb * MOE_BLK
    blk_start = jnp.arange(nb, dtype=jnp.int32) * MOE_BLK
    valid = blk_start < pad_end[-1]
    blk_cls = jnp.sum((pad_end[None, :] <= blk_start[:, None]).astype(jnp.int32), axis=-1)
    last_cls = jnp.sum((pad_end <= pad_end[-1] - 1).astype(jnp.int32))
    blk_cls = jnp.minimum(jnp.where(valid, blk_cls, last_cls), N_CLASSES - 1)
    nvalid = jnp.where(valid, jnp.clip(pad_start[blk_cls] + counts[blk_cls] - blk_start, 0, MOE_BLK), 0)
    pair_a = jnp.array([p[0] for p in PAIRS], jnp.int32)
    pair_b = jnp.array([p[1] for p in PAIRS], jnp.int32)
    grp = blk_cls // len(PAIRS)
    ea = grp * EXPERTS_PER_GROUP + pair_a[blk_cls % len(PAIRS)]
    eb = grp * EXPERTS_PER_GROUP + pair_b[blk_cls % len(PAIRS)]

    xs = _sc_scatter(h2_sub.reshape(t * SUB_X, LANE), _subrow_index(dest, SUB_X), p_rows * SUB_X)
    ys, next_bf16 = _experts(xs.reshape(p_rows // 8, SUB_X, 8, LANE), ea, eb, nvalid.astype(jnp.int32), weights,
                             next_f32)
    idx = _subrow_index(dest, SUB_Y)
    ys2 = ys.reshape(p_rows * SUB_Y, LANE)
    bounds = [0] + [s * SUB_Y for s in split] + [t * SUB_Y]
    parts = [_sc_gather(ys2, idx[a:b]).reshape((b - a) // (8 * SUB_Y), SUB_Y, 8, LANE)
             for a, b in zip(bounds[:-1], bounds[1:])]
    return (parts if split else parts[0]), next_bf16


def _final_kernel(x_ref, y_ref, g5_ref, gf_ref, o_ref):
    x = x_ref[...] + g5_ref[...] * _unpack_rows(_load_subrows(y_ref, 0, SUB_Y))
    ms = jnp.mean(x * x, axis=-1, keepdims=True)
    o_ref[...] = x * lax.rsqrt(ms + EPS) * gf_ref[...]


def _final(x_mid, y, mod5, layer, g_final, *, tiles_per_seq):
    t, d = x_mid.shape
    tok = pl.BlockSpec((TOK_TILE, d), lambda i: (i, 0))
    return pl.pallas_call(
        _final_kernel,
        grid=(t // TOK_TILE,),
        in_specs=[tok, pl.BlockSpec((TOK_TILE // 8, SUB_Y, 8, LANE), lambda i: (i, 0, 0, 0)),
                  pl.BlockSpec((None, None, None, 1, d), lambda i: (layer, i // tiles_per_seq, 5, 0, 0)),
                  pl.BlockSpec(g_final.shape, lambda i: (0, 0))],
        out_specs=tok,
        out_shape=jax.ShapeDtypeStruct((t, d), F32),
        compiler_params=_cparams(("parallel",)),
        name="final_norm",
    )(x_mid, y, mod5, g_final)


def _rope_tables(seq):
    t = jnp.arange(seq)
    row = (t // GRID_W).astype(F32)
    col = (t % GRID_W).astype(F32)
    half = GLA_DK // 2
    inv = ROPE_BASE ** (-jnp.arange(0, half, 2, dtype=F32) / half)
    ang = jnp.concatenate([row[:, None] * inv, col[:, None] * inv], axis=-1)
    cos = jnp.repeat(jnp.cos(ang), 2, axis=-1)
    sin = (jnp.sin(ang)[:, :, None] * jnp.array([-1.0, 1.0], F32)).reshape(seq, GLA_DK)
    cos = jnp.tile(cos, (1, GLA_HEADS))
    sin = jnp.tile(sin, (1, GLA_HEADS))
    padw = QK_PAD - GLA_QK
    cos = jnp.pad(cos, ((0, 0), (0, padw)), constant_values=1.0)
    sin = jnp.pad(sin, ((0, 0), (0, padw)))
    cos = jnp.concatenate([cos, jnp.ones((TOK_TILE, QK_PAD), F32)], axis=0)
    sin = jnp.concatenate([sin, jnp.zeros((TOK_TILE, QK_PAD), F32)], axis=0)
    return cos, sin


def _pad_last(w, n):
    return jnp.pad(w, [(0, 0)] * (w.ndim - 1) + [(0, n - w.shape[-1])])


def _in_weights(w_in):
    offs = np.cumsum([0, GLA_QK, GLA_QK, GLA_WIDTH, GLA_WIDTH, 2 * GLA_LOWRANK, NA_WIDTH, NA_WIDTH, NA_WIDTH,
                      2 * CONV_WIDTH])
    seg = [w_in[..., offs[i]:offs[i + 1]] for i in range(9)]
    cols = [_pad_last(seg[0], QK_PAD), _pad_last(seg[1], QK_PAD), _pad_last(seg[4], LANE), seg[2], seg[3], seg[5],
            seg[6], seg[7], seg[8]]
    w = jnp.concatenate(cols, axis=-1).astype(BF16)
    assert w.shape[-1] == IN_COLS_PAD
    return w


def _gate_weights(wa_f, ba_f, wa_b, ba_b):
    depth = wa_f.shape[0]
    zero = jnp.zeros((depth, GLA_LOWRANK, QK_PAD), F32)
    top = jnp.concatenate([_pad_last(wa_f, QK_PAD), zero], axis=-1)
    bot = jnp.concatenate([zero, _pad_last(wa_b, QK_PAD)], axis=-1)
    rest = jnp.zeros((depth, LANE - 2 * GLA_LOWRANK, 2 * QK_PAD), F32)
    wa = jnp.concatenate([top, bot, rest], axis=1)
    ba = jnp.concatenate([_pad_last(ba_f, QK_PAD), _pad_last(ba_b, QK_PAD)], axis=-1)[:, None, :]
    return wa, ba


def _na_bias_tables(rpb):
    cq = np.arange(GRID_W)
    c0 = np.clip(cq - NA_KW // 2, 0, GRID_W - NA_KW)
    kc = np.arange(GRID_W)
    valid = (kc[None, :] >= c0[:, None]) & (kc[None, :] < c0[:, None] + NA_KW)
    w = GRID_W
    padded = jnp.pad(rpb, [(0, 0)] * 3 + [(w - NA_KW, w - NA_KW)], constant_values=NEG_BIG)
    flat = jnp.tile(padded, (1, 1, 1, w))
    band = flat[..., w - 1:w - 1 + w * (2 * w - 2)].reshape(rpb.shape[:3] + (w, 2 * w - 2))[..., :w]
    band = jnp.where(valid, band, NEG_BIG)
    tab = jnp.stack([band[:, :, NA_KH - 1 - v:2 * NA_KH - 1 - v] for v in range(NA_KH)], axis=2)
    tab = jnp.transpose(tab, (0, 1, 2, 4, 3, 5))
    return tab.reshape(rpb.shape[:2] + (NA_KH, GRID_W, NA_KH * GRID_W)).astype(F32)


def _head_mean_matrix():
    h = np.arange(GLA_WIDTH) // GLA_DV
    e = (h[:, None] == h[None, :]).astype(np.float32)
    return jnp.asarray(np.concatenate([e, e], axis=0), dtype=BF16)


def kernel(x, c, ctx, c_ctx, w_mod, b_mod, g_mix, g_ffn, w_in, gla_wa_f, gla_ba_f, gla_wa_b, gla_ba_b, gla_g_norm,
           na_rpb, conv_w, conv_b, conv_ln_g, conv_ln_b, w_out, router_w, router_bias, w_gate, w_up, w_down,
           g_final):
    bsz, seq, d = x.shape
    ctx_len = ctx.shape[1]
    depth = w_mod.shape[0]
    t_lat, t_ctx = bsz * seq, bsz * ctx_len
    assert seq % TOK_TILE == 0 and t_ctx % TOK_TILE == 0 and seq % ctx_len == 0
    assert d == 2 * SUB_Y * LANE
    assert seq // GRID_W >= NA_KH and (seq // GRID_W) % NA_UNROLL == 0
    assert ctx_len % (GLA_CHUNK * GLA_STEPS) == 0 and seq % (GLA_CHUNK * GLA_STEPS) == 0
    tiles_per_seq = seq // TOK_TILE
    n_lat_tiles = t_lat // TOK_TILE
    n_all_tiles = (t_lat + t_ctx) // TOK_TILE
    ctx_blk0 = t_lat // ctx_len

    mod_rows = -(-(bsz + 1) // 8) * 8
    c_pad = jnp.zeros((mod_rows, d), F32).at[:bsz].set(c).at[bsz].set(c_ctx)
    mod = _modulation(c_pad, w_mod, b_mod)
    mod5 = mod.reshape(depth, mod_rows, 6, 1, d)

    cos_t, sin_t = _rope_tables(seq)
    e_mat = _head_mean_matrix()
    rw = jnp.pad(router_w, ((0, 0), (0, LANE - N_EXPERTS)))
    rw_hi = rw.astype(BF16)
    rw_lo = (rw - rw_hi.astype(F32)).astype(BF16)
    rw_split = jnp.concatenate([rw_hi, rw_lo], axis=1)
    bias_col = router_bias.reshape(N_EXPERTS, 1).astype(F32)
    zero_state = jnp.zeros((bsz, GLA_WIDTH, QK_PAD), F32)

    w_aug = _in_weights(w_in)
    wa_aug, ba_aug = _gate_weights(gla_wa_f, gla_ba_f, gla_wa_b, gla_ba_b)
    bias_tab = _na_bias_tables(na_rpb)
    row = lambda a: a[:, None, :]
    gmix, gffn, gnorm = row(g_mix), row(g_ffn), row(gla_g_norm)
    cb, clg, clb = row(conv_b), row(conv_ln_g), row(conv_ln_b)
    w_out_b = w_out.astype(BF16)
    expert_f32 = (w_gate, w_up, w_down)
    layer_rows = w_gate.shape[1] * w_gate.shape[2]
    assert w_down.shape[1] * w_down.shape[2] == layer_rows
    expert_rows = tuple(w.reshape(depth * layer_rows, w.shape[-1]) for w in expert_f32)
    as_experts = lambda flat: tuple(a.reshape(w.shape[1:]) for a, w in zip(flat, expert_f32))

    x_all = (x.reshape(t_lat, d), ctx.reshape(t_ctx, d))
    half_tiles = n_all_tiles // 2
    y_moe = None
    for l in range(depth):
        last = l == depth - 1
        common = dict(n_lat_tiles=n_lat_tiles, tiles_per_seq=tiles_per_seq, n_batch=bsz, res_layer=l - 1)
        if l == 0:
            x_all, proj = _inproj(x_all, None, mod5, l, gmix, cos_t, sin_t, w_aug, wa_aug, ba_aug, **common)
        else:
            x_new, proj = _inproj(x_all, y_moe[0], mod5, l, gmix, cos_t, sin_t, w_aug, wa_aug, ba_aug,
                                  tile0=0, n_call_tiles=half_tiles, **common)
            x_all, proj = _inproj(x_all, y_moe[1], mod5, l, gmix, cos_t, sin_t, w_aug, wa_aug, ba_aug,
                                  tile0=half_tiles, n_call_tiles=n_all_tiles - half_tiles, prev=(x_new,) + tuple(proj),
                                  **common)
        q, k, v, sgate, gf, gb, nq, nk, nv, hcv = proj

        o_g, st_f, st_b = _gla(q, k, v, gf, gb, zero_state, zero_state, None, seq=ctx_len, blk0=ctx_blk0,
                               n_batch=bsz)
        o_g, _, _ = _gla(q, k, v, gf, gb, st_f, st_b, o_g, seq=seq, blk0=0, n_batch=bsz)

        o_n, cast0 = _na_latent(nq, nk, nv, bias_tab, l, (expert_rows, layer_rows, 0) if l == 0 else None,
                                seq=seq, ctx_len=ctx_len, n_batch=bsz, ctx_blk0=ctx_blk0)
        if l == 0:
            weights = as_experts(cast0)
        o_c = _conv(hcv, conv_w, cb, clg, clb, l, None, seq=seq, blk0=0, n_batch=bsz)
        if not last:
            o_n = _na_context(nq, nk, nv, o_n, ctx_len=ctx_len, n_batch=bsz, ctx_blk0=ctx_blk0)
            o_c = _conv(hcv, conv_w, cb, clg, clb, l, o_c, seq=ctx_len, blk0=ctx_blk0, n_batch=bsz)

        n_tiles = n_lat_tiles if last else n_all_tiles
        x_mid, h2, meta, cnt = _outproj(
            o_g[0], o_g[1], sgate, o_n, o_c, x_all, mod5, l, gnorm, e_mat, w_out_b, gffn, rw_split, bias_col,
            n_tiles=n_tiles, n_lat_tiles=n_lat_tiles, tiles_per_seq=tiles_per_seq, n_batch=bsz)
        next_f32 = None if last else (expert_rows, layer_rows, l + 1)
        y_moe, next_bf16 = _moe(h2, meta, cnt, weights, next_f32, split=() if last else (half_tiles * TOK_TILE,))
        if not last:
            weights = as_experts(next_bf16)
        x_all = x_mid

    out = _final(x_all, y_moe, mod5, depth - 1, g_final.reshape(1, d), tiles_per_seq=tiles_per_seq)
    return out.reshape(bsz, seq, d)
```

```python
import functools

import numpy as np
import jax
import jax.numpy as jnp
from jax import lax
from jax.experimental import pallas as pl
from jax.experimental.pallas import tpu as pltpu
from jax.experimental.pallas import tpu_sc as plsc

GRID_W = 64
EPS = 1e-6
GLA_HEADS, GLA_DK, GLA_DV = 4, 48, 96
GLA_QK = GLA_HEADS * GLA_DK
GLA_WIDTH = GLA_HEADS * GLA_DV
GLA_LOWRANK = 16
GLA_TAU = 16.0
ROPE_BASE = 10000.0
NA_HEADS, NA_DH = 6, 64
NA_WIDTH = NA_HEADS * NA_DH
NA_KH, NA_KW = 8, 16
CONV_WIDTH, CONV_K = 256, 31
N_EXPERTS, N_GROUPS, EXPERTS_PER_GROUP = 16, 4, 4
PAIRS = ((0, 1), (0, 2), (0, 3), (1, 2), (1, 3), (2, 3))
N_CLASSES = N_GROUPS * len(PAIRS)

LANE = 128
SUBLANES = 8
QK_PAD = 256
C_Q, C_K = 0, 256
C_A, C_V, C_GATE = 512, 640, 1024
C_NQ, C_NK, C_NV, C_CONV = 1408, 1792, 2176, 2560
IN_COLS_PAD = 3072
MXU_N = 256

TOK_TILE = 1024
ROW_PARTS = 4
GLA_CHUNK = 64
GLA_SUB = 16
GLA_STEPS = 4
MOE_BLK = 256
NA_UNROLL = 8
NEG_BIG = -1e30
VMEM_PER_CORE = 64 * 1024 * 1024
VMEM_LIMIT = VMEM_PER_CORE - 8 * 1024 * 1024

F32 = jnp.float32
BF16 = jnp.bfloat16


def _cparams(sem):
    return pltpu.CompilerParams(dimension_semantics=sem, vmem_limit_bytes=VMEM_LIMIT)


def _cast_specs(next_f32, n_steps, step_of):
    stacked, layer_rows, layer = next_f32
    cast_steps = 1 << (n_steps.bit_length() - 1)
    cast_rows = layer_rows // cast_steps
    assert cast_rows * cast_steps == layer_rows and cast_rows % 16 == 0
    step = lambda *g: jnp.minimum(step_of(*g), cast_steps - 1)
    in_specs = [pl.BlockSpec((cast_rows, a.shape[1]), lambda *g: (layer * cast_steps + step(*g), 0)) for a in stacked]
    out_specs = [pl.BlockSpec((cast_rows, a.shape[1]), lambda *g: (step(*g), 0)) for a in stacked]
    out_shape = [jax.ShapeDtypeStruct((layer_rows, a.shape[1]), BF16) for a in stacked]
    return in_specs, out_specs, out_shape


def _cast_slices(srcs, dsts):
    for src, dst in zip(srcs, dsts):
        dst[...] = src[...].astype(BF16)


def _layer_spec(a, layer):
    return pl.BlockSpec((None,) + a.shape[1:], lambda *_: (layer,) + (0,) * (a.ndim - 1),
                        pipeline_mode=pl.Buffered(1))


def _dot(a, b):
    return jnp.dot(a, b, preferred_element_type=F32)


def _dot_nt(a, b):
    return lax.dot_general(a, b, (((1,), (1,)), ((), ())), preferred_element_type=F32)


def _split(a):
    hi = a.astype(BF16)
    lo = (a - hi.astype(F32)).astype(BF16)
    return hi, lo


def _sigmoid(x):
    return 1.0 / (1.0 + jnp.exp(-x))


SUB_X, SUB_Y = 5, 4
SC_WINDOW = 256


def _pack_bf16_pairs(a, b):
    ua = lax.bitcast_convert_type(a.astype(BF16).astype(F32), jnp.uint32)
    ub = lax.bitcast_convert_type(b.astype(BF16).astype(F32), jnp.uint32)
    return jnp.bitwise_or(jnp.right_shift(ua, jnp.uint32(16)), ub)


def _unpack_bf16_pairs(w):
    lo = lax.bitcast_convert_type(jnp.left_shift(w, jnp.uint32(16)), F32)
    hi = lax.bitcast_convert_type(jnp.bitwise_and(w, jnp.uint32(0xFFFF0000)), F32)
    return lo, hi


def _store_subrows(ref, val, j0, row0=0):
    r = val.shape[0]
    for j in range(val.shape[1] // LANE):
        ref[row0 // 8:(row0 + r) // 8, j0 + j, :, :] = val[:, j * LANE:(j + 1) * LANE].reshape(r // 8, 8, LANE)


def _load_subrows(ref, j0, n, row0=0, rows=None):
    r = ref.shape[0] * 8 if rows is None else rows
    return jnp.concatenate([ref[row0 // 8:(row0 + r) // 8, j0 + j, :, :].reshape(r, LANE) for j in range(n)],
                           axis=1)


def _pack_rows(x):
    half = x.shape[1] // 2
    return _pack_bf16_pairs(x[:, :half], x[:, half:])


def _unpack_rows(w):
    lo, hi = _unpack_bf16_pairs(w)
    return jnp.concatenate([lo, hi], axis=1)


def _sc_mesh():
    return plsc.VectorSubcoreMesh(core_axis_name="c", subcore_axis_name="s")


def _sc_scatter(src, idx, n_out):
    n, w = src.shape
    idx2 = idx.reshape(1, n)

    @pl.kernel(out_type=jax.ShapeDtypeStruct((n_out, w), src.dtype), mesh=_sc_mesh(), scratch_types=[])
    def scatter_kernel(x_hbm, i_hbm, o_hbm):
        def body(x_vmem, i_vmem):
            pltpu.sync_copy(x_vmem, o_hbm.at[i_vmem.at[0]])

        pltpu.emit_pipeline(
            body,
            grid=(n // SC_WINDOW,),
            in_specs=[pl.BlockSpec((SC_WINDOW, w), index_map=lambda i: (i, 0)),
                      pl.BlockSpec((1, SC_WINDOW), index_map=lambda i: (0, i))],
            out_specs=[],
            core_axis_name=("c", "s"),
            dimension_semantics=(pltpu.PARALLEL,),
        )(x_hbm, i_hbm)

    return scatter_kernel(src, idx2)


def _sc_gather(src, idx):
    n = idx.shape[0]
    w = src.shape[1]
    idx2 = idx.reshape(1, n)

    @pl.kernel(out_type=jax.ShapeDtypeStruct((n, w), src.dtype), mesh=_sc_mesh())
    def gather_kernel(x_hbm, i_hbm, o_hbm):
        def body(i_vmem, o_vmem):
            pltpu.sync_copy(x_hbm.at[i_vmem.at[0]], o_vmem)

        pltpu.emit_pipeline(
            body,
            grid=(n // SC_WINDOW,),
            in_specs=[pl.BlockSpec((1, SC_WINDOW), index_map=lambda i: (0, i))],
            out_specs=[pl.BlockSpec((SC_WINDOW, w), index_map=lambda i: (i, 0))],
            core_axis_name=("c", "s"),
            dimension_semantics=(pltpu.PARALLEL,),
        )(i_hbm, o_hbm)

    return gather_kernel(src, idx2)


def _mod_kernel(c_ref, w_ref, b_ref, o_ref):
    cv = c_ref[...]
    s = cv * _sigmoid(cv)
    s_hi, s_lo = _split(s)
    w_hi, w_lo = _split(w_ref[...])
    o_ref[...] = _dot(s_hi, w_hi) + _dot(s_lo, w_hi) + _dot(s_hi, w_lo) + b_ref[...]


def _modulation(c_pad, w_mod, b_mod):
    depth, d, six_d = w_mod.shape
    rows = c_pad.shape[0]
    nt = 1536
    return pl.pallas_call(
        _mod_kernel,
        grid=(depth, six_d // nt),
        in_specs=[
            pl.BlockSpec((rows, d), lambda l, j: (0, 0)),
            pl.BlockSpec((None, d, nt), lambda l, j: (l, 0, j)),
            pl.BlockSpec((None, 1, nt), lambda l, j: (l, 0, j)),
        ],
        out_specs=pl.BlockSpec((None, rows, nt), lambda l, j: (l, 0, j)),
        out_shape=jax.ShapeDtypeStruct((depth, rows, six_d), F32),
        compiler_params=_cparams(("arbitrary", "arbitrary")),
        name="modulation",
    )(c_pad, w_mod, b_mod.reshape(depth, 1, six_d))


def _inproj_kernel(*refs, has_res, n_lat_tiles, n_alias):
    if has_res:
        x_ref, y_ref, g5_ref = refs[:3]
    else:
        x_ref, xctx_ref = refs[:2]
        is_lat = pl.program_id(0) < n_lat_tiles
    refs = refs[3 if has_res else 2:]
    (sh_ref, sc_ref, gmix_ref, cos_ref, sin_ref, w_ref, wa_ref, ba_ref) = refs[:8]
    (xnew_ref, q_ref, k_ref, v_ref, sg_ref, gf_ref, gb_ref, nq_ref, nk_ref, nv_ref, hcv_ref) = refs[8 + n_alias:]

    tile = x_ref.shape[0]
    part = tile // ROW_PARTS
    wa_hi, wa_lo = _split(wa_ref[...])
    even_lane = jnp.bitwise_and(lax.broadcasted_iota(jnp.int32, (1, LANE), 1), 1) == 0
    hbs = []
    for p in range(ROW_PARTS):
        rows = pl.ds(p * part, part)
        if has_res:
            x = x_ref[rows, :] + g5_ref[...] * _unpack_rows(_load_subrows(y_ref, 0, SUB_Y, row0=p * part, rows=part))
        else:
            x = jnp.where(is_lat, x_ref[rows, :], xctx_ref[rows, :])
        xnew_ref[rows, :] = x
        ms = jnp.mean(x * x, axis=-1, keepdims=True)
        h = x * lax.rsqrt(ms + EPS) * gmix_ref[...]
        h = h * (1.0 + sc_ref[...]) + sh_ref[...]
        hbs.append(h.astype(BF16))

    for p, hb in enumerate(hbs):
        rows = pl.ds(p * part, part)

        def proj(c0, n, hb=hb):
            return _dot(hb, w_ref[:, c0:c0 + n])

        cos = cos_ref[rows, :]
        sin = sin_ref[rows, :]

        def rope(t):
            halves = []
            for c0 in range(0, QK_PAD, LANE):
                th = t[:, c0:c0 + LANE]
                halves.append(jnp.where(even_lane, pltpu.roll(th, LANE - 1, 1), pltpu.roll(th, 1, 1)))
            return t * cos + jnp.concatenate(halves, axis=1) * sin

        mid = (C_A + C_CONV) // 2
        assert (mid - C_A) % MXU_N == 0 and C_NQ < mid < C_NK
        d1 = proj(C_A, mid - C_A)
        a_hi, a_lo = _split(d1[:, :LANE])
        q_ref[rows, :] = (rope(proj(C_Q, QK_PAD)) * (GLA_DK ** -0.5)).astype(BF16)
        k_ref[rows, :] = rope(proj(C_K, QK_PAD)).astype(BF16)

        z = _dot(a_hi, wa_hi) + _dot(a_lo, wa_hi) + _dot(a_hi, wa_lo) + ba_ref[...]
        logsig = jnp.minimum(z, 0.0) - jnp.log(1.0 + jnp.exp(-jnp.abs(z)))
        g = logsig * (1.0 / GLA_TAU)
        gf_ref[rows, :] = g[:, :QK_PAD]
        gb_ref[rows, :] = g[:, QK_PAD:]

        v_ref[rows, :] = d1[:, C_V - C_A:C_GATE - C_A].astype(BF16)
        gate = d1[:, C_GATE - C_A:C_NQ - C_A]
        sg_ref[rows, :] = (gate * _sigmoid(gate)).astype(BF16)
        d2 = proj(mid, C_CONV - mid)
        nq = jnp.concatenate([d1[:, C_NQ - C_A:], d2[:, :C_NK - mid]], axis=1)
        nq_ref[rows, :] = (nq * (NA_DH ** -0.5)).astype(BF16)
        nk_ref[rows, :] = d2[:, C_NK - mid:C_NV - mid].astype(BF16)
        nv_ref[rows, :] = d2[:, C_NV - mid:].astype(BF16)
        u = proj(C_CONV, 2 * CONV_WIDTH)
        hcv_ref[rows, :] = u[:, :CONV_WIDTH] * _sigmoid(u[:, CONV_WIDTH:])


def _inproj(x_all, y, mod5, layer, gmix, cos_t, sin_t, w_aug, wa_aug, ba_aug, *, n_lat_tiles, tiles_per_seq,
            n_batch, res_layer, tile0=0, n_call_tiles=None, prev=None):
    has_res = y is not None
    d = x_all.shape[1] if has_res else x_all[0].shape[1]
    t_all = x_all.shape[0] if has_res else x_all[0].shape[0] + x_all[1].shape[0]
    nt = t_all // TOK_TILE if n_call_tiles is None else n_call_tiles

    def bidx(i):
        i = i + tile0
        return jnp.where(i < n_lat_tiles, i // tiles_per_seq, n_batch)

    def ridx(i):
        i = i + tile0
        return jnp.where(i < n_lat_tiles, i % tiles_per_seq, tiles_per_seq)

    def modspec(l, j):
        return pl.BlockSpec((None, None, None, 1, d), lambda i: (l, bidx(i), j, 0, 0))

    tok = lambda w: pl.BlockSpec((TOK_TILE, w), lambda i: (i + tile0, 0))
    full = lambda a: pl.BlockSpec(a.shape, lambda i: (0,) * a.ndim)

    if has_res:
        in_specs = [tok(d), pl.BlockSpec((TOK_TILE // 8, SUB_Y, 8, LANE), lambda i: (i, 0, 0, 0)),
                    modspec(res_layer, 5)]
        args = [x_all, y, mod5]
    else:
        in_specs = [pl.BlockSpec((TOK_TILE, d), lambda i: (jnp.minimum(i, n_lat_tiles - 1), 0)),
                    pl.BlockSpec((TOK_TILE, d), lambda i: (jnp.maximum(i - n_lat_tiles, 0), 0))]
        args = list(x_all)
    in_specs += [modspec(layer, 0), modspec(layer, 1), _layer_spec(gmix, layer),
                 pl.BlockSpec((TOK_TILE, QK_PAD), lambda i: (ridx(i), 0)),
                 pl.BlockSpec((TOK_TILE, QK_PAD), lambda i: (ridx(i), 0)),
                 _layer_spec(w_aug, layer), _layer_spec(wa_aug, layer), _layer_spec(ba_aug, layer)]
    args += [mod5, mod5, gmix, cos_t, sin_t, w_aug, wa_aug, ba_aug]

    out_widths = [(d, F32), (QK_PAD, BF16), (QK_PAD, BF16), (GLA_WIDTH, BF16), (GLA_WIDTH, BF16), (QK_PAD, F32),
                  (QK_PAD, F32), (NA_WIDTH, BF16), (NA_WIDTH, BF16), (NA_WIDTH, BF16), (CONV_WIDTH, F32)]
    out_specs = [tok(w) for w, _ in out_widths]
    out_shape = [jax.ShapeDtypeStruct((t_all, w), dt) for w, dt in out_widths]
    aliases = {}
    if prev is not None:
        aliases = {len(args) + n: n for n in range(len(prev))}
        in_specs += [pl.BlockSpec(memory_space=pl.ANY)] * len(prev)
        args += list(prev)
    res = pl.pallas_call(
        functools.partial(_inproj_kernel, has_res=has_res, n_lat_tiles=n_lat_tiles, n_alias=len(aliases)),
        grid=(nt,),
        in_specs=in_specs,
        out_specs=out_specs,
        out_shape=out_shape,
        input_output_aliases=aliases,
        compiler_params=_cparams(("parallel",)),
        name="inproj",
    )(*args)
    return res[0], res[1:]


def _gla_masks():
    c, sub = GLA_CHUNK, GLA_SUB
    lane_qk = lax.broadcasted_iota(jnp.int32, (1, QK_PAD), 1)
    head_qk = ((lane_qk >= GLA_DK).astype(jnp.int32) + (lane_qk >= 2 * GLA_DK).astype(jnp.int32)
               + (lane_qk >= 3 * GLA_DK).astype(jnp.int32) + 4 * (lane_qk >= 4 * GLA_DK).astype(jnp.int32))
    row_h = jnp.right_shift(lax.broadcasted_iota(jnp.int32, (c, 1), 0), GLA_SUB.bit_length() - 1)
    hm = (row_h == head_qk).astype(F32)
    row_v = lax.broadcasted_iota(jnp.int32, (GLA_WIDTH, 1), 0)
    head_v = ((row_v >= GLA_DV).astype(jnp.int32) + (row_v >= 2 * GLA_DV).astype(jnp.int32)
              + (row_v >= 3 * GLA_DV).astype(jnp.int32))
    bd = (head_v == head_qk).astype(F32)
    lane_v = lax.broadcasted_iota(jnp.int32, (1, GLA_WIDTH), 1)
    vm = [((lane_v >= h * GLA_DV) & (lane_v < (h + 1) * GLA_DV)).astype(F32) for h in range(GLA_HEADS)]
    return hm, bd, vm


def _gla_steps(q, k, v, g, s_t, hm, bd, vm):
    c, sub = GLA_CHUNK, GLA_SUB
    nsub = c // sub
    dirs = (True, False)
    items = [(u, d) for u in range(GLA_STEPS) for d in range(2)]
    ri = lax.broadcasted_iota(jnp.int32, (c, c), 0)
    ci = lax.broadcasted_iota(jnp.int32, (c, c), 1)
    key_row = lax.broadcasted_iota(jnp.int32, (c, 1), 0)
    att_row = jnp.bitwise_and(ri, sub - 1)
    tri = [((ci <= ri) if fwd else (ci >= ri)).astype(BF16) for fwd in dirs]

    cums = {}
    for u, d in items:
        g_hi, g_lo = _split(g[u][d])
        cums[u, d] = _dot(tri[d], g_hi) + _dot(tri[d], g_lo)

    qe, kv, decay, atts = {}, {}, {}, {}
    for u, d in items:
        fwd = dirs[d]
        cum = cums[u, d]
        tot = cum[c - 1:c] if fwd else cum[0:1]
        qe[u, d] = (q[u][d] * jnp.exp(cum)).astype(BF16)
        k_end = (k[u][d] * jnp.exp(tot - cum)).astype(BF16)
        kv[u, d] = lax.dot_general(v[u][d], k_end, (((0,), (0,)), ((), ())), preferred_element_type=F32)
        decay[u, d] = jnp.exp(tot)
        att_d = []
        for i in range(nsub):
            lo, hi = i * sub, (i + 1) * sub
            if fwd:
                ref = cum[lo - 1:lo] if i > 0 else jnp.zeros((1, QK_PAD), F32)
                key_ok = key_row < hi
                causal = ci <= att_row + lo
            else:
                ref = cum[hi:hi + 1] if i < nsub - 1 else jnp.zeros((1, QK_PAD), F32)
                key_ok = key_row >= lo
                causal = ci >= att_row + lo
            qi = q[u][d][lo:hi] * jnp.exp(cum[lo:hi] - ref)
            qs = (jnp.concatenate([qi] * GLA_HEADS, axis=0) * hm).astype(BF16)
            ki = (k[u][d] * jnp.exp(jnp.where(key_ok, ref - cum, NEG_BIG))).astype(BF16)
            att = _dot_nt(qs, ki)
            att_d.append(jnp.where(causal, att, 0.0).astype(BF16))
        atts[u, d] = jnp.concatenate(att_d, axis=0)

    s = list(s_t)
    o_inter = {}
    for u, d in items:
        o_inter[u, d] = _dot_nt(qe[u, d], s[d].astype(BF16))
        s[d] = s[d] * decay[u, d] + bd * kv[u, d]

    outs = [[None, None] for _ in range(GLA_STEPS)]
    for u, d in items:
        r = _dot(atts[u, d], v[u][d])
        blocks = []
        for i in range(nsub):
            base = i * c
            oi = r[base:base + sub] * vm[0]
            for h in range(1, GLA_HEADS):
                oi = oi + r[base + h * sub:base + (h + 1) * sub] * vm[h]
            blocks.append(oi)
        outs[u][d] = o_inter[u, d] + jnp.concatenate(blocks, axis=0)
    return outs, s


def _gla_kernel(q_ref, k_ref, v_ref, gf_ref, gb_ref, sf0_ref, sb0_ref, of_ref, ob_ref, sf_ref, sb_ref):
    n = q_ref.shape[0]
    nc = n // GLA_CHUNK
    hm, bd, vm = _gla_masks()
    sf_ref[...] = sf0_ref[...]
    sb_ref[...] = sb0_ref[...]

    def body(j, carry):
        rows = [[pl.ds(pl.multiple_of(cidx * GLA_CHUNK, GLA_CHUNK), GLA_CHUNK)
                 for cidx in (j * GLA_STEPS + u, nc - 1 - (j * GLA_STEPS + u))] for u in range(GLA_STEPS)]
        q = [[q_ref[r, :].astype(F32) for r in ru] for ru in rows]
        k = [[k_ref[r, :].astype(F32) for r in ru] for ru in rows]
        v = [[v_ref[r, :] for r in ru] for ru in rows]
        g = [[gf_ref[ru[0], :], gb_ref[ru[1], :]] for ru in rows]
        outs, s_new = _gla_steps(q, k, v, g, [sf_ref[...], sb_ref[...]], hm, bd, vm)
        sf_ref[...] = s_new[0]
        sb_ref[...] = s_new[1]
        for u in range(GLA_STEPS):
            of_ref[rows[u][0], :] = outs[u][0]
            ob_ref[rows[u][1], :] = outs[u][1]
        return carry

    lax.fori_loop(0, nc // GLA_STEPS, body, 0)


def _gla(q, k, v, gf, gb, sf0, sb0, o_prev, *, seq, blk0, n_batch):
    t_all = q.shape[0]
    tokw = lambda w: pl.BlockSpec((seq, w), lambda b: (blk0 + b, 0))
    st = pl.BlockSpec((None, GLA_WIDTH, QK_PAD), lambda b: (b, 0, 0))
    in_specs = [tokw(QK_PAD), tokw(QK_PAD), tokw(GLA_WIDTH), tokw(QK_PAD), tokw(QK_PAD), st, st]
    args = [q, k, v, gf, gb, sf0, sb0]
    aliases = {}
    n_in = len(args)
    if o_prev is not None:
        in_specs += [pl.BlockSpec(memory_space=pl.ANY)] * 2
        args += list(o_prev)
        aliases = {n_in: 0, n_in + 1: 1}

    def kern(*refs):
        _gla_kernel(*refs[:n_in], *refs[len(args):])

    st_shape = jax.ShapeDtypeStruct((n_batch, GLA_WIDTH, QK_PAD), F32)
    o_shape = jax.ShapeDtypeStruct((t_all, GLA_WIDTH), F32)
    o_f, o_b, s_f, s_b = pl.pallas_call(
        kern,
        grid=(n_batch,),
        in_specs=in_specs,
        out_specs=[tokw(GLA_WIDTH), tokw(GLA_WIDTH), st, st],
        out_shape=[o_shape, o_shape, st_shape, st_shape],
        input_output_aliases=aliases,
        compiler_params=_cparams(("parallel",)),
        name="gla",
    )(*args)
    return (o_f, o_b), s_f, s_b


def _na_kernel(q_ref, k_ref, v_ref, kc_ref, vc_ref, bias_ref, *rest, n_cast):
    cast_in, (o_ref, *cast_out) = rest[:n_cast], rest[n_cast:]
    _cast_slices(cast_in, cast_out)
    n = q_ref.shape[0]
    rows = n // GRID_W
    nkeys = NA_KH * GRID_W
    lane = lax.broadcasted_iota(jnp.int32, (1, LANE), 1)
    first = lane < NA_DH
    kc = kc_ref[...]
    vc = vc_ref[...]

    def body(jb, carry):
        items = []
        for j in range(NA_UNROLL):
            r = jb * NA_UNROLL + j
            r0 = jnp.clip(r - NA_KH // 2, 0, rows - NA_KH)
            var = r - r0
            qrows = pl.ds(pl.multiple_of(r * GRID_W, GRID_W), GRID_W)
            krows = pl.ds(pl.multiple_of(r0 * GRID_W, GRID_W), nkeys)
            qr = q_ref[qrows, :]
            kb = k_ref[krows, :]
            q2 = jnp.concatenate([jnp.where(first, qr, jnp.zeros_like(qr)),
                                  jnp.where(first, jnp.zeros_like(qr), qr)], axis=0)
            items.append((qrows, krows, var, _dot_nt(q2, kb), _dot_nt(q2, kc)))
        probs = []
        for qrows, krows, var, s_loc, s_ctx in items:
            s_loc = s_loc + jnp.concatenate([bias_ref[0, var], bias_ref[1, var]], axis=0)
            m = jnp.maximum(jnp.max(s_loc, axis=-1, keepdims=True), jnp.max(s_ctx, axis=-1, keepdims=True))
            p_loc = jnp.exp(s_loc - m)
            p_ctx = jnp.exp(s_ctx - m)
            l = jnp.sum(p_loc, axis=-1, keepdims=True) + jnp.sum(p_ctx, axis=-1, keepdims=True)
            probs.append((p_loc.astype(BF16), p_ctx.astype(BF16), l))
        for (qrows, krows, var, _, _), (p_loc, p_ctx, l) in zip(items, probs):
            o = (_dot(p_loc, v_ref[krows, :]) + _dot(p_ctx, vc)) / l
            o_ref[qrows, :] = jnp.where(first, o[:GRID_W], o[GRID_W:]).astype(o_ref.dtype)
        return carry

    lax.fori_loop(0, rows // NA_UNROLL, body, 0)


def _na_latent(nq, nk, nv, bias_tab, layer, cast_f32=None, *, seq, ctx_len, n_batch, ctx_blk0):
    t_all = nq.shape[0]
    npair = NA_HEADS // 2
    lat = pl.BlockSpec((seq, LANE), lambda b, p: (b, p))
    ctx = pl.BlockSpec((ctx_len, LANE), lambda b, p: (ctx_blk0 + b, p))
    bias = pl.BlockSpec((None, 2) + bias_tab.shape[2:], lambda b, p: (layer, p, 0, 0, 0))
    in_specs = [lat, lat, lat, ctx, ctx, bias]
    out_specs = [lat]
    out_shape = [jax.ShapeDtypeStruct((t_all, NA_WIDTH), BF16)]
    args = [nq, nk, nv, nk, nv, bias_tab]
    if cast_f32 is not None:
        c_in, c_out, c_shape = _cast_specs(cast_f32, n_batch * npair, lambda b, p: b * npair + p)
        in_specs += c_in
        out_specs += c_out
        out_shape += c_shape
        args += list(cast_f32[0])
    res = pl.pallas_call(
        functools.partial(_na_kernel, n_cast=0 if cast_f32 is None else len(cast_f32[0])),
        grid=(n_batch, npair),
        in_specs=in_specs,
        out_specs=out_specs,
        out_shape=out_shape,
        compiler_params=_cparams(("arbitrary", "arbitrary")),
        name="na_latent",
    )(*args)
    return res[0], tuple(res[1:])


def _na_ctx_kernel(q_ref, k_ref, v_ref, o_in_ref, o_ref):
    del o_in_ref
    lane = lax.broadcasted_iota(jnp.int32, (1, LANE), 1)
    first = lane < NA_DH
    q = q_ref[...]
    k = k_ref[...]
    v = v_ref[...]
    res = []
    for h in range(2):
        sel = first if h == 0 else jnp.logical_not(first)
        qh = jnp.where(sel, q, jnp.zeros_like(q))
        s = _dot_nt(qh, k)
        m = jnp.max(s, axis=-1, keepdims=True)
        p = jnp.exp(s - m)
        l = jnp.sum(p, axis=-1, keepdims=True)
        res.append(_dot(p.astype(BF16), v) / l)
    o_ref[...] = jnp.where(first, res[0], res[1]).astype(o_ref.dtype)


def _na_context(nq, nk, nv, o_prev, *, ctx_len, n_batch, ctx_blk0):
    npair = NA_HEADS // 2
    ctx = pl.BlockSpec((ctx_len, LANE), lambda b, p: (ctx_blk0 + b, p))
    return pl.pallas_call(
        _na_ctx_kernel,
        grid=(n_batch, npair),
        in_specs=[ctx, ctx, ctx, pl.BlockSpec(memory_space=pl.ANY)],
        out_specs=ctx,
        out_shape=jax.ShapeDtypeStruct(o_prev.shape, o_prev.dtype),
        input_output_aliases={3: 0},
        compiler_params=_cparams(("parallel", "arbitrary")),
        name="na_context",
    )(nq, nk, nv, o_prev)


CONV_HALO = 16
CONV_ROWS = 128


def _conv_kernel(*refs, has_prev):
    if has_prev:
        h_ref, w_ref, b_ref, lg_ref, lb_ref, _, o_ref, pad_ref, sh_ref = refs
    else:
        h_ref, w_ref, b_ref, lg_ref, lb_ref, o_ref, pad_ref, sh_ref = refs
    n = h_ref.shape[0]
    zeros = jnp.zeros((CONV_HALO, CONV_WIDTH), F32)
    pad_ref[0:CONV_HALO, :] = zeros
    pad_ref[CONV_HALO + n:CONV_HALO + n + CONV_HALO, :] = zeros
    pad_ref[CONV_HALO:CONV_HALO + n, :] = h_ref[...]
    span = n + 2 * CONV_HALO - SUBLANES
    for s in range(1, SUBLANES):
        sh_ref[s - 1, 0:span, :] = pad_ref[s:s + span, :]
    w = w_ref[...]
    off = CONV_HALO - CONV_K // 2

    def chunk(cidx, carry):
        base = pl.multiple_of(cidx * CONV_ROWS, CONV_ROWS)
        acc = jnp.zeros((CONV_ROWS, CONV_WIDTH), F32) + b_ref[...]
        for j in range(CONV_K):
            s = (off + j) % SUBLANES
            win = pl.ds(base + (off + j - s), CONV_ROWS)
            tap = pad_ref[win, :] if s == 0 else sh_ref[s - 1, win, :]
            acc = acc + tap * w[j:j + 1, :]
        mu = jnp.mean(acc, axis=-1, keepdims=True)
        xc = acc - mu
        var = jnp.mean(xc * xc, axis=-1, keepdims=True)
        y = xc * lax.rsqrt(var + EPS) * lg_ref[...] + lb_ref[...]
        o_ref[pl.ds(base, CONV_ROWS), :] = (y * _sigmoid(y)).astype(o_ref.dtype)
        return carry

    lax.fori_loop(0, n // CONV_ROWS, chunk, 0)


def _conv(hcv, w, b, lg, lb, layer, o_prev, *, seq, blk0, n_batch):
    t_all = hcv.shape[0]
    tok = pl.BlockSpec((seq, CONV_WIDTH), lambda i: (blk0 + i, 0))
    full = lambda a: pl.BlockSpec(a.shape, lambda i: (0,) * a.ndim)
    in_specs = [tok] + [_layer_spec(a, layer) for a in (w, b, lg, lb)]
    args = [hcv, w, b, lg, lb]
    aliases = {}
    if o_prev is not None:
        in_specs.append(pl.BlockSpec(memory_space=pl.ANY))
        args.append(o_prev)
        aliases = {5: 0}
    return pl.pallas_call(
        functools.partial(_conv_kernel, has_prev=o_prev is not None),
        grid=(n_batch,),
        in_specs=in_specs,
        out_specs=tok,
        out_shape=jax.ShapeDtypeStruct((t_all, CONV_WIDTH), BF16),
        scratch_shapes=[pltpu.VMEM((seq + 2 * CONV_HALO, CONV_WIDTH), F32),
                        pltpu.VMEM((SUBLANES - 1, seq + 2 * CONV_HALO, CONV_WIDTH), F32)],
        input_output_aliases=aliases,
        compiler_params=_cparams(("parallel",)),
        name="conv",
    )(*args)


def _outproj_kernel(ogf_ref, ogb_ref, sg_ref, on_ref, oc_ref, x_ref, g2_ref, sh_ref, sc_ref, gn_ref, e_ref,
                    wo_ref, gffn_ref, rw_ref, rbias_ref, xmid_ref, h2_ref, meta_ref, cnt_ref, carry_ref):
    tile = x_ref.shape[0]
    part = tile // ROW_PARTS
    parts = [pl.ds(p * part, part) for p in range(ROW_PARTS)]
    e = e_ref[...]
    rw = rw_ref[...]

    @pl.when(pl.program_id(0) == 0)
    def _():
        carry_ref[...] = jnp.zeros_like(carry_ref)

    carry = carry_ref[...]
    meta_ref[...] = jnp.zeros_like(meta_ref)

    ofs, mss = [], []
    for rows in parts:
        of = ogf_ref[rows, :] + ogb_ref[rows, :]
        sq_hi, sq_lo = _split(of * of)
        ofs.append(of)
        mss.append(_dot(jnp.concatenate([sq_hi, sq_lo], axis=1), e) * (1.0 / GLA_DV))
    ys = []
    for rows, of, ms in zip(parts, ofs, mss):
        og = of * lax.rsqrt(ms + EPS) * gn_ref[...] * sg_ref[rows, :].astype(F32)
        mix = jnp.concatenate([og.astype(BF16), on_ref[rows, :], oc_ref[rows, :]], axis=1)
        ys.append(_dot(mix, wo_ref[...]))
    for p, (rows, y) in enumerate(zip(parts, ys)):
        x = x_ref[rows, :] + g2_ref[...] * y
        xmid_ref[rows, :] = x
        ms2 = jnp.mean(x * x, axis=-1, keepdims=True)
        h2 = x * lax.rsqrt(ms2 + EPS) * gffn_ref[...]
        h2 = h2 * (1.0 + sc_ref[...]) + sh_ref[...]
        h_hi, h_lo = _split(h2)
        _store_subrows(h2_ref, _pack_rows(h2), 0, row0=p * part)
        hw = _dot(h_hi, rw)
        logits = hw[:, :LANE] + hw[:, LANE:] + _dot(h_lo, rw[:, :LANE])
        cls, rank, wtok, carry = _route_tokens(logits, rbias_ref[...], carry)
        meta_ref[0:1, rows] = cls
        meta_ref[1:2, rows] = rank
        _store_subrows(h2_ref, wtok, SUB_Y, row0=p * part)
    carry_ref[...] = carry
    cnt_ref[...] = jnp.broadcast_to(carry, cnt_ref.shape)


def _outproj(o_gf, o_gb, sgate, o_n, o_c, x_all, mod5, layer, gnorm, e_mat, w_out, gffn, rw_split, bias_col, *,
             n_tiles, n_lat_tiles, tiles_per_seq, n_batch):
    d = x_all.shape[1]
    t_out = n_tiles * TOK_TILE

    def bidx(i):
        return jnp.where(i < n_lat_tiles, i // tiles_per_seq, n_batch)

    def modspec(j):
        return pl.BlockSpec((None, None, None, 1, d), lambda i: (layer, bidx(i), j, 0, 0))

    tok = lambda w: pl.BlockSpec((TOK_TILE, w), lambda i: (i, 0))
    full = lambda a: pl.BlockSpec(a.shape, lambda i: (0,) * a.ndim)
    return pl.pallas_call(
        _outproj_kernel,
        grid=(n_tiles,),
        in_specs=[tok(GLA_WIDTH), tok(GLA_WIDTH), tok(GLA_WIDTH), tok(NA_WIDTH), tok(CONV_WIDTH), tok(d),
                  modspec(2), modspec(3), modspec(4), _layer_spec(gnorm, layer), full(e_mat),
                  _layer_spec(w_out, layer), _layer_spec(gffn, layer),
                  full(rw_split), full(bias_col)],
        out_specs=[tok(d), pl.BlockSpec((TOK_TILE // 8, SUB_X, 8, LANE), lambda i: (i, 0, 0, 0)),
                   pl.BlockSpec((8, TOK_TILE), lambda i: (0, i)), pl.BlockSpec((32, LANE), lambda i: (0, 0))],
        out_shape=[jax.ShapeDtypeStruct((t_out, d), F32),
                   jax.ShapeDtypeStruct((t_out // 8, SUB_X, 8, LANE), jnp.uint32),
                   jax.ShapeDtypeStruct((8, t_out), F32), jax.ShapeDtypeStruct((32, LANE), F32)],
        scratch_shapes=[pltpu.VMEM((32, 1), F32)],
        compiler_params=_cparams(("arbitrary",)),
        name="outproj",
    )(o_gf, o_gb, sgate, o_n, o_c, x_all, mod5, mod5, mod5, gnorm, e_mat, w_out, gffn, rw_split, bias_col)


def _route_tokens(lg, bias, carry):
    tile = lg.shape[0]
    lt = lg.T
    aff = _sigmoid(lt[0:N_EXPERTS])
    sel = aff + bias
    s = [sel[e:e + 1] for e in range(N_EXPERTS)]
    a = [aff[e:e + 1] for e in range(N_EXPERTS)]

    def top2sum(v):
        best = v[0] + v[1]
        for i, j in PAIRS[1:]:
            best = jnp.maximum(best, v[i] + v[j])
        return best

    gs = [top2sum(s[4 * g:4 * g + 4]) for g in range(N_GROUPS)]
    gbest = jnp.zeros_like(gs[0], dtype=jnp.int32)
    gmax = gs[0]
    for g in range(1, N_GROUPS):
        upd = gs[g] > gmax
        gbest = jnp.where(upd, g, gbest)
        gmax = jnp.where(upd, gs[g], gmax)

    def pick(vals, j):
        out = vals[j]
        for g in range(1, N_GROUPS):
            out = jnp.where(gbest == g, vals[4 * g + j], out)
        return out

    sv = [pick(s, j) for j in range(EXPERTS_PER_GROUP)]
    av = [pick(a, j) for j in range(EXPERTS_PER_GROUP)]
    i1 = jnp.zeros_like(gbest)
    m1 = sv[0]
    for j in range(1, EXPERTS_PER_GROUP):
        upd = sv[j] > m1
        i1 = jnp.where(upd, j, i1)
        m1 = jnp.where(upd, sv[j], m1)
    i2 = jnp.full_like(gbest, -1)
    m2 = jnp.zeros_like(m1)
    for j in range(EXPERTS_PER_GROUP):
        upd = (i1 != j) & ((sv[j] > m2) | (i2 < 0))
        i2 = jnp.where(upd, j, i2)
        m2 = jnp.where(upd, sv[j], m2)
    ia = jnp.minimum(i1, i2)
    ib = jnp.maximum(i1, i2)
    pair = jnp.where(ia == 0, ib - 1, jnp.where(ia == 1, ib + 1, 5))
    cls = gbest * len(PAIRS) + pair

    def take(vals, idx):
        out = vals[0]
        for j in range(1, EXPERTS_PER_GROUP):
            out = jnp.where(idx == j, vals[j], out)
        return out

    w1 = take(av, i1)
    w2 = take(av, i2)
    tot = w1 + w2
    wa = jnp.where(i1 < i2, w1, w2) / tot
    wb = jnp.where(i1 < i2, w2, w1) / tot

    crow = lax.broadcasted_iota(jnp.int32, (32, tile), 0)
    oh = (crow == cls).astype(F32)
    us = lax.broadcasted_iota(jnp.int32, (tile, tile), 0)
    ut = lax.broadcasted_iota(jnp.int32, (tile, tile), 1)
    upper = (us < ut).astype(BF16)
    prefix = _dot(oh.astype(BF16), upper)
    rank = jnp.sum(oh * (prefix + carry), axis=0, keepdims=True)
    carry_new = carry + jnp.sum(oh, axis=1, keepdims=True)

    wrow = lax.broadcasted_iota(jnp.int32, (LANE, tile), 0)
    wmat = jnp.where(wrow == 0, wa, jnp.where(wrow == 1, wb, 0.0))
    wtok = lax.bitcast_convert_type(wmat.T, jnp.uint32)
    return cls.astype(F32), rank, wtok, carry_new


FF_TILE = 512


def _expert_kernel(ea_ref, eb_ref, nvalid_ref, xs_ref, wga_ref, wua_ref, wda_ref, wgb_ref, wub_ref, wdb_ref,
                   *rest, n_cast):
    del ea_ref, eb_ref
    j = pl.program_id(0)
    nvalid = nvalid_ref[j]
    nxt_in, (y_ref, *nxt_out) = rest[:n_cast], rest[n_cast:]

    def cast_next():
        _cast_slices(nxt_in, nxt_out)

    @pl.when(nvalid == 0)
    def _():
        cast_next()
        y_ref[...] = jnp.zeros_like(y_ref)

    @pl.when(nvalid != 0)
    def _():
        cast_next()
        rows = y_ref.shape[0] * 8
        live = lax.broadcasted_iota(jnp.int32, (rows, 1), 0) < nvalid
        x = jnp.where(live, _unpack_rows(_load_subrows(xs_ref, 0, SUB_Y)), 0.0).astype(BF16)
        ws = jnp.where(live, lax.bitcast_convert_type(xs_ref[:, SUB_Y, :, :].reshape(rows, LANE), F32), 0.0)
        ff = wga_ref.shape[1]

        items = [(w, f0) for w in ((wga_ref, wua_ref, wda_ref), (wgb_ref, wub_ref, wdb_ref))
                 for f0 in range(0, ff, FF_TILE)]

        def up(item):
            (wg_ref, wu_ref, _), f0 = item
            return _dot(x, wg_ref[:, f0:f0 + FF_TILE]), _dot(x, wu_ref[:, f0:f0 + FF_TILE])

        ups = [up(items[0]), up(items[1])]
        parts = []
        for c, ((_, _, wd_ref), f0) in enumerate(items):
            hg, hu = ups[c]
            hh = (hg * _sigmoid(hg) * hu).astype(BF16)
            if c + 2 < len(items):
                ups.append(up(items[c + 2]))
            parts.append(_dot(hh, wd_ref[f0:f0 + FF_TILE, :]))
        per = len(items) // 2
        ya = functools.reduce(lambda a, b: a + b, parts[:per])
        yb = functools.reduce(lambda a, b: a + b, parts[per:])
        _store_subrows(y_ref, _pack_rows(ya * ws[:, 0:1] + yb * ws[:, 1:2]), 0)


def _experts(xs_sub, ea, eb, nvalid, weights, next_f32):
    wg, wu, wd = weights
    nb = xs_sub.shape[0] * 8 // MOE_BLK
    d, ff = wg.shape[1], wg.shape[2]
    wspec_in = lambda which: pl.BlockSpec((None, d, ff), lambda j, ea, eb, v: ((ea, eb)[which][j], 0, 0))
    wspec_out = lambda which: pl.BlockSpec((None, ff, d), lambda j, ea, eb, v: ((ea, eb)[which][j], 0, 0))
    in_specs = [pl.BlockSpec((MOE_BLK // 8, SUB_X, 8, LANE), lambda j, ea, eb, v: (j, 0, 0, 0)),
                wspec_in(0), wspec_in(0), wspec_out(0), wspec_in(1), wspec_in(1), wspec_out(1)]
    out_specs = [pl.BlockSpec((MOE_BLK // 8, SUB_Y, 8, LANE), lambda j, ea, eb, v: (j, 0, 0, 0))]
    out_shape = [jax.ShapeDtypeStruct((nb * MOE_BLK // 8, SUB_Y, 8, LANE), jnp.uint32)]
    args = [ea, eb, nvalid, xs_sub, wg, wu, wd, wg, wu, wd]
    if next_f32 is not None:
        c_in, c_out, c_shape = _cast_specs(next_f32, nb, lambda j, *_: j)
        in_specs += c_in
        out_specs += c_out
        out_shape += c_shape
        args += list(next_f32[0])
    grid_spec = pltpu.PrefetchScalarGridSpec(num_scalar_prefetch=3, grid=(nb,), in_specs=in_specs,
                                             out_specs=out_specs)
    res = pl.pallas_call(
        functools.partial(_expert_kernel, n_cast=0 if next_f32 is None else len(next_f32[0])),
        grid_spec=grid_spec,
        out_shape=out_shape,
        compiler_params=_cparams(("arbitrary",)),
        name="experts",
    )(*args)
    return res[0], tuple(res[1:])


def _subrow_index(dest, nsub):
    t = dest.shape[0]
    base = ((dest // 8) * (nsub * 8) + dest % 8).astype(F32).reshape(t // LANE, LANE)
    src = np.arange(LANE)
    grp, r = src // 8, src % 8
    sel = np.zeros((LANE, LANE * nsub), np.float32)
    off = np.zeros((LANE * nsub,), np.int32)
    for j in range(nsub):
        pos = grp * (nsub * 8) + j * 8 + r
        sel[src, pos] = 1.0
        off[pos] = j * 8
    idx = jnp.dot(base, jnp.asarray(sel), precision=lax.Precision.HIGHEST).astype(jnp.int32) + jnp.asarray(off)
    return idx.reshape(t * nsub)


def _moe(h2_sub, meta, cnt, weights, next_f32, split=()):
    t = meta.shape[1]
    cls = meta[0].astype(jnp.int32)
    rank = meta[1].astype(jnp.int32)
    counts = cnt[:N_CLASSES, 0].astype(jnp.int32)
    padded = (counts + MOE_BLK - 1) // MOE_BLK * MOE_BLK
    pad_end = jnp.cumsum(padded)
    pad_start = pad_end - padded
    class_ids = jnp.arange(N_CLASSES, dtype=jnp.int32)
    dest = rank + jnp.sum(jnp.where(cls[:, None] == class_ids[None, :], pad_start[None, :], 0), axis=1)
    nb = t // MOE_BLK + N_CLASSES
    p_rows = nb * MOE_BLK
    blk_start = jnp.arange(nb, dtype=jnp.int32) * MOE_BLK
    valid = blk_start < pad_end[-1]
    blk_cls = jnp.sum((pad_end[None, :] <= blk_start[:, None]).astype(jnp.int32), axis=-1)
    last_cls = jnp.sum((pad_end <= pad_end[-1] - 1).astype(jnp.int32))
    blk_cls = jnp.minimum(jnp.where(valid, blk_cls, last_cls), N_CLASSES - 1)
    nvalid = jnp.where(valid, jnp.clip(pad_start[blk_cls] + counts[blk_cls] - blk_start, 0, MOE_BLK), 0)
    pair_a = jnp.array([p[0] for p in PAIRS], jnp.int32)
    pair_b = jnp.array([p[1] for p in PAIRS], jnp.int32)
    grp = blk_cls // len(PAIRS)
    ea = grp * EXPERTS_PER_GROUP + pair_a[blk_cls % len(PAIRS)]
    eb = grp * EXPERTS_PER_GROUP + pair_b[blk_cls % len(PAIRS)]

    xs = _sc_scatter(h2_sub.reshape(t * SUB_X, LANE), _subrow_index(dest, SUB_X), p_rows * SUB_X)
    ys, next_bf16 = _experts(xs.reshape(p_rows // 8, SUB_X, 8, LANE), ea, eb, nvalid.astype(jnp.int32), weights,
                             next_f32)
    idx = _subrow_index(dest, SUB_Y)
    ys2 = ys.reshape(p_rows * SUB_Y, LANE)
    bounds = [0] + [s * SUB_Y for s in split] + [t * SUB_Y]
    parts = [_sc_gather(ys2, idx[a:b]).reshape((b - a) // (8 * SUB_Y), SUB_Y, 8, LANE)
             for a, b in zip(bounds[:-1], bounds[1:])]
    return (parts if split else parts[0]), next_bf16


def _final_kernel(x_ref, y_ref, g5_ref, gf_ref, o_ref):
    x = x_ref[...] + g5_ref[...] * _unpack_rows(_load_subrows(y_ref, 0, SUB_Y))
    ms = jnp.mean(x * x, axis=-1, keepdims=True)
    o_ref[...] = x * lax.rsqrt(ms + EPS) * gf_ref[...]


def _final(x_mid, y, mod5, layer, g_final, *, tiles_per_seq):
    t, d = x_mid.shape
    tok = pl.BlockSpec((TOK_TILE, d), lambda i: (i, 0))
    return pl.pallas_call(
        _final_kernel,
        grid=(t // TOK_TILE,),
        in_specs=[tok, pl.BlockSpec((TOK_TILE // 8, SUB_Y, 8, LANE), lambda i: (i, 0, 0, 0)),
                  pl.BlockSpec((None, None, None, 1, d), lambda i: (layer, i // tiles_per_seq, 5, 0, 0)),
                  pl.BlockSpec(g_final.shape, lambda i: (0, 0))],
        out_specs=tok,
        out_shape=jax.ShapeDtypeStruct((t, d), F32),
        compiler_params=_cparams(("parallel",)),
        name="final_norm",
    )(x_mid, y, mod5, g_final)


def _rope_tables(seq):
    t = jnp.arange(seq)
    row = (t // GRID_W).astype(F32)
    col = (t % GRID_W).astype(F32)
    half = GLA_DK // 2
    inv = ROPE_BASE ** (-jnp.arange(0, half, 2, dtype=F32) / half)
    ang = jnp.concatenate([row[:, None] * inv, col[:, None] * inv], axis=-1)
    cos = jnp.repeat(jnp.cos(ang), 2, axis=-1)
    sin = (jnp.sin(ang)[:, :, None] * jnp.array([-1.0, 1.0], F32)).reshape(seq, GLA_DK)
    cos = jnp.tile(cos, (1, GLA_HEADS))
    sin = jnp.tile(sin, (1, GLA_HEADS))
    padw = QK_PAD - GLA_QK
    cos = jnp.pad(cos, ((0, 0), (0, padw)), constant_values=1.0)
    sin = jnp.pad(sin, ((0, 0), (0, padw)))
    cos = jnp.concatenate([cos, jnp.ones((TOK_TILE, QK_PAD), F32)], axis=0)
    sin = jnp.concatenate([sin, jnp.zeros((TOK_TILE, QK_PAD), F32)], axis=0)
    return cos, sin


def _pad_last(w, n):
    return jnp.pad(w, [(0, 0)] * (w.ndim - 1) + [(0, n - w.shape[-1])])


def _in_weights(w_in):
    offs = np.cumsum([0, GLA_QK, GLA_QK, GLA_WIDTH, GLA_WIDTH, 2 * GLA_LOWRANK, NA_WIDTH, NA_WIDTH, NA_WIDTH,
                      2 * CONV_WIDTH])
    seg = [w_in[..., offs[i]:offs[i + 1]] for i in range(9)]
    cols = [_pad_last(seg[0], QK_PAD), _pad_last(seg[1], QK_PAD), _pad_last(seg[4], LANE), seg[2], seg[3], seg[5],
            seg[6], seg[7], seg[8]]
    w = jnp.concatenate(cols, axis=-1).astype(BF16)
    assert w.shape[-1] == IN_COLS_PAD
    return w


def _gate_weights(wa_f, ba_f, wa_b, ba_b):
    depth = wa_f.shape[0]
    zero = jnp.zeros((depth, GLA_LOWRANK, QK_PAD), F32)
    top = jnp.concatenate([_pad_last(wa_f, QK_PAD), zero], axis=-1)
    bot = jnp.concatenate([zero, _pad_last(wa_b, QK_PAD)], axis=-1)
    rest = jnp.zeros((depth, LANE - 2 * GLA_LOWRANK, 2 * QK_PAD), F32)
    wa = jnp.concatenate([top, bot, rest], axis=1)
    ba = jnp.concatenate([_pad_last(ba_f, QK_PAD), _pad_last(ba_b, QK_PAD)], axis=-1)[:, None, :]
    return wa, ba


def _na_bias_tables(rpb):
    cq = np.arange(GRID_W)
    c0 = np.clip(cq - NA_KW // 2, 0, GRID_W - NA_KW)
    kc = np.arange(GRID_W)
    valid = (kc[None, :] >= c0[:, None]) & (kc[None, :] < c0[:, None] + NA_KW)
    w = GRID_W
    padded = jnp.pad(rpb, [(0, 0)] * 3 + [(w - NA_KW, w - NA_KW)], constant_values=NEG_BIG)
    flat = jnp.tile(padded, (1, 1, 1, w))
    band = flat[..., w - 1:w - 1 + w * (2 * w - 2)].reshape(rpb.shape[:3] + (w, 2 * w - 2))[..., :w]
    band = jnp.where(valid, band, NEG_BIG)
    tab = jnp.stack([band[:, :, NA_KH - 1 - v:2 * NA_KH - 1 - v] for v in range(NA_KH)], axis=2)
    tab = jnp.transpose(tab, (0, 1, 2, 4, 3, 5))
    return tab.reshape(rpb.shape[:2] + (NA_KH, GRID_W, NA_KH * GRID_W)).astype(F32)


def _head_mean_matrix():
    h = np.arange(GLA_WIDTH) // GLA_DV
    e = (h[:, None] == h[None, :]).astype(np.float32)
    return jnp.asarray(np.concatenate([e, e], axis=0), dtype=BF16)


def kernel(x, c, ctx, c_ctx, w_mod, b_mod, g_mix, g_ffn, w_in, gla_wa_f, gla_ba_f, gla_wa_b, gla_ba_b, gla_g_norm,
           na_rpb, conv_w, conv_b, conv_ln_g, conv_ln_b, w_out, router_w, router_bias, w_gate, w_up, w_down,
           g_final):
    bsz, seq, d = x.shape
    ctx_len = ctx.shape[1]
    depth = w_mod.shape[0]
    t_lat, t_ctx = bsz * seq, bsz * ctx_len
    assert seq % TOK_TILE == 0 and t_ctx % TOK_TILE == 0 and seq % ctx_len == 0
    assert d == 2 * SUB_Y * LANE
    assert seq // GRID_W >= NA_KH and (seq // GRID_W) % NA_UNROLL == 0
    assert ctx_len % (GLA_CHUNK * GLA_STEPS) == 0 and seq % (GLA_CHUNK * GLA_STEPS) == 0
    tiles_per_seq = seq // TOK_TILE
    n_lat_tiles = t_lat // TOK_TILE
    n_all_tiles = (t_lat + t_ctx) // TOK_TILE
    ctx_blk0 = t_lat // ctx_len

    mod_rows = -(-(bsz + 1) // 8) * 8
    c_pad = jnp.zeros((mod_rows, d), F32).at[:bsz].set(c).at[bsz].set(c_ctx)
    mod = _modulation(c_pad, w_mod, b_mod)
    mod5 = mod.reshape(depth, mod_rows, 6, 1, d)

    cos_t, sin_t = _rope_tables(seq)
    e_mat = _head_mean_matrix()
    rw = jnp.pad(router_w, ((0, 0), (0, LANE - N_EXPERTS)))
    rw_hi = rw.astype(BF16)
    rw_lo = (rw - rw_hi.astype(F32)).astype(BF16)
    rw_split = jnp.concatenate([rw_hi, rw_lo], axis=1)
    bias_col = router_bias.reshape(N_EXPERTS, 1).astype(F32)
    zero_state = jnp.zeros((bsz, GLA_WIDTH, QK_PAD), F32)

    w_aug = _in_weights(w_in)
    wa_aug, ba_aug = _gate_weights(gla_wa_f, gla_ba_f, gla_wa_b, gla_ba_b)
    bias_tab = _na_bias_tables(na_rpb)
    row = lambda a: a[:, None, :]
    gmix, gffn, gnorm = row(g_mix), row(g_ffn), row(gla_g_norm)
    cb, clg, clb = row(conv_b), row(conv_ln_g), row(conv_ln_b)
    w_out_b = w_out.astype(BF16)
    expert_f32 = (w_gate, w_up, w_down)
    layer_rows = w_gate.shape[1] * w_gate.shape[2]
    assert w_down.shape[1] * w_down.shape[2] == layer_rows
    expert_rows = tuple(w.reshape(depth * layer_rows, w.shape[-1]) for w in expert_f32)
    as_experts = lambda flat: tuple(a.reshape(w.shape[1:]) for a, w in zip(flat, expert_f32))

    x_all = (x.reshape(t_lat, d), ctx.reshape(t_ctx, d))
    half_tiles = max(n_all_tiles // 4, 1)
    y_moe = None
    for l in range(depth):
        last = l == depth - 1
        common = dict(n_lat_tiles=n_lat_tiles, tiles_per_seq=tiles_per_seq, n_batch=bsz, res_layer=l - 1)
        if l == 0:
            x_all, proj = _inproj(x_all, None, mod5, l, gmix, cos_t, sin_t, w_aug, wa_aug, ba_aug, **common)
        else:
            x_new, proj = _inproj(x_all, y_moe[0], mod5, l, gmix, cos_t, sin_t, w_aug, wa_aug, ba_aug,
                                  tile0=0, n_call_tiles=half_tiles, **common)
            x_all, proj = _inproj(x_all, y_moe[1], mod5, l, gmix, cos_t, sin_t, w_aug, wa_aug, ba_aug,
                                  tile0=half_tiles, n_call_tiles=n_all_tiles - half_tiles, prev=(x_new,) + tuple(proj),
                                  **common)
        q, k, v, sgate, gf, gb, nq, nk, nv, hcv = proj

        o_g, st_f, st_b = _gla(q, k, v, gf, gb, zero_state, zero_state, None, seq=ctx_len, blk0=ctx_blk0,
                               n_batch=bsz)
        o_g, _, _ = _gla(q, k, v, gf, gb, st_f, st_b, o_g, seq=seq, blk0=0, n_batch=bsz)

        o_n, cast0 = _na_latent(nq, nk, nv, bias_tab, l, (expert_rows, layer_rows, 0) if l == 0 else None,
                                seq=seq, ctx_len=ctx_len, n_batch=bsz, ctx_blk0=ctx_blk0)
        if l == 0:
            weights = as_experts(cast0)
        o_c = _conv(hcv, conv_w, cb, clg, clb, l, None, seq=seq, blk0=0, n_batch=bsz)
        if not last:
            o_n = _na_context(nq, nk, nv, o_n, ctx_len=ctx_len, n_batch=bsz, ctx_blk0=ctx_blk0)
            o_c = _conv(hcv, conv_w, cb, clg, clb, l, o_c, seq=ctx_len, blk0=ctx_blk0, n_batch=bsz)

        n_tiles = n_lat_tiles if last else n_all_tiles
        x_mid, h2, meta, cnt = _outproj(
            o_g[0], o_g[1], sgate, o_n, o_c, x_all, mod5, l, gnorm, e_mat, w_out_b, gffn, rw_split, bias_col,
            n_tiles=n_tiles, n_lat_tiles=n_lat_tiles, tiles_per_seq=tiles_per_seq, n_batch=bsz)
        next_f32 = None if last else (expert_rows, layer_rows, l + 1)
        y_moe, next_bf16 = _moe(h2, meta, cnt, weights, next_f32, split=() if last else (half_tiles * TOK_TILE,))
        if not last:
            weights = as_experts(next_bf16)
        x_all = x_mid

    out = _final(x_all, y_moe, mod5, depth - 1, g_final.reshape(1, d), tiles_per_seq=tiles_per_seq)
    return out.reshape(bsz, seq, d)
```
